```python
import math
import jax, jax.numpy as jnp
from jax import lax
import numpy as np

D_MODEL = 2048
BATCH = 8
SEQ = 8192
DEPTH = 4

CHUNK = 64
Q_BLOCK = 128
D_MIX = D_MODEL

MLA_HEADS = 6
MLA_NOPE = 128
MLA_ROPE = 64
MLA_V = 128
MLA_Q_RANK = 512
MLA_KV_RANK = 256
MLA_WIDTH = MLA_HEADS * MLA_V
ROPE_THETA = 10000.0

SG_GROUPS = 4
SG_GROUP_CH = 128
SG_WIDTH = SG_GROUPS * SG_GROUP_CH
SG_CHUNK = 128

SB_HEADS = 4
SB_HEAD_DIM = 128
SB_WIDTH = SB_HEADS * SB_HEAD_DIM

MEM_TOKENS = 256
MEM_HEADS = 4
MEM_HEAD_DIM = 64
MEM_WIDTH = MEM_HEADS * MEM_HEAD_DIM

IN_SIZES = (MLA_Q_RANK, MLA_KV_RANK, MLA_ROPE, MLA_WIDTH,
            SG_WIDTH, SG_WIDTH, SG_WIDTH,
            SB_WIDTH, SB_WIDTH, SB_WIDTH, SB_WIDTH,
            MEM_WIDTH, MEM_WIDTH)
D_IN = sum(IN_SIZES)

DEEPNORM_ALPHA = (2.0 * DEPTH) ** 0.25
DEEPNORM_BETA = (8.0 * DEPTH) ** -0.25
LN_EPS = 1e-5
RMS_EPS = 1e-6

kernel_name = "hybrid_mla_gmlp_stickbreak_deepnorm"


def _layer_norm(x, g, b):
    xf = x.astype(jnp.float32)
    mu = jnp.mean(xf, axis=-1, keepdims=True)
    xc = xf - mu
    var = jnp.mean(xc * xc, axis=-1, keepdims=True)
    return (xc * lax.rsqrt(var + LN_EPS) * g.astype(jnp.float32) + b.astype(jnp.float32)).astype(x.dtype)


def _rms_norm(x, g):
    xf = x.astype(jnp.float32)
    ms = jnp.mean(xf * xf, axis=-1, keepdims=True)
    return (xf * lax.rsqrt(ms + RMS_EPS) * g.astype(jnp.float32)).astype(x.dtype)


def _rope(x, cos, sin):
    half = x.shape[-1] // 2
    x1, x2 = x[..., :half], x[..., half:]
    return jnp.concatenate([x1 * cos - x2 * sin, x1 * sin + x2 * cos], axis=-1)


def _sweep_query_blocks(block_fn, q):
    b, s, h, d = q.shape
    nb = s // Q_BLOCK
    qb = q.reshape(b, nb, Q_BLOCK, h, d).transpose(1, 0, 2, 3, 4)
    out = lax.map(block_fn, (qb, jnp.arange(nb, dtype=jnp.int32)))
    return out.transpose(1, 0, 2, 3, 4).reshape(b, s, h, out.shape[-1])


def _chunk_causal_softmax_attention(q, k, v, scale):
    key_chunk = jnp.arange(k.shape[1]) // CHUNK

    def block(args):
        qblk, i = args
        qpos = i * Q_BLOCK + jnp.arange(Q_BLOCK)
        logits = jnp.einsum('bqhd,bkhd->bhqk', qblk, k).astype(jnp.float32) * scale
        mask = key_chunk[None, :] <= (qpos // CHUNK)[:, None]
        p = jax.nn.softmax(jnp.where(mask, logits, -jnp.inf), axis=-1)
        return jnp.einsum('bhqk,bkhd->bqhd', p.astype(v.dtype), v)

    return _sweep_query_blocks(block, q)


def _stick_breaking_attention(q, k, v, scale):
    kpos = jnp.arange(k.shape[1])

    def block(args):
        qblk, i = args
        qpos = i * Q_BLOCK + jnp.arange(Q_BLOCK)
        z = jnp.einsum('bqhd,bkhd->bhqk', qblk, k).astype(jnp.float32) * scale
        strict = kpos[None, :] < qpos[:, None]
        log_beta = jax.nn.log_sigmoid(z)
        log_1mb = jnp.where(strict, jax.nn.log_sigmoid(-z), 0.0)
        rev = lax.cumsum(log_1mb, axis=3, reverse=True)
        log_a = log_beta + rev - log_1mb
        a = jnp.where(strict, jnp.exp(log_a), 0.0)
        return jnp.einsum('bhqk,bkhd->bqhd', a.astype(v.dtype), v)

    return _sweep_query_blocks(block, q)


def _fwd_setup_inputs(seed: int = 0) -> dict:
    key = jax.random.key(seed)
    ks = jax.random.split(key, 20)
    f32 = jnp.float32
    nrm = lambda k, shape, s: jax.random.normal(k, shape, f32) * s
    x = jax.random.normal(ks[0], (BATCH, SEQ, D_MODEL), f32)
    mem = jax.random.normal(ks[1], (BATCH, MEM_TOKENS, D_MODEL), f32)
    offset = jax.random.randint(ks[2], (BATCH, 1), 0, 4096, dtype=jnp.int32)
    positions = (offset + jnp.arange(SEQ, dtype=jnp.int32)[None, :]).astype(jnp.int32)
    w_in = nrm(ks[3], (DEPTH, D_MODEL, D_IN), D_MODEL ** -0.5)
    q_norm_g = 1.0 + nrm(ks[4], (DEPTH, MLA_Q_RANK), 0.01)
    w_uq = nrm(ks[5], (DEPTH, MLA_Q_RANK, MLA_HEADS * (MLA_NOPE + MLA_ROPE)), MLA_Q_RANK ** -0.5)
    kv_norm_g = 1.0 + nrm(ks[6], (DEPTH, MLA_KV_RANK), 0.01)
    w_ukv = nrm(ks[7], (DEPTH, MLA_KV_RANK, MLA_HEADS * (MLA_NOPE + MLA_V)), MLA_KV_RANK ** -0.5)
    sg_ln_g = 1.0 + nrm(ks[8], (DEPTH, SG_WIDTH), 0.01)
    sg_ln_b = nrm(ks[9], (DEPTH, SG_WIDTH), 0.01)
    sg_w = nrm(ks[10], (DEPTH, SG_GROUPS, SG_CHUNK, SG_CHUNK), SG_CHUNK ** -0.5)
    sg_b = 1.0 + nrm(ks[11], (DEPTH, SG_GROUPS, SG_CHUNK), 0.01)
    w_mem_k = nrm(ks[12], (DEPTH, D_MODEL, MEM_WIDTH), D_MODEL ** -0.5)
    w_mem_v = nrm(ks[13], (DEPTH, D_MODEL, MEM_WIDTH), D_MODEL ** -0.5)
    w_out = nrm(ks[14], (DEPTH, D_MIX, D_MODEL), D_MIX ** -0.5 * DEEPNORM_BETA)
    ln_g = 1.0 + nrm(ks[15], (DEPTH, D_MODEL), 0.01)
    ln_b = nrm(ks[16], (DEPTH, D_MODEL), 0.01)
    return {"x": x, "mem": mem, "positions": positions, "w_in": w_in,
            "q_norm_g": q_norm_g, "w_uq": w_uq, "kv_norm_g": kv_norm_g, "w_ukv": w_ukv,
            "sg_ln_g": sg_ln_g, "sg_ln_b": sg_ln_b, "sg_w": sg_w, "sg_b": sg_b,
            "w_mem_k": w_mem_k, "w_mem_v": w_mem_v, "w_out": w_out,
            "ln_g": ln_g, "ln_b": ln_b}


def _fwd_reference(x, mem, positions, w_in, q_norm_g, w_uq, kv_norm_g, w_ukv,
              sg_ln_g, sg_ln_b, sg_w, sg_b, w_mem_k, w_mem_v, w_out, ln_g, ln_b):
    b, s, _ = x.shape
    inv_freq = ROPE_THETA ** (-jnp.arange(0, MLA_ROPE, 2, dtype=jnp.float32) / MLA_ROPE)
    ang = positions.astype(jnp.float32)[..., None] * inv_freq[None, None, :]
    cos = jnp.cos(ang).astype(x.dtype)
    sin = jnp.sin(ang).astype(x.dtype)
    split_idx = [int(v) for v in np.cumsum(IN_SIZES)[:-1]]
    p_in = jnp.arange(SG_CHUNK) // CHUNK
    sg_mask = (p_in[None, :] <= p_in[:, None]).astype(x.dtype)
    mla_scale = 1.0 / math.sqrt(MLA_NOPE + MLA_ROPE)
    sb_scale = 1.0 / math.sqrt(SB_HEAD_DIM)
    mem_scale = 1.0 / math.sqrt(MEM_HEAD_DIM)

    for l in range(DEPTH):
        h = jnp.einsum('bsd,de->bse', x, w_in[l])
        (c_q, c_kv, k_pe, g_a, sg_u, sg_v, g_b, sb_q, sb_k, sb_v, g_c, m_q, g_m) = jnp.split(h, split_idx, axis=-1)

        q = jnp.einsum('bsr,re->bse', _rms_norm(c_q, q_norm_g[l]), w_uq[l]).reshape(b, s, MLA_HEADS, MLA_NOPE + MLA_ROPE)
        q = jnp.concatenate([q[..., :MLA_NOPE], _rope(q[..., MLA_NOPE:], cos[:, :, None, :], sin[:, :, None, :])], axis=-1)
        kv = jnp.einsum('bsr,re->bse', _rms_norm(c_kv, kv_norm_g[l]), w_ukv[l]).reshape(b, s, MLA_HEADS, MLA_NOPE + MLA_V)
        k_rot = jnp.broadcast_to(_rope(k_pe, cos, sin)[:, :, None, :], (b, s, MLA_HEADS, MLA_ROPE))
        k = jnp.concatenate([kv[..., :MLA_NOPE], k_rot], axis=-1)
        o_a = _chunk_causal_softmax_attention(q, k, kv[..., MLA_NOPE:], mla_scale).reshape(b, s, MLA_WIDTH)

        u = jax.nn.gelu(sg_u)
        vn = _layer_norm(jax.nn.gelu(sg_v), sg_ln_g[l], sg_ln_b[l])
        vn = vn.reshape(b, s // SG_CHUNK, SG_CHUNK, SG_GROUPS, SG_GROUP_CH)
        w_sp = sg_w[l] * sg_mask[None]
        mixed = jnp.einsum('gts,bnsgc->bntgc', w_sp, vn) + sg_b[l].T[None, None, :, :, None]
        o_b = u * mixed.reshape(b, s, SG_WIDTH)

        o_c = _stick_breaking_attention(sb_q.reshape(b, s, SB_HEADS, SB_HEAD_DIM),
                                        sb_k.reshape(b, s, SB_HEADS, SB_HEAD_DIM),
                                        sb_v.reshape(b, s, SB_HEADS, SB_HEAD_DIM), sb_scale).reshape(b, s, SB_WIDTH)

        mk = jnp.einsum('bmd,de->bme', mem, w_mem_k[l]).reshape(b, MEM_TOKENS, MEM_HEADS, MEM_HEAD_DIM)
        mv = jnp.einsum('bmd,de->bme', mem, w_mem_v[l]).reshape(b, MEM_TOKENS, MEM_HEADS, MEM_HEAD_DIM)
        mq = m_q.reshape(b, s, MEM_HEADS, MEM_HEAD_DIM)
        mp = jax.nn.softmax(jnp.einsum('bshe,bmhe->bhsm', mq, mk).astype(jnp.float32) * mem_scale, axis=-1)
        o_m = jnp.einsum('bhsm,bmhe->bshe', mp.astype(mv.dtype), mv).reshape(b, s, MEM_WIDTH)

        y = jnp.concatenate([o_a * jax.nn.silu(g_a), o_b * jax.nn.silu(g_b),
                             o_c * jax.nn.silu(g_c), o_m * jax.nn.silu(g_m)], axis=-1)
        y = jnp.einsum('bse,ed->bsd', y, w_out[l])

        x = _layer_norm(DEEPNORM_ALPHA * x + y, ln_g[l], ln_b[l])
    return x


import jax as _jax
import jax.numpy as _jnp

TWIN_FORMAT = 'train_step'
FWD_PARAMS = ['x', 'mem', 'positions', 'w_in', 'q_norm_g', 'w_uq', 'kv_norm_g', 'w_ukv', 'sg_ln_g', 'sg_ln_b', 'sg_w', 'sg_b', 'w_mem_k', 'w_mem_v', 'w_out', 'ln_g', 'ln_b']
TWIN_WEIGHTS = ['w_in', 'q_norm_g', 'w_uq', 'kv_norm_g', 'w_ukv', 'sg_ln_g', 'sg_ln_b', 'sg_w', 'sg_b', 'w_mem_k', 'w_mem_v', 'w_out', 'ln_g', 'ln_b']
TWIN_DIFF_INPUT = 'x'
TWIN_INPUTS = ['x', 'mem', 'positions', 'w_in', 'q_norm_g', 'w_uq', 'kv_norm_g', 'w_ukv', 'sg_ln_g', 'sg_ln_b', 'sg_w', 'sg_b', 'w_mem_k', 'w_mem_v', 'w_out', 'ln_g', 'ln_b', 'loss_target', 'm_w_in', 'm_q_norm_g', 'm_w_uq', 'm_kv_norm_g', 'm_w_ukv', 'm_sg_ln_g', 'm_sg_ln_b', 'm_sg_w', 'm_sg_b', 'm_w_mem_k', 'm_w_mem_v', 'm_w_out', 'm_ln_g', 'm_ln_b', 'v_w_in', 'v_q_norm_g', 'v_w_uq', 'v_kv_norm_g', 'v_w_ukv', 'v_sg_ln_g', 'v_sg_ln_b', 'v_sg_w', 'v_sg_b', 'v_w_mem_k', 'v_w_mem_v', 'v_w_out', 'v_ln_g', 'v_ln_b']
TWIN_OUTPUTS = ['loss', 'grad_x', 'grad_w_in', 'grad_q_norm_g', 'grad_w_uq', 'grad_kv_norm_g', 'grad_w_ukv', 'grad_sg_ln_g', 'grad_sg_ln_b', 'grad_sg_w', 'grad_sg_b', 'grad_w_mem_k', 'grad_w_mem_v', 'grad_w_out', 'grad_ln_g', 'grad_ln_b', 'delta_w_in', 'delta_q_norm_g', 'delta_w_uq', 'delta_kv_norm_g', 'delta_w_ukv', 'delta_sg_ln_g', 'delta_sg_ln_b', 'delta_sg_w', 'delta_sg_b', 'delta_w_mem_k', 'delta_w_mem_v', 'delta_w_out', 'delta_ln_g', 'delta_ln_b', 'new_m_w_in', 'new_m_q_norm_g', 'new_m_w_uq', 'new_m_kv_norm_g', 'new_m_w_ukv', 'new_m_sg_ln_g', 'new_m_sg_ln_b', 'new_m_sg_w', 'new_m_sg_b', 'new_m_w_mem_k', 'new_m_w_mem_v', 'new_m_w_out', 'new_m_ln_g', 'new_m_ln_b', 'new_v_w_in', 'new_v_q_norm_g', 'new_v_w_uq', 'new_v_kv_norm_g', 'new_v_w_ukv', 'new_v_sg_ln_g', 'new_v_sg_ln_b', 'new_v_sg_w', 'new_v_sg_b', 'new_v_w_mem_k', 'new_v_w_mem_v', 'new_v_w_out', 'new_v_ln_g', 'new_v_ln_b']
TWIN_LEAF_KINDS = {'loss': 'loss', 'grad_x': 'grad_x', 'grad_w_in': 'grad_w', 'grad_q_norm_g': 'grad_w', 'grad_w_uq': 'grad_w', 'grad_kv_norm_g': 'grad_w', 'grad_w_ukv': 'grad_w', 'grad_sg_ln_g': 'grad_w', 'grad_sg_ln_b': 'grad_w', 'grad_sg_w': 'grad_w', 'grad_sg_b': 'grad_w', 'grad_w_mem_k': 'grad_w', 'grad_w_mem_v': 'grad_w', 'grad_w_out': 'grad_w', 'grad_ln_g': 'grad_w', 'grad_ln_b': 'grad_w', 'delta_w_in': 'delta_w', 'delta_q_norm_g': 'delta_w', 'delta_w_uq': 'delta_w', 'delta_kv_norm_g': 'delta_w', 'delta_w_ukv': 'delta_w', 'delta_sg_ln_g': 'delta_w', 'delta_sg_ln_b': 'delta_w', 'delta_sg_w': 'delta_w', 'delta_sg_b': 'delta_w', 'delta_w_mem_k': 'delta_w', 'delta_w_mem_v': 'delta_w', 'delta_w_out': 'delta_w', 'delta_ln_g': 'delta_w', 'delta_ln_b': 'delta_w', 'new_m_w_in': 'new_m', 'new_m_q_norm_g': 'new_m', 'new_m_w_uq': 'new_m', 'new_m_kv_norm_g': 'new_m', 'new_m_w_ukv': 'new_m', 'new_m_sg_ln_g': 'new_m', 'new_m_sg_ln_b': 'new_m', 'new_m_sg_w': 'new_m', 'new_m_sg_b': 'new_m', 'new_m_w_mem_k': 'new_m', 'new_m_w_mem_v': 'new_m', 'new_m_w_out': 'new_m', 'new_m_ln_g': 'new_m', 'new_m_ln_b': 'new_m', 'new_v_w_in': 'new_v', 'new_v_q_norm_g': 'new_v', 'new_v_w_uq': 'new_v', 'new_v_kv_norm_g': 'new_v', 'new_v_w_ukv': 'new_v', 'new_v_sg_ln_g': 'new_v', 'new_v_sg_ln_b': 'new_v', 'new_v_sg_w': 'new_v', 'new_v_sg_b': 'new_v', 'new_v_w_mem_k': 'new_v', 'new_v_w_mem_v': 'new_v', 'new_v_w_out': 'new_v', 'new_v_ln_g': 'new_v', 'new_v_ln_b': 'new_v'}


def _forward(args):
    return _fwd_reference(*[args[k] for k in FWD_PARAMS])


def _output_shape():
    def fwd():
        inp = _fwd_setup_inputs(0)
        return _fwd_reference(*[inp[k] for k in FWD_PARAMS])
    out = _jax.eval_shape(fwd)
    return out.shape, out.dtype

N_MICROBATCH = 1
ADAM_LR = 0.001
ADAM_B1 = 0.9
ADAM_B2 = 0.999
ADAM_EPS = 1e-08
ADAM_WD = 0.01
ADAM_STEP = 10
PER_EXAMPLE_BATCH_AXIS = {'x': 0, 'mem': 0, 'positions': 0, 'loss_target': 0}
SHARED_INPUTS = []
_WEIGHT_DTYPES = {'w_in': _jnp.float32, 'q_norm_g': _jnp.float32, 'w_uq': _jnp.float32, 'kv_norm_g': _jnp.float32, 'w_ukv': _jnp.float32, 'sg_ln_g': _jnp.float32, 'sg_ln_b': _jnp.float32, 'sg_w': _jnp.float32, 'sg_b': _jnp.float32, 'w_mem_k': _jnp.float32, 'w_mem_v': _jnp.float32, 'w_out': _jnp.float32, 'ln_g': _jnp.float32, 'ln_b': _jnp.float32}
MOMENT_SCALE = {'w_in': 1.002379e-02, 'q_norm_g': 3.337038e-03, 'w_uq': 2.246014e-03, 'kv_norm_g': 7.175562e-03, 'w_ukv': 2.612092e-03, 'sg_ln_g': 1.090239e-02, 'sg_ln_b': 1.065476e-02, 'sg_w': 1.063569e-02, 'sg_b': 1.232342e-02, 'w_mem_k': 1.941340e-03, 'w_mem_v': 1.987728e-03, 'w_out': 2.442818e-02, 'ln_g': 1.599035e+01, 'ln_b': 3.569020e-01}


def _to_microbatches(a, axis):
    t = _jnp.moveaxis(a, axis, 0)
    t = t.reshape((N_MICROBATCH, t.shape[0] // N_MICROBATCH) + t.shape[1:])
    return _jnp.moveaxis(t, 1, axis + 1)


def setup_inputs(seed: int = 0) -> dict:
    inp = _fwd_setup_inputs(seed)
    key = _jax.random.fold_in(_jax.random.key(seed), 7919)
    shape, _ = _output_shape()
    out = dict(inp)
    out["loss_target"] = _jax.random.normal(_jax.random.fold_in(key, 0), shape, _jnp.float32)
    for i, name in enumerate(TWIN_WEIGHTS):
        w = inp[name].astype(_jnp.float32)
        if MOMENT_SCALE is None:
            s = _jnp.sqrt(_jnp.mean(_jnp.square(w)) + 1e-30)
        else:
            s = MOMENT_SCALE[name]
        km, kv = _jax.random.split(_jax.random.fold_in(key, i + 1))
        out[name] = w
        out["m_" + name] = s * _jax.random.normal(km, w.shape, _jnp.float32)
        out["v_" + name] = (s * s) * _jax.random.uniform(kv, w.shape, _jnp.float32, 0.5, 1.5)
    if N_MICROBATCH > 1:
        for name, axis in PER_EXAMPLE_BATCH_AXIS.items():
            out[name] = _to_microbatches(out[name], axis)
    return {'x': out['x'], 'mem': out['mem'], 'positions': out['positions'], 'w_in': out['w_in'], 'q_norm_g': out['q_norm_g'], 'w_uq': out['w_uq'], 'kv_norm_g': out['kv_norm_g'], 'w_ukv': out['w_ukv'], 'sg_ln_g': out['sg_ln_g'], 'sg_ln_b': out['sg_ln_b'], 'sg_w': out['sg_w'], 'sg_b': out['sg_b'], 'w_mem_k': out['w_mem_k'], 'w_mem_v': out['w_mem_v'], 'w_out': out['w_out'], 'ln_g': out['ln_g'], 'ln_b': out['ln_b'], 'loss_target': out['loss_target'], 'm_w_in': out['m_w_in'], 'm_q_norm_g': out['m_q_norm_g'], 'm_w_uq': out['m_w_uq'], 'm_kv_norm_g': out['m_kv_norm_g'], 'm_w_ukv': out['m_w_ukv'], 'm_sg_ln_g': out['m_sg_ln_g'], 'm_sg_ln_b': out['m_sg_ln_b'], 'm_sg_w': out['m_sg_w'], 'm_sg_b': out['m_sg_b'], 'm_w_mem_k': out['m_w_mem_k'], 'm_w_mem_v': out['m_w_mem_v'], 'm_w_out': out['m_w_out'], 'm_ln_g': out['m_ln_g'], 'm_ln_b': out['m_ln_b'], 'v_w_in': out['v_w_in'], 'v_q_norm_g': out['v_q_norm_g'], 'v_w_uq': out['v_w_uq'], 'v_kv_norm_g': out['v_kv_norm_g'], 'v_w_ukv': out['v_w_ukv'], 'v_sg_ln_g': out['v_sg_ln_g'], 'v_sg_ln_b': out['v_sg_ln_b'], 'v_sg_w': out['v_sg_w'], 'v_sg_b': out['v_sg_b'], 'v_w_mem_k': out['v_w_mem_k'], 'v_w_mem_v': out['v_w_mem_v'], 'v_w_out': out['v_w_out'], 'v_ln_g': out['v_ln_g'], 'v_ln_b': out['v_ln_b']}


def _loss(weights, diff, rest, loss_target):
    with _jax.named_scope("forward"):
        args = {**rest, TWIN_DIFF_INPUT: diff, **{k: w.astype(_WEIGHT_DTYPES[k]) for k, w in weights.items()}}
        y = _forward(args)
    with _jax.named_scope("loss_head"):
        err = _jnp.square(y.astype(_jnp.float32) - loss_target)
        return 0.5 * _jnp.sum(_jnp.mean(err, axis=-1)) if err.ndim else 0.5 * err


def _adamw(w, g, m, v):
    m = ADAM_B1 * m + (1.0 - ADAM_B1) * g
    v = ADAM_B2 * v + (1.0 - ADAM_B2) * _jnp.square(g)
    m_hat = m / (1.0 - ADAM_B1 ** ADAM_STEP)
    v_hat = v / (1.0 - ADAM_B2 ** ADAM_STEP)
    delta = -ADAM_LR * (m_hat / (_jnp.sqrt(v_hat) + ADAM_EPS) + ADAM_WD * w)
    return delta, m, v


def reference(x, mem, positions, w_in, q_norm_g, w_uq, kv_norm_g, w_ukv, sg_ln_g, sg_ln_b, sg_w, sg_b, w_mem_k, w_mem_v, w_out, ln_g, ln_b, loss_target, m_w_in, m_q_norm_g, m_w_uq, m_kv_norm_g, m_w_ukv, m_sg_ln_g, m_sg_ln_b, m_sg_w, m_sg_b, m_w_mem_k, m_w_mem_v, m_w_out, m_ln_g, m_ln_b, v_w_in, v_q_norm_g, v_w_uq, v_kv_norm_g, v_w_ukv, v_sg_ln_g, v_sg_ln_b, v_sg_w, v_sg_b, v_w_mem_k, v_w_mem_v, v_w_out, v_ln_g, v_ln_b):
    given = dict(x=x, mem=mem, positions=positions, w_in=w_in, q_norm_g=q_norm_g, w_uq=w_uq, kv_norm_g=kv_norm_g, w_ukv=w_ukv, sg_ln_g=sg_ln_g, sg_ln_b=sg_ln_b, sg_w=sg_w, sg_b=sg_b, w_mem_k=w_mem_k, w_mem_v=w_mem_v, w_out=w_out, ln_g=ln_g, ln_b=ln_b, loss_target=loss_target, m_w_in=m_w_in, m_q_norm_g=m_q_norm_g, m_w_uq=m_w_uq, m_kv_norm_g=m_kv_norm_g, m_w_ukv=m_w_ukv, m_sg_ln_g=m_sg_ln_g, m_sg_ln_b=m_sg_ln_b, m_sg_w=m_sg_w, m_sg_b=m_sg_b, m_w_mem_k=m_w_mem_k, m_w_mem_v=m_w_mem_v, m_w_out=m_w_out, m_ln_g=m_ln_g, m_ln_b=m_ln_b, v_w_in=v_w_in, v_q_norm_g=v_q_norm_g, v_w_uq=v_w_uq, v_kv_norm_g=v_kv_norm_g, v_w_ukv=v_w_ukv, v_sg_ln_g=v_sg_ln_g, v_sg_ln_b=v_sg_ln_b, v_sg_w=v_sg_w, v_sg_b=v_sg_b, v_w_mem_k=v_w_mem_k, v_w_mem_v=v_w_mem_v, v_w_out=v_w_out, v_ln_g=v_ln_g, v_ln_b=v_ln_b)
    weights = {n: given[n] for n in TWIN_WEIGHTS}
    shared = {n: given[n] for n in SHARED_INPUTS}
    per_example = {n: given[n] for n in ['x', 'mem', 'positions']}
    grad_fn = _jax.value_and_grad(_loss, argnums=(0, 1))

    def one_microbatch(ex, loss_target):
        ex = dict(ex)
        diff = ex.pop(TWIN_DIFF_INPUT)
        return grad_fn(weights, diff, {**shared, **ex}, loss_target)

    if N_MICROBATCH == 1:
        loss, (grad_w, grad_x) = one_microbatch(per_example, given["loss_target"])
    else:
        def body(carry, xs):
            loss_sum, grad_sum = carry
            l_k, (gw_k, gx_k) = one_microbatch(xs[0], xs[1])
            with _jax.named_scope("update"):
                return (loss_sum + l_k, _jax.tree.map(_jnp.add, grad_sum, gw_k)), gx_k

        init = (_jnp.zeros((), _jnp.float32), _jax.tree.map(_jnp.zeros_like, weights))
        (loss, grad_w), grad_x = _jax.lax.scan(body, init, (per_example, given["loss_target"]))
    with _jax.named_scope("update"):
        delta_w, new_m, new_v = {}, {}, {}
        for n in TWIN_WEIGHTS:
            delta_w[n], new_m[n], new_v[n] = _adamw(weights[n], grad_w[n], given["m_" + n], given["v_" + n])
    return (loss, grad_x, *[grad_w[n] for n in TWIN_WEIGHTS], *[delta_w[n] for n in TWIN_WEIGHTS],
            *[new_m[n] for n in TWIN_WEIGHTS], *[new_v[n] for n in TWIN_WEIGHTS])
```

```python
import functools
import math

import jax
import jax.numpy as jnp
from jax import lax
from jax.experimental import pallas as pl
from jax.experimental.pallas import tpu as pltpu

F32 = jnp.float32
MXU = jnp.bfloat16

DEPTH_FOR_NORM = 4
D_MODEL = 2048
CHUNK = 64
MLA_HEADS, MLA_NOPE, MLA_ROPE, MLA_V = 6, 128, 64, 128
MLA_Q_RANK, MLA_KV_RANK = 512, 256
MLA_WIDTH = MLA_HEADS * MLA_V
MLA_HEAD_PAD = 256
ROPE_THETA = 10000.0
SG_GROUPS, SG_CH, SG_CHUNK = 4, 128, 128
SG_WIDTH = SG_GROUPS * SG_CH
SB_HEADS, SB_DIM = 4, 128
SB_WIDTH = SB_HEADS * SB_DIM
MEM_TOKENS, MEM_HEADS, MEM_DIM = 256, 4, 64
MEM_WIDTH = MEM_HEADS * MEM_DIM
D_IN = 5696
D_IN_PAD = 5760
ALPHA = (2.0 * DEPTH_FOR_NORM) ** 0.25
LN_EPS = 1e-5
RMS_EPS = 1e-6
MLA_SCALE = 1.0 / math.sqrt(MLA_NOPE + MLA_ROPE)
SB_SCALE = 1.0 / math.sqrt(SB_DIM)
MEM_SCALE = 1.0 / math.sqrt(MEM_DIM)
ADAM_LR, ADAM_B1, ADAM_B2, ADAM_EPS, ADAM_WD, ADAM_STEP = 0.001, 0.9, 0.999, 1e-08, 0.01, 10
N_DEV = 8
LANES = 128
PACK_ROWS = 1024
NEG = -1e30
VMEM_LIMIT = 56 * 1024 * 1024

C_CQ, C_CKV, C_GA, C_SGU, C_SGV, C_GB = 0, 512, 768, 1536, 2048, 2560
C_SBQ, C_SBK, C_SBV, C_GC, C_MQ, C_GM, C_KPE = 3072, 3584, 4096, 4608, 5120, 5376, 5632

SHARDED = (("w_in", 2), ("w_uq", 2), ("w_ukv", 2), ("w_mem_k", 1), ("w_mem_v", 1), ("w_out", 1))
REPLICATED = ("q_norm_g", "kv_norm_g", "sg_ln_g", "sg_ln_b", "sg_w", "sg_b", "ln_g", "ln_b")
WEIGHTS = ("w_in", "q_norm_g", "w_uq", "kv_norm_g", "w_ukv", "sg_ln_g", "sg_ln_b", "sg_w", "sg_b",
           "w_mem_k", "w_mem_v", "w_out", "ln_g", "ln_b")


def _params(sem):
    return pltpu.CompilerParams(dimension_semantics=sem, vmem_limit_bytes=VMEM_LIMIT)


def _dot(a, b, dims):
    return lax.dot_general(a.astype(MXU), b.astype(MXU), (dims, ((), ())), preferred_element_type=F32)


NN, NT, TN = ((1,), (0,)), ((1,), (1,)), ((0,), (0,))


@jax.custom_vjp
def _mm(a, b):
    return _dot(a, b, NN)


def _mm_fwd(a, b):
    return _dot(a, b, NN), (a, b)


def _mm_bwd(res, g):
    a, b = res
    return _dot(g, b, NT), _dot(a, g, TN)


_mm.defvjp(_mm_fwd, _mm_bwd)


@jax.custom_vjp
def _mm_nt(a, b):
    return _dot(a, b, NT)


def _mm_nt_fwd(a, b):
    return _dot(a, b, NT), (a, b)


def _mm_nt_bwd(res, g):
    a, b = res
    return _dot(g, b, NN), _dot(g, a, TN)


_mm_nt.defvjp(_mm_nt_fwd, _mm_nt_bwd)


def _split_dot(x, tri):
    hi = x.astype(MXU)
    r1 = x - hi.astype(F32)
    mid = r1.astype(MXU)
    lo = (r1 - mid.astype(F32)).astype(MXU)
    return _dot(hi, tri, NN) + _dot(mid, tri, NN) + _dot(lo, tri, NN)


def _col2row(col):
    return jnp.transpose(jnp.broadcast_to(col, (col.shape[0], LANES)))[0:8]


def _row2col(row):
    return jnp.transpose(jnp.broadcast_to(row, (LANES, row.shape[1])))[:, 0:1]


def _matmul(name, a, b, mode, tm, tn, tk, out_dtype=F32, add=None, add_scale=1.0, second_dtype=None):
    if mode == "nn":
        (m, k), n = a.shape, b.shape[1]
    elif mode == "nt":
        (m, k), n = a.shape, b.shape[0]
    else:
        (k, m), n = a.shape, b.shape[1]
    tm, tn, tk = min(tm, m), min(tn, n), min(tk, k)
    assert m % tm == 0 and n % tn == 0 and k % tk == 0, (name, m, n, k, tm, tn, tk)
    nk = k // tk
    dims = {"nn": NN, "nt": NT, "tn": TN}[mode]
    a_spec = (pl.BlockSpec((tk, tm), lambda j, i, kk: (kk, i)) if mode == "tn"
              else pl.BlockSpec((tm, tk), lambda j, i, kk: (i, kk)))
    b_spec = (pl.BlockSpec((tn, tk), lambda j, i, kk: (j, kk)) if mode == "nt"
              else pl.BlockSpec((tk, tn), lambda j, i, kk: (kk, j)))
    o_spec = pl.BlockSpec((tm, tn), lambda j, i, kk: (i, j))
    in_specs, operands = [a_spec, b_spec], [a, b]
    if add is not None:
        in_specs.append(o_spec)
        operands.append(add)
    out_shape = [jax.ShapeDtypeStruct((m, n), out_dtype)]
    out_specs = [o_spec]
    if second_dtype is not None:
        out_shape.append(jax.ShapeDtypeStruct((m, n), second_dtype))
        out_specs.append(o_spec)

    def body(*refs):
        a_ref, b_ref = refs[0], refs[1]
        add_ref = refs[2] if add is not None else None
        outs = refs[len(operands):-1]
        acc_ref = refs[-1]
        kk = pl.program_id(2)

        @pl.when(kk == 0)
        def _():
            acc_ref[...] = jnp.zeros_like(acc_ref)

        acc_ref[...] += _dot(a_ref[...], b_ref[...], dims)

        @pl.when(kk == nk - 1)
        def _():
            r = acc_ref[...]
            if add_ref is not None:
                r = r + add_scale * add_ref[...]
            outs[0][...] = r.astype(out_dtype)
            if second_dtype is not None:
                outs[1][...] = r.astype(second_dtype)

    res = pl.pallas_call(
        body, name=name, grid=(n // tn, m // tm, nk), in_specs=in_specs, out_specs=out_specs, out_shape=out_shape,
        scratch_shapes=[pltpu.VMEM((tm, tn), F32)],
        compiler_params=_params(("parallel", "parallel", "arbitrary")),
    )(*operands)
    return res if second_dtype is not None else res[0]


def _accumulate(ref, val, first):
    @pl.when(first)
    def _():
        ref[...] = val

    @pl.when(jnp.logical_not(first))
    def _():
        ref[...] += val


def _rowcall(name, fn, rows, tm, row_in, const_in, row_out, acc_out=()):
    tm = min(tm, rows)
    assert rows % tm == 0
    n_in = len(row_in) + len(const_in)
    in_specs = [pl.BlockSpec((tm, w), lambda i, cb=cb: (i, cb)) for _, w, cb in row_in]
    in_specs += [pl.BlockSpec(c.shape, lambda i, nd=c.ndim: (0,) * nd) for c in const_in]
    out_specs = [pl.BlockSpec(blk, imap) for _, _, blk, imap in row_out]
    out_specs += [pl.BlockSpec(s, lambda i, nd=len(s): (0,) * nd) for s, _ in acc_out]
    out_shape = [jax.ShapeDtypeStruct(s, dt) for s, dt, _, _ in row_out]
    out_shape += [jax.ShapeDtypeStruct(s, dt) for s, dt in acc_out]

    def body(*refs):
        vals = fn(*[r[...] for r in refs[:n_in]])
        outs = refs[n_in:]
        first = pl.program_id(0) == 0
        for k, (_, dt, _, _) in enumerate(row_out):
            outs[k][...] = vals[k].astype(dt)
        for k in range(len(acc_out)):
            _accumulate(outs[len(row_out) + k], vals[len(row_out) + k], first)

    return pl.pallas_call(
        body, name=name, grid=(rows // tm,), in_specs=in_specs, out_specs=out_specs, out_shape=out_shape,
        compiler_params=_params(("arbitrary",) if acc_out else ("parallel",)),
    )(*[a for a, _, _ in row_in], *const_in)


def _ro(rows, width, dtype, tm):
    tm = min(tm, rows)
    return ((rows, width), dtype, (tm, width), lambda i: (i, 0))


def _rms(x, g):
    ms = jnp.mean(x * x, axis=-1, keepdims=True)
    return x * lax.rsqrt(ms + RMS_EPS) * g


def _ln(x, g, b):
    mu = jnp.mean(x, axis=-1, keepdims=True)
    xc = x - mu
    var = jnp.mean(xc * xc, axis=-1, keepdims=True)
    return xc * lax.rsqrt(var + LN_EPS) * g + b


def _gelu(x):
    return 0.5 * x * (1.0 + jnp.tanh(math.sqrt(2.0 / math.pi) * (x + 0.044715 * (x * x * x))))


def _silu(x):
    return x / (1.0 + jnp.exp(-x))


@jax.custom_vjp
def _rope(r, cc, sa, sb):
    return r * cc + pltpu.roll(r, 96, 1) * sa + pltpu.roll(r, 32, 1) * sb


def _rope_fwd(r, cc, sa, sb):
    return _rope(r, cc, sa, sb), (cc, sa, sb)


def _rope_bwd(res, d):
    cc, sa, sb = res
    return d * cc + pltpu.roll(d * sa, 32, 1) + pltpu.roll(d * sb, 96, 1), None, None, None


_rope.defvjp(_rope_fwd, _rope_bwd)


def _rope_tables_fn(pos, invf):
    ang = pos.astype(F32) * invf
    lane = lax.broadcasted_iota(jnp.int32, ang.shape, 1)
    cos, sin = jnp.cos(ang), jnp.sin(ang)
    zero = jnp.zeros_like(ang)
    cc = jnp.where(lane < 64, cos, zero)
    sa = jnp.where(lane < 32, -sin, zero)
    sb = jnp.where((lane >= 32) & (lane < 64), sin, zero)
    return cc, sa, sb


def _prep_fn(cq, ckv, kpe, cc, sa, sb, gq, wuq, gkv, wukv):
    q = _mm(_rms(cq, gq), wuq)
    kv = _mm(_rms(ckv, gkv), wukv)
    krot = _rope(kpe, cc, sa, sb)
    qs, ks = [], []
    for h in range(MLA_HEADS):
        o = h * MLA_HEAD_PAD
        qs += [q[:, o:o + 128], _rope(q[:, o + 128:o + 256], cc, sa, sb)]
        ks += [kv[:, h * 128:(h + 1) * 128], krot]
    return jnp.concatenate(qs, axis=1), jnp.concatenate(ks, axis=1), kv[:, MLA_HEADS * 128:]


def _post_fn(ga, sgu, sgv, gb, gc, mq, gm, oa, oc, mkv, lng, lnb, sgw, biasf):
    tm = ga.shape[0]
    u = _gelu(sgu)
    vn = _ln(_gelu(sgv), lng, lnb)
    t_chunk = lax.broadcasted_iota(jnp.int32, (SG_CHUNK, SG_CHUNK), 0) // CHUNK
    s_chunk = lax.broadcasted_iota(jnp.int32, (SG_CHUNK, SG_CHUNK), 1) // CHUNK
    sg_mask = (s_chunk <= t_chunk).astype(F32)
    cols = []
    for g in range(SG_GROUPS):
        wg = sgw[g] * sg_mask
        vg = vn[:, g * SG_CH:(g + 1) * SG_CH]
        cols.append(jnp.concatenate(
            [_mm(wg, vg[n * SG_CHUNK:(n + 1) * SG_CHUNK, :]) for n in range(tm // SG_CHUNK)], axis=0))
    mixed = jnp.concatenate(cols, axis=1) + jnp.concatenate([biasf] * (tm // SG_CHUNK), axis=0)
    ob = u * mixed
    mk, mv = mkv[:, :MEM_WIDTH], mkv[:, MEM_WIDTH:]
    head = lax.broadcasted_iota(jnp.int32, (1, MEM_WIDTH), 1) // MEM_DIM
    om = jnp.zeros((tm, MEM_WIDTH), F32)
    for hd in range(MEM_HEADS):
        hm = (head == hd).astype(F32)
        s = _mm_nt(mq, mk * hm) * MEM_SCALE
        e = jnp.exp(s - lax.stop_gradient(jnp.max(s, axis=-1, keepdims=True)))
        p = e / jnp.sum(e, axis=-1, keepdims=True)
        om = om + _mm(p, mv * hm)
    return jnp.concatenate([oa * _silu(ga), ob * _silu(gb), oc * _silu(gc), om * _silu(gm)], axis=1)


def _post_bwd_fn(dyg, ga, sgu, sgv, gb, gc, mq, gm, oa, oc, mkv, lng, lnb, sgw, biasf):
    _, vjp = jax.vjp(_post_fn, ga, sgu, sgv, gb, gc, mq, gm, oa, oc, mkv, lng, lnb, sgw, biasf)
    dga, dsgu, dsgv, dgb, dgc, dmq, dgm, doa, doc, dmkv, dlng, dlnb, dsgw, dbias = vjp(dyg)
    drows = jnp.stack([_col2row(jnp.sum((doa * oa)[:, h * 128:(h + 1) * 128], axis=1, keepdims=True))
                       for h in range(MLA_HEADS)], axis=0)
    return dga, dsgu, dsgv, dgb, dgc, dmq, dgm, doa, doc, drows, dmkv, dlng, dlnb, dsgw, dbias


def _prep_bwd_fn(dq, dk, dv, cq, ckv, kpe, cc, sa, sb, gq, wuq, gkv, wukv):
    wuq, wukv = wuq.astype(F32), wukv.astype(F32)
    _, vjp = jax.vjp(lambda a, b, c, d, e, f, g: _prep_fn(a, b, c, cc, sa, sb, d, e, f, g), cq, ckv, kpe, gq, wuq, gkv, wukv)
    return vjp((dq, dk, dv))


def _ln_bwd_fn(dxn, z, g, b):
    _, vjp = jax.vjp(_ln, z, g, b)
    dz, dg, db = vjp(dxn)
    return dz, dz, dg, db


def _outproj_fn(yg, x, w, g, b):
    z = ALPHA * x + _dot(yg, w, NN)
    xn = _ln(z, g, b)
    return z, xn, xn


def _loss_fn(xl, tgt):
    err = xl - tgt
    part = 0.5 * jnp.sum(jnp.mean(err * err, axis=-1, keepdims=True), axis=0, keepdims=True)
    return err * (1.0 / D_MODEL), jnp.broadcast_to(part, (1, LANES))


def _mla_fwd(q, k, v, tq):
    t = q.shape[0]
    tq = min(tq, t)
    nq = t // tq

    def body(q_ref, k_ref, v_ref, o_ref, lse_ref):
        i = pl.program_id(1)
        qb = q_ref[...]
        r_chunk = lax.broadcasted_iota(jnp.int32, (tq, tq), 0) // CHUNK
        c_chunk = lax.broadcasted_iota(jnp.int32, (tq, tq), 1) // CHUNK
        diag_ok = c_chunk <= r_chunk

        def block(j, carry, masked):
            m, l, acc = carry
            off = pl.multiple_of(j * tq, tq)
            s = _dot(qb, k_ref[pl.ds(off, tq), :], NT) * MLA_SCALE
            if masked:
                s = jnp.where(diag_ok, s, NEG)
            m2 = jnp.maximum(m, jnp.max(s, axis=1, keepdims=True))
            p = jnp.exp(s - m2)
            a = jnp.exp(m - m2)
            return m2, a * l + jnp.sum(p, axis=1, keepdims=True), a * acc + _dot(p, v_ref[pl.ds(off, tq), :], NN)

        init = (jnp.full((tq, 1), NEG, F32), jnp.zeros((tq, 1), F32), jnp.zeros((tq, MLA_V), F32))
        carry = lax.fori_loop(0, i, lambda j, c: block(j, c, False), init)
        m, l, acc = block(i, carry, True)
        o_ref[...] = acc / l
        lse_ref[...] = _col2row(m + jnp.log(l))

    return pl.pallas_call(
        body, name="mla_fwd", grid=(MLA_HEADS, nq),
        in_specs=[pl.BlockSpec((tq, MLA_HEAD_PAD), lambda h, i: (i, h)),
                  pl.BlockSpec((t, MLA_HEAD_PAD), lambda h, i: (0, h)),
                  pl.BlockSpec((t, MLA_V), lambda h, i: (0, h))],
        out_specs=[pl.BlockSpec((tq, MLA_V), lambda h, i: (i, h)),
                   pl.BlockSpec((None, 8, tq), lambda h, i: (h, 0, i))],
        out_shape=[jax.ShapeDtypeStruct((t, MLA_WIDTH), F32), jax.ShapeDtypeStruct((MLA_HEADS, 8, t), F32)],
        compiler_params=_params(("parallel", "parallel")),
    )(q, k, v)


def _mla_bwd(q, k, v, do, lse, drow, tq):
    t = q.shape[0]
    tq = min(tq, t)
    nq = t // tq

    def body(k_ref, v_ref, q_ref, do_ref, lse_ref, d_ref, dk_ref, dv_ref, dq_ref):
        j = pl.program_id(1)

        @pl.when(j == 0)
        def _():
            dq_ref[...] = jnp.zeros_like(dq_ref)

        kj, vj = k_ref[...], v_ref[...]
        k_chunk = lax.broadcasted_iota(jnp.int32, (tq, tq), 0) // CHUNK
        q_chunk = lax.broadcasted_iota(jnp.int32, (tq, tq), 1) // CHUNK
        diag_ok = k_chunk <= q_chunk

        def block(i, carry, masked):
            dk, dv = carry
            off = pl.multiple_of(i * tq, tq)
            qi, doi = q_ref[pl.ds(off, tq), :], do_ref[pl.ds(off, tq), :]
            st = _dot(kj, qi, NT) * MLA_SCALE
            pt = jnp.exp(st - lse_ref[0:1, pl.ds(off, tq)])
            if masked:
                pt = jnp.where(diag_ok, pt, 0.0)
            dv = dv + _dot(pt, doi, NN)
            dst = (pt * (_dot(vj, doi, NT) - d_ref[0:1, pl.ds(off, tq)]) * MLA_SCALE).astype(MXU)
            dk = dk + _dot(dst, qi, NN)
            dq_ref[pl.ds(off, tq), :] += _dot(dst, kj, TN)
            return dk, dv

        carry = block(j, (jnp.zeros((tq, MLA_HEAD_PAD), F32), jnp.zeros((tq, MLA_V), F32)), True)
        dk, dv = lax.fori_loop(j + 1, nq, lambda i, c: block(i, c, False), carry)
        dk_ref[...] = dk
        dv_ref[...] = dv

    dk, dv, dq = pl.pallas_call(
        body, name="mla_bwd", grid=(MLA_HEADS, nq),
        in_specs=[pl.BlockSpec((tq, MLA_HEAD_PAD), lambda h, j: (j, h)),
                  pl.BlockSpec((tq, MLA_V), lambda h, j: (j, h)),
                  pl.BlockSpec((t, MLA_HEAD_PAD), lambda h, j: (0, h)),
                  pl.BlockSpec((t, MLA_V), lambda h, j: (0, h)),
                  pl.BlockSpec((None, 8, t), lambda h, j: (h, 0, 0)),
                  pl.BlockSpec((None, 8, t), lambda h, j: (h, 0, 0))],
        out_specs=[pl.BlockSpec((tq, MLA_HEAD_PAD), lambda h, j: (j, h)),
                   pl.BlockSpec((tq, MLA_V), lambda h, j: (j, h)),
                   pl.BlockSpec((t, MLA_HEAD_PAD), lambda h, j: (0, h))],
        out_shape=[jax.ShapeDtypeStruct((t, MLA_HEADS * MLA_HEAD_PAD), F32),
                   jax.ShapeDtypeStruct((t, MLA_WIDTH), F32),
                   jax.ShapeDtypeStruct((t, MLA_HEADS * MLA_HEAD_PAD), F32)],
        compiler_params=_params(("parallel", "arbitrary")),
    )(k, v, q, do, lse, drow)
    return dq, dk, dv


def _sb_logs(z):
    soft = jnp.log(1.0 + jnp.exp(-jnp.abs(z)))
    return jnp.minimum(z, 0.0) - soft, jnp.minimum(-z, 0.0) - soft


def _tri(n, keep):
    r = lax.broadcasted_iota(jnp.int32, (n, n), 0)
    c = lax.broadcasted_iota(jnp.int32, (n, n), 1)
    return keep(r, c)


def _sb_fwd(hb, tq):
    t = hb.shape[0]
    tq = min(tq, t)
    nq = t // tq
    cq, ck, cv = C_SBQ // SB_DIM, C_SBK // SB_DIM, C_SBV // SB_DIM

    def body(q_ref, k_ref, v_ref, o_ref, tot_ref):
        i = pl.program_id(1)
        qb = q_ref[...]
        strict = _tri(tq, lambda r, c: c < r)
        later = _tri(tq, lambda r, c: r > c).astype(MXU)

        def block(j, carry, masked):
            run, acc = carry
            off = pl.multiple_of(j * tq, tq)
            z = _dot(qb, k_ref[pl.ds(off, tq), :], NT) * SB_SCALE
            lb, l1 = _sb_logs(z)
            if masked:
                l1 = jnp.where(strict, l1, 0.0)
            a = jnp.exp(lb + _split_dot(l1, later) + run)
            if masked:
                a = jnp.where(strict, a, 0.0)
            return run + jnp.sum(l1, axis=1, keepdims=True), acc + _dot(a, v_ref[pl.ds(off, tq), :], NN)

        carry = block(i, (jnp.zeros((tq, 1), F32), jnp.zeros((tq, SB_DIM), F32)), True)
        run, acc = lax.fori_loop(0, i, lambda jj, c: block(i - 1 - jj, c, False), carry)
        o_ref[...] = acc
        tot_ref[...] = _col2row(run)

    return pl.pallas_call(
        body, name="sb_fwd", grid=(SB_HEADS, nq),
        in_specs=[pl.BlockSpec((tq, SB_DIM), lambda h, i: (i, cq + h)),
                  pl.BlockSpec((t, SB_DIM), lambda h, i: (0, ck + h)),
                  pl.BlockSpec((t, SB_DIM), lambda h, i: (0, cv + h))],
        out_specs=[pl.BlockSpec((tq, SB_DIM), lambda h, i: (i, h)),
                   pl.BlockSpec((None, 8, tq), lambda h, i: (h, 0, i))],
        out_shape=[jax.ShapeDtypeStruct((t, SB_WIDTH), F32), jax.ShapeDtypeStruct((SB_HEADS, 8, t), F32)],
        compiler_params=_params(("parallel", "parallel")),
    )(hb, hb, hb)


def _sb_bwd(hb, do, tot, tq):
    t = hb.shape[0]
    tq = min(tq, t)
    nq = t // tq
    cq, ck, cv = C_SBQ // SB_DIM, C_SBK // SB_DIM, C_SBV // SB_DIM

    def body(q_ref, do_ref, tot_ref, k_ref, v_ref, dq_ref, dk_ref, dv_ref):
        i = pl.program_id(1)

        @pl.when(i == 0)
        def _():
            dk_ref[...] = jnp.zeros_like(dk_ref)
            dv_ref[...] = jnp.zeros_like(dv_ref)

        qb, dob = q_ref[...], do_ref[...]
        total = _row2col(tot_ref[0:1, :])
        strict = _tri(tq, lambda r, c: c < r)
        upto = _tri(tq, lambda r, c: r <= c).astype(MXU)
        before = _tri(tq, lambda r, c: r < c).astype(MXU)

        def block(j, carry, masked):
            pre, gpre, dq = carry
            off = pl.multiple_of(j * tq, tq)
            kj, vj = k_ref[pl.ds(off, tq), :], v_ref[pl.ds(off, tq), :]
            z = _dot(qb, kj, NT) * SB_SCALE
            lb, l1 = _sb_logs(z)
            if masked:
                l1 = jnp.where(strict, l1, 0.0)
            a = jnp.exp(lb + (total - (_split_dot(l1, upto) + pre)))
            if masked:
                a = jnp.where(strict, a, 0.0)
            g = a * _dot(dob, vj, NT)
            beta = jnp.exp(lb)
            dz = (g * (1.0 - beta) - (_split_dot(g, before) + gpre) * beta) * SB_SCALE
            if masked:
                dz = jnp.where(strict, dz, 0.0)
            dzb = dz.astype(MXU)
            dk_ref[pl.ds(off, tq), :] += _dot(dzb, qb, TN)
            dv_ref[pl.ds(off, tq), :] += _dot(a, dob, TN)
            return (pre + jnp.sum(l1, axis=1, keepdims=True), gpre + jnp.sum(g, axis=1, keepdims=True),
                    dq + _dot(dzb, kj, NN))

        init = (jnp.zeros((tq, 1), F32), jnp.zeros((tq, 1), F32), jnp.zeros((tq, SB_DIM), F32))
        carry = lax.fori_loop(0, i, lambda j, c: block(j, c, False), init)
        dq_ref[...] = block(i, carry, True)[2].astype(dq_ref.dtype)

    return pl.pallas_call(
        body, name="sb_bwd", grid=(SB_HEADS, nq),
        in_specs=[pl.BlockSpec((tq, SB_DIM), lambda h, i: (i, cq + h)),
                  pl.BlockSpec((tq, SB_DIM), lambda h, i: (i, h)),
                  pl.BlockSpec((None, 8, tq), lambda h, i: (h, 0, i)),
                  pl.BlockSpec((t, SB_DIM), lambda h, i: (0, ck + h)),
                  pl.BlockSpec((t, SB_DIM), lambda h, i: (0, cv + h))],
        out_specs=[pl.BlockSpec((tq, SB_DIM), lambda h, i: (i, h)),
                   pl.BlockSpec((t, SB_DIM), lambda h, i: (0, h)),
                   pl.BlockSpec((t, SB_DIM), lambda h, i: (0, h))],
        out_shape=[jax.ShapeDtypeStruct((t, SB_WIDTH), MXU), jax.ShapeDtypeStruct((t, SB_WIDTH), F32),
                   jax.ShapeDtypeStruct((t, SB_WIDTH), F32)],
        compiler_params=_params(("parallel", "arbitrary")),
    )(hb, do, tot, hb, hb)


TM = 256
TQ_MLA = 256
TQ_SB = 256


def _layer_fwd(xf, xb, w, consts):
    t = xf.shape[0]
    cc, sa, sb, mem = consts
    h, hb = _matmul("in_proj", xb, w["w_in"], "nn", 512, 1152, 2048, second_dtype=MXU)
    gq, gkv = w["q_norm_g"][None, :], w["kv_norm_g"][None, :]
    q, k, v = _rowcall(
        "mla_prep", _prep_fn, t, TM,
        [(h, 512, C_CQ // 512), (h, 256, C_CKV // 256), (h, 128, C_KPE // 128), (cc, 128, 0), (sa, 128, 0), (sb, 128, 0)],
        [gq, w["w_uq"], gkv, w["w_ukv"]],
        [_ro(t, MLA_HEADS * MLA_HEAD_PAD, MXU, TM), _ro(t, MLA_HEADS * MLA_HEAD_PAD, MXU, TM), _ro(t, MLA_WIDTH, MXU, TM)])
    oa, lse = _mla_fwd(q, k, v, TQ_MLA)
    oc, tot = _sb_fwd(hb, TQ_SB)
    mkv = _matmul("mem_kv", mem, w["w_mem_kv"], "nn", 256, 512, 2048)
    lng, lnb = w["sg_ln_g"][None, :], w["sg_ln_b"][None, :]
    biasf = jnp.repeat(w["sg_b"].T, SG_CH, axis=1)
    post_rows = [(h, 768, C_GA // 768), (h, 512, C_SGU // 512), (h, 512, C_SGV // 512), (h, 512, C_GB // 512),
                 (h, 512, C_GC // 512), (h, 256, C_MQ // 256), (h, 256, C_GM // 256), (oa, 768, 0), (oc, 512, 0)]
    post_consts = [mkv, lng, lnb, w["sg_w"], biasf]
    (yg,) = _rowcall("mixer_post", lambda *a: (_post_fn(*a),), t, TM, post_rows, post_consts, [_ro(t, D_MODEL, MXU, TM)])
    z, xn, xnb = _rowcall(
        "out_proj_ln", _outproj_fn, t, TM, [(yg, D_MODEL, 0), (xf, D_MODEL, 0)],
        [w["w_out"], w["ln_g"][None, :], w["ln_b"][None, :]],
        [_ro(t, D_MODEL, F32, TM), _ro(t, D_MODEL, F32, TM), _ro(t, D_MODEL, MXU, TM)])
    saved = dict(xb=xb, h=h, hb=hb, q=q, k=k, v=v, oa=oa, lse=lse, oc=oc, tot=tot, mkv=mkv, yg=yg, z=z,
                 post_rows=post_rows, post_consts=post_consts, gq=gq, gkv=gkv)
    return xn, xnb, saved


def _layer_bwd(dxn, w, s, consts):
    t = dxn.shape[0]
    cc, sa, sb, mem = consts
    h = s["h"]
    dz, dzb, dlng2, dlnb2 = _rowcall(
        "ln_bwd", _ln_bwd_fn, t, TM, [(dxn, D_MODEL, 0), (s["z"], D_MODEL, 0)],
        [w["ln_g"][None, :], w["ln_b"][None, :]],
        [_ro(t, D_MODEL, F32, TM), _ro(t, D_MODEL, MXU, TM)], [((1, D_MODEL), F32), ((1, D_MODEL), F32)])
    dyg = _matmul("out_proj_dx", dzb, w["w_out"], "nt", 512, 1024, 2048)
    dw_out = _matmul("out_proj_dw", s["yg"], dzb, "tn", 1024, 1024, 512)
    tm = min(TM, t)
    outs = _rowcall(
        "mixer_post_bwd", _post_bwd_fn, t, TM, [(dyg, D_MODEL, 0)] + s["post_rows"], s["post_consts"],
        [_ro(t, 768, MXU, TM), _ro(t, 512, MXU, TM), _ro(t, 512, MXU, TM), _ro(t, 512, MXU, TM), _ro(t, 512, MXU, TM),
         _ro(t, 256, MXU, TM), _ro(t, 256, MXU, TM), _ro(t, 768, MXU, TM), _ro(t, 512, MXU, TM),
         ((MLA_HEADS, 8, t), F32, (MLA_HEADS, 8, tm), lambda i: (0, 0, i))],
        [((MEM_TOKENS, 2 * MEM_WIDTH), F32), ((1, SG_WIDTH), F32), ((1, SG_WIDTH), F32),
         ((SG_GROUPS, SG_CHUNK, SG_CHUNK), F32), ((SG_CHUNK, SG_WIDTH), F32)])
    dga, dsgu, dsgv, dgb, dgc, dmq, dgm, doa, doc, drow, dmkv, dsg_ln_g, dsg_ln_b, dsgw, dbias = outs
    dw_mem_kv = _matmul("mem_kv_dw", mem, dmkv, "tn", 1024, 512, 256)
    dsbq, dsbk, dsbv = _sb_bwd(s["hb"], doc, s["tot"], TQ_SB)
    dq, dk, dv = _mla_bwd(s["q"], s["k"], s["v"], doa, s["lse"], drow, TQ_MLA)
    wq, wkv = MLA_HEADS * MLA_HEAD_PAD, MLA_WIDTH
    dcq, dckv, dkpe, dgq, dwuq, dgkv, dwukv = _rowcall(
        "mla_prep_bwd", _prep_bwd_fn, t, TM,
        [(dq, wq, 0), (dk, wq, 0), (dv, wkv, 0), (h, 512, C_CQ // 512), (h, 256, C_CKV // 256), (h, 128, C_KPE // 128),
         (cc, 128, 0), (sa, 128, 0), (sb, 128, 0)],
        [s["gq"], w["w_uq"], s["gkv"], w["w_ukv"]],
        [_ro(t, 512, MXU, TM), _ro(t, 256, MXU, TM), _ro(t, 128, MXU, TM)],
        [((1, MLA_Q_RANK), F32), ((MLA_Q_RANK, wq), F32), ((1, MLA_KV_RANK), F32), ((MLA_KV_RANK, 2 * wkv), F32)])
    dh = jnp.concatenate([dcq, dckv, dga, dsgu, dsgv, dgb, dsbq, dsbk.astype(MXU), dsbv.astype(MXU), dgc, dmq, dgm, dkpe], axis=1)
    dx = _matmul("in_proj_dx", dh, w["w_in"], "nt", 512, 1024, 1152, add=dz, add_scale=ALPHA)
    dw_in = _matmul("in_proj_dw", s["xb"], dh, "tn", 1024, 1152, 512)
    grads = dict(w_in=dw_in, q_norm_g=dgq[0], w_uq=dwuq, kv_norm_g=dgkv[0], w_ukv=dwukv, sg_ln_g=dsg_ln_g[0],
                 sg_ln_b=dsg_ln_b[0], sg_w=dsgw, sg_bias=dbias, w_mem_kv=dw_mem_kv, w_out=dw_out, ln_g=dlng2[0], ln_b=dlnb2[0])
    return dx, grads


def _local_step(x, mem, positions, target, full, depth):
    t = x.shape[0]
    inv_freq = ROPE_THETA ** (-jnp.arange(0, MLA_ROPE, 2, dtype=F32) / MLA_ROPE)
    invf = jnp.tile(inv_freq, 4)[None, :]
    cc, sa, sb = _rowcall("rope_tables", _rope_tables_fn, t, TM, [(positions[:, None], 1, 0)], [invf],
                          [_ro(t, 128, F32, TM)] * 3)
    consts = (cc, sa, sb, mem)
    xf, xb, saved = x, x.astype(MXU), []
    for l in range(depth):
        xf, xb, s = _layer_fwd(xf, xb, full[l], consts)
        saved.append(s)
    dx, loss_part = _rowcall("loss", _loss_fn, t, TM, [(xf, D_MODEL, 0), (target, D_MODEL, 0)], [],
                             [_ro(t, D_MODEL, F32, TM)], [((1, LANES), F32)])
    grads = [None] * depth
    for l in reversed(range(depth)):
        dx, grads[l] = _layer_bwd(dx, full[l], saved[l], consts)
    db = jnp.concatenate([g["sg_bias"] for g in grads], axis=0)

    def lane_sums(d):
        return (jnp.concatenate([jnp.broadcast_to(jnp.sum(d[:, g * SG_CH:(g + 1) * SG_CH], axis=1, keepdims=True),
                                                  (d.shape[0], SG_CH)) for g in range(SG_GROUPS)], axis=1),)

    (dbs,) = _rowcall("sg_bias_sum", lane_sums, db.shape[0], SG_CHUNK, [(db, SG_WIDTH, 0)], [], [_ro(db.shape[0], SG_WIDTH, F32, SG_CHUNK)])
    dsgb = dbs[:, ::SG_CH].reshape(depth, SG_CHUNK, SG_GROUPS).transpose(0, 2, 1)
    return loss_part[0, 0], dx, grads, dsgb


def _permute_in_cols(w):
    pad = jnp.zeros(w.shape[:-1] + (D_IN_PAD - D_IN,), w.dtype)
    return jnp.concatenate([w[..., :768], w[..., 832:], w[..., 768:832], pad], axis=-1)


def _unpermute_in_cols(g):
    return jnp.concatenate([g[..., :768], g[..., C_KPE:C_KPE + MLA_ROPE], g[..., 768:C_KPE]], axis=-1)


def _pad_uq(w):
    w = w.reshape(w.shape[:-1] + (MLA_HEADS, MLA_NOPE + MLA_ROPE))
    w = jnp.pad(w, [(0, 0)] * (w.ndim - 1) + [(0, MLA_HEAD_PAD - MLA_NOPE - MLA_ROPE)])
    return w.reshape(w.shape[:-2] + (MLA_HEADS * MLA_HEAD_PAD,))


def _unpad_uq(g):
    g = g.reshape(g.shape[:-1] + (MLA_HEADS, MLA_HEAD_PAD))[..., :MLA_NOPE + MLA_ROPE]
    return g.reshape(g.shape[:-2] + (MLA_HEADS * (MLA_NOPE + MLA_ROPE),))


def _split_ukv(w):
    w = w.reshape(w.shape[:-1] + (MLA_HEADS, 2, 128))
    return jnp.swapaxes(w, -3, -2).reshape(w.shape[:-3] + (MLA_HEADS * 256,))


def _unsplit_ukv(g):
    g = g.reshape(g.shape[:-1] + (2, MLA_HEADS, 128))
    return jnp.swapaxes(g, -3, -2).reshape(g.shape[:-3] + (MLA_HEADS * 256,))


def _pack_rows(flat, rows):
    return jnp.pad(flat, (0, rows * LANES - flat.shape[0])).reshape(rows, LANES)


def _exchange(name, src, per_peer):
    rows = src.shape[-2]

    def body(src_ref, out_ref, send_sems, recv_sems, local_sem):
        x, y, c = lax.axis_index("x"), lax.axis_index("y"), lax.axis_index("c")
        me = 4 * x + 2 * y + c

        def block_for(p):
            return src_ref.at[p] if per_peer else src_ref

        local = pltpu.make_async_copy(block_for(me), out_ref.at[me], local_sem)
        local.start()
        sends, peers = [], []
        for k in range(1, N_DEV):
            px = 1 - x if (k >> 2) & 1 else x
            py = 1 - y if (k >> 1) & 1 else y
            pc = 1 - c if k & 1 else c
            peer = 4 * px + 2 * py + pc
            cp = pltpu.make_async_remote_copy(
                src_ref=block_for(peer), dst_ref=out_ref.at[me], send_sem=send_sems.at[k - 1], recv_sem=recv_sems.at[k - 1],
                device_id=(px, py, pc), device_id_type=pl.DeviceIdType.MESH)
            cp.start()
            sends.append(cp)
            peers.append((peer, (px, py, pc)))
        for k, (peer, pid) in enumerate(peers):
            pltpu.make_async_remote_copy(
                src_ref=block_for(peer), dst_ref=out_ref.at[peer], send_sem=send_sems.at[k], recv_sem=recv_sems.at[k],
                device_id=pid, device_id_type=pl.DeviceIdType.MESH).wait_recv()
        for cp in sends:
            cp.wait_send()
        local.wait()

    return pl.pallas_call(
        body, name=name, out_shape=jax.ShapeDtypeStruct((N_DEV, rows, LANES), src.dtype),
        in_specs=[pl.BlockSpec(memory_space=pl.ANY)], out_specs=pl.BlockSpec(memory_space=pl.ANY),
        scratch_shapes=[pltpu.SemaphoreType.DMA((N_DEV - 1,)), pltpu.SemaphoreType.DMA((N_DEV - 1,)), pltpu.SemaphoreType.DMA],
    )(src)


def _adamw(recv, wp, mp, vp):
    rows = wp.shape[0]
    c1 = 1.0 - ADAM_B1 ** ADAM_STEP
    c2 = 1.0 - ADAM_B2 ** ADAM_STEP

    def body(r_ref, w_ref, m_ref, v_ref, g_ref, d_ref, nm_ref, nv_ref):
        g = r_ref[0]
        for k in range(1, N_DEV):
            g = g + r_ref[k]
        m = ADAM_B1 * m_ref[...] + (1.0 - ADAM_B1) * g
        v = ADAM_B2 * v_ref[...] + (1.0 - ADAM_B2) * (g * g)
        g_ref[...] = g
        nm_ref[...] = m
        nv_ref[...] = v
        d_ref[...] = -ADAM_LR * ((m / c1) / (jnp.sqrt(v / c2) + ADAM_EPS) + ADAM_WD * w_ref[...])

    blk = pl.BlockSpec((PACK_ROWS, LANES), lambda i: (i, 0))
    return pl.pallas_call(
        body, name="adamw", grid=(rows // PACK_ROWS,),
        in_specs=[pl.BlockSpec((N_DEV, PACK_ROWS, LANES), lambda i: (0, i, 0)), blk, blk, blk],
        out_specs=[blk] * 4, out_shape=[jax.ShapeDtypeStruct((rows, LANES), F32)] * 4,
        compiler_params=_params(("parallel",)),
    )(recv, wp, mp, vp)


def _shard_of(full, axis):
    n = full.shape[axis] // N_DEV
    parts = full.reshape(full.shape[:axis] + (N_DEV, n) + full.shape[axis + 1:])
    return jnp.moveaxis(parts, axis, 0).reshape(N_DEV, -1)


def kernel(x, mem, positions, w_in, q_norm_g, w_uq, kv_norm_g, w_ukv, sg_ln_g, sg_ln_b, sg_w, sg_b, w_mem_k, w_mem_v, w_out, ln_g, ln_b, loss_target, m_w_in, m_q_norm_g, m_w_uq, m_kv_norm_g, m_w_ukv, m_sg_ln_g, m_sg_ln_b, m_sg_w, m_sg_b, m_w_mem_k, m_w_mem_v, m_w_out, m_ln_g, m_ln_b, v_w_in, v_q_norm_g, v_w_uq, v_kv_norm_g, v_w_ukv, v_sg_ln_g, v_sg_ln_b, v_sg_w, v_sg_b, v_w_mem_k, v_w_mem_v, v_w_out, v_ln_g, v_ln_b):
    wl = dict(w_in=w_in, q_norm_g=q_norm_g, w_uq=w_uq, kv_norm_g=kv_norm_g, w_ukv=w_ukv, sg_ln_g=sg_ln_g, sg_ln_b=sg_ln_b,
              sg_w=sg_w, sg_b=sg_b, w_mem_k=w_mem_k, w_mem_v=w_mem_v, w_out=w_out, ln_g=ln_g, ln_b=ln_b)
    ml = dict(w_in=m_w_in, q_norm_g=m_q_norm_g, w_uq=m_w_uq, kv_norm_g=m_kv_norm_g, w_ukv=m_w_ukv, sg_ln_g=m_sg_ln_g,
              sg_ln_b=m_sg_ln_b, sg_w=m_sg_w, sg_b=m_sg_b, w_mem_k=m_w_mem_k, w_mem_v=m_w_mem_v, w_out=m_w_out, ln_g=m_ln_g, ln_b=m_ln_b)
    vl = dict(w_in=v_w_in, q_norm_g=v_q_norm_g, w_uq=v_w_uq, kv_norm_g=v_kv_norm_g, w_ukv=v_w_ukv, sg_ln_g=v_sg_ln_g,
              sg_ln_b=v_sg_ln_b, sg_w=v_sg_w, sg_b=v_sg_b, w_mem_k=v_w_mem_k, w_mem_v=v_w_mem_v, w_out=v_w_out, ln_g=v_ln_g, ln_b=v_ln_b)
    depth = w_in.shape[0]
    names = [n for n, _ in SHARDED] + list(REPLICATED)
    sizes = [wl[n].size for n in names]
    n_sharded = sum(sizes[:len(SHARDED)])

    flat = jnp.concatenate([wl[n].astype(MXU).reshape(-1) for n, _ in SHARDED])
    assert n_sharded % (16 * LANES) == 0
    gathered = _exchange("gather_weights", flat.reshape(-1, LANES), per_peer=False).reshape(N_DEV, -1)
    whole, off = {}, 0
    for (n, axis), size in zip(SHARDED, sizes):
        part = gathered[:, off:off + size].reshape((N_DEV,) + wl[n].shape)
        part = jnp.moveaxis(part, 0, axis)
        whole[n] = part.reshape(part.shape[:axis] + (N_DEV * wl[n].shape[axis],) + part.shape[axis + 2:])
        off += size
    w_in_p = _permute_in_cols(whole["w_in"])
    w_uq_p = _pad_uq(whole["w_uq"])
    w_ukv_p = _split_ukv(whole["w_ukv"])
    w_mem_kv = jnp.concatenate([whole["w_mem_k"], whole["w_mem_v"]], axis=-1)
    full = [dict(w_in=w_in_p[l], w_uq=w_uq_p[l], w_ukv=w_ukv_p[l], w_mem_kv=w_mem_kv[l], w_out=whole["w_out"][l],
                 q_norm_g=q_norm_g[l], kv_norm_g=kv_norm_g[l], sg_ln_g=sg_ln_g[l], sg_ln_b=sg_ln_b[l], sg_w=sg_w[l],
                 sg_b=sg_b[l], ln_g=ln_g[l], ln_b=ln_b[l]) for l in range(depth)]

    loss_part, grad_x, grads, dsgb = _local_step(x[0], mem[0], positions[0], loss_target[0], full, depth)
    loss = lax.psum(loss_part, ("x", "y", "c"))

    def stacked(key):
        return jnp.stack([g[key] for g in grads])

    dmem = stacked("w_mem_kv")
    gfull = dict(w_in=_unpermute_in_cols(stacked("w_in")), w_uq=_unpad_uq(stacked("w_uq")), w_ukv=_unsplit_ukv(stacked("w_ukv")),
                 w_mem_k=dmem[..., :MEM_WIDTH], w_mem_v=dmem[..., MEM_WIDTH:], w_out=stacked("w_out"),
                 q_norm_g=stacked("q_norm_g"), kv_norm_g=stacked("kv_norm_g"), sg_ln_g=stacked("sg_ln_g"),
                 sg_ln_b=stacked("sg_ln_b"), sg_w=stacked("sg_w"), sg_b=dsgb, ln_g=stacked("ln_g"), ln_b=stacked("ln_b"))

    total = sum(sizes)
    rows = -(-total // (PACK_ROWS * LANES)) * PACK_ROWS
    per_dest = jnp.concatenate([_shard_of(gfull[n], axis) for n, axis in SHARDED]
                               + [jnp.broadcast_to(gfull[n].reshape(1, -1), (N_DEV, gfull[n].size)) for n in REPLICATED], axis=1)
    per_dest = jnp.pad(per_dest, ((0, 0), (0, rows * LANES - total))).reshape(N_DEV, rows, LANES)
    recv = _exchange("exchange_grads", per_dest, per_peer=True)

    def packed(d):
        return _pack_rows(jnp.concatenate([d[n].reshape(-1) for n in names]), rows)

    outs = _adamw(recv, packed(wl), packed(ml), packed(vl))
    unpacked = []
    for o in outs:
        o, off, d = o.reshape(-1), 0, {}
        for n, size in zip(names, sizes):
            d[n] = o[off:off + size].reshape(wl[n].shape)
            off += size
        unpacked.append(d)
    return (loss, grad_x[None], *[unpacked[0][n] for n in WEIGHTS], *[unpacked[1][n] for n in WEIGHTS],
            *[unpacked[2][n] for n in WEIGHTS], *[unpacked[3][n] for n in WEIGHTS])
```

```python
import math

import jax
import jax.numpy as jnp
from jax import lax
from jax.experimental import pallas as pl
from jax.experimental.pallas import tpu as pltpu

F32 = jnp.float32
MXU = jnp.bfloat16

DEPTH_FOR_NORM = 4
D_MODEL = 2048
CHUNK = 64
MLA_HEADS, MLA_NOPE, MLA_ROPE, MLA_V = 6, 128, 64, 128
MLA_Q_RANK, MLA_KV_RANK = 512, 256
MLA_WIDTH = MLA_HEADS * MLA_V
MLA_HEAD_PAD = 256
ROPE_THETA = 10000.0
SG_GROUPS, SG_CH, SG_CHUNK = 4, 128, 128
SG_WIDTH = SG_GROUPS * SG_CH
SB_HEADS, SB_DIM = 4, 128
SB_WIDTH = SB_HEADS * SB_DIM
MEM_TOKENS, MEM_HEADS, MEM_DIM = 256, 4, 64
MEM_WIDTH = MEM_HEADS * MEM_DIM
D_IN = 5696
D_IN_PAD = 5760
ALPHA = (2.0 * DEPTH_FOR_NORM) ** 0.25
LN_EPS = 1e-5
RMS_EPS = 1e-6
MLA_SCALE = 1.0 / math.sqrt(MLA_NOPE + MLA_ROPE)
SB_SCALE = 1.0 / math.sqrt(SB_DIM)
MEM_SCALE = 1.0 / math.sqrt(MEM_DIM)
ADAM_LR, ADAM_B1, ADAM_B2, ADAM_EPS, ADAM_WD, ADAM_STEP = 0.001, 0.9, 0.999, 1e-08, 0.01, 10
N_DEV = 8
LANES = 128
NEG = -1e30
SCAN = 256
VMEM_LIMIT = 56 * 1024 * 1024

C_CQ, C_CKV, C_GA, C_SGU, C_SGV, C_GB = 0, 512, 768, 1536, 2048, 2560
C_SBQ, C_SBK, C_SBV, C_GC, C_MQ, C_GM, C_KPE = 3072, 3584, 4096, 4608, 5120, 5376, 5632

REPLICATED = ("q_norm_g", "kv_norm_g", "sg_ln_g", "sg_ln_b", "sg_w", "sg_b", "ln_g", "ln_b")
WEIGHTS = ("w_in", "q_norm_g", "w_uq", "kv_norm_g", "w_ukv", "sg_ln_g", "sg_ln_b", "sg_w", "sg_b",
           "w_mem_k", "w_mem_v", "w_out", "ln_g", "ln_b")


def _params(sem):
    return pltpu.CompilerParams(dimension_semantics=sem, vmem_limit_bytes=VMEM_LIMIT)


def _dot(a, b, dims):
    return lax.dot_general(a.astype(MXU), b.astype(MXU), (dims, ((), ())), preferred_element_type=F32)


NN, NT, TN = ((1,), (0,)), ((1,), (1,)), ((0,), (0,))


@jax.custom_vjp
def _mm(a, b):
    return _dot(a, b, NN)


def _mm_fwd(a, b):
    return _dot(a, b, NN), (a, b)


def _mm_bwd(res, g):
    a, b = res
    return _dot(g, b, NT), _dot(a, g, TN)


_mm.defvjp(_mm_fwd, _mm_bwd)


@jax.custom_vjp
def _mm_nt(a, b):
    return _dot(a, b, NT)


def _mm_nt_fwd(a, b):
    return _dot(a, b, NT), (a, b)


def _mm_nt_bwd(res, g):
    a, b = res
    return _dot(g, b, NN), _dot(g, a, TN)


_mm_nt.defvjp(_mm_nt_fwd, _mm_nt_bwd)


def _split_dot(x, tri):
    hi = x.astype(MXU)
    lo = (x - hi.astype(F32)).astype(MXU)
    return _dot(hi, tri, NN) + _dot(lo, tri, NN)


def _scan_keys(x, tri, reverse):
    w = tri.shape[0]
    nb = x.shape[1] // w
    outs, carry = [None] * nb, None
    for c in (reversed(range(nb)) if reverse else range(nb)):
        xc = x[:, c * w:(c + 1) * w]
        sc = _split_dot(xc, tri)
        outs[c] = sc if carry is None else sc + carry
        rs = jnp.sum(xc, axis=1, keepdims=True)
        carry = rs if carry is None else carry + rs
    return (outs[0] if nb == 1 else jnp.concatenate(outs, axis=1)), carry


def _col2row(col):
    return jnp.transpose(jnp.broadcast_to(col, (col.shape[0], LANES)))[0:8]


def _row2col(row):
    return jnp.transpose(jnp.broadcast_to(row, (LANES, row.shape[1])))[:, 0:1]


def _matmul(name, a, b, mode, tm, tn, tk, out_dtype=F32, add=None, add_scale=1.0, second_dtype=None):
    if mode == "nn":
        (m, k), n = a.shape, b.shape[1]
    elif mode == "nt":
        (m, k), n = a.shape, b.shape[0]
    else:
        (k, m), n = a.shape, b.shape[1]
    tm, tn, tk = min(tm, m), min(tn, n), min(tk, k)
    assert m % tm == 0 and n % tn == 0 and k % tk == 0, (name, m, n, k, tm, tn, tk)
    nk = k // tk
    dims = {"nn": NN, "nt": NT, "tn": TN}[mode]
    a_spec = (pl.BlockSpec((tk, tm), lambda j, i, kk: (kk, i)) if mode == "tn"
              else pl.BlockSpec((tm, tk), lambda j, i, kk: (i, kk)))
    b_spec = (pl.BlockSpec((tn, tk), lambda j, i, kk: (j, kk)) if mode == "nt"
              else pl.BlockSpec((tk, tn), lambda j, i, kk: (kk, j)))
    o_spec = pl.BlockSpec((tm, tn), lambda j, i, kk: (i, j))
    in_specs, operands = [a_spec, b_spec], [a, b]
    if add is not None:
        in_specs.append(o_spec)
        operands.append(add)
    out_shape = [jax.ShapeDtypeStruct((m, n), out_dtype)]
    out_specs = [o_spec]
    if second_dtype is not None:
        out_shape.append(jax.ShapeDtypeStruct((m, n), second_dtype))
        out_specs.append(o_spec)

    def body(*refs):
        a_ref, b_ref = refs[0], refs[1]
        add_ref = refs[2] if add is not None else None
        outs = refs[len(operands):-1]
        acc_ref = refs[-1]
        kk = pl.program_id(2)

        @pl.when(kk == 0)
        def _():
            acc_ref[...] = jnp.zeros_like(acc_ref)

        acc_ref[...] += _dot(a_ref[...], b_ref[...], dims)

        @pl.when(kk == nk - 1)
        def _():
            r = acc_ref[...]
            if add_ref is not None:
                r = r + add_scale * add_ref[...]
            outs[0][...] = r.astype(out_dtype)
            if second_dtype is not None:
                outs[1][...] = r.astype(second_dtype)

    res = pl.pallas_call(
        body, name=name, grid=(n // tn, m // tm, nk), in_specs=in_specs, out_specs=out_specs, out_shape=out_shape,
        scratch_shapes=[pltpu.VMEM((tm, tn), F32)],
        compiler_params=_params(("parallel", "parallel", "arbitrary")),
    )(*operands)
    return res if second_dtype is not None else res[0]


def _accumulate(ref, val, first):
    @pl.when(first)
    def _():
        ref[...] = val

    @pl.when(jnp.logical_not(first))
    def _():
        ref[...] += val


def _rowcall(name, fn, rows, tm, row_in, const_in, row_out, acc_out=()):
    tm = min(tm, rows)
    assert rows % tm == 0
    n_in = len(row_in) + len(const_in)
    in_specs = [pl.BlockSpec((tm, w), lambda i, cb=cb: (i, cb)) for _, w, cb in row_in]
    in_specs += [pl.BlockSpec(c.shape, lambda i, nd=c.ndim: (0,) * nd) for c in const_in]
    out_specs = [pl.BlockSpec(blk, imap) for _, _, blk, imap in row_out]
    out_specs += [pl.BlockSpec(s, lambda i, nd=len(s): (0,) * nd) for s, _ in acc_out]
    out_shape = [jax.ShapeDtypeStruct(s, dt) for s, dt, _, _ in row_out]
    out_shape += [jax.ShapeDtypeStruct(s, dt) for s, dt in acc_out]

    def body(*refs):
        vals = fn(*[r[...] for r in refs[:n_in]])
        outs = refs[n_in:]
        first = pl.program_id(0) == 0
        for k, (_, dt, _, _) in enumerate(row_out):
            outs[k][...] = vals[k].astype(dt)
        for k in range(len(acc_out)):
            _accumulate(outs[len(row_out) + k], vals[len(row_out) + k], first)

    return pl.pallas_call(
        body, name=name, grid=(rows // tm,), in_specs=in_specs, out_specs=out_specs, out_shape=out_shape,
        compiler_params=_params(("arbitrary",) if acc_out else ("parallel",)),
    )(*[a for a, _, _ in row_in], *const_in)


def _ro(rows, width, dtype, tm):
    tm = min(tm, rows)
    return ((rows, width), dtype, (tm, width), lambda i: (i, 0))


def _rms(x, g):
    ms = jnp.mean(x * x, axis=-1, keepdims=True)
    return x * lax.rsqrt(ms + RMS_EPS) * g


def _ln(x, g, b):
    mu = jnp.mean(x, axis=-1, keepdims=True)
    xc = x - mu
    var = jnp.mean(xc * xc, axis=-1, keepdims=True)
    return xc * lax.rsqrt(var + LN_EPS) * g + b


def _gelu(x):
    return 0.5 * x * (1.0 + jnp.tanh(math.sqrt(2.0 / math.pi) * (x + 0.044715 * (x * x * x))))


def _silu(x):
    return x / (1.0 + jnp.exp(-x))


@jax.custom_vjp
def _rope(r, cc, sa, sb):
    return r * cc + pltpu.roll(r, 96, 1) * sa + pltpu.roll(r, 32, 1) * sb


def _rope_fwd(r, cc, sa, sb):
    return _rope(r, cc, sa, sb), (cc, sa, sb)


def _rope_bwd(res, d):
    cc, sa, sb = res
    return d * cc + pltpu.roll(d * sa, 32, 1) + pltpu.roll(d * sb, 96, 1), None, None, None


_rope.defvjp(_rope_fwd, _rope_bwd)


def _rope_tables_fn(pos, invf):
    ang = pos.astype(F32) * invf
    lane = lax.broadcasted_iota(jnp.int32, ang.shape, 1)
    cos, sin = jnp.cos(ang), jnp.sin(ang)
    zero = jnp.zeros_like(ang)
    cc = jnp.where(lane < 64, cos, zero)
    sa = jnp.where(lane < 32, -sin, zero)
    sb = jnp.where((lane >= 32) & (lane < 64), sin, zero)
    return cc, sa, sb


def _prep_fn(cq, ckv, kpe, cc, sa, sb, gq, wuq, gkv, wukv):
    q = _mm_nt(_rms(cq, gq), wuq)
    kv = _mm_nt(_rms(ckv, gkv), wukv)
    krot = _rope(kpe, cc, sa, sb)
    qs, ks = [], []
    for h in range(MLA_HEADS):
        o = h * MLA_HEAD_PAD
        qs += [q[:, o:o + 128], _rope(q[:, o + 128:o + 256], cc, sa, sb)]
        ks += [kv[:, h * 128:(h + 1) * 128], krot]
    return jnp.concatenate(qs, axis=1), jnp.concatenate(ks, axis=1), kv[:, MLA_HEADS * 128:]


def _post_fn(ga, sgu, sgv, gb, gc, mq, gm, oa, oc, mkv, lng, lnb, sgw, biasf):
    tm = ga.shape[0]
    u = _gelu(sgu)
    vn = _ln(_gelu(sgv), lng, lnb)
    t_chunk = lax.broadcasted_iota(jnp.int32, (SG_CHUNK, SG_CHUNK), 0) // CHUNK
    s_chunk = lax.broadcasted_iota(jnp.int32, (SG_CHUNK, SG_CHUNK), 1) // CHUNK
    sg_mask = (s_chunk <= t_chunk).astype(F32)
    cols = []
    for g in range(SG_GROUPS):
        wg = sgw[g] * sg_mask
        vg = vn[:, g * SG_CH:(g + 1) * SG_CH]
        cols.append(jnp.concatenate(
            [_mm(wg, vg[n * SG_CHUNK:(n + 1) * SG_CHUNK, :]) for n in range(tm // SG_CHUNK)], axis=0))
    mixed = jnp.concatenate(cols, axis=1) + jnp.concatenate([biasf] * (tm // SG_CHUNK), axis=0)
    ob = u * mixed
    mk, mv = mkv[:, :MEM_WIDTH], mkv[:, MEM_WIDTH:]
    head = lax.broadcasted_iota(jnp.int32, (1, MEM_WIDTH), 1) // MEM_DIM
    om = jnp.zeros((tm, MEM_WIDTH), F32)
    for hd in range(MEM_HEADS):
        hm = (head == hd).astype(F32)
        s = _mm_nt(mq, mk * hm) * MEM_SCALE
        e = jnp.exp(s - lax.stop_gradient(jnp.max(s, axis=-1, keepdims=True)))
        p = e / jnp.sum(e, axis=-1, keepdims=True)
        om = om + _mm(p, mv * hm)
    return jnp.concatenate([oa * _silu(ga), ob * _silu(gb), oc * _silu(gc), om * _silu(gm)], axis=1)


def _post_bwd_fn(dyg, ga, sgu, sgv, gb, gc, mq, gm, oa, oc, mkv, lng, lnb, sgw, biasf):
    _, vjp = jax.vjp(_post_fn, ga, sgu, sgv, gb, gc, mq, gm, oa, oc, mkv, lng, lnb, sgw, biasf)
    dga, dsgu, dsgv, dgb, dgc, dmq, dgm, doa, doc, dmkv, dlng, dlnb, dsgw, dbias = vjp(dyg)
    drows = jnp.stack([_col2row(jnp.sum((doa * oa)[:, h * 128:(h + 1) * 128], axis=1, keepdims=True))
                       for h in range(MLA_HEADS)], axis=0)
    return dga, dsgu, dsgv, dgb, dgc, dmq, dgm, doa, doc, drows, dmkv, dlng, dlnb, dsgw, dbias


def _prep_bwd_fn(dq, dk, dv, cq, ckv, kpe, cc, sa, sb, gq, wuq, gkv, wukv):
    wuq, wukv = wuq.astype(F32), wukv.astype(F32)
    _, vjp = jax.vjp(lambda a, b, c, d, e, f, g: _prep_fn(a, b, c, cc, sa, sb, d, e, f, g), cq, ckv, kpe, gq, wuq, gkv, wukv)
    return vjp((dq, dk, dv))


def _ln_bwd_fn(dxn, z, g, b):
    _, vjp = jax.vjp(_ln, z, g, b)
    dz, dg, db = vjp(dxn)
    return dz, dz, dg, db


def _outproj_fn(yg, x, w, g, b):
    z = ALPHA * x + _dot(yg, w, NN)
    xn = _ln(z, g, b)
    return z, xn, xn


def _loss_fn(xl, tgt):
    err = xl - tgt
    part = 0.5 * jnp.sum(jnp.mean(err * err, axis=-1, keepdims=True), axis=0, keepdims=True)
    return err * (1.0 / D_MODEL), jnp.broadcast_to(part, (1, LANES))


def _mla_fwd(q, k, v, tq):
    t = q.shape[0]
    tq = min(tq, t)
    nq = t // tq

    def body(q_ref, k_ref, v_ref, o_ref, lse_ref):
        i = pl.program_id(1)
        qb = q_ref[...]
        r_chunk = lax.broadcasted_iota(jnp.int32, (tq, tq), 0) // CHUNK
        c_chunk = lax.broadcasted_iota(jnp.int32, (tq, tq), 1) // CHUNK
        diag_ok = c_chunk <= r_chunk

        def block(j, carry, masked):
            m, l, acc = carry
            off = pl.multiple_of(j * tq, tq)
            s = _dot(qb, k_ref[pl.ds(off, tq), :], NT) * MLA_SCALE
            if masked:
                s = jnp.where(diag_ok, s, NEG)
            m2 = jnp.maximum(m, jnp.max(s, axis=1, keepdims=True))
            p = jnp.exp(s - m2)
            a = jnp.exp(m - m2)
            return m2, a * l + jnp.sum(p, axis=1, keepdims=True), a * acc + _dot(p, v_ref[pl.ds(off, tq), :], NN)

        init = (jnp.full((tq, 1), NEG, F32), jnp.zeros((tq, 1), F32), jnp.zeros((tq, MLA_V), F32))
        carry = lax.fori_loop(0, i, lambda j, c: block(j, c, False), init)
        m, l, acc = block(i, carry, True)
        o_ref[...] = acc / l
        lse_ref[...] = _col2row(m + jnp.log(l))

    return pl.pallas_call(
        body, name="mla_fwd", grid=(MLA_HEADS, nq),
        in_specs=[pl.BlockSpec((tq, MLA_HEAD_PAD), lambda h, i: (i, h)),
                  pl.BlockSpec((t, MLA_HEAD_PAD), lambda h, i: (0, h)),
                  pl.BlockSpec((t, MLA_V), lambda h, i: (0, h))],
        out_specs=[pl.BlockSpec((tq, MLA_V), lambda h, i: (i, h)),
                   pl.BlockSpec((None, 8, tq), lambda h, i: (h, 0, i))],
        out_shape=[jax.ShapeDtypeStruct((t, MLA_WIDTH), F32), jax.ShapeDtypeStruct((MLA_HEADS, 8, t), F32)],
        compiler_params=_params(("parallel", "parallel")),
    )(q, k, v)


def _mla_bwd(q, k, v, do, lse, drow, tq):
    t = q.shape[0]
    tq = min(tq, t)
    nq = t // tq

    def body(k_ref, v_ref, q_ref, do_ref, lse_ref, d_ref, dk_ref, dv_ref, dq_ref):
        j = pl.program_id(1)

        @pl.when(j == 0)
        def _():
            dq_ref[...] = jnp.zeros_like(dq_ref)

        kj, vj = k_ref[...], v_ref[...]
        k_chunk = lax.broadcasted_iota(jnp.int32, (tq, tq), 0) // CHUNK
        q_chunk = lax.broadcasted_iota(jnp.int32, (tq, tq), 1) // CHUNK
        diag_ok = k_chunk <= q_chunk

        def block(i, carry, masked):
            dk, dv = carry
            off = pl.multiple_of(i * tq, tq)
            qi, doi = q_ref[pl.ds(off, tq), :], do_ref[pl.ds(off, tq), :]
            st = _dot(kj, qi, NT) * MLA_SCALE
            pt = jnp.exp(st - lse_ref[0:1, pl.ds(off, tq)])
            if masked:
                pt = jnp.where(diag_ok, pt, 0.0)
            dv = dv + _dot(pt, doi, NN)
            dst = (pt * (_dot(vj, doi, NT) - d_ref[0:1, pl.ds(off, tq)]) * MLA_SCALE).astype(MXU)
            dk = dk + _dot(dst, qi, NN)
            dq_ref[pl.ds(off, tq), :] += _dot(dst, kj, TN)
            return dk, dv

        carry = block(j, (jnp.zeros((tq, MLA_HEAD_PAD), F32), jnp.zeros((tq, MLA_V), F32)), True)
        dk, dv = lax.fori_loop(j + 1, nq, lambda i, c: block(i, c, False), carry)
        dk_ref[...] = dk
        dv_ref[...] = dv

    dk, dv, dq = pl.pallas_call(
        body, name="mla_bwd", grid=(MLA_HEADS, nq),
        in_specs=[pl.BlockSpec((tq, MLA_HEAD_PAD), lambda h, j: (j, h)),
                  pl.BlockSpec((tq, MLA_V), lambda h, j: (j, h)),
                  pl.BlockSpec((t, MLA_HEAD_PAD), lambda h, j: (0, h)),
                  pl.BlockSpec((t, MLA_V), lambda h, j: (0, h)),
                  pl.BlockSpec((None, 8, t), lambda h, j: (h, 0, 0)),
                  pl.BlockSpec((None, 8, t), lambda h, j: (h, 0, 0))],
        out_specs=[pl.BlockSpec((tq, MLA_HEAD_PAD), lambda h, j: (j, h)),
                   pl.BlockSpec((tq, MLA_V), lambda h, j: (j, h)),
                   pl.BlockSpec((t, MLA_HEAD_PAD), lambda h, j: (0, h))],
        out_shape=[jax.ShapeDtypeStruct((t, MLA_HEADS * MLA_HEAD_PAD), F32),
                   jax.ShapeDtypeStruct((t, MLA_WIDTH), F32),
                   jax.ShapeDtypeStruct((t, MLA_HEADS * MLA_HEAD_PAD), F32)],
        compiler_params=_params(("parallel", "arbitrary")),
    )(k, v, q, do, lse, drow)
    return dq, dk, dv


def _sb_logs(z):
    soft = jnp.log(1.0 + jnp.exp(-jnp.abs(z)))
    return jnp.minimum(z, 0.0) - soft, jnp.minimum(-z, 0.0) - soft


def _tri(n, keep):
    r = lax.broadcasted_iota(jnp.int32, (n, n), 0)
    c = lax.broadcasted_iota(jnp.int32, (n, n), 1)
    return keep(r, c)


def _sb_fwd(hb, tq):
    t = hb.shape[0]
    tq = min(tq, t)
    nq = t // tq
    cq, ck, cv = C_SBQ // SB_DIM, C_SBK // SB_DIM, C_SBV // SB_DIM

    def body(q_ref, k_ref, v_ref, o_ref, tot_ref):
        i = pl.program_id(1)
        qb = q_ref[...]
        strict = _tri(tq, lambda r, c: c < r)
        later = _tri(min(SCAN, tq), lambda r, c: r > c).astype(MXU)

        def block(j, carry, masked):
            run, acc = carry
            off = pl.multiple_of(j * tq, tq)
            z = _dot(qb, k_ref[pl.ds(off, tq), :], NT) * SB_SCALE
            lb, l1 = _sb_logs(z)
            if masked:
                l1 = jnp.where(strict, l1, 0.0)
            after, rowsum = _scan_keys(l1, later, True)
            a = jnp.exp(lb + after + run)
            if masked:
                a = jnp.where(strict, a, 0.0)
            return run + rowsum, acc + _dot(a, v_ref[pl.ds(off, tq), :], NN)

        carry = block(i, (jnp.zeros((tq, 1), F32), jnp.zeros((tq, SB_DIM), F32)), True)
        run, acc = lax.fori_loop(0, i, lambda jj, c: block(i - 1 - jj, c, False), carry)
        o_ref[...] = acc
        tot_ref[...] = _col2row(run)

    return pl.pallas_call(
        body, name="sb_fwd", grid=(SB_HEADS, nq),
        in_specs=[pl.BlockSpec((tq, SB_DIM), lambda h, i: (i, cq + h)),
                  pl.BlockSpec((t, SB_DIM), lambda h, i: (0, ck + h)),
                  pl.BlockSpec((t, SB_DIM), lambda h, i: (0, cv + h))],
        out_specs=[pl.BlockSpec((tq, SB_DIM), lambda h, i: (i, h)),
                   pl.BlockSpec((None, 8, tq), lambda h, i: (h, 0, i))],
        out_shape=[jax.ShapeDtypeStruct((t, SB_WIDTH), F32), jax.ShapeDtypeStruct((SB_HEADS, 8, t), F32)],
        compiler_params=_params(("parallel", "parallel")),
    )(hb, hb, hb)


def _sb_bwd(hb, do, tot, tq):
    t = hb.shape[0]
    tq = min(tq, t)
    nq = t // tq
    cq, ck, cv = C_SBQ // SB_DIM, C_SBK // SB_DIM, C_SBV // SB_DIM

    def body(q_ref, do_ref, tot_ref, k_ref, v_ref, dq_ref, dk_ref, dv_ref):
        i = pl.program_id(1)

        @pl.when(i == 0)
        def _():
            dk_ref[...] = jnp.zeros_like(dk_ref)
            dv_ref[...] = jnp.zeros_like(dv_ref)

        qb, dob = q_ref[...], do_ref[...]
        total = _row2col(tot_ref[0:1, :])
        strict = _tri(tq, lambda r, c: c < r)
        upto = _tri(min(SCAN, tq), lambda r, c: r <= c).astype(MXU)
        before = _tri(min(SCAN, tq), lambda r, c: r < c).astype(MXU)

        def block(j, carry, masked):
            pre, gpre, dq = carry
            off = pl.multiple_of(j * tq, tq)
            kj, vj = k_ref[pl.ds(off, tq), :], v_ref[pl.ds(off, tq), :]
            z = _dot(qb, kj, NT) * SB_SCALE
            lb, l1 = _sb_logs(z)
            if masked:
                l1 = jnp.where(strict, l1, 0.0)
            upto_sum, rowsum = _scan_keys(l1, upto, False)
            a = jnp.exp(lb + (total - (upto_sum + pre)))
            if masked:
                a = jnp.where(strict, a, 0.0)
            g = a * _dot(dob, vj, NT)
            before_sum, grow = _scan_keys(g, before, False)
            beta = jnp.exp(lb)
            dz = (g * (1.0 - beta) - (before_sum + gpre) * beta) * SB_SCALE
            if masked:
                dz = jnp.where(strict, dz, 0.0)
            dzb = dz.astype(MXU)
            dk_ref[pl.ds(off, tq), :] += _dot(dzb, qb, TN)
            dv_ref[pl.ds(off, tq), :] += _dot(a, dob, TN)
            return pre + rowsum, gpre + grow, dq + _dot(dzb, kj, NN)

        init = (jnp.zeros((tq, 1), F32), jnp.zeros((tq, 1), F32), jnp.zeros((tq, SB_DIM), F32))
        carry = lax.fori_loop(0, i, lambda j, c: block(j, c, False), init)
        dq_ref[...] = block(i, carry, True)[2].astype(dq_ref.dtype)

    return pl.pallas_call(
        body, name="sb_bwd", grid=(SB_HEADS, nq),
        in_specs=[pl.BlockSpec((tq, SB_DIM), lambda h, i: (i, cq + h)),
                  pl.BlockSpec((tq, SB_DIM), lambda h, i: (i, h)),
                  pl.BlockSpec((None, 8, tq), lambda h, i: (h, 0, i)),
                  pl.BlockSpec((t, SB_DIM), lambda h, i: (0, ck + h)),
                  pl.BlockSpec((t, SB_DIM), lambda h, i: (0, cv + h))],
        out_specs=[pl.BlockSpec((tq, SB_DIM), lambda h, i: (i, h)),
                   pl.BlockSpec((t, SB_DIM), lambda h, i: (0, h)),
                   pl.BlockSpec((t, SB_DIM), lambda h, i: (0, h))],
        out_shape=[jax.ShapeDtypeStruct((t, SB_WIDTH), MXU), jax.ShapeDtypeStruct((t, SB_WIDTH), F32),
                   jax.ShapeDtypeStruct((t, SB_WIDTH), F32)],
        compiler_params=_params(("parallel", "arbitrary")),
    )(hb, do, tot, hb, hb)


TM = 256
TQ_MLA = 1024
TQ_SB = 512


def _layer_fwd(xf, xb, w, consts):
    t = xf.shape[0]
    cc, sa, sb, mem = consts
    h, hb = _matmul("in_proj", xb, w["w_in"], "nt", 512, 1152, 2048, second_dtype=MXU)
    gq, gkv = w["q_norm_g"][None, :], w["kv_norm_g"][None, :]
    q, k, v = _rowcall(
        "mla_prep", _prep_fn, t, TM,
        [(h, 512, C_CQ // 512), (h, 256, C_CKV // 256), (h, 128, C_KPE // 128), (cc, 128, 0), (sa, 128, 0), (sb, 128, 0)],
        [gq, w["w_uq"], gkv, w["w_ukv"]],
        [_ro(t, MLA_HEADS * MLA_HEAD_PAD, MXU, TM), _ro(t, MLA_HEADS * MLA_HEAD_PAD, MXU, TM), _ro(t, MLA_WIDTH, MXU, TM)])
    oa, lse = _mla_fwd(q, k, v, TQ_MLA)
    oc, tot = _sb_fwd(hb, TQ_SB)
    mkv = _matmul("mem_kv", mem, w["w_mem_kv"], "nn", 256, 512, 2048)
    lng, lnb = w["sg_ln_g"][None, :], w["sg_ln_b"][None, :]
    biasf = jnp.repeat(w["sg_b"].T, SG_CH, axis=1)
    post_rows = [(h, 768, C_GA // 768), (h, 512, C_SGU // 512), (h, 512, C_SGV // 512), (h, 512, C_GB // 512),
                 (h, 512, C_GC // 512), (h, 256, C_MQ // 256), (h, 256, C_GM // 256), (oa, 768, 0), (oc, 512, 0)]
    post_consts = [mkv, lng, lnb, w["sg_w"], biasf]
    (yg,) = _rowcall("mixer_post", lambda *a: (_post_fn(*a),), t, TM, post_rows, post_consts, [_ro(t, D_MODEL, MXU, TM)])
    z, xn, xnb = _rowcall(
        "out_proj_ln", _outproj_fn, t, TM, [(yg, D_MODEL, 0), (xf, D_MODEL, 0)],
        [w["w_out"], w["ln_g"][None, :], w["ln_b"][None, :]],
        [_ro(t, D_MODEL, F32, TM), _ro(t, D_MODEL, F32, TM), _ro(t, D_MODEL, MXU, TM)])
    saved = dict(xb=xb, h=h, hb=hb, q=q, k=k, v=v, oa=oa, lse=lse, oc=oc, tot=tot, mkv=mkv, yg=yg, z=z,
                 post_rows=post_rows, post_consts=post_consts, gq=gq, gkv=gkv)
    return xn, xnb, saved


def _layer_bwd(dxn, w, s, consts):
    t = dxn.shape[0]
    cc, sa, sb, mem = consts
    h = s["h"]
    dz, dzb, dlng2, dlnb2 = _rowcall(
        "ln_bwd", _ln_bwd_fn, t, TM, [(dxn, D_MODEL, 0), (s["z"], D_MODEL, 0)],
        [w["ln_g"][None, :], w["ln_b"][None, :]],
        [_ro(t, D_MODEL, F32, TM), _ro(t, D_MODEL, MXU, TM)], [((1, D_MODEL), F32), ((1, D_MODEL), F32)])
    dyg = _matmul("out_proj_dx", dzb, w["w_out"], "nt", 512, 1024, 2048)
    dw_out = _matmul("out_proj_dw", s["yg"], dzb, "tn", 1024, 1024, 512)
    tm = min(TM, t)
    outs = _rowcall(
        "mixer_post_bwd", _post_bwd_fn, t, TM, [(dyg, D_MODEL, 0)] + s["post_rows"], s["post_consts"],
        [_ro(t, 768, MXU, TM), _ro(t, 512, MXU, TM), _ro(t, 512, MXU, TM), _ro(t, 512, MXU, TM), _ro(t, 512, MXU, TM),
         _ro(t, 256, MXU, TM), _ro(t, 256, MXU, TM), _ro(t, 768, MXU, TM), _ro(t, 512, MXU, TM),
         ((MLA_HEADS, 8, t), F32, (MLA_HEADS, 8, tm), lambda i: (0, 0, i))],
        [((MEM_TOKENS, 2 * MEM_WIDTH), F32), ((1, SG_WIDTH), F32), ((1, SG_WIDTH), F32),
         ((SG_GROUPS, SG_CHUNK, SG_CHUNK), F32), ((SG_CHUNK, SG_WIDTH), F32)])
    dga, dsgu, dsgv, dgb, dgc, dmq, dgm, doa, doc, drow, dmkv, dsg_ln_g, dsg_ln_b, dsgw, dbias = outs
    dw_mem_kv = _matmul("mem_kv_dw", mem, dmkv, "tn", 1024, 512, 256)
    dsbq, dsbk, dsbv = _sb_bwd(s["hb"], doc, s["tot"], TQ_SB)
    dq, dk, dv = _mla_bwd(s["q"], s["k"], s["v"], doa, s["lse"], drow, TQ_MLA)
    wq, wkv = MLA_HEADS * MLA_HEAD_PAD, MLA_WIDTH
    dcq, dckv, dkpe, dgq, dwuq, dgkv, dwukv = _rowcall(
        "mla_prep_bwd", _prep_bwd_fn, t, TM,
        [(dq, wq, 0), (dk, wq, 0), (dv, wkv, 0), (h, 512, C_CQ // 512), (h, 256, C_CKV // 256), (h, 128, C_KPE // 128),
         (cc, 128, 0), (sa, 128, 0), (sb, 128, 0)],
        [s["gq"], w["w_uq"], s["gkv"], w["w_ukv"]],
        [_ro(t, 512, MXU, TM), _ro(t, 256, MXU, TM), _ro(t, 128, MXU, TM)],
        [((1, MLA_Q_RANK), F32), ((wq, MLA_Q_RANK), F32), ((1, MLA_KV_RANK), F32), ((2 * wkv, MLA_KV_RANK), F32)])
    dh = jnp.concatenate([dcq, dckv, dga, dsgu, dsgv, dgb, dsbq, dsbk.astype(MXU), dsbv.astype(MXU), dgc, dmq, dgm, dkpe], axis=1)
    dx = _matmul("in_proj_dx", dh, w["w_in"], "nn", 512, 1024, 1152, add=dz, add_scale=ALPHA)
    dw_in = _matmul("in_proj_dw", dh, s["xb"], "tn", 1152, 1024, 512)
    grads = dict(w_in=dw_in, q_norm_g=dgq[0], w_uq=dwuq, kv_norm_g=dgkv[0], w_ukv=dwukv, sg_ln_g=dsg_ln_g[0],
                 sg_ln_b=dsg_ln_b[0], sg_w=dsgw, sg_bias=dbias, w_mem_kv=dw_mem_kv, w_out=dw_out, ln_g=dlng2[0], ln_b=dlnb2[0])
    return dx, grads


def _local_step(x, mem, positions, target, full, depth):
    t = x.shape[0]
    inv_freq = ROPE_THETA ** (-jnp.arange(0, MLA_ROPE, 2, dtype=F32) / MLA_ROPE)
    invf = jnp.tile(inv_freq, 4)[None, :]
    cc, sa, sb = _rowcall("rope_tables", _rope_tables_fn, t, TM, [(positions[:, None], 1, 0)], [invf],
                          [_ro(t, 128, F32, TM)] * 3)
    consts = (cc, sa, sb, mem)
    xf, xb, saved = x, x.astype(MXU), []
    for l in range(depth):
        xf, xb, s = _layer_fwd(xf, xb, full[l], consts)
        saved.append(s)
    dx, loss_part = _rowcall("loss", _loss_fn, t, TM, [(xf, D_MODEL, 0), (target, D_MODEL, 0)], [],
                             [_ro(t, D_MODEL, F32, TM)], [((1, LANES), F32)])
    grads = [None] * depth
    for l in reversed(range(depth)):
        dx, grads[l] = _layer_bwd(dx, full[l], saved[l], consts)
    db = jnp.concatenate([g["sg_bias"] for g in grads], axis=0)

    def lane_sums(d):
        return (jnp.concatenate([jnp.broadcast_to(jnp.sum(d[:, g * SG_CH:(g + 1) * SG_CH], axis=1, keepdims=True),
                                                  (d.shape[0], SG_CH)) for g in range(SG_GROUPS)], axis=1),)

    (dbs,) = _rowcall("sg_bias_sum", lane_sums, db.shape[0], SG_CHUNK, [(db, SG_WIDTH, 0)], [], [_ro(db.shape[0], SG_WIDTH, F32, SG_CHUNK)])
    dsgb = dbs[:, ::SG_CH].reshape(depth, SG_CHUNK, SG_GROUPS).transpose(0, 2, 1)
    return loss_part[0, 0], dx, grads, dsgb


SHARD_IN = D_IN // N_DEV
SHARD_IN_PAD = D_IN_PAD // N_DEV
IN_RANGES = ((0, 768), (832, D_IN), (768, 832))


def _in_rows_from_shards(w):
    pieces = []
    for a, b in IN_RANGES:
        while a < b:
            e = min(b, (a // SHARD_IN + 1) * SHARD_IN)
            row = (a // SHARD_IN) * SHARD_IN_PAD + a % SHARD_IN
            pieces.append(w[row:row + e - a])
            a = e
    pieces.append(jnp.zeros((D_IN_PAD - D_IN, w.shape[1]), w.dtype))
    return jnp.concatenate(pieces, axis=0)


def _in_rows_to_shards(g):
    pieces = []
    for d in range(N_DEV):
        a, b = d * SHARD_IN, (d + 1) * SHARD_IN
        while a < b:
            e = min(b, 768 if a < 768 else (832 if a < 832 else D_IN))
            row = a if a < 768 else (C_KPE + a - 768 if a < 832 else a - 64)
            pieces.append(g[row:row + e - a])
            a = e
        pieces.append(jnp.zeros((SHARD_IN_PAD - SHARD_IN, g.shape[1]), g.dtype))
    return jnp.concatenate(pieces, axis=0)


def _uq_rows_pad(w):
    w = w.reshape(MLA_HEADS, MLA_NOPE + MLA_ROPE, w.shape[-1])
    return jnp.pad(w, ((0, 0), (0, MLA_HEAD_PAD - MLA_NOPE - MLA_ROPE), (0, 0))).reshape(MLA_HEADS * MLA_HEAD_PAD, -1)


def _uq_rows_unpad(g):
    return g.reshape(MLA_HEADS, MLA_HEAD_PAD, -1)[:, :MLA_NOPE + MLA_ROPE].reshape(MLA_HEADS * (MLA_NOPE + MLA_ROPE), -1)


def _ukv_rows_split(w):
    return jnp.swapaxes(w.reshape(MLA_HEADS, 2, 128, -1), 0, 1).reshape(MLA_HEADS * 256, -1)


def _ukv_rows_unsplit(g):
    return jnp.swapaxes(g.reshape(2, MLA_HEADS, 128, -1), 0, 1).reshape(MLA_HEADS * 256, -1)


def _exchange(name, srcs, scatter):
    n_arr = len(srcs)
    pieces = [(s.shape[0], s.shape[1] // N_DEV, s.shape[2]) if sc else s.shape for s, sc in zip(srcs, scatter)]

    def body(*refs):
        src_refs, out_refs = refs[:n_arr], refs[n_arr:2 * n_arr]
        send_sems, recv_sems, local_sems = refs[2 * n_arr:]
        x, y, c = lax.axis_index("x"), lax.axis_index("y"), lax.axis_index("c")
        me = 4 * x + 2 * y + c

        def block_for(a, p):
            if not scatter[a]:
                return src_refs[a]
            n = pieces[a][1]
            return src_refs[a].at[:, pl.ds(pl.multiple_of(p * n, 8), n), :]

        locals_ = [pltpu.make_async_copy(block_for(a, me), out_refs[a].at[me], local_sems.at[a]) for a in range(n_arr)]
        for cp in locals_:
            cp.start()
        sends, peers = [], []
        for k in range(1, N_DEV):
            px = 1 - x if (k >> 2) & 1 else x
            py = 1 - y if (k >> 1) & 1 else y
            pc = 1 - c if k & 1 else c
            peer = 4 * px + 2 * py + pc
            for a in range(n_arr):
                cp = pltpu.make_async_remote_copy(
                    src_ref=block_for(a, peer), dst_ref=out_refs[a].at[me], send_sem=send_sems.at[k - 1, a],
                    recv_sem=recv_sems.at[k - 1, a], device_id=(px, py, pc), device_id_type=pl.DeviceIdType.MESH)
                cp.start()
                sends.append(cp)
            peers.append((peer, (px, py, pc)))
        for k, (peer, pid) in enumerate(peers):
            for a in range(n_arr):
                pltpu.make_async_remote_copy(
                    src_ref=block_for(a, peer), dst_ref=out_refs[a].at[peer], send_sem=send_sems.at[k, a],
                    recv_sem=recv_sems.at[k, a], device_id=pid, device_id_type=pl.DeviceIdType.MESH).wait_recv()
        for cp in sends:
            cp.wait_send()
        for cp in locals_:
            cp.wait()

    return pl.pallas_call(
        body, name=name, out_shape=[jax.ShapeDtypeStruct((N_DEV,) + tuple(p), s.dtype) for p, s in zip(pieces, srcs)],
        in_specs=[pl.BlockSpec(memory_space=pl.ANY)] * n_arr, out_specs=[pl.BlockSpec(memory_space=pl.ANY)] * n_arr,
        scratch_shapes=[pltpu.SemaphoreType.DMA((N_DEV - 1, n_arr)), pltpu.SemaphoreType.DMA((N_DEV - 1, n_arr)),
                        pltpu.SemaphoreType.DMA((n_arr,))],
    )(*srcs)


BLOCK_BYTES = 6 << 20


def _row_block(rows, row_bytes):
    if rows * row_bytes <= BLOCK_BYTES or rows % 8:
        return rows
    return max(tb for tb in range(8, rows + 1, 8) if rows % tb == 0 and (tb == 8 or tb * row_bytes <= BLOCK_BYTES))


def _sum_contribs(recv):
    n, rows, width = recv.shape
    tb = _row_block(rows, width * 4 * (n + 1))

    def body(r_ref, g_ref):
        g = r_ref[0]
        for k in range(1, n):
            g = g + r_ref[k]
        g_ref[...] = g

    return pl.pallas_call(
        body, name="sum_grads", grid=(rows // tb,), in_specs=[pl.BlockSpec((n, tb, width), lambda i: (0, i, 0))],
        out_specs=pl.BlockSpec((tb, width), lambda i: (i, 0)), out_shape=jax.ShapeDtypeStruct((rows, width), F32),
        compiler_params=_params(("parallel",)),
    )(recv)


def _adamw(contribs, w, m, v):
    n, rows, width = contribs.shape
    tb = _row_block(rows, width * 4 * (n + 7))
    c1 = 1.0 - ADAM_B1 ** ADAM_STEP
    c2 = 1.0 - ADAM_B2 ** ADAM_STEP

    def body(r_ref, w_ref, m_ref, v_ref, g_ref, d_ref, nm_ref, nv_ref):
        g = r_ref[0]
        for k in range(1, n):
            g = g + r_ref[k]
        nm = ADAM_B1 * m_ref[...] + (1.0 - ADAM_B1) * g
        nv = ADAM_B2 * v_ref[...] + (1.0 - ADAM_B2) * (g * g)
        g_ref[...] = g
        nm_ref[...] = nm
        nv_ref[...] = nv
        d_ref[...] = -ADAM_LR * ((nm / c1) / (jnp.sqrt(nv / c2) + ADAM_EPS) + ADAM_WD * w_ref[...])

    blk = pl.BlockSpec((tb, width), lambda i: (i, 0))
    return pl.pallas_call(
        body, name="adamw", grid=(rows // tb,),
        in_specs=[pl.BlockSpec((n, tb, width), lambda i: (0, i, 0)), blk, blk, blk],
        out_specs=[blk] * 4, out_shape=[jax.ShapeDtypeStruct((rows, width), F32)] * 4,
        compiler_params=_params(("parallel",)),
    )(contribs, w, m, v)


def kernel(x, mem, positions, w_in, q_norm_g, w_uq, kv_norm_g, w_ukv, sg_ln_g, sg_ln_b, sg_w, sg_b, w_mem_k, w_mem_v, w_out, ln_g, ln_b, loss_target, m_w_in, m_q_norm_g, m_w_uq, m_kv_norm_g, m_w_ukv, m_sg_ln_g, m_sg_ln_b, m_sg_w, m_sg_b, m_w_mem_k, m_w_mem_v, m_w_out, m_ln_g, m_ln_b, v_w_in, v_q_norm_g, v_w_uq, v_kv_norm_g, v_w_ukv, v_sg_ln_g, v_sg_ln_b, v_sg_w, v_sg_b, v_w_mem_k, v_w_mem_v, v_w_out, v_ln_g, v_ln_b):
    wl = dict(w_in=w_in, q_norm_g=q_norm_g, w_uq=w_uq, kv_norm_g=kv_norm_g, w_ukv=w_ukv, sg_ln_g=sg_ln_g, sg_ln_b=sg_ln_b,
              sg_w=sg_w, sg_b=sg_b, w_mem_k=w_mem_k, w_mem_v=w_mem_v, w_out=w_out, ln_g=ln_g, ln_b=ln_b)
    ml = dict(w_in=m_w_in, q_norm_g=m_q_norm_g, w_uq=m_w_uq, kv_norm_g=m_kv_norm_g, w_ukv=m_w_ukv, sg_ln_g=m_sg_ln_g,
              sg_ln_b=m_sg_ln_b, sg_w=m_sg_w, sg_b=m_sg_b, w_mem_k=m_w_mem_k, w_mem_v=m_w_mem_v, w_out=m_w_out, ln_g=m_ln_g, ln_b=m_ln_b)
    vl = dict(w_in=v_w_in, q_norm_g=v_q_norm_g, w_uq=v_w_uq, kv_norm_g=v_kv_norm_g, w_ukv=v_w_ukv, sg_ln_g=v_sg_ln_g,
              sg_ln_b=v_sg_ln_b, sg_w=v_sg_w, sg_b=v_sg_b, w_mem_k=v_w_mem_k, w_mem_v=v_w_mem_v, w_out=v_w_out, ln_g=v_ln_g, ln_b=v_ln_b)
    depth = w_in.shape[0]

    w_in_t = jnp.pad(jnp.swapaxes(w_in, 1, 2), ((0, 0), (0, SHARD_IN_PAD - SHARD_IN), (0, 0)))
    shards = [w_in_t, jnp.swapaxes(w_uq, 1, 2), jnp.swapaxes(w_ukv, 1, 2), jnp.concatenate([w_mem_k, w_mem_v], axis=-1), w_out]
    g_in, g_uq, g_ukv, g_mem, g_out = _exchange("gather_weights", [s.astype(MXU) for s in shards], [False] * 5)
    full = [dict(w_in=_in_rows_from_shards(g_in[:, l].reshape(D_IN_PAD, D_MODEL)),
                 w_uq=_uq_rows_pad(g_uq[:, l].reshape(-1, MLA_Q_RANK)),
                 w_ukv=_ukv_rows_split(g_ukv[:, l].reshape(-1, MLA_KV_RANK)),
                 w_mem_kv=g_mem[:, l].reshape(D_MODEL, 2 * MEM_WIDTH), w_out=g_out[:, l].reshape(D_MODEL, D_MODEL),
                 q_norm_g=q_norm_g[l], kv_norm_g=kv_norm_g[l], sg_ln_g=sg_ln_g[l], sg_ln_b=sg_ln_b[l], sg_w=sg_w[l],
                 sg_b=sg_b[l], ln_g=ln_g[l], ln_b=ln_b[l]) for l in range(depth)]

    loss_part, grad_x, grads, dsgb = _local_step(x[0], mem[0], positions[0], loss_target[0], full, depth)
    loss = lax.psum(loss_part, ("x", "y", "c"))

    def stacked(key, f=lambda g: g):
        return jnp.stack([f(g[key]) for g in grads])

    rep_g = dict(sg_b=dsgb, **{n: stacked(n) for n in REPLICATED if n != "sg_b"})
    rep_sizes = [wl[n].size for n in REPLICATED]
    rep_rows = -(-sum(rep_sizes) // (8 * LANES)) * 8

    def packed(d):
        flat = jnp.concatenate([d[n].reshape(-1) for n in REPLICATED])
        return jnp.pad(flat, (0, rep_rows * LANES - flat.shape[0])).reshape(rep_rows, LANES)

    r_in, r_uq, r_ukv, r_mem, r_out, r_rep = _exchange(
        "exchange_grads",
        [stacked("w_in", _in_rows_to_shards), stacked("w_uq", _uq_rows_unpad), stacked("w_ukv", _ukv_rows_unsplit),
         stacked("w_mem_kv"), stacked("w_out"), packed(rep_g)], [True] * 5 + [False])

    def rows2d(a):
        return a.reshape(-1, a.shape[-1])

    def update(name, contribs):
        outs = _adamw(contribs, rows2d(wl[name]), rows2d(ml[name]), rows2d(vl[name]))
        return [o.reshape(wl[name].shape) for o in outs]

    def from_transposed(recv, keep):
        s = _sum_contribs(recv.reshape(N_DEV, -1, recv.shape[-1])).reshape(recv.shape[1:])
        return rows2d(jnp.swapaxes(s[:, :keep], 1, 2))[None]

    res = dict(
        w_in=update("w_in", from_transposed(r_in, SHARD_IN)),
        w_uq=update("w_uq", from_transposed(r_uq, r_uq.shape[2])),
        w_ukv=update("w_ukv", from_transposed(r_ukv, r_ukv.shape[2])),
        w_mem_k=update("w_mem_k", r_mem[..., :MEM_WIDTH].reshape(N_DEV, -1, MEM_WIDTH)),
        w_mem_v=update("w_mem_v", r_mem[..., MEM_WIDTH:].reshape(N_DEV, -1, MEM_WIDTH)),
        w_out=update("w_out", r_out.reshape(N_DEV, -1, D_MODEL)))
    rep_outs = _adamw(r_rep, packed(wl), packed(ml), packed(vl))
    off = 0
    for n, size in zip(REPLICATED, rep_sizes):
        res[n] = [o.reshape(-1)[off:off + size].reshape(wl[n].shape) for o in rep_outs]
        off += size
    return (loss, grad_x[None], *[res[n][0] for n in WEIGHTS], *[res[n][1] for n in WEIGHTS],
            *[res[n][2] for n in WEIGHTS], *[res[n][3] for n in WEIGHTS])
```

```python
import functools
import math

import jax
import jax.numpy as jnp
from jax import lax
from jax.experimental import pallas as pl
from jax.experimental.pallas import tpu as pltpu

F32 = jnp.float32
MXU = jnp.bfloat16

DEPTH_FOR_NORM = 4
D_MODEL = 2048
CHUNK = 64
MLA_HEADS, MLA_NOPE, MLA_ROPE, MLA_V = 6, 128, 64, 128
MLA_Q_RANK, MLA_KV_RANK = 512, 256
MLA_WIDTH = MLA_HEADS * MLA_V
MLA_HEAD_PAD = 256
ROPE_THETA = 10000.0
SG_GROUPS, SG_CH, SG_CHUNK = 4, 128, 128
SG_WIDTH = SG_GROUPS * SG_CH
SB_HEADS, SB_DIM = 4, 128
SB_WIDTH = SB_HEADS * SB_DIM
MEM_TOKENS, MEM_HEADS, MEM_DIM = 256, 4, 64
MEM_WIDTH = MEM_HEADS * MEM_DIM
D_IN = 5696
D_IN_PAD = 5760
ALPHA = (2.0 * DEPTH_FOR_NORM) ** 0.25
LN_EPS = 1e-5
RMS_EPS = 1e-6
MLA_SCALE = 1.0 / math.sqrt(MLA_NOPE + MLA_ROPE)
SB_SCALE = 1.0 / math.sqrt(SB_DIM)
MEM_SCALE = 1.0 / math.sqrt(MEM_DIM)
LOG2E = math.log2(math.e)
ADAM_LR, ADAM_B1, ADAM_B2, ADAM_EPS, ADAM_WD, ADAM_STEP = 0.001, 0.9, 0.999, 1e-08, 0.01, 10
N_DEV = 8
LANES = 128
NEG = -1e30
SCAN = 256
VMEM_LIMIT = 56 * 1024 * 1024

C_CQ, C_CKV, C_GA, C_SGU, C_SGV, C_GB = 0, 512, 768, 1536, 2048, 2560
C_SBQ, C_SBK, C_SBV, C_GC, C_MQ, C_GM, C_KPE = 3072, 3584, 4096, 4608, 5120, 5376, 5632

REPLICATED = ("q_norm_g", "kv_norm_g", "sg_ln_g", "sg_ln_b", "sg_w", "sg_b", "ln_g", "ln_b")
WEIGHTS = ("w_in", "q_norm_g", "w_uq", "kv_norm_g", "w_ukv", "sg_ln_g", "sg_ln_b", "sg_w", "sg_b",
           "w_mem_k", "w_mem_v", "w_out", "ln_g", "ln_b")


def _params(sem):
    return pltpu.CompilerParams(dimension_semantics=sem, vmem_limit_bytes=VMEM_LIMIT)


def _dot(a, b, dims):
    return lax.dot_general(a.astype(MXU), b.astype(MXU), (dims, ((), ())), preferred_element_type=F32)


NN, NT, TN = ((1,), (0,)), ((1,), (1,)), ((0,), (0,))


@jax.custom_vjp
def _mm(a, b):
    return _dot(a, b, NN)


def _mm_fwd(a, b):
    return _dot(a, b, NN), (a, b)


def _mm_bwd(res, g):
    a, b = res
    return _dot(g, b, NT), _dot(a, g, TN)


_mm.defvjp(_mm_fwd, _mm_bwd)


@jax.custom_vjp
def _mm_nt(a, b):
    return _dot(a, b, NT)


def _mm_nt_fwd(a, b):
    return _dot(a, b, NT), (a, b)


def _mm_nt_bwd(res, g):
    a, b = res
    return _dot(g, b, NN), _dot(g, a, TN)


_mm_nt.defvjp(_mm_nt_fwd, _mm_nt_bwd)


def _split_dot(x, tri):
    hi = x.astype(MXU)
    lo = (x - hi.astype(F32)).astype(MXU)
    return _dot(hi, tri, NN) + _dot(lo, tri, NN)


def _scan_keys(x, tri, reverse):
    w = tri.shape[0]
    nb = x.shape[1] // w
    outs, carry = [None] * nb, None
    for c in (reversed(range(nb)) if reverse else range(nb)):
        xc = x[:, c * w:(c + 1) * w]
        sc = _split_dot(xc, tri)
        outs[c] = sc if carry is None else sc + carry
        rs = jnp.sum(xc, axis=1, keepdims=True)
        carry = rs if carry is None else carry + rs
    return (outs[0] if nb == 1 else jnp.concatenate(outs, axis=1)), carry


def _col2row(col):
    return jnp.transpose(jnp.broadcast_to(col, (col.shape[0], LANES)))[0:8]


def _row2col(row):
    return jnp.transpose(jnp.broadcast_to(row, (LANES, row.shape[1])))[:, 0:1]


def _matmul(name, a, b, mode, tm, tn, tk, out_dtype=F32, add=None, add_scale=1.0, second_dtype=None):
    if mode == "nn":
        (m, k), n = a.shape, b.shape[1]
    elif mode == "nt":
        (m, k), n = a.shape, b.shape[0]
    else:
        (k, m), n = a.shape, b.shape[1]
    tm, tn, tk = min(tm, m), min(tn, n), min(tk, k)
    assert m % tm == 0 and n % tn == 0 and k % tk == 0, (name, m, n, k, tm, tn, tk)
    nk = k // tk
    dims = {"nn": NN, "nt": NT, "tn": TN}[mode]
    a_spec = (pl.BlockSpec((tk, tm), lambda j, i, kk: (kk, i)) if mode == "tn"
              else pl.BlockSpec((tm, tk), lambda j, i, kk: (i, kk)))
    b_spec = (pl.BlockSpec((tn, tk), lambda j, i, kk: (j, kk)) if mode == "nt"
              else pl.BlockSpec((tk, tn), lambda j, i, kk: (kk, j)))
    o_spec = pl.BlockSpec((tm, tn), lambda j, i, kk: (i, j))
    in_specs, operands = [a_spec, b_spec], [a, b]
    if add is not None:
        in_specs.append(o_spec)
        operands.append(add)
    out_shape = [jax.ShapeDtypeStruct((m, n), out_dtype)]
    out_specs = [o_spec]
    if second_dtype is not None:
        out_shape.append(jax.ShapeDtypeStruct((m, n), second_dtype))
        out_specs.append(o_spec)

    def body(*refs):
        a_ref, b_ref = refs[0], refs[1]
        add_ref = refs[2] if add is not None else None
        outs = refs[len(operands):-1]
        acc_ref = refs[-1]
        kk = pl.program_id(2)

        @pl.when(kk == 0)
        def _():
            acc_ref[...] = jnp.zeros_like(acc_ref)

        acc_ref[...] += _dot(a_ref[...], b_ref[...], dims)

        @pl.when(kk == nk - 1)
        def _():
            r = acc_ref[...]
            if add_ref is not None:
                r = r + add_scale * add_ref[...]
            outs[0][...] = r.astype(out_dtype)
            if second_dtype is not None:
                outs[1][...] = r.astype(second_dtype)

    res = pl.pallas_call(
        body, name=name, grid=(n // tn, m // tm, nk), in_specs=in_specs, out_specs=out_specs, out_shape=out_shape,
        scratch_shapes=[pltpu.VMEM((tm, tn), F32)],
        compiler_params=_params(("parallel", "parallel", "arbitrary")),
    )(*operands)
    return res if second_dtype is not None else res[0]


def _accumulate(ref, val, first):
    @pl.when(first)
    def _():
        ref[...] = val

    @pl.when(jnp.logical_not(first))
    def _():
        ref[...] += val


def _rowcall(name, fn, rows, tm, row_in, const_in, row_out, acc_out=()):
    tm = min(tm, rows)
    assert rows % tm == 0
    n_in = len(row_in) + len(const_in)
    in_specs = [pl.BlockSpec((tm, w), lambda i, cb=cb: (i, cb)) for _, w, cb in row_in]
    in_specs += [pl.BlockSpec(c.shape, lambda i, nd=c.ndim: (0,) * nd) for c in const_in]
    out_specs = [pl.BlockSpec(blk, imap) for _, _, blk, imap in row_out]
    out_specs += [pl.BlockSpec(s, lambda i, nd=len(s): (0,) * nd) for s, _ in acc_out]
    out_shape = [jax.ShapeDtypeStruct(s, dt) for s, dt, _, _ in row_out]
    out_shape += [jax.ShapeDtypeStruct(s, dt) for s, dt in acc_out]

    def body(*refs):
        vals = fn(*[r[...] for r in refs[:n_in]])
        outs = refs[n_in:]
        first = pl.program_id(0) == 0
        for k, (_, dt, _, _) in enumerate(row_out):
            outs[k][...] = vals[k].astype(dt)
        for k in range(len(acc_out)):
            _accumulate(outs[len(row_out) + k], vals[len(row_out) + k], first)

    return pl.pallas_call(
        body, name=name, grid=(rows // tm,), in_specs=in_specs, out_specs=out_specs, out_shape=out_shape,
        compiler_params=_params(("arbitrary",) if acc_out else ("parallel",)),
    )(*[a for a, _, _ in row_in], *const_in)


def _ro(rows, width, dtype, tm):
    tm = min(tm, rows)
    return ((rows, width), dtype, (tm, width), lambda i: (i, 0))


def _rms(x, g):
    ms = jnp.mean(x * x, axis=-1, keepdims=True)
    return x * lax.rsqrt(ms + RMS_EPS) * g


def _ln(x, g, b):
    mu = jnp.mean(x, axis=-1, keepdims=True)
    xc = x - mu
    var = jnp.mean(xc * xc, axis=-1, keepdims=True)
    return xc * lax.rsqrt(var + LN_EPS) * g + b


def _gelu(x):
    return 0.5 * x * (1.0 + jnp.tanh(math.sqrt(2.0 / math.pi) * (x + 0.044715 * (x * x * x))))


def _silu(x):
    return x / (1.0 + jnp.exp(-x))


@jax.custom_vjp
def _rope(r, cc, sa, sb):
    return r * cc + pltpu.roll(r, 96, 1) * sa + pltpu.roll(r, 32, 1) * sb


def _rope_fwd(r, cc, sa, sb):
    return _rope(r, cc, sa, sb), (cc, sa, sb)


def _rope_bwd(res, d):
    cc, sa, sb = res
    return d * cc + pltpu.roll(d * sa, 32, 1) + pltpu.roll(d * sb, 96, 1), None, None, None


_rope.defvjp(_rope_fwd, _rope_bwd)


def _rope_tables_fn(pos, invf):
    ang = pos.astype(F32) * invf
    lane = lax.broadcasted_iota(jnp.int32, ang.shape, 1)
    cos, sin = jnp.cos(ang), jnp.sin(ang)
    zero = jnp.zeros_like(ang)
    cc = jnp.where(lane < 64, cos, zero)
    sa = jnp.where(lane < 32, -sin, zero)
    sb = jnp.where((lane >= 32) & (lane < 64), sin, zero)
    return cc, sa, sb


def _prep_fn(cq, ckv, kpe, cc, sa, sb, gq, wuq, gkv, wukv):
    q = _mm_nt(_rms(cq, gq), wuq)
    kv = _mm_nt(_rms(ckv, gkv), wukv)
    krot = _rope(kpe, cc, sa, sb)
    qs, ks = [], []
    for h in range(MLA_HEADS):
        o = h * MLA_HEAD_PAD
        qs += [q[:, o:o + 128], _rope(q[:, o + 128:o + 256], cc, sa, sb)]
        ks += [kv[:, h * 128:(h + 1) * 128], krot]
    return jnp.concatenate(qs, axis=1), jnp.concatenate(ks, axis=1), kv[:, MLA_HEADS * 128:]


def _post_fn(ga, sgu, sgv, gb, gc, mq, gm, oa, oc, mkv, lng, lnb, sgw, biasf):
    tm = ga.shape[0]
    u = _gelu(sgu)
    vn = _ln(_gelu(sgv), lng, lnb)
    t_chunk = lax.broadcasted_iota(jnp.int32, (SG_CHUNK, SG_CHUNK), 0) // CHUNK
    s_chunk = lax.broadcasted_iota(jnp.int32, (SG_CHUNK, SG_CHUNK), 1) // CHUNK
    sg_mask = (s_chunk <= t_chunk).astype(F32)
    cols = []
    for g in range(SG_GROUPS):
        wg = sgw[g] * sg_mask
        vg = vn[:, g * SG_CH:(g + 1) * SG_CH]
        cols.append(jnp.concatenate(
            [_mm(wg, vg[n * SG_CHUNK:(n + 1) * SG_CHUNK, :]) for n in range(tm // SG_CHUNK)], axis=0))
    mixed = jnp.concatenate(cols, axis=1) + jnp.concatenate([biasf] * (tm // SG_CHUNK), axis=0)
    ob = u * mixed
    mk, mv = mkv[:, :MEM_WIDTH], mkv[:, MEM_WIDTH:]
    head = lax.broadcasted_iota(jnp.int32, (1, MEM_WIDTH), 1) // MEM_DIM
    om = jnp.zeros((tm, MEM_WIDTH), F32)
    for hd in range(MEM_HEADS):
        hm = (head == hd).astype(F32)
        s = _mm_nt(mq, mk * hm) * MEM_SCALE
        e = jnp.exp(s - lax.stop_gradient(jnp.max(s, axis=-1, keepdims=True)))
        p = e / jnp.sum(e, axis=-1, keepdims=True)
        om = om + _mm(p, mv * hm)
    return jnp.concatenate([oa * _silu(ga), ob * _silu(gb), oc * _silu(gc), om * _silu(gm)], axis=1)


def _post_bwd_fn(dyg, ga, sgu, sgv, gb, gc, mq, gm, oa, oc, mkv, lng, lnb, sgw, biasf):
    _, vjp = jax.vjp(_post_fn, ga, sgu, sgv, gb, gc, mq, gm, oa, oc, mkv, lng, lnb, sgw, biasf)
    dga, dsgu, dsgv, dgb, dgc, dmq, dgm, doa, doc, dmkv, dlng, dlnb, dsgw, dbias = vjp(dyg)
    drows = jnp.stack([_col2row(jnp.sum((doa * oa)[:, h * 128:(h + 1) * 128], axis=1, keepdims=True))
                       for h in range(MLA_HEADS)], axis=0)
    return dga, dsgu, dsgv, dgb, dgc, dmq, dgm, doa, doc, drows, dmkv, dlng, dlnb, dsgw, dbias


def _prep_bwd_fn(dq, dk, dv, cq, ckv, kpe, cc, sa, sb, gq, wuq, gkv, wukv):
    wuq, wukv = wuq.astype(F32), wukv.astype(F32)
    _, vjp = jax.vjp(lambda a, b, c, d, e, f, g: _prep_fn(a, b, c, cc, sa, sb, d, e, f, g), cq, ckv, kpe, gq, wuq, gkv, wukv)
    return vjp((dq, dk, dv))


def _ln_bwd_fn(dxn, z, g, b):
    _, vjp = jax.vjp(_ln, z, g, b)
    dz, dg, db = vjp(dxn)
    return dz, dz, dg, db


def _outproj_fn(yg, x, w, g, b):
    z = ALPHA * x + _dot(yg, w, NN)
    xn = _ln(z, g, b)
    return z, xn, xn


def _loss_fn(xl, tgt):
    err = xl - tgt
    part = 0.5 * jnp.sum(jnp.mean(err * err, axis=-1, keepdims=True), axis=0, keepdims=True)
    return err * (1.0 / D_MODEL), jnp.broadcast_to(part, (1, LANES))


def _mla_fwd(q, k, v, tq):
    t = q.shape[0]
    tq = min(tq, t)
    nq = t // tq

    def body(q_ref, k_ref, v_ref, o_ref, lse_ref):
        i = pl.program_id(1)
        qb = q_ref[...]
        r_chunk = lax.broadcasted_iota(jnp.int32, (tq, tq), 0) // CHUNK
        c_chunk = lax.broadcasted_iota(jnp.int32, (tq, tq), 1) // CHUNK
        diag_ok = c_chunk <= r_chunk

        def block(j, carry, masked):
            m, l, acc = carry
            off = pl.multiple_of(j * tq, tq)
            s = _dot(qb, k_ref[pl.ds(off, tq), :], NT) * MLA_SCALE
            if masked:
                s = jnp.where(diag_ok, s, NEG)
            m2 = jnp.maximum(m, jnp.max(s, axis=1, keepdims=True))
            p = jnp.exp(s - m2)
            a = jnp.exp(m - m2)
            return m2, a * l + jnp.sum(p, axis=1, keepdims=True), a * acc + _dot(p, v_ref[pl.ds(off, tq), :], NN)

        init = (jnp.full((tq, 1), NEG, F32), jnp.zeros((tq, 1), F32), jnp.zeros((tq, MLA_V), F32))
        carry = lax.fori_loop(0, i, lambda j, c: block(j, c, False), init)
        m, l, acc = block(i, carry, True)
        o_ref[...] = acc / l
        lse_ref[...] = _col2row(m + jnp.log(l))

    return pl.pallas_call(
        body, name="mla_fwd", grid=(MLA_HEADS, nq),
        in_specs=[pl.BlockSpec((tq, MLA_HEAD_PAD), lambda h, i: (i, h)),
                  pl.BlockSpec((t, MLA_HEAD_PAD), lambda h, i: (0, h)),
                  pl.BlockSpec((t, MLA_V), lambda h, i: (0, h))],
        out_specs=[pl.BlockSpec((tq, MLA_V), lambda h, i: (i, h)),
                   pl.BlockSpec((None, 8, tq), lambda h, i: (h, 0, i))],
        out_shape=[jax.ShapeDtypeStruct((t, MLA_WIDTH), F32), jax.ShapeDtypeStruct((MLA_HEADS, 8, t), F32)],
        compiler_params=_params(("parallel", "parallel")),
    )(q, k, v)


def _mla_bwd(q, k, v, do, lse, drow, tq):
    t = q.shape[0]
    tq = min(tq, t)
    nq = t // tq

    def body(k_ref, v_ref, q_ref, do_ref, lse_ref, d_ref, dk_ref, dv_ref, dq_ref):
        j = pl.program_id(1)

        @pl.when(j == 0)
        def _():
            dq_ref[...] = jnp.zeros_like(dq_ref)

        kj, vj = k_ref[...], v_ref[...]
        k_chunk = lax.broadcasted_iota(jnp.int32, (tq, tq), 0) // CHUNK
        q_chunk = lax.broadcasted_iota(jnp.int32, (tq, tq), 1) // CHUNK
        diag_ok = k_chunk <= q_chunk

        def block(i, carry, masked):
            dk, dv = carry
            off = pl.multiple_of(i * tq, tq)
            qi, doi = q_ref[pl.ds(off, tq), :], do_ref[pl.ds(off, tq), :]
            st = _dot(kj, qi, NT) * MLA_SCALE
            pt = jnp.exp(st - lse_ref[0:1, pl.ds(off, tq)])
            if masked:
                pt = jnp.where(diag_ok, pt, 0.0)
            dv = dv + _dot(pt, doi, NN)
            dst = (pt * (_dot(vj, doi, NT) - d_ref[0:1, pl.ds(off, tq)]) * MLA_SCALE).astype(MXU)
            dk = dk + _dot(dst, qi, NN)
            dq_ref[pl.ds(off, tq), :] += _dot(dst, kj, TN)
            return dk, dv

        carry = block(j, (jnp.zeros((tq, MLA_HEAD_PAD), F32), jnp.zeros((tq, MLA_V), F32)), True)
        dk, dv = lax.fori_loop(j + 1, nq, lambda i, c: block(i, c, False), carry)
        dk_ref[...] = dk
        dv_ref[...] = dv

    dk, dv, dq = pl.pallas_call(
        body, name="mla_bwd", grid=(MLA_HEADS, nq),
        in_specs=[pl.BlockSpec((tq, MLA_HEAD_PAD), lambda h, j: (j, h)),
                  pl.BlockSpec((tq, MLA_V), lambda h, j: (j, h)),
                  pl.BlockSpec((t, MLA_HEAD_PAD), lambda h, j: (0, h)),
                  pl.BlockSpec((t, MLA_V), lambda h, j: (0, h)),
                  pl.BlockSpec((None, 8, t), lambda h, j: (h, 0, 0)),
                  pl.BlockSpec((None, 8, t), lambda h, j: (h, 0, 0))],
        out_specs=[pl.BlockSpec((tq, MLA_HEAD_PAD), lambda h, j: (j, h)),
                   pl.BlockSpec((tq, MLA_V), lambda h, j: (j, h)),
                   pl.BlockSpec((t, MLA_HEAD_PAD), lambda h, j: (0, h))],
        out_shape=[jax.ShapeDtypeStruct((t, MLA_HEADS * MLA_HEAD_PAD), F32),
                   jax.ShapeDtypeStruct((t, MLA_WIDTH), F32),
                   jax.ShapeDtypeStruct((t, MLA_HEADS * MLA_HEAD_PAD), F32)],
        compiler_params=_params(("parallel", "arbitrary")),
    )(k, v, q, do, lse, drow)
    return dq, dk, dv


def _sb_logs(z2):
    lb = jnp.minimum(z2, 0.0) - jnp.log2(1.0 + jnp.exp2(jnp.minimum(z2, -z2)))
    return lb, lb - z2


def _tri(n, keep):
    r = lax.broadcasted_iota(jnp.int32, (n, n), 0)
    c = lax.broadcasted_iota(jnp.int32, (n, n), 1)
    return keep(r, c)


def _sb_fwd(hb, tq, comm=None):
    t = hb.shape[0]
    tq = min(tq, t)
    nq = t // tq
    cq, ck, cv = C_SBQ // SB_DIM, C_SBK // SB_DIM, C_SBV // SB_DIM

    def body(q_ref, k_ref, v_ref, o_ref, tot_ref):
        i = pl.program_id(1)
        qb = q_ref[...]
        strict = _tri(tq, lambda r, c: c < r)
        later = _tri(min(SCAN, tq), lambda r, c: r > c).astype(MXU)

        def block(j, carry, masked):
            run, acc = carry
            off = pl.multiple_of(j * tq, tq)
            lb, l1 = _sb_logs(_dot(qb, k_ref[pl.ds(off, tq), :], NT) * (SB_SCALE * LOG2E))
            if masked:
                l1 = jnp.where(strict, l1, 0.0)
            after, rowsum = _scan_keys(l1, later, True)
            a = jnp.exp2(lb + after + run)
            if masked:
                a = jnp.where(strict, a, 0.0)
            return run + rowsum, acc + _dot(a, v_ref[pl.ds(off, tq), :], NN)

        carry = block(i, (jnp.zeros((tq, 1), F32), jnp.zeros((tq, SB_DIM), F32)), True)
        run, acc = lax.fori_loop(0, i, lambda jj, c: block(i - 1 - jj, c, False), carry)
        o_ref[...] = acc
        tot_ref[...] = _col2row(run)

    (o, tot), received = _comm_call(
        body, "sb_fwd", (SB_HEADS, nq),
        [pl.BlockSpec((tq, SB_DIM), lambda h, i: (i, cq + h)),
         pl.BlockSpec((t, SB_DIM), lambda h, i: (0, ck + h)),
         pl.BlockSpec((t, SB_DIM), lambda h, i: (0, cv + h))],
        [pl.BlockSpec((tq, SB_DIM), lambda h, i: (i, h)),
         pl.BlockSpec((None, 8, tq), lambda h, i: (h, 0, i))],
        [jax.ShapeDtypeStruct((t, SB_WIDTH), F32), jax.ShapeDtypeStruct((SB_HEADS, 8, t), F32)],
        [hb, hb, hb], ("parallel", "parallel"), comm)
    return o, tot, received


def _sb_bwd(hb, do, tot, tq, comm=None):
    t = hb.shape[0]
    tq = min(tq, t)
    nq = t // tq
    cq, ck, cv = C_SBQ // SB_DIM, C_SBK // SB_DIM, C_SBV // SB_DIM

    def body(q_ref, do_ref, tot_ref, k_ref, v_ref, dq_ref, dk_ref, dv_ref):
        i = pl.program_id(1)

        @pl.when(i == 0)
        def _():
            dk_ref[...] = jnp.zeros_like(dk_ref)
            dv_ref[...] = jnp.zeros_like(dv_ref)

        qb, dob = q_ref[...], do_ref[...]
        total = _row2col(tot_ref[0:1, :])
        strict = _tri(tq, lambda r, c: c < r)
        upto = _tri(min(SCAN, tq), lambda r, c: r <= c).astype(MXU)
        before = _tri(min(SCAN, tq), lambda r, c: r < c).astype(MXU)

        def block(j, carry, masked):
            pre, gpre, dq = carry
            off = pl.multiple_of(j * tq, tq)
            kj, vj = k_ref[pl.ds(off, tq), :], v_ref[pl.ds(off, tq), :]
            lb, l1 = _sb_logs(_dot(qb, kj, NT) * (SB_SCALE * LOG2E))
            if masked:
                l1 = jnp.where(strict, l1, 0.0)
            upto_sum, rowsum = _scan_keys(l1, upto, False)
            a = jnp.exp2(lb + (total - (upto_sum + pre)))
            if masked:
                a = jnp.where(strict, a, 0.0)
            g = a * _dot(dob, vj, NT)
            before_sum, grow = _scan_keys(g, before, False)
            beta = jnp.exp2(lb)
            dz = (g * (1.0 - beta) - (before_sum + gpre) * beta) * SB_SCALE
            if masked:
                dz = jnp.where(strict, dz, 0.0)
            dzb = dz.astype(MXU)
            dk_ref[pl.ds(off, tq), :] += _dot(dzb, qb, TN)
            dv_ref[pl.ds(off, tq), :] += _dot(a, dob, TN)
            return pre + rowsum, gpre + grow, dq + _dot(dzb, kj, NN)

        init = (jnp.zeros((tq, 1), F32), jnp.zeros((tq, 1), F32), jnp.zeros((tq, SB_DIM), F32))
        carry = lax.fori_loop(0, i, lambda j, c: block(j, c, False), init)
        dq_ref[...] = block(i, carry, True)[2].astype(dq_ref.dtype)

    (dq, dk, dv), received = _comm_call(
        body, "sb_bwd", (SB_HEADS, nq),
        [pl.BlockSpec((tq, SB_DIM), lambda h, i: (i, cq + h)),
         pl.BlockSpec((tq, SB_DIM), lambda h, i: (i, h)),
         pl.BlockSpec((None, 8, tq), lambda h, i: (h, 0, i)),
         pl.BlockSpec((t, SB_DIM), lambda h, i: (0, ck + h)),
         pl.BlockSpec((t, SB_DIM), lambda h, i: (0, cv + h))],
        [pl.BlockSpec((tq, SB_DIM), lambda h, i: (i, h)),
         pl.BlockSpec((t, SB_DIM), lambda h, i: (0, h)),
         pl.BlockSpec((t, SB_DIM), lambda h, i: (0, h))],
        [jax.ShapeDtypeStruct((t, SB_WIDTH), MXU), jax.ShapeDtypeStruct((t, SB_WIDTH), F32),
         jax.ShapeDtypeStruct((t, SB_WIDTH), F32)],
        [hb, do, tot, hb, hb], ("parallel", "arbitrary"), comm)
    return dq, dk, dv, received


TM = 256
TQ_MLA = 1024
TQ_SB = 512


def _layer_fwd(xf, xb, w, consts, comm=None):
    t = xf.shape[0]
    cc, sa, sb, mem = consts
    h, hb = _matmul("in_proj", xb, w["w_in"], "nt", 512, 1152, 2048, second_dtype=MXU)
    gq, gkv = w["q_norm_g"][None, :], w["kv_norm_g"][None, :]
    q, k, v = _rowcall(
        "mla_prep", _prep_fn, t, TM,
        [(h, 512, C_CQ // 512), (h, 256, C_CKV // 256), (h, 128, C_KPE // 128), (cc, 128, 0), (sa, 128, 0), (sb, 128, 0)],
        [gq, w["w_uq"], gkv, w["w_ukv"]],
        [_ro(t, MLA_HEADS * MLA_HEAD_PAD, MXU, TM), _ro(t, MLA_HEADS * MLA_HEAD_PAD, MXU, TM), _ro(t, MLA_WIDTH, MXU, TM)])
    oa, lse = _mla_fwd(q, k, v, TQ_MLA)
    oc, tot, received = _sb_fwd(hb, TQ_SB, comm)
    mkv = _matmul("mem_kv", mem, w["w_mem_kv"], "nn", 256, 512, 2048)
    lng, lnb = w["sg_ln_g"][None, :], w["sg_ln_b"][None, :]
    biasf = jnp.repeat(w["sg_b"].T, SG_CH, axis=1)
    post_rows = [(h, 768, C_GA // 768), (h, 512, C_SGU // 512), (h, 512, C_SGV // 512), (h, 512, C_GB // 512),
                 (h, 512, C_GC // 512), (h, 256, C_MQ // 256), (h, 256, C_GM // 256), (oa, 768, 0), (oc, 512, 0)]
    post_consts = [mkv, lng, lnb, w["sg_w"], biasf]
    (yg,) = _rowcall("mixer_post", lambda *a: (_post_fn(*a),), t, TM, post_rows, post_consts, [_ro(t, D_MODEL, MXU, TM)])
    z, xn, xnb = _rowcall(
        "out_proj_ln", _outproj_fn, t, TM, [(yg, D_MODEL, 0), (xf, D_MODEL, 0)],
        [w["w_out"], w["ln_g"][None, :], w["ln_b"][None, :]],
        [_ro(t, D_MODEL, F32, TM), _ro(t, D_MODEL, F32, TM), _ro(t, D_MODEL, MXU, TM)])
    saved = dict(xb=xb, h=h, hb=hb, q=q, k=k, v=v, oa=oa, lse=lse, oc=oc, tot=tot, mkv=mkv, yg=yg, z=z,
                 post_rows=post_rows, post_consts=post_consts, gq=gq, gkv=gkv)
    return xn, xnb, saved, received


def _layer_bwd(dxn, w, s, consts, comm=None):
    t = dxn.shape[0]
    cc, sa, sb, mem = consts
    h = s["h"]
    dz, dzb, dlng2, dlnb2 = _rowcall(
        "ln_bwd", _ln_bwd_fn, t, TM, [(dxn, D_MODEL, 0), (s["z"], D_MODEL, 0)],
        [w["ln_g"][None, :], w["ln_b"][None, :]],
        [_ro(t, D_MODEL, F32, TM), _ro(t, D_MODEL, MXU, TM)], [((1, D_MODEL), F32), ((1, D_MODEL), F32)])
    dyg = _matmul("out_proj_dx", dzb, w["w_out"], "nt", 512, 1024, 2048)
    dw_out = _matmul("out_proj_dw", s["yg"], dzb, "tn", 1024, 1024, 512)
    tm = min(TM, t)
    outs = _rowcall(
        "mixer_post_bwd", _post_bwd_fn, t, TM, [(dyg, D_MODEL, 0)] + s["post_rows"], s["post_consts"],
        [_ro(t, 768, MXU, TM), _ro(t, 512, MXU, TM), _ro(t, 512, MXU, TM), _ro(t, 512, MXU, TM), _ro(t, 512, MXU, TM),
         _ro(t, 256, MXU, TM), _ro(t, 256, MXU, TM), _ro(t, 768, MXU, TM), _ro(t, 512, MXU, TM),
         ((MLA_HEADS, 8, t), F32, (MLA_HEADS, 8, tm), lambda i: (0, 0, i))],
        [((MEM_TOKENS, 2 * MEM_WIDTH), F32), ((1, SG_WIDTH), F32), ((1, SG_WIDTH), F32),
         ((SG_GROUPS, SG_CHUNK, SG_CHUNK), F32), ((SG_CHUNK, SG_WIDTH), F32)])
    dga, dsgu, dsgv, dgb, dgc, dmq, dgm, doa, doc, drow, dmkv, dsg_ln_g, dsg_ln_b, dsgw, dbias = outs
    dw_mem_kv = _matmul("mem_kv_dw", mem, dmkv, "tn", 1024, 512, 256)
    dsbq, dsbk, dsbv, received = _sb_bwd(s["hb"], doc, s["tot"], TQ_SB, comm)
    dq, dk, dv = _mla_bwd(s["q"], s["k"], s["v"], doa, s["lse"], drow, TQ_MLA)
    wq, wkv = MLA_HEADS * MLA_HEAD_PAD, MLA_WIDTH
    dcq, dckv, dkpe, dgq, dwuq, dgkv, dwukv = _rowcall(
        "mla_prep_bwd", _prep_bwd_fn, t, TM,
        [(dq, wq, 0), (dk, wq, 0), (dv, wkv, 0), (h, 512, C_CQ // 512), (h, 256, C_CKV // 256), (h, 128, C_KPE // 128),
         (cc, 128, 0), (sa, 128, 0), (sb, 128, 0)],
        [s["gq"], w["w_uq"], s["gkv"], w["w_ukv"]],
        [_ro(t, 512, MXU, TM), _ro(t, 256, MXU, TM), _ro(t, 128, MXU, TM)],
        [((1, MLA_Q_RANK), F32), ((wq, MLA_Q_RANK), F32), ((1, MLA_KV_RANK), F32), ((2 * wkv, MLA_KV_RANK), F32)])
    dh = jnp.concatenate([dcq, dckv, dga, dsgu, dsgv, dgb, dsbq, dsbk.astype(MXU), dsbv.astype(MXU), dgc, dmq, dgm, dkpe], axis=1)
    dx = _matmul("in_proj_dx", dh, w["w_in"], "nn", 512, 1024, 1152, add=dz, add_scale=ALPHA)
    dw_in = _matmul("in_proj_dw", dh, s["xb"], "tn", 1152, 1024, 512)
    grads = dict(w_in=dw_in, q_norm_g=dgq[0], w_uq=dwuq, kv_norm_g=dgkv[0], w_ukv=dwukv, sg_ln_g=dsg_ln_g[0],
                 sg_ln_b=dsg_ln_b[0], sg_w=dsgw, sg_bias=dbias, w_mem_kv=dw_mem_kv, w_out=dw_out, ln_g=dlng2[0], ln_b=dlnb2[0])
    return dx, grads, received


def _local_step(x, mem, positions, target, depth, weights_of, gather_comm, grad_comm, recv):
    t = x.shape[0]
    inv_freq = ROPE_THETA ** (-jnp.arange(0, MLA_ROPE, 2, dtype=F32) / MLA_ROPE)
    invf = jnp.tile(inv_freq, 4)[None, :]
    cc, sa, sb = _rowcall("rope_tables", _rope_tables_fn, t, TM, [(positions[:, None], 1, 0)], [invf],
                          [_ro(t, 128, F32, TM)] * 3)
    consts = (cc, sa, sb, mem)
    xf, xb, saved, weights, received = x, x.astype(MXU), [], [], None
    for l in range(depth):
        weights.append(weights_of(l, received))
        xf, xb, s, received = _layer_fwd(xf, xb, weights[l], consts, gather_comm(l + 1) if l + 1 < depth else None)
        saved.append(s)
    dx, loss_part = _rowcall("loss", _loss_fn, t, TM, [(xf, D_MODEL, 0), (target, D_MODEL, 0)], [],
                             [_ro(t, D_MODEL, F32, TM)], [((1, LANES), F32)])
    grads = [None] * depth
    for l in reversed(range(depth)):
        comm = grad_comm(l + 1, grads[l + 1], recv) if l + 1 < depth else None
        dx, grads[l], received = _layer_bwd(dx, weights[l], saved[l], consts, comm)
        if comm is not None:
            recv = received
    db = jnp.concatenate([g["sg_bias"] for g in grads], axis=0)

    def lane_sums(d):
        return (jnp.concatenate([jnp.broadcast_to(jnp.sum(d[:, g * SG_CH:(g + 1) * SG_CH], axis=1, keepdims=True),
                                                  (d.shape[0], SG_CH)) for g in range(SG_GROUPS)], axis=1),)

    (dbs,) = _rowcall("sg_bias_sum", lane_sums, db.shape[0], SG_CHUNK, [(db, SG_WIDTH, 0)], [], [_ro(db.shape[0], SG_WIDTH, F32, SG_CHUNK)])
    dsgb = dbs[:, ::SG_CH].reshape(depth, SG_CHUNK, SG_GROUPS).transpose(0, 2, 1)
    return loss_part[0, 0], dx, grads, dsgb, recv


SHARD_IN = D_IN // N_DEV
SHARD_IN_PAD = D_IN_PAD // N_DEV
IN_RANGES = ((0, 768), (832, D_IN), (768, 832))


def _in_rows_from_shards(w):
    pieces = []
    for a, b in IN_RANGES:
        while a < b:
            e = min(b, (a // SHARD_IN + 1) * SHARD_IN)
            row = (a // SHARD_IN) * SHARD_IN_PAD + a % SHARD_IN
            pieces.append(w[row:row + e - a])
            a = e
    pieces.append(jnp.zeros((D_IN_PAD - D_IN, w.shape[1]), w.dtype))
    return jnp.concatenate(pieces, axis=0)


def _in_rows_to_shards(g):
    pieces = []
    for d in range(N_DEV):
        a, b = d * SHARD_IN, (d + 1) * SHARD_IN
        while a < b:
            e = min(b, 768 if a < 768 else (832 if a < 832 else D_IN))
            row = a if a < 768 else (C_KPE + a - 768 if a < 832 else a - 64)
            pieces.append(g[row:row + e - a])
            a = e
        pieces.append(jnp.zeros((SHARD_IN_PAD - SHARD_IN, g.shape[1]), g.dtype))
    return jnp.concatenate(pieces, axis=0)


def _uq_rows_pad(w):
    w = w.reshape(MLA_HEADS, MLA_NOPE + MLA_ROPE, w.shape[-1])
    return jnp.pad(w, ((0, 0), (0, MLA_HEAD_PAD - MLA_NOPE - MLA_ROPE), (0, 0))).reshape(MLA_HEADS * MLA_HEAD_PAD, -1)


def _uq_rows_unpad(g):
    return g.reshape(MLA_HEADS, MLA_HEAD_PAD, -1)[:, :MLA_NOPE + MLA_ROPE].reshape(MLA_HEADS * (MLA_NOPE + MLA_ROPE), -1)


def _ukv_rows_split(w):
    return jnp.swapaxes(w.reshape(MLA_HEADS, 2, 128, -1), 0, 1).reshape(MLA_HEADS * 256, -1)


def _ukv_rows_unsplit(g):
    return jnp.swapaxes(g.reshape(2, MLA_HEADS, 128, -1), 0, 1).reshape(MLA_HEADS * 256, -1)


class _Comm:
    def __init__(self, items):
        self.items = items
        self.n = len(items)
        self.pieces = [((s.shape[0] // N_DEV,) + s.shape[1:]) if sc else s.shape for s, sc, _ in items]
        self.inputs = [s for s, _, _ in items] + [d[0] for _, _, d in items if d is not None]
        any_space = pl.BlockSpec(memory_space=pl.ANY)
        self.in_specs = [any_space] * len(self.inputs)
        self.out_specs = [any_space] * self.n
        self.out_shape = [jax.ShapeDtypeStruct((N_DEV,) + tuple(p), s.dtype) if d is None
                          else jax.ShapeDtypeStruct(d[0].shape, d[0].dtype) for p, (s, _, d) in zip(self.pieces, items)]
        self.scratch = [pltpu.SemaphoreType.DMA((N_DEV - 1, self.n)), pltpu.SemaphoreType.DMA((N_DEV - 1, self.n)),
                        pltpu.SemaphoreType.DMA((self.n,))]

    def aliases(self, first_in, first_out):
        passed = [a for a, (_, _, d) in enumerate(self.items) if d is not None]
        return {first_in + self.n + b: first_out + a for b, a in enumerate(passed)}

    def copies(self, in_refs, out_refs, sems):
        send_sems, recv_sems, local_sems = sems
        x, y, c = lax.axis_index("x"), lax.axis_index("y"), lax.axis_index("c")
        me = 4 * x + 2 * y + c

        def block_for(a, p):
            if not self.items[a][1]:
                return in_refs[a]
            n = self.pieces[a][0]
            return in_refs[a].at[pl.ds(pl.multiple_of(p * n, 8), n)]

        def slot(a, dev):
            dst = self.items[a][2]
            return out_refs[a].at[dev] if dst is None else out_refs[a].at[dev, dst[1]]

        starts, recv_waits, send_waits = [], [], []
        for a in range(self.n):
            cp = pltpu.make_async_copy(block_for(a, me), slot(a, me), local_sems.at[a])
            starts.append(cp.start)
            send_waits.append(cp.wait)
        for k in range(1, N_DEV):
            px = 1 - x if (k >> 2) & 1 else x
            py = 1 - y if (k >> 1) & 1 else y
            pc = 1 - c if k & 1 else c
            peer = 4 * px + 2 * py + pc
            for a in range(self.n):
                out = pltpu.make_async_remote_copy(
                    src_ref=block_for(a, peer), dst_ref=slot(a, me), send_sem=send_sems.at[k - 1, a],
                    recv_sem=recv_sems.at[k - 1, a], device_id=(px, py, pc), device_id_type=pl.DeviceIdType.MESH)
                back = pltpu.make_async_remote_copy(
                    src_ref=block_for(a, peer), dst_ref=slot(a, peer), send_sem=send_sems.at[k - 1, a],
                    recv_sem=recv_sems.at[k - 1, a], device_id=(px, py, pc), device_id_type=pl.DeviceIdType.MESH)
                starts.append(out.start)
                send_waits.append(out.wait_send)
                recv_waits.append(back.wait_recv)
        return starts, recv_waits + send_waits


def _comm_call(body, name, grid, in_specs, out_specs, out_shape, operands, semantics, comm):
    n_in, n_out = len(operands), len(out_shape)
    if comm is None:
        outs = pl.pallas_call(body, name=name, grid=grid, in_specs=in_specs, out_specs=out_specs, out_shape=out_shape,
                              compiler_params=_params(semantics))(*operands)
        return outs, []
    n_ci = len(comm.inputs)

    def fused(*refs):
        ins, c_in = refs[:n_in], refs[n_in:n_in + n_ci]
        outs = refs[n_in + n_ci:n_in + n_ci + n_out]
        c_out = refs[n_in + n_ci + n_out:n_in + n_ci + n_out + comm.n]
        starts, waits = comm.copies(c_in, c_out, refs[n_in + n_ci + n_out + comm.n:])
        first = functools.reduce(jnp.logical_and, [pl.program_id(d) == 0 for d in range(len(grid))])
        last = functools.reduce(jnp.logical_and, [pl.program_id(d) == g - 1 for d, g in enumerate(grid)])

        @pl.when(first)
        def _():
            for start in starts:
                start()

        body(*ins, *outs)

        @pl.when(last)
        def _():
            for wait in waits:
                wait()

    outs = pl.pallas_call(
        fused, name=name, grid=grid, in_specs=list(in_specs) + comm.in_specs, out_specs=list(out_specs) + comm.out_specs,
        out_shape=list(out_shape) + comm.out_shape, scratch_shapes=comm.scratch,
        input_output_aliases=comm.aliases(n_in, n_out), compiler_params=_params(("arbitrary",) * len(grid)),
    )(*operands, *comm.inputs)
    return outs[:n_out], outs[n_out:]


def _exchange(name, items):
    return _comm_call(lambda: None, name, (1,), [], [], [], [], ("arbitrary",), _Comm(items))[1]


BLOCK_BYTES = 6 << 20


def _row_block(rows, row_bytes):
    if rows * row_bytes <= BLOCK_BYTES or rows % 8:
        return rows
    return max(tb for tb in range(8, rows + 1, 8) if rows % tb == 0 and (tb == 8 or tb * row_bytes <= BLOCK_BYTES))


def _sum_contribs(recv):
    n, rows, width = recv.shape
    tb = _row_block(rows, width * 4 * (n + 1))

    def body(r_ref, g_ref):
        g = r_ref[0]
        for k in range(1, n):
            g = g + r_ref[k]
        g_ref[...] = g

    return pl.pallas_call(
        body, name="sum_grads", grid=(rows // tb,), in_specs=[pl.BlockSpec((n, tb, width), lambda i: (0, i, 0))],
        out_specs=pl.BlockSpec((tb, width), lambda i: (i, 0)), out_shape=jax.ShapeDtypeStruct((rows, width), F32),
        compiler_params=_params(("parallel",)),
    )(recv)


def _adamw(contribs, w, m, v):
    n, rows, width = contribs.shape
    tb = _row_block(rows, width * 4 * (n + 7))
    c1 = 1.0 - ADAM_B1 ** ADAM_STEP
    c2 = 1.0 - ADAM_B2 ** ADAM_STEP

    def body(r_ref, w_ref, m_ref, v_ref, g_ref, d_ref, nm_ref, nv_ref):
        g = r_ref[0]
        for k in range(1, n):
            g = g + r_ref[k]
        nm = ADAM_B1 * m_ref[...] + (1.0 - ADAM_B1) * g
        nv = ADAM_B2 * v_ref[...] + (1.0 - ADAM_B2) * (g * g)
        g_ref[...] = g
        nm_ref[...] = nm
        nv_ref[...] = nv
        d_ref[...] = -ADAM_LR * ((nm / c1) / (jnp.sqrt(nv / c2) + ADAM_EPS) + ADAM_WD * w_ref[...])

    blk = pl.BlockSpec((tb, width), lambda i: (i, 0))
    return pl.pallas_call(
        body, name="adamw", grid=(rows // tb,),
        in_specs=[pl.BlockSpec((n, tb, width), lambda i: (0, i, 0)), blk, blk, blk],
        out_specs=[blk] * 4, out_shape=[jax.ShapeDtypeStruct((rows, width), F32)] * 4,
        compiler_params=_params(("parallel",)),
    )(contribs, w, m, v)


def kernel(x, mem, positions, w_in, q_norm_g, w_uq, kv_norm_g, w_ukv, sg_ln_g, sg_ln_b, sg_w, sg_b, w_mem_k, w_mem_v, w_out, ln_g, ln_b, loss_target, m_w_in, m_q_norm_g, m_w_uq, m_kv_norm_g, m_w_ukv, m_sg_ln_g, m_sg_ln_b, m_sg_w, m_sg_b, m_w_mem_k, m_w_mem_v, m_w_out, m_ln_g, m_ln_b, v_w_in, v_q_norm_g, v_w_uq, v_kv_norm_g, v_w_ukv, v_sg_ln_g, v_sg_ln_b, v_sg_w, v_sg_b, v_w_mem_k, v_w_mem_v, v_w_out, v_ln_g, v_ln_b):
    wl = dict(w_in=w_in, q_norm_g=q_norm_g, w_uq=w_uq, kv_norm_g=kv_norm_g, w_ukv=w_ukv, sg_ln_g=sg_ln_g, sg_ln_b=sg_ln_b,
              sg_w=sg_w, sg_b=sg_b, w_mem_k=w_mem_k, w_mem_v=w_mem_v, w_out=w_out, ln_g=ln_g, ln_b=ln_b)
    ml = dict(w_in=m_w_in, q_norm_g=m_q_norm_g, w_uq=m_w_uq, kv_norm_g=m_kv_norm_g, w_ukv=m_w_ukv, sg_ln_g=m_sg_ln_g,
              sg_ln_b=m_sg_ln_b, sg_w=m_sg_w, sg_b=m_sg_b, w_mem_k=m_w_mem_k, w_mem_v=m_w_mem_v, w_out=m_w_out, ln_g=m_ln_g, ln_b=m_ln_b)
    vl = dict(w_in=v_w_in, q_norm_g=v_q_norm_g, w_uq=v_w_uq, kv_norm_g=v_kv_norm_g, w_ukv=v_w_ukv, sg_ln_g=v_sg_ln_g,
              sg_ln_b=v_sg_ln_b, sg_w=v_sg_w, sg_b=v_sg_b, w_mem_k=v_w_mem_k, w_mem_v=v_w_mem_v, w_out=v_w_out, ln_g=v_ln_g, ln_b=v_ln_b)
    depth = w_in.shape[0]

    w_in_t = jnp.pad(jnp.swapaxes(w_in, 1, 2), ((0, 0), (0, SHARD_IN_PAD - SHARD_IN), (0, 0)))
    shards = [s.astype(MXU) for s in (w_in_t, jnp.swapaxes(w_uq, 1, 2), jnp.swapaxes(w_ukv, 1, 2),
                                      jnp.concatenate([w_mem_k, w_mem_v], axis=-1), w_out)]

    def gather_items(l):
        return [(s[l], False, None) for s in shards]

    first = _exchange("gather_weights", gather_items(0))

    def weights_of(l, received):
        g_in, g_uq, g_ukv, g_mem, g_out = first if l == 0 else received
        return dict(w_in=_in_rows_from_shards(g_in.reshape(D_IN_PAD, D_MODEL)), w_uq=_uq_rows_pad(g_uq.reshape(-1, MLA_Q_RANK)),
                    w_ukv=_ukv_rows_split(g_ukv.reshape(-1, MLA_KV_RANK)), w_mem_kv=g_mem.reshape(D_MODEL, 2 * MEM_WIDTH),
                    w_out=g_out.reshape(D_MODEL, D_MODEL), q_norm_g=q_norm_g[l], kv_norm_g=kv_norm_g[l], sg_ln_g=sg_ln_g[l],
                    sg_ln_b=sg_ln_b[l], sg_w=sg_w[l], sg_b=sg_b[l], ln_g=ln_g[l], ln_b=ln_b[l])

    def grad_items(l, g, recv):
        arrays = [_in_rows_to_shards(g["w_in"]), _uq_rows_unpad(g["w_uq"]), _ukv_rows_unsplit(g["w_ukv"]), g["w_mem_kv"], g["w_out"]]
        return [(a, True, (buf, l)) for a, buf in zip(arrays, recv)]

    recv = [lax.empty((N_DEV, depth, rows // N_DEV, width), F32)
            for rows, width in ((D_IN_PAD, D_MODEL), (MLA_HEADS * (MLA_NOPE + MLA_ROPE), MLA_Q_RANK),
                                (MLA_HEADS * 256, MLA_KV_RANK), (D_MODEL, 2 * MEM_WIDTH), (D_MODEL, D_MODEL))]
    loss_part, grad_x, grads, dsgb, recv = _local_step(
        x[0], mem[0], positions[0], loss_target[0], depth, weights_of, lambda l: _Comm(gather_items(l)),
        lambda l, g, bufs: _Comm(grad_items(l, g, bufs)), recv)
    loss = lax.psum(loss_part, ("x", "y", "c"))

    rep_g = dict(sg_b=dsgb, **{n: jnp.stack([g[n] for g in grads]) for n in REPLICATED if n != "sg_b"})
    rep_sizes = [wl[n].size for n in REPLICATED]
    rep_rows = -(-sum(rep_sizes) // (8 * LANES)) * 8

    def packed(d):
        flat = jnp.concatenate([d[n].reshape(-1) for n in REPLICATED])
        return jnp.pad(flat, (0, rep_rows * LANES - flat.shape[0])).reshape(rep_rows, LANES)

    r_in, r_uq, r_ukv, r_mem, r_out, r_rep = _exchange(
        "exchange_grads", grad_items(0, grads[0], recv) + [(packed(rep_g), False, None)])

    def rows2d(a):
        return a.reshape(-1, a.shape[-1])

    def update(name, contribs):
        outs = _adamw(contribs, rows2d(wl[name]), rows2d(ml[name]), rows2d(vl[name]))
        return [o.reshape(wl[name].shape) for o in outs]

    def from_transposed(recv, keep):
        s = _sum_contribs(recv.reshape(N_DEV, -1, recv.shape[-1])).reshape(recv.shape[1:])
        return rows2d(jnp.swapaxes(s[:, :keep], 1, 2))[None]

    res = dict(
        w_in=update("w_in", from_transposed(r_in, SHARD_IN)),
        w_uq=update("w_uq", from_transposed(r_uq, r_uq.shape[2])),
        w_ukv=update("w_ukv", from_transposed(r_ukv, r_ukv.shape[2])),
        w_mem_k=update("w_mem_k", r_mem[..., :MEM_WIDTH].reshape(N_DEV, -1, MEM_WIDTH)),
        w_mem_v=update("w_mem_v", r_mem[..., MEM_WIDTH:].reshape(N_DEV, -1, MEM_WIDTH)),
        w_out=update("w_out", r_out.reshape(N_DEV, -1, D_MODEL)))
    rep_outs = _adamw(r_rep, packed(wl), packed(ml), packed(vl))
    off = 0
    for n, size in zip(REPLICATED, rep_sizes):
        res[n] = [o.reshape(-1)[off:off + size].reshape(wl[n].shape) for o in rep_outs]
        off += size
    return (loss, grad_x[None], *[res[n][0] for n in WEIGHTS], *[res[n][1] for n in WEIGHTS],
            *[res[n][2] for n in WEIGHTS], *[res[n][3] for n in WEIGHTS])
```

```python
import functools
import math

import jax
import jax.numpy as jnp
from jax import lax
from jax.experimental import pallas as pl
from jax.experimental.pallas import tpu as pltpu

F32 = jnp.float32
MXU = jnp.bfloat16

DEPTH_FOR_NORM = 4
D_MODEL = 2048
CHUNK = 64
MLA_HEADS, MLA_NOPE, MLA_ROPE, MLA_V = 6, 128, 64, 128
MLA_Q_RANK, MLA_KV_RANK = 512, 256
MLA_WIDTH = MLA_HEADS * MLA_V
MLA_HEAD_PAD = 256
ROPE_THETA = 10000.0
SG_GROUPS, SG_CH, SG_CHUNK = 4, 128, 128
SG_WIDTH = SG_GROUPS * SG_CH
SB_HEADS, SB_DIM = 4, 128
SB_WIDTH = SB_HEADS * SB_DIM
MEM_TOKENS, MEM_HEADS, MEM_DIM = 256, 4, 64
MEM_WIDTH = MEM_HEADS * MEM_DIM
D_IN = 5696
D_IN_PAD = 5760
ALPHA = (2.0 * DEPTH_FOR_NORM) ** 0.25
LN_EPS = 1e-5
RMS_EPS = 1e-6
MLA_SCALE = 1.0 / math.sqrt(MLA_NOPE + MLA_ROPE)
SB_SCALE = 1.0 / math.sqrt(SB_DIM)
MEM_SCALE = 1.0 / math.sqrt(MEM_DIM)
LOG2E = math.log2(math.e)
ADAM_LR, ADAM_B1, ADAM_B2, ADAM_EPS, ADAM_WD, ADAM_STEP = 0.001, 0.9, 0.999, 1e-08, 0.01, 10
N_DEV = 8
LANES = 128
NEG = -1e30
SCAN = 256
SB_CUTOFF = 256.0
VMEM_LIMIT = 56 * 1024 * 1024

C_CQ, C_CKV, C_GA, C_SGU, C_SGV, C_GB = 0, 512, 768, 1536, 2048, 2560
C_SBQ, C_SBK, C_SBV, C_GC, C_MQ, C_GM, C_KPE = 3072, 3584, 4096, 4608, 5120, 5376, 5632

REPLICATED = ("q_norm_g", "kv_norm_g", "sg_ln_g", "sg_ln_b", "sg_w", "sg_b", "ln_g", "ln_b")
WEIGHTS = ("w_in", "q_norm_g", "w_uq", "kv_norm_g", "w_ukv", "sg_ln_g", "sg_ln_b", "sg_w", "sg_b",
           "w_mem_k", "w_mem_v", "w_out", "ln_g", "ln_b")


def _params(sem):
    return pltpu.CompilerParams(dimension_semantics=sem, vmem_limit_bytes=VMEM_LIMIT)


def _dot(a, b, dims):
    return lax.dot_general(a.astype(MXU), b.astype(MXU), (dims, ((), ())), preferred_element_type=F32)


NN, NT, TN = ((1,), (0,)), ((1,), (1,)), ((0,), (0,))


@jax.custom_vjp
def _mm(a, b):
    return _dot(a, b, NN)


def _mm_fwd(a, b):
    return _dot(a, b, NN), (a, b)


def _mm_bwd(res, g):
    a, b = res
    return _dot(g, b, NT), _dot(a, g, TN)


_mm.defvjp(_mm_fwd, _mm_bwd)


@jax.custom_vjp
def _mm_nt(a, b):
    return _dot(a, b, NT)


def _mm_nt_fwd(a, b):
    return _dot(a, b, NT), (a, b)


def _mm_nt_bwd(res, g):
    a, b = res
    return _dot(g, b, NN), _dot(g, a, TN)


_mm_nt.defvjp(_mm_nt_fwd, _mm_nt_bwd)


def _split_dot(x, tri):
    hi = x.astype(MXU)
    lo = (x - hi.astype(F32)).astype(MXU)
    return _dot(hi, tri, NN) + _dot(lo, tri, NN)


def _scan_keys(x, tri, reverse):
    w = tri.shape[0]
    nb = x.shape[1] // w
    outs, carry = [None] * nb, None
    for c in (reversed(range(nb)) if reverse else range(nb)):
        xc = x[:, c * w:(c + 1) * w]
        sc = _split_dot(xc, tri)
        outs[c] = sc if carry is None else sc + carry
        rs = jnp.sum(xc, axis=1, keepdims=True)
        carry = rs if carry is None else carry + rs
    return (outs[0] if nb == 1 else jnp.concatenate(outs, axis=1)), carry


def _col2row(col):
    return jnp.transpose(jnp.broadcast_to(col, (col.shape[0], LANES)))[0:8]


def _row2col(row):
    return jnp.transpose(jnp.broadcast_to(row, (LANES, row.shape[1])))[:, 0:1]


def _matmul(name, a, b, mode, tm, tn, tk, out_dtype=F32, add=None, add_scale=1.0, second_dtype=None):
    if mode == "nn":
        (m, k), n = a.shape, b.shape[1]
    elif mode == "nt":
        (m, k), n = a.shape, b.shape[0]
    else:
        (k, m), n = a.shape, b.shape[1]
    tm, tn, tk = min(tm, m), min(tn, n), min(tk, k)
    assert m % tm == 0 and n % tn == 0 and k % tk == 0, (name, m, n, k, tm, tn, tk)
    nk = k // tk
    dims = {"nn": NN, "nt": NT, "tn": TN}[mode]
    a_spec = (pl.BlockSpec((tk, tm), lambda j, i, kk: (kk, i)) if mode == "tn"
              else pl.BlockSpec((tm, tk), lambda j, i, kk: (i, kk)))
    b_spec = (pl.BlockSpec((tn, tk), lambda j, i, kk: (j, kk)) if mode == "nt"
              else pl.BlockSpec((tk, tn), lambda j, i, kk: (kk, j)))
    o_spec = pl.BlockSpec((tm, tn), lambda j, i, kk: (i, j))
    in_specs, operands = [a_spec, b_spec], [a, b]
    if add is not None:
        in_specs.append(o_spec)
        operands.append(add)
    out_shape = [jax.ShapeDtypeStruct((m, n), out_dtype)]
    out_specs = [o_spec]
    if second_dtype is not None:
        out_shape.append(jax.ShapeDtypeStruct((m, n), second_dtype))
        out_specs.append(o_spec)

    def body(*refs):
        a_ref, b_ref = refs[0], refs[1]
        add_ref = refs[2] if add is not None else None
        outs = refs[len(operands):-1]
        acc_ref = refs[-1]
        kk = pl.program_id(2)

        @pl.when(kk == 0)
        def _():
            acc_ref[...] = jnp.zeros_like(acc_ref)

        acc_ref[...] += _dot(a_ref[...], b_ref[...], dims)

        @pl.when(kk == nk - 1)
        def _():
            r = acc_ref[...]
            if add_ref is not None:
                r = r + add_scale * add_ref[...]
            outs[0][...] = r.astype(out_dtype)
            if second_dtype is not None:
                outs[1][...] = r.astype(second_dtype)

    res = pl.pallas_call(
        body, name=name, grid=(n // tn, m // tm, nk), in_specs=in_specs, out_specs=out_specs, out_shape=out_shape,
        scratch_shapes=[pltpu.VMEM((tm, tn), F32)],
        compiler_params=_params(("parallel", "parallel", "arbitrary")),
    )(*operands)
    return res if second_dtype is not None else res[0]


def _accumulate(ref, val, first):
    @pl.when(first)
    def _():
        ref[...] = val

    @pl.when(jnp.logical_not(first))
    def _():
        ref[...] += val


def _rowcall(name, fn, rows, tm, row_in, const_in, row_out, acc_out=()):
    tm = min(tm, rows)
    assert rows % tm == 0
    n_in = len(row_in) + len(const_in)
    in_specs = [pl.BlockSpec((tm, w), lambda i, cb=cb: (i, cb)) for _, w, cb in row_in]
    in_specs += [pl.BlockSpec(c.shape, lambda i, nd=c.ndim: (0,) * nd) for c in const_in]
    out_specs = [pl.BlockSpec(blk, imap) for _, _, blk, imap in row_out]
    out_specs += [pl.BlockSpec(s, lambda i, nd=len(s): (0,) * nd) for s, _ in acc_out]
    out_shape = [jax.ShapeDtypeStruct(s, dt) for s, dt, _, _ in row_out]
    out_shape += [jax.ShapeDtypeStruct(s, dt) for s, dt in acc_out]

    def body(*refs):
        vals = fn(*[r[...] for r in refs[:n_in]])
        outs = refs[n_in:]
        first = pl.program_id(0) == 0
        for k, (_, dt, _, _) in enumerate(row_out):
            outs[k][...] = vals[k].astype(dt)
        for k in range(len(acc_out)):
            _accumulate(outs[len(row_out) + k], vals[len(row_out) + k], first)

    return pl.pallas_call(
        body, name=name, grid=(rows // tm,), in_specs=in_specs, out_specs=out_specs, out_shape=out_shape,
        compiler_params=_params(("arbitrary",) if acc_out else ("parallel",)),
    )(*[a for a, _, _ in row_in], *const_in)


def _ro(rows, width, dtype, tm):
    tm = min(tm, rows)
    return ((rows, width), dtype, (tm, width), lambda i: (i, 0))


def _rms(x, g):
    ms = jnp.mean(x * x, axis=-1, keepdims=True)
    return x * lax.rsqrt(ms + RMS_EPS) * g


def _ln(x, g, b):
    mu = jnp.mean(x, axis=-1, keepdims=True)
    xc = x - mu
    var = jnp.mean(xc * xc, axis=-1, keepdims=True)
    return xc * lax.rsqrt(var + LN_EPS) * g + b


def _gelu(x):
    return 0.5 * x * (1.0 + jnp.tanh(math.sqrt(2.0 / math.pi) * (x + 0.044715 * (x * x * x))))


def _silu(x):
    return x / (1.0 + jnp.exp(-x))


@jax.custom_vjp
def _rope(r, cc, sa, sb):
    return r * cc + pltpu.roll(r, 96, 1) * sa + pltpu.roll(r, 32, 1) * sb


def _rope_fwd(r, cc, sa, sb):
    return _rope(r, cc, sa, sb), (cc, sa, sb)


def _rope_bwd(res, d):
    cc, sa, sb = res
    return d * cc + pltpu.roll(d * sa, 32, 1) + pltpu.roll(d * sb, 96, 1), None, None, None


_rope.defvjp(_rope_fwd, _rope_bwd)


def _rope_tables_fn(pos, invf):
    ang = pos.astype(F32) * invf
    lane = lax.broadcasted_iota(jnp.int32, ang.shape, 1)
    cos, sin = jnp.cos(ang), jnp.sin(ang)
    zero = jnp.zeros_like(ang)
    cc = jnp.where(lane < 64, cos, zero)
    sa = jnp.where(lane < 32, -sin, zero)
    sb = jnp.where((lane >= 32) & (lane < 64), sin, zero)
    return cc, sa, sb


def _prep_fn(cq, ckv, kpe, cc, sa, sb, gq, wuq, gkv, wukv):
    q = _mm_nt(_rms(cq, gq), wuq)
    kv = _mm_nt(_rms(ckv, gkv), wukv)
    krot = _rope(kpe, cc, sa, sb)
    qs, ks = [], []
    for h in range(MLA_HEADS):
        o = h * MLA_HEAD_PAD
        qs += [q[:, o:o + 128], _rope(q[:, o + 128:o + 256], cc, sa, sb)]
        ks += [kv[:, h * 128:(h + 1) * 128], krot]
    return jnp.concatenate(qs, axis=1), jnp.concatenate(ks, axis=1), kv[:, MLA_HEADS * 128:]


def _post_fn(ga, sgu, sgv, gb, gc, mq, gm, oa, oc, mkv, lng, lnb, sgw, biasf):
    tm = ga.shape[0]
    u = _gelu(sgu)
    vn = _ln(_gelu(sgv), lng, lnb)
    t_chunk = lax.broadcasted_iota(jnp.int32, (SG_CHUNK, SG_CHUNK), 0) // CHUNK
    s_chunk = lax.broadcasted_iota(jnp.int32, (SG_CHUNK, SG_CHUNK), 1) // CHUNK
    sg_mask = (s_chunk <= t_chunk).astype(F32)
    cols = []
    for g in range(SG_GROUPS):
        wg = sgw[g] * sg_mask
        vg = vn[:, g * SG_CH:(g + 1) * SG_CH]
        cols.append(jnp.concatenate(
            [_mm(wg, vg[n * SG_CHUNK:(n + 1) * SG_CHUNK, :]) for n in range(tm // SG_CHUNK)], axis=0))
    mixed = jnp.concatenate(cols, axis=1) + jnp.concatenate([biasf] * (tm // SG_CHUNK), axis=0)
    ob = u * mixed
    mk, mv = mkv[:, :MEM_WIDTH], mkv[:, MEM_WIDTH:]
    head = lax.broadcasted_iota(jnp.int32, (1, MEM_WIDTH), 1) // MEM_DIM
    om = jnp.zeros((tm, MEM_WIDTH), F32)
    for hd in range(MEM_HEADS):
        hm = (head == hd).astype(F32)
        s = _mm_nt(mq, mk * hm) * MEM_SCALE
        e = jnp.exp(s - lax.stop_gradient(jnp.max(s, axis=-1, keepdims=True)))
        p = e / jnp.sum(e, axis=-1, keepdims=True)
        om = om + _mm(p, mv * hm)
    return jnp.concatenate([oa * _silu(ga), ob * _silu(gb), oc * _silu(gc), om * _silu(gm)], axis=1)


def _post_bwd_fn(dyg, ga, sgu, sgv, gb, gc, mq, gm, oa, oc, mkv, lng, lnb, sgw, biasf):
    _, vjp = jax.vjp(_post_fn, ga, sgu, sgv, gb, gc, mq, gm, oa, oc, mkv, lng, lnb, sgw, biasf)
    dga, dsgu, dsgv, dgb, dgc, dmq, dgm, doa, doc, dmkv, dlng, dlnb, dsgw, dbias = vjp(dyg)
    drows = jnp.stack([_col2row(jnp.sum((doa * oa)[:, h * 128:(h + 1) * 128], axis=1, keepdims=True))
                       for h in range(MLA_HEADS)], axis=0)
    return dga, dsgu, dsgv, dgb, dgc, dmq, dgm, doa, doc, drows, dmkv, dlng, dlnb, dsgw, dbias


def _prep_bwd_fn(dq, dk, dv, cq, ckv, kpe, cc, sa, sb, gq, wuq, gkv, wukv):
    wuq, wukv = wuq.astype(F32), wukv.astype(F32)
    _, vjp = jax.vjp(lambda a, b, c, d, e, f, g: _prep_fn(a, b, c, cc, sa, sb, d, e, f, g), cq, ckv, kpe, gq, wuq, gkv, wukv)
    return vjp((dq, dk, dv))


def _ln_bwd_fn(dxn, z, g, b):
    _, vjp = jax.vjp(_ln, z, g, b)
    dz, dg, db = vjp(dxn)
    return dz, dz, dg, db


def _outproj_fn(yg, x, w, g, b):
    z = ALPHA * x + _dot(yg, w, NN)
    xn = _ln(z, g, b)
    return z, xn, xn


def _loss_fn(xl, tgt):
    err = xl - tgt
    part = 0.5 * jnp.sum(jnp.mean(err * err, axis=-1, keepdims=True), axis=0, keepdims=True)
    return err * (1.0 / D_MODEL), jnp.broadcast_to(part, (1, LANES))


def _mla_fwd(q, k, v, tq, comm=None):
    t = q.shape[0]
    tq = min(tq, t)
    nq = t // tq

    def body(q_ref, k_ref, v_ref, o_ref, lse_ref):
        i = pl.program_id(1)
        qb = q_ref[...]
        r_chunk = lax.broadcasted_iota(jnp.int32, (tq, tq), 0) // CHUNK
        c_chunk = lax.broadcasted_iota(jnp.int32, (tq, tq), 1) // CHUNK
        diag_ok = c_chunk <= r_chunk

        def block(j, carry, masked):
            m, l, acc = carry
            off = pl.multiple_of(j * tq, tq)
            s = _dot(qb, k_ref[pl.ds(off, tq), :], NT) * (MLA_SCALE * LOG2E)
            if masked:
                s = jnp.where(diag_ok, s, NEG)
            m2 = jnp.maximum(m, jnp.max(s, axis=1, keepdims=True))
            p = jnp.exp2(s - m2)
            a = jnp.exp2(m - m2)
            return m2, a * l + jnp.sum(p, axis=1, keepdims=True), a * acc + _dot(p, v_ref[pl.ds(off, tq), :], NN)

        init = (jnp.full((tq, 1), NEG, F32), jnp.zeros((tq, 1), F32), jnp.zeros((tq, MLA_V), F32))
        carry = lax.fori_loop(0, i, lambda j, c: block(j, c, False), init)
        m, l, acc = block(i, carry, True)
        o_ref[...] = acc / l
        lse_ref[...] = _col2row(m + jnp.log2(l))

    (o, lse), received = _comm_call(
        body, "mla_fwd", (MLA_HEADS, nq),
        [pl.BlockSpec((tq, MLA_HEAD_PAD), lambda h, i: (i, h)),
         pl.BlockSpec((t, MLA_HEAD_PAD), lambda h, i: (0, h)),
         pl.BlockSpec((t, MLA_V), lambda h, i: (0, h))],
        [pl.BlockSpec((tq, MLA_V), lambda h, i: (i, h)),
         pl.BlockSpec((None, 8, tq), lambda h, i: (h, 0, i))],
        [jax.ShapeDtypeStruct((t, MLA_WIDTH), F32), jax.ShapeDtypeStruct((MLA_HEADS, 8, t), F32)],
        [q, k, v], ("parallel", "parallel"), comm)
    return o, lse, received


def _mla_bwd(q, k, v, do, lse, drow, tq, comm=None):
    t = q.shape[0]
    tq = min(tq, t)
    nq = t // tq

    def body(k_ref, v_ref, q_ref, do_ref, lse_ref, d_ref, dk_ref, dv_ref, dq_ref):
        j = pl.program_id(1)

        @pl.when(j == 0)
        def _():
            dq_ref[...] = jnp.zeros_like(dq_ref)

        kj, vj = k_ref[...], v_ref[...]
        k_chunk = lax.broadcasted_iota(jnp.int32, (tq, tq), 0) // CHUNK
        q_chunk = lax.broadcasted_iota(jnp.int32, (tq, tq), 1) // CHUNK
        diag_ok = k_chunk <= q_chunk

        def block(i, carry, masked):
            dk, dv = carry
            off = pl.multiple_of(i * tq, tq)
            qi, doi = q_ref[pl.ds(off, tq), :], do_ref[pl.ds(off, tq), :]
            st = _dot(kj, qi, NT) * (MLA_SCALE * LOG2E)
            pt = jnp.exp2(st - lse_ref[0:1, pl.ds(off, tq)])
            if masked:
                pt = jnp.where(diag_ok, pt, 0.0)
            dv = dv + _dot(pt, doi, NN)
            dst = (pt * (_dot(vj, doi, NT) - d_ref[0:1, pl.ds(off, tq)]) * MLA_SCALE).astype(MXU)
            dk = dk + _dot(dst, qi, NN)
            dq_ref[pl.ds(off, tq), :] += _dot(dst, kj, TN)
            return dk, dv

        carry = block(j, (jnp.zeros((tq, MLA_HEAD_PAD), F32), jnp.zeros((tq, MLA_V), F32)), True)
        dk, dv = lax.fori_loop(j + 1, nq, lambda i, c: block(i, c, False), carry)
        dk_ref[...] = dk
        dv_ref[...] = dv

    (dk, dv, dq), received = _comm_call(
        body, "mla_bwd", (MLA_HEADS, nq),
        [pl.BlockSpec((tq, MLA_HEAD_PAD), lambda h, j: (j, h)),
         pl.BlockSpec((tq, MLA_V), lambda h, j: (j, h)),
         pl.BlockSpec((t, MLA_HEAD_PAD), lambda h, j: (0, h)),
         pl.BlockSpec((t, MLA_V), lambda h, j: (0, h)),
         pl.BlockSpec((None, 8, t), lambda h, j: (h, 0, 0)),
         pl.BlockSpec((None, 8, t), lambda h, j: (h, 0, 0))],
        [pl.BlockSpec((tq, MLA_HEAD_PAD), lambda h, j: (j, h)),
         pl.BlockSpec((tq, MLA_V), lambda h, j: (j, h)),
         pl.BlockSpec((t, MLA_HEAD_PAD), lambda h, j: (0, h))],
        [jax.ShapeDtypeStruct((t, MLA_HEADS * MLA_HEAD_PAD), F32), jax.ShapeDtypeStruct((t, MLA_WIDTH), F32),
         jax.ShapeDtypeStruct((t, MLA_HEADS * MLA_HEAD_PAD), F32)],
        [k, v, q, do, lse, drow], ("parallel", "arbitrary"), comm)
    return dq, dk, dv, received


def _sb_logs(z2):
    lb = jnp.minimum(z2, 0.0) - jnp.log2(1.0 + jnp.exp2(jnp.minimum(z2, -z2)))
    return lb, lb - z2


def _tri(n, keep):
    r = lax.broadcasted_iota(jnp.int32, (n, n), 0)
    c = lax.broadcasted_iota(jnp.int32, (n, n), 1)
    return keep(r, c)


def _sb_fwd(hb, tq):
    t = hb.shape[0]
    tq = min(tq, t)
    nq = t // tq
    cq, ck, cv = C_SBQ // SB_DIM, C_SBK // SB_DIM, C_SBV // SB_DIM

    def body(q_ref, k_ref, v_ref, o_ref, tot_ref, first_ref):
        h, i = pl.program_id(0), pl.program_id(1)
        qb = q_ref[...]
        strict = _tri(tq, lambda r, c: c < r)
        later = _tri(min(SCAN, tq), lambda r, c: r > c).astype(MXU)

        def block(j, carry, masked):
            run, acc = carry
            off = pl.multiple_of(j * tq, tq)
            lb, l1 = _sb_logs(_dot(qb, k_ref[pl.ds(off, tq), :], NT) * (SB_SCALE * LOG2E))
            if masked:
                l1 = jnp.where(strict, l1, 0.0)
            after, rowsum = _scan_keys(l1, later, True)
            a = jnp.exp2(lb + after + run)
            if masked:
                a = jnp.where(strict, a, 0.0)
            return run + rowsum, acc + _dot(a, v_ref[pl.ds(off, tq), :], NN)

        carry = block(i, (jnp.zeros((tq, 1), F32), jnp.zeros((tq, SB_DIM), F32)), True)

        def live(state):
            jj, (run, _) = state
            return jnp.logical_and(jj < i, jnp.max(run) > -SB_CUTOFF)

        jj, (run, acc) = lax.while_loop(live, lambda st: (st[0] + 1, block(i - 1 - st[0], st[1], False)),
                                        (jnp.int32(0), carry))
        o_ref[...] = acc
        tot_ref[...] = _col2row(run)
        first_ref[h, i] = i - jj

    return _comm_call(
        body, "sb_fwd", (SB_HEADS, nq),
        [pl.BlockSpec((tq, SB_DIM), lambda h, i: (i, cq + h)),
         pl.BlockSpec((t, SB_DIM), lambda h, i: (0, ck + h)),
         pl.BlockSpec((t, SB_DIM), lambda h, i: (0, cv + h))],
        [pl.BlockSpec((tq, SB_DIM), lambda h, i: (i, h)),
         pl.BlockSpec((None, 8, tq), lambda h, i: (h, 0, i)),
         pl.BlockSpec(memory_space=pltpu.SMEM)],
        [jax.ShapeDtypeStruct((t, SB_WIDTH), F32), jax.ShapeDtypeStruct((SB_HEADS, 8, t), F32),
         jax.ShapeDtypeStruct((SB_HEADS, nq), jnp.int32)],
        [hb, hb, hb], ("arbitrary", "arbitrary"), None)[0]


def _sb_bwd(hb, do, tot, first, tq):
    t = hb.shape[0]
    tq = min(tq, t)
    nq = t // tq
    cq, ck, cv = C_SBQ // SB_DIM, C_SBK // SB_DIM, C_SBV // SB_DIM

    def body(q_ref, do_ref, tot_ref, k_ref, v_ref, first_ref, dq_ref, dk_ref, dv_ref):
        h, i = pl.program_id(0), pl.program_id(1)

        @pl.when(i == 0)
        def _():
            dk_ref[...] = jnp.zeros_like(dk_ref)
            dv_ref[...] = jnp.zeros_like(dv_ref)

        qb, dob = q_ref[...], do_ref[...]
        total = _row2col(tot_ref[0:1, :])
        strict = _tri(tq, lambda r, c: c < r)
        upto = _tri(min(SCAN, tq), lambda r, c: r <= c).astype(MXU)
        before = _tri(min(SCAN, tq), lambda r, c: r < c).astype(MXU)

        def block(j, carry, masked):
            pre, gpre, dq = carry
            off = pl.multiple_of(j * tq, tq)
            kj, vj = k_ref[pl.ds(off, tq), :], v_ref[pl.ds(off, tq), :]
            lb, l1 = _sb_logs(_dot(qb, kj, NT) * (SB_SCALE * LOG2E))
            if masked:
                l1 = jnp.where(strict, l1, 0.0)
            upto_sum, rowsum = _scan_keys(l1, upto, False)
            a = jnp.exp2(lb + (total - (upto_sum + pre)))
            if masked:
                a = jnp.where(strict, a, 0.0)
            g = a * _dot(dob, vj, NT)
            before_sum, grow = _scan_keys(g, before, False)
            beta = jnp.exp2(lb)
            dz = (g * (1.0 - beta) - (before_sum + gpre) * beta) * SB_SCALE
            if masked:
                dz = jnp.where(strict, dz, 0.0)
            dzb = dz.astype(MXU)
            dk_ref[pl.ds(off, tq), :] += _dot(dzb, qb, TN)
            dv_ref[pl.ds(off, tq), :] += _dot(a, dob, TN)
            return pre + rowsum, gpre + grow, dq + _dot(dzb, kj, NN)

        init = (jnp.zeros((tq, 1), F32), jnp.zeros((tq, 1), F32), jnp.zeros((tq, SB_DIM), F32))
        carry = lax.fori_loop(first_ref[h, i], i, lambda j, c: block(j, c, False), init)
        dq_ref[...] = block(i, carry, True)[2].astype(dq_ref.dtype)

    return _comm_call(
        body, "sb_bwd", (SB_HEADS, nq),
        [pl.BlockSpec((tq, SB_DIM), lambda h, i: (i, cq + h)),
         pl.BlockSpec((tq, SB_DIM), lambda h, i: (i, h)),
         pl.BlockSpec((None, 8, tq), lambda h, i: (h, 0, i)),
         pl.BlockSpec((t, SB_DIM), lambda h, i: (0, ck + h)),
         pl.BlockSpec((t, SB_DIM), lambda h, i: (0, cv + h)),
         pl.BlockSpec(memory_space=pltpu.SMEM)],
        [pl.BlockSpec((tq, SB_DIM), lambda h, i: (i, h)),
         pl.BlockSpec((t, SB_DIM), lambda h, i: (0, h)),
         pl.BlockSpec((t, SB_DIM), lambda h, i: (0, h))],
        [jax.ShapeDtypeStruct((t, SB_WIDTH), MXU), jax.ShapeDtypeStruct((t, SB_WIDTH), F32),
         jax.ShapeDtypeStruct((t, SB_WIDTH), F32)],
        [hb, do, tot, hb, hb, first], ("parallel", "arbitrary"), None)[0]


TM = 256
TQ_MLA = 1024
TQ_SB = 512


def _layer_fwd(xf, xb, w, consts, comm=None):
    t = xf.shape[0]
    cc, sa, sb, mem = consts
    h, hb = _matmul("in_proj", xb, w["w_in"], "nt", 512, 1152, 2048, second_dtype=MXU)
    gq, gkv = w["q_norm_g"][None, :], w["kv_norm_g"][None, :]
    q, k, v = _rowcall(
        "mla_prep", _prep_fn, t, TM,
        [(h, 512, C_CQ // 512), (h, 256, C_CKV // 256), (h, 128, C_KPE // 128), (cc, 128, 0), (sa, 128, 0), (sb, 128, 0)],
        [gq, w["w_uq"], gkv, w["w_ukv"]],
        [_ro(t, MLA_HEADS * MLA_HEAD_PAD, MXU, TM), _ro(t, MLA_HEADS * MLA_HEAD_PAD, MXU, TM), _ro(t, MLA_WIDTH, MXU, TM)])
    oa, lse, received = _mla_fwd(q, k, v, TQ_MLA, comm)
    oc, tot, first = _sb_fwd(hb, TQ_SB)
    mkv = _matmul("mem_kv", mem, w["w_mem_kv"], "nn", 256, 512, 2048)
    lng, lnb = w["sg_ln_g"][None, :], w["sg_ln_b"][None, :]
    biasf = jnp.repeat(w["sg_b"].T, SG_CH, axis=1)
    post_rows = [(h, 768, C_GA // 768), (h, 512, C_SGU // 512), (h, 512, C_SGV // 512), (h, 512, C_GB // 512),
                 (h, 512, C_GC // 512), (h, 256, C_MQ // 256), (h, 256, C_GM // 256), (oa, 768, 0), (oc, 512, 0)]
    post_consts = [mkv, lng, lnb, w["sg_w"], biasf]
    (yg,) = _rowcall("mixer_post", lambda *a: (_post_fn(*a),), t, TM, post_rows, post_consts, [_ro(t, D_MODEL, MXU, TM)])
    z, xn, xnb = _rowcall(
        "out_proj_ln", _outproj_fn, t, TM, [(yg, D_MODEL, 0), (xf, D_MODEL, 0)],
        [w["w_out"], w["ln_g"][None, :], w["ln_b"][None, :]],
        [_ro(t, D_MODEL, F32, TM), _ro(t, D_MODEL, F32, TM), _ro(t, D_MODEL, MXU, TM)])
    saved = dict(xb=xb, h=h, hb=hb, q=q, k=k, v=v, oa=oa, lse=lse, oc=oc, tot=tot, first=first, mkv=mkv, yg=yg, z=z,
                 post_rows=post_rows, post_consts=post_consts, gq=gq, gkv=gkv)
    return xn, xnb, saved, received


def _layer_bwd(dxn, w, s, consts, comm=None):
    t = dxn.shape[0]
    cc, sa, sb, mem = consts
    h = s["h"]
    dz, dzb, dlng2, dlnb2 = _rowcall(
        "ln_bwd", _ln_bwd_fn, t, TM, [(dxn, D_MODEL, 0), (s["z"], D_MODEL, 0)],
        [w["ln_g"][None, :], w["ln_b"][None, :]],
        [_ro(t, D_MODEL, F32, TM), _ro(t, D_MODEL, MXU, TM)], [((1, D_MODEL), F32), ((1, D_MODEL), F32)])
    dyg = _matmul("out_proj_dx", dzb, w["w_out"], "nt", 512, 1024, 2048)
    dw_out = _matmul("out_proj_dw", s["yg"], dzb, "tn", 1024, 1024, 512)
    tm = min(TM, t)
    outs = _rowcall(
        "mixer_post_bwd", _post_bwd_fn, t, TM, [(dyg, D_MODEL, 0)] + s["post_rows"], s["post_consts"],
        [_ro(t, 768, MXU, TM), _ro(t, 512, MXU, TM), _ro(t, 512, MXU, TM), _ro(t, 512, MXU, TM), _ro(t, 512, MXU, TM),
         _ro(t, 256, MXU, TM), _ro(t, 256, MXU, TM), _ro(t, 768, MXU, TM), _ro(t, 512, MXU, TM),
         ((MLA_HEADS, 8, t), F32, (MLA_HEADS, 8, tm), lambda i: (0, 0, i))],
        [((MEM_TOKENS, 2 * MEM_WIDTH), F32), ((1, SG_WIDTH), F32), ((1, SG_WIDTH), F32),
         ((SG_GROUPS, SG_CHUNK, SG_CHUNK), F32), ((SG_CHUNK, SG_WIDTH), F32)])
    dga, dsgu, dsgv, dgb, dgc, dmq, dgm, doa, doc, drow, dmkv, dsg_ln_g, dsg_ln_b, dsgw, dbias = outs
    dw_mem_kv = _matmul("mem_kv_dw", mem, dmkv, "tn", 1024, 512, 256)
    dsbq, dsbk, dsbv = _sb_bwd(s["hb"], doc, s["tot"], s["first"], TQ_SB)
    dq, dk, dv, received = _mla_bwd(s["q"], s["k"], s["v"], doa, s["lse"], drow, TQ_MLA, comm)
    wq, wkv = MLA_HEADS * MLA_HEAD_PAD, MLA_WIDTH
    dcq, dckv, dkpe, dgq, dwuq, dgkv, dwukv = _rowcall(
        "mla_prep_bwd", _prep_bwd_fn, t, TM,
        [(dq, wq, 0), (dk, wq, 0), (dv, wkv, 0), (h, 512, C_CQ // 512), (h, 256, C_CKV // 256), (h, 128, C_KPE // 128),
         (cc, 128, 0), (sa, 128, 0), (sb, 128, 0)],
        [s["gq"], w["w_uq"], s["gkv"], w["w_ukv"]],
        [_ro(t, 512, MXU, TM), _ro(t, 256, MXU, TM), _ro(t, 128, MXU, TM)],
        [((1, MLA_Q_RANK), F32), ((wq, MLA_Q_RANK), F32), ((1, MLA_KV_RANK), F32), ((2 * wkv, MLA_KV_RANK), F32)])
    dh = jnp.concatenate([dcq, dckv, dga, dsgu, dsgv, dgb, dsbq, dsbk.astype(MXU), dsbv.astype(MXU), dgc, dmq, dgm, dkpe], axis=1)
    dx = _matmul("in_proj_dx", dh, w["w_in"], "nn", 512, 1024, 1152, add=dz, add_scale=ALPHA)
    dw_in = _matmul("in_proj_dw", dh, s["xb"], "tn", 1152, 1024, 512)
    grads = dict(w_in=dw_in, q_norm_g=dgq[0], w_uq=dwuq, kv_norm_g=dgkv[0], w_ukv=dwukv, sg_ln_g=dsg_ln_g[0],
                 sg_ln_b=dsg_ln_b[0], sg_w=dsgw, sg_bias=dbias, w_mem_kv=dw_mem_kv, w_out=dw_out, ln_g=dlng2[0], ln_b=dlnb2[0])
    return dx, grads, received


def _local_step(x, mem, positions, target, depth, weights_of, gather_comm, grad_comm, recv):
    t = x.shape[0]
    inv_freq = ROPE_THETA ** (-jnp.arange(0, MLA_ROPE, 2, dtype=F32) / MLA_ROPE)
    invf = jnp.tile(inv_freq, 4)[None, :]
    cc, sa, sb = _rowcall("rope_tables", _rope_tables_fn, t, TM, [(positions[:, None], 1, 0)], [invf],
                          [_ro(t, 128, F32, TM)] * 3)
    consts = (cc, sa, sb, mem)
    xf, xb, saved, weights, received = x, x.astype(MXU), [], [], None
    for l in range(depth):
        weights.append(weights_of(l, received))
        xf, xb, s, received = _layer_fwd(xf, xb, weights[l], consts, gather_comm(l + 1) if l + 1 < depth else None)
        saved.append(s)
    dx, loss_part = _rowcall("loss", _loss_fn, t, TM, [(xf, D_MODEL, 0), (target, D_MODEL, 0)], [],
                             [_ro(t, D_MODEL, F32, TM)], [((1, LANES), F32)])
    grads = [None] * depth
    for l in reversed(range(depth)):
        comm = grad_comm(l + 1, grads[l + 1], recv) if l + 1 < depth else None
        dx, grads[l], received = _layer_bwd(dx, weights[l], saved[l], consts, comm)
        if comm is not None:
            recv = received
    db = jnp.concatenate([g["sg_bias"] for g in grads], axis=0)

    def lane_sums(d):
        return (jnp.concatenate([jnp.broadcast_to(jnp.sum(d[:, g * SG_CH:(g + 1) * SG_CH], axis=1, keepdims=True),
                                                  (d.shape[0], SG_CH)) for g in range(SG_GROUPS)], axis=1),)

    (dbs,) = _rowcall("sg_bias_sum", lane_sums, db.shape[0], SG_CHUNK, [(db, SG_WIDTH, 0)], [], [_ro(db.shape[0], SG_WIDTH, F32, SG_CHUNK)])
    dsgb = dbs[:, ::SG_CH].reshape(depth, SG_CHUNK, SG_GROUPS).transpose(0, 2, 1)
    return loss_part[0, 0], dx, grads, dsgb, recv


SHARD_IN = D_IN // N_DEV
SHARD_IN_PAD = D_IN_PAD // N_DEV
IN_RANGES = ((0, 768), (832, D_IN), (768, 832))


def _in_rows_from_shards(w):
    pieces = []
    for a, b in IN_RANGES:
        while a < b:
            e = min(b, (a // SHARD_IN + 1) * SHARD_IN)
            row = (a // SHARD_IN) * SHARD_IN_PAD + a % SHARD_IN
            pieces.append(w[row:row + e - a])
            a = e
    pieces.append(jnp.zeros((D_IN_PAD - D_IN, w.shape[1]), w.dtype))
    return jnp.concatenate(pieces, axis=0)


def _in_rows_to_shards(g):
    pieces = []
    for d in range(N_DEV):
        a, b = d * SHARD_IN, (d + 1) * SHARD_IN
        while a < b:
            e = min(b, 768 if a < 768 else (832 if a < 832 else D_IN))
            row = a if a < 768 else (C_KPE + a - 768 if a < 832 else a - 64)
            pieces.append(g[row:row + e - a])
            a = e
        pieces.append(jnp.zeros((SHARD_IN_PAD - SHARD_IN, g.shape[1]), g.dtype))
    return jnp.concatenate(pieces, axis=0)


def _uq_rows_pad(w):
    w = w.reshape(MLA_HEADS, MLA_NOPE + MLA_ROPE, w.shape[-1])
    return jnp.pad(w, ((0, 0), (0, MLA_HEAD_PAD - MLA_NOPE - MLA_ROPE), (0, 0))).reshape(MLA_HEADS * MLA_HEAD_PAD, -1)


def _uq_rows_unpad(g):
    return g.reshape(MLA_HEADS, MLA_HEAD_PAD, -1)[:, :MLA_NOPE + MLA_ROPE].reshape(MLA_HEADS * (MLA_NOPE + MLA_ROPE), -1)


def _ukv_rows_split(w):
    return jnp.swapaxes(w.reshape(MLA_HEADS, 2, 128, -1), 0, 1).reshape(MLA_HEADS * 256, -1)


def _ukv_rows_unsplit(g):
    return jnp.swapaxes(g.reshape(2, MLA_HEADS, 128, -1), 0, 1).reshape(MLA_HEADS * 256, -1)


class _Comm:
    def __init__(self, items):
        self.items = items
        self.n = len(items)
        self.pieces = [((s.shape[0] // N_DEV,) + s.shape[1:]) if sc else s.shape for s, sc, _ in items]
        self.inputs = [s for s, _, _ in items] + [d[0] for _, _, d in items if d is not None]
        any_space = pl.BlockSpec(memory_space=pl.ANY)
        self.in_specs = [any_space] * len(self.inputs)
        self.out_specs = [any_space] * self.n
        self.out_shape = [jax.ShapeDtypeStruct((N_DEV,) + tuple(p), s.dtype) if d is None
                          else jax.ShapeDtypeStruct(d[0].shape, d[0].dtype) for p, (s, _, d) in zip(self.pieces, items)]
        self.scratch = [pltpu.SemaphoreType.DMA((N_DEV - 1, self.n)), pltpu.SemaphoreType.DMA((N_DEV - 1, self.n)),
                        pltpu.SemaphoreType.DMA((self.n,))]

    def aliases(self, first_in, first_out):
        passed = [a for a, (_, _, d) in enumerate(self.items) if d is not None]
        return {first_in + self.n + b: first_out + a for b, a in enumerate(passed)}

    def copies(self, in_refs, out_refs, sems):
        send_sems, recv_sems, local_sems = sems
        x, y, c = lax.axis_index("x"), lax.axis_index("y"), lax.axis_index("c")
        me = 4 * x + 2 * y + c

        def block_for(a, p):
            if not self.items[a][1]:
                return in_refs[a]
            n = self.pieces[a][0]
            return in_refs[a].at[pl.ds(pl.multiple_of(p * n, 8), n)]

        def slot(a, dev):
            dst = self.items[a][2]
            return out_refs[a].at[dev] if dst is None else out_refs[a].at[dev, dst[1]]

        starts, recv_waits, send_waits = [], [], []
        for a in range(self.n):
            cp = pltpu.make_async_copy(block_for(a, me), slot(a, me), local_sems.at[a])
            starts.append(cp.start)
            send_waits.append(cp.wait)
        for k in range(1, N_DEV):
            px = 1 - x if (k >> 2) & 1 else x
            py = 1 - y if (k >> 1) & 1 else y
            pc = 1 - c if k & 1 else c
            peer = 4 * px + 2 * py + pc
            for a in range(self.n):
                out = pltpu.make_async_remote_copy(
                    src_ref=block_for(a, peer), dst_ref=slot(a, me), send_sem=send_sems.at[k - 1, a],
                    recv_sem=recv_sems.at[k - 1, a], device_id=(px, py, pc), device_id_type=pl.DeviceIdType.MESH)
                back = pltpu.make_async_remote_copy(
                    src_ref=block_for(a, peer), dst_ref=slot(a, peer), send_sem=send_sems.at[k - 1, a],
                    recv_sem=recv_sems.at[k - 1, a], device_id=(px, py, pc), device_id_type=pl.DeviceIdType.MESH)
                starts.append(out.start)
                send_waits.append(out.wait_send)
                recv_waits.append(back.wait_recv)
        return starts, recv_waits + send_waits


def _comm_call(body, name, grid, in_specs, out_specs, out_shape, operands, semantics, comm):
    n_in, n_out = len(operands), len(out_shape)
    if comm is None:
        outs = pl.pallas_call(body, name=name, grid=grid, in_specs=in_specs, out_specs=out_specs, out_shape=out_shape,
                              compiler_params=_params(semantics))(*operands)
        return outs, []
    n_ci = len(comm.inputs)

    def fused(*refs):
        ins, c_in = refs[:n_in], refs[n_in:n_in + n_ci]
        outs = refs[n_in + n_ci:n_in + n_ci + n_out]
        c_out = refs[n_in + n_ci + n_out:n_in + n_ci + n_out + comm.n]
        starts, waits = comm.copies(c_in, c_out, refs[n_in + n_ci + n_out + comm.n:])
        first = functools.reduce(jnp.logical_and, [pl.program_id(d) == 0 for d in range(len(grid))])
        last = functools.reduce(jnp.logical_and, [pl.program_id(d) == g - 1 for d, g in enumerate(grid)])

        @pl.when(first)
        def _():
            for start in starts:
                start()

        body(*ins, *outs)

        @pl.when(last)
        def _():
            for wait in waits:
                wait()

    outs = pl.pallas_call(
        fused, name=name, grid=grid, in_specs=list(in_specs) + comm.in_specs, out_specs=list(out_specs) + comm.out_specs,
        out_shape=list(out_shape) + comm.out_shape, scratch_shapes=comm.scratch,
        input_output_aliases=comm.aliases(n_in, n_out), compiler_params=_params(("arbitrary",) * len(grid)),
    )(*operands, *comm.inputs)
    return outs[:n_out], outs[n_out:]


def _exchange(name, items):
    return _comm_call(lambda: None, name, (1,), [], [], [], [], ("arbitrary",), _Comm(items))[1]


BLOCK_BYTES = 6 << 20


def _row_block(rows, row_bytes):
    if rows * row_bytes <= BLOCK_BYTES or rows % 8:
        return rows
    return max(tb for tb in range(8, rows + 1, 8) if rows % tb == 0 and (tb == 8 or tb * row_bytes <= BLOCK_BYTES))


def _sum_contribs(recv):
    n, rows, width = recv.shape
    tb = _row_block(rows, width * 4 * (n + 1))

    def body(r_ref, g_ref):
        g = r_ref[0]
        for k in range(1, n):
            g = g + r_ref[k]
        g_ref[...] = g

    return pl.pallas_call(
        body, name="sum_grads", grid=(rows // tb,), in_specs=[pl.BlockSpec((n, tb, width), lambda i: (0, i, 0))],
        out_specs=pl.BlockSpec((tb, width), lambda i: (i, 0)), out_shape=jax.ShapeDtypeStruct((rows, width), F32),
        compiler_params=_params(("parallel",)),
    )(recv)


def _adamw(contribs, w, m, v):
    n, rows, width = contribs.shape
    tb = _row_block(rows, width * 4 * (n + 7))
    c1 = 1.0 - ADAM_B1 ** ADAM_STEP
    c2 = 1.0 - ADAM_B2 ** ADAM_STEP

    def body(r_ref, w_ref, m_ref, v_ref, g_ref, d_ref, nm_ref, nv_ref):
        g = r_ref[0]
        for k in range(1, n):
            g = g + r_ref[k]
        nm = ADAM_B1 * m_ref[...] + (1.0 - ADAM_B1) * g
        nv = ADAM_B2 * v_ref[...] + (1.0 - ADAM_B2) * (g * g)
        g_ref[...] = g
        nm_ref[...] = nm
        nv_ref[...] = nv
        d_ref[...] = -ADAM_LR * ((nm / c1) / (jnp.sqrt(nv / c2) + ADAM_EPS) + ADAM_WD * w_ref[...])

    blk = pl.BlockSpec((tb, width), lambda i: (i, 0))
    return pl.pallas_call(
        body, name="adamw", grid=(rows // tb,),
        in_specs=[pl.BlockSpec((n, tb, width), lambda i: (0, i, 0)), blk, blk, blk],
        out_specs=[blk] * 4, out_shape=[jax.ShapeDtypeStruct((rows, width), F32)] * 4,
        compiler_params=_params(("parallel",)),
    )(contribs, w, m, v)


def kernel(x, mem, positions, w_in, q_norm_g, w_uq, kv_norm_g, w_ukv, sg_ln_g, sg_ln_b, sg_w, sg_b, w_mem_k, w_mem_v, w_out, ln_g, ln_b, loss_target, m_w_in, m_q_norm_g, m_w_uq, m_kv_norm_g, m_w_ukv, m_sg_ln_g, m_sg_ln_b, m_sg_w, m_sg_b, m_w_mem_k, m_w_mem_v, m_w_out, m_ln_g, m_ln_b, v_w_in, v_q_norm_g, v_w_uq, v_kv_norm_g, v_w_ukv, v_sg_ln_g, v_sg_ln_b, v_sg_w, v_sg_b, v_w_mem_k, v_w_mem_v, v_w_out, v_ln_g, v_ln_b):
    wl = dict(w_in=w_in, q_norm_g=q_norm_g, w_uq=w_uq, kv_norm_g=kv_norm_g, w_ukv=w_ukv, sg_ln_g=sg_ln_g, sg_ln_b=sg_ln_b,
              sg_w=sg_w, sg_b=sg_b, w_mem_k=w_mem_k, w_mem_v=w_mem_v, w_out=w_out, ln_g=ln_g, ln_b=ln_b)
    ml = dict(w_in=m_w_in, q_norm_g=m_q_norm_g, w_uq=m_w_uq, kv_norm_g=m_kv_norm_g, w_ukv=m_w_ukv, sg_ln_g=m_sg_ln_g,
              sg_ln_b=m_sg_ln_b, sg_w=m_sg_w, sg_b=m_sg_b, w_mem_k=m_w_mem_k, w_mem_v=m_w_mem_v, w_out=m_w_out, ln_g=m_ln_g, ln_b=m_ln_b)
    vl = dict(w_in=v_w_in, q_norm_g=v_q_norm_g, w_uq=v_w_uq, kv_norm_g=v_kv_norm_g, w_ukv=v_w_ukv, sg_ln_g=v_sg_ln_g,
              sg_ln_b=v_sg_ln_b, sg_w=v_sg_w, sg_b=v_sg_b, w_mem_k=v_w_mem_k, w_mem_v=v_w_mem_v, w_out=v_w_out, ln_g=v_ln_g, ln_b=v_ln_b)
    depth = w_in.shape[0]

    w_in_t = jnp.pad(jnp.swapaxes(w_in, 1, 2), ((0, 0), (0, SHARD_IN_PAD - SHARD_IN), (0, 0)))
    shards = [s.astype(MXU) for s in (w_in_t, jnp.swapaxes(w_uq, 1, 2), jnp.swapaxes(w_ukv, 1, 2),
                                      jnp.concatenate([w_mem_k, w_mem_v], axis=-1), w_out)]

    def gather_items(l):
        return [(s[l], False, None) for s in shards]

    first = _exchange("gather_weights", gather_items(0))

    def weights_of(l, received):
        g_in, g_uq, g_ukv, g_mem, g_out = first if l == 0 else received
        return dict(w_in=_in_rows_from_shards(g_in.reshape(D_IN_PAD, D_MODEL)), w_uq=_uq_rows_pad(g_uq.reshape(-1, MLA_Q_RANK)),
                    w_ukv=_ukv_rows_split(g_ukv.reshape(-1, MLA_KV_RANK)), w_mem_kv=g_mem.reshape(D_MODEL, 2 * MEM_WIDTH),
                    w_out=g_out.reshape(D_MODEL, D_MODEL), q_norm_g=q_norm_g[l], kv_norm_g=kv_norm_g[l], sg_ln_g=sg_ln_g[l],
                    sg_ln_b=sg_ln_b[l], sg_w=sg_w[l], sg_b=sg_b[l], ln_g=ln_g[l], ln_b=ln_b[l])

    def grad_items(l, g, recv):
        arrays = [_in_rows_to_shards(g["w_in"]), _uq_rows_unpad(g["w_uq"]), _ukv_rows_unsplit(g["w_ukv"]), g["w_mem_kv"], g["w_out"]]
        return [(a, True, (buf, l)) for a, buf in zip(arrays, recv)]

    recv = [lax.empty((N_DEV, depth, rows // N_DEV, width), F32)
            for rows, width in ((D_IN_PAD, D_MODEL), (MLA_HEADS * (MLA_NOPE + MLA_ROPE), MLA_Q_RANK),
                                (MLA_HEADS * 256, MLA_KV_RANK), (D_MODEL, 2 * MEM_WIDTH), (D_MODEL, D_MODEL))]
    loss_part, grad_x, grads, dsgb, recv = _local_step(
        x[0], mem[0], positions[0], loss_target[0], depth, weights_of, lambda l: _Comm(gather_items(l)),
        lambda l, g, bufs: _Comm(grad_items(l, g, bufs)), recv)
    loss = lax.psum(loss_part, ("x", "y", "c"))

    rep_g = dict(sg_b=dsgb, **{n: jnp.stack([g[n] for g in grads]) for n in REPLICATED if n != "sg_b"})
    rep_sizes = [wl[n].size for n in REPLICATED]
    rep_rows = -(-sum(rep_sizes) // (8 * LANES)) * 8

    def packed(d):
        flat = jnp.concatenate([d[n].reshape(-1) for n in REPLICATED])
        return jnp.pad(flat, (0, rep_rows * LANES - flat.shape[0])).reshape(rep_rows, LANES)

    r_in, r_uq, r_ukv, r_mem, r_out, r_rep = _exchange(
        "exchange_grads", grad_items(0, grads[0], recv) + [(packed(rep_g), False, None)])

    def rows2d(a):
        return a.reshape(-1, a.shape[-1])

    def update(name, contribs):
        outs = _adamw(contribs, rows2d(wl[name]), rows2d(ml[name]), rows2d(vl[name]))
        return [o.reshape(wl[name].shape) for o in outs]

    def from_transposed(recv, keep):
        s = _sum_contribs(recv.reshape(N_DEV, -1, recv.shape[-1])).reshape(recv.shape[1:])
        return rows2d(jnp.swapaxes(s[:, :keep], 1, 2))[None]

    res = dict(
        w_in=update("w_in", from_transposed(r_in, SHARD_IN)),
        w_uq=update("w_uq", from_transposed(r_uq, r_uq.shape[2])),
        w_ukv=update("w_ukv", from_transposed(r_ukv, r_ukv.shape[2])),
        w_mem_k=update("w_mem_k", r_mem[..., :MEM_WIDTH].reshape(N_DEV, -1, MEM_WIDTH)),
        w_mem_v=update("w_mem_v", r_mem[..., MEM_WIDTH:].reshape(N_DEV, -1, MEM_WIDTH)),
        w_out=update("w_out", r_out.reshape(N_DEV, -1, D_MODEL)))
    rep_outs = _adamw(r_rep, packed(wl), packed(ml), packed(vl))
    off = 0
    for n, size in zip(REPLICATED, rep_sizes):
        res[n] = [o.reshape(-1)[off:off + size].reshape(wl[n].shape) for o in rep_outs]
        off += size
    return (loss, grad_x[None], *[res[n][0] for n in WEIGHTS], *[res[n][1] for n in WEIGHTS],
            *[res[n][2] for n in WEIGHTS], *[res[n][3] for n in WEIGHTS])
```

```python
import functools
import math

import jax
import jax.numpy as jnp
from jax import lax
from jax.experimental import pallas as pl
from jax.experimental.pallas import tpu as pltpu

F32 = jnp.float32
MXU = jnp.bfloat16

DEPTH_FOR_NORM = 4
D_MODEL = 2048
CHUNK = 64
MLA_HEADS, MLA_NOPE, MLA_ROPE, MLA_V = 6, 128, 64, 128
MLA_Q_RANK, MLA_KV_RANK = 512, 256
MLA_WIDTH = MLA_HEADS * MLA_V
MLA_HEAD_PAD = 256
ROPE_THETA = 10000.0
SG_GROUPS, SG_CH, SG_CHUNK = 4, 128, 128
SG_WIDTH = SG_GROUPS * SG_CH
SB_HEADS, SB_DIM = 4, 128
SB_WIDTH = SB_HEADS * SB_DIM
MEM_TOKENS, MEM_HEADS, MEM_DIM = 256, 4, 64
MEM_WIDTH = MEM_HEADS * MEM_DIM
D_IN = 5696
D_IN_PAD = 5760
ALPHA = (2.0 * DEPTH_FOR_NORM) ** 0.25
LN_EPS = 1e-5
RMS_EPS = 1e-6
MLA_SCALE = 1.0 / math.sqrt(MLA_NOPE + MLA_ROPE)
SB_SCALE = 1.0 / math.sqrt(SB_DIM)
MEM_SCALE = 1.0 / math.sqrt(MEM_DIM)
LOG2E = math.log2(math.e)
ADAM_LR, ADAM_B1, ADAM_B2, ADAM_EPS, ADAM_WD, ADAM_STEP = 0.001, 0.9, 0.999, 1e-08, 0.01, 10
N_DEV = 8
LANES = 128
NEG = -1e30
SCAN = 256
SB_CUTOFF = 256.0
VMEM_LIMIT = 56 * 1024 * 1024

C_CQ, C_CKV, C_GA, C_SGU, C_SGV, C_GB = 0, 512, 768, 1536, 2048, 2560
C_SBQ, C_SBK, C_SBV, C_GC, C_MQ, C_GM, C_KPE = 3072, 3584, 4096, 4608, 5120, 5376, 5632

REPLICATED = ("q_norm_g", "kv_norm_g", "sg_ln_g", "sg_ln_b", "sg_w", "sg_b", "ln_g", "ln_b")
WEIGHTS = ("w_in", "q_norm_g", "w_uq", "kv_norm_g", "w_ukv", "sg_ln_g", "sg_ln_b", "sg_w", "sg_b",
           "w_mem_k", "w_mem_v", "w_out", "ln_g", "ln_b")


def _params(sem):
    return pltpu.CompilerParams(dimension_semantics=sem, vmem_limit_bytes=VMEM_LIMIT)


def _dot(a, b, dims):
    return lax.dot_general(a.astype(MXU), b.astype(MXU), (dims, ((), ())), preferred_element_type=F32)


NN, NT, TN = ((1,), (0,)), ((1,), (1,)), ((0,), (0,))


@jax.custom_vjp
def _mm(a, b):
    return _dot(a, b, NN)


def _mm_fwd(a, b):
    return _dot(a, b, NN), (a, b)


def _mm_bwd(res, g):
    a, b = res
    return _dot(g, b, NT), _dot(a, g, TN)


_mm.defvjp(_mm_fwd, _mm_bwd)


@jax.custom_vjp
def _mm_nt(a, b):
    return _dot(a, b, NT)


def _mm_nt_fwd(a, b):
    return _dot(a, b, NT), (a, b)


def _mm_nt_bwd(res, g):
    a, b = res
    return _dot(g, b, NN), _dot(g, a, TN)


_mm_nt.defvjp(_mm_nt_fwd, _mm_nt_bwd)


def _split_dot(x, tri):
    hi = x.astype(MXU)
    lo = (x - hi.astype(F32)).astype(MXU)
    return _dot(hi, tri, NN) + _dot(lo, tri, NN)


def _scan_keys(x, tri, reverse):
    w = tri.shape[0]
    nb = x.shape[1] // w
    outs, carry = [None] * nb, None
    for c in (reversed(range(nb)) if reverse else range(nb)):
        xc = x[:, c * w:(c + 1) * w]
        sc = _split_dot(xc, tri)
        outs[c] = sc if carry is None else sc + carry
        rs = jnp.sum(xc, axis=1, keepdims=True)
        carry = rs if carry is None else carry + rs
    return (outs[0] if nb == 1 else jnp.concatenate(outs, axis=1)), carry


def _col2row(col):
    return jnp.transpose(jnp.broadcast_to(col, (col.shape[0], LANES)))[0:8]


def _row2col(row):
    return jnp.transpose(jnp.broadcast_to(row, (LANES, row.shape[1])))[:, 0:1]


def _matmul(name, a, b, mode, tm, tn, tk, out_dtype=F32, add=None, add_scale=1.0, second_dtype=None,
            comm=None, with_received=False):
    if mode == "nn":
        (m, k), n = a.shape, b.shape[1]
    elif mode == "nt":
        (m, k), n = a.shape, b.shape[0]
    else:
        (k, m), n = a.shape, b.shape[1]
    tm, tn, tk = min(tm, m), min(tn, n), min(tk, k)
    assert m % tm == 0 and n % tn == 0 and k % tk == 0, (name, m, n, k, tm, tn, tk)
    nk = k // tk
    dims = {"nn": NN, "nt": NT, "tn": TN}[mode]
    a_spec = (pl.BlockSpec((tk, tm), lambda j, i, kk: (kk, i)) if mode == "tn"
              else pl.BlockSpec((tm, tk), lambda j, i, kk: (i, kk)))
    b_spec = (pl.BlockSpec((tn, tk), lambda j, i, kk: (j, kk)) if mode == "nt"
              else pl.BlockSpec((tk, tn), lambda j, i, kk: (kk, j)))
    o_spec = pl.BlockSpec((tm, tn), lambda j, i, kk: (i, j))
    in_specs, operands = [a_spec, b_spec], [a, b]
    if add is not None:
        in_specs.append(o_spec)
        operands.append(add)
    out_shape = [jax.ShapeDtypeStruct((m, n), out_dtype)]
    out_specs = [o_spec]
    if second_dtype is not None:
        out_shape.append(jax.ShapeDtypeStruct((m, n), second_dtype))
        out_specs.append(o_spec)

    def body(*refs):
        a_ref, b_ref = refs[0], refs[1]
        add_ref = refs[2] if add is not None else None
        outs = refs[len(operands):-1]
        acc_ref = refs[-1]
        kk = pl.program_id(2)
        if nk == 1:
            r = _dot(a_ref[...], b_ref[...], dims)
            if add_ref is not None:
                r = r + add_scale * add_ref[...]
            outs[0][...] = r.astype(out_dtype)
            if second_dtype is not None:
                outs[1][...] = r.astype(second_dtype)
            return

        @pl.when(kk == 0)
        def _():
            acc_ref[...] = jnp.zeros_like(acc_ref)

        acc_ref[...] += _dot(a_ref[...], b_ref[...], dims)

        @pl.when(kk == nk - 1)
        def _():
            r = acc_ref[...]
            if add_ref is not None:
                r = r + add_scale * add_ref[...]
            outs[0][...] = r.astype(out_dtype)
            if second_dtype is not None:
                outs[1][...] = r.astype(second_dtype)

    res, received = _comm_call(body, name, (n // tn, m // tm, nk), in_specs, out_specs, out_shape, operands,
                               ("parallel", "parallel", "arbitrary"), comm, scratch=[pltpu.VMEM((tm, tn) if nk > 1 else (8, LANES), F32)])
    res = res if second_dtype is not None else res[0]
    return (res, received) if with_received else res


def _accumulate(ref, val, first):
    @pl.when(first)
    def _():
        ref[...] = val

    @pl.when(jnp.logical_not(first))
    def _():
        ref[...] += val


def _rowcall(name, fn, rows, tm, row_in, const_in, row_out, acc_out=()):
    tm = min(tm, rows)
    assert rows % tm == 0
    n_in = len(row_in) + len(const_in)
    in_specs = [pl.BlockSpec((tm, w), lambda i, cb=cb: (i, cb)) for _, w, cb in row_in]
    in_specs += [pl.BlockSpec(c.shape, lambda i, nd=c.ndim: (0,) * nd) for c in const_in]
    out_specs = [pl.BlockSpec(blk, imap) for _, _, blk, imap in row_out]
    out_specs += [pl.BlockSpec(s, lambda i, nd=len(s): (0,) * nd) for s, _ in acc_out]
    out_shape = [jax.ShapeDtypeStruct(s, dt) for s, dt, _, _ in row_out]
    out_shape += [jax.ShapeDtypeStruct(s, dt) for s, dt in acc_out]

    def body(*refs):
        vals = fn(*[r[...] for r in refs[:n_in]])
        outs = refs[n_in:]
        first = pl.program_id(0) == 0
        for k, (_, dt, _, _) in enumerate(row_out):
            outs[k][...] = vals[k].astype(dt)
        for k in range(len(acc_out)):
            _accumulate(outs[len(row_out) + k], vals[len(row_out) + k], first)

    return pl.pallas_call(
        body, name=name, grid=(rows // tm,), in_specs=in_specs, out_specs=out_specs, out_shape=out_shape,
        compiler_params=_params(("arbitrary",) if acc_out else ("parallel",)),
    )(*[a for a, _, _ in row_in], *const_in)


def _ro(rows, width, dtype, tm):
    tm = min(tm, rows)
    return ((rows, width), dtype, (tm, width), lambda i: (i, 0))


def _rms(x, g):
    ms = jnp.mean(x * x, axis=-1, keepdims=True)
    return x * lax.rsqrt(ms + RMS_EPS) * g


def _ln(x, g, b):
    mu = jnp.mean(x, axis=-1, keepdims=True)
    xc = x - mu
    var = jnp.mean(xc * xc, axis=-1, keepdims=True)
    return xc * lax.rsqrt(var + LN_EPS) * g + b


def _gelu(x):
    return 0.5 * x * (1.0 + jnp.tanh(math.sqrt(2.0 / math.pi) * (x + 0.044715 * (x * x * x))))


def _silu(x):
    return x / (1.0 + jnp.exp(-x))


@jax.custom_vjp
def _rope(r, cc, sa, sb):
    return r * cc + pltpu.roll(r, 96, 1) * sa + pltpu.roll(r, 32, 1) * sb


def _rope_fwd(r, cc, sa, sb):
    return _rope(r, cc, sa, sb), (cc, sa, sb)


def _rope_bwd(res, d):
    cc, sa, sb = res
    return d * cc + pltpu.roll(d * sa, 32, 1) + pltpu.roll(d * sb, 96, 1), None, None, None


_rope.defvjp(_rope_fwd, _rope_bwd)


def _rope_tables_fn(pos, invf):
    ang = pos.astype(F32) * invf
    lane = lax.broadcasted_iota(jnp.int32, ang.shape, 1)
    cos, sin = jnp.cos(ang), jnp.sin(ang)
    zero = jnp.zeros_like(ang)
    cc = jnp.where(lane < 64, cos, zero)
    sa = jnp.where(lane < 32, -sin, zero)
    sb = jnp.where((lane >= 32) & (lane < 64), sin, zero)
    return cc, sa, sb


def _prep_fn(cq, ckv, kpe, cc, sa, sb, gq, wuq, gkv, wukv):
    q = _mm_nt(_rms(cq, gq), wuq)
    kv = _mm_nt(_rms(ckv, gkv), wukv)
    krot = _rope(kpe, cc, sa, sb)
    qs, ks = [], []
    for h in range(MLA_HEADS):
        o = h * MLA_HEAD_PAD
        qs += [q[:, o:o + 128], _rope(q[:, o + 128:o + 256], cc, sa, sb)]
        ks += [kv[:, h * 128:(h + 1) * 128], krot]
    return jnp.concatenate(qs, axis=1), jnp.concatenate(ks, axis=1), kv[:, MLA_HEADS * 128:]


def _post_fn(ga, sgu, sgv, gb, gc, mq, gm, oa, oc, mkv, lng, lnb, sgw, biasf):
    tm = ga.shape[0]
    u = _gelu(sgu)
    vn = _ln(_gelu(sgv), lng, lnb)
    t_chunk = lax.broadcasted_iota(jnp.int32, (SG_CHUNK, SG_CHUNK), 0) // CHUNK
    s_chunk = lax.broadcasted_iota(jnp.int32, (SG_CHUNK, SG_CHUNK), 1) // CHUNK
    sg_mask = (s_chunk <= t_chunk).astype(F32)
    cols = []
    for g in range(SG_GROUPS):
        wg = sgw[g] * sg_mask
        vg = vn[:, g * SG_CH:(g + 1) * SG_CH]
        cols.append(jnp.concatenate(
            [_mm(wg, vg[n * SG_CHUNK:(n + 1) * SG_CHUNK, :]) for n in range(tm // SG_CHUNK)], axis=0))
    mixed = jnp.concatenate(cols, axis=1) + jnp.concatenate([biasf] * (tm // SG_CHUNK), axis=0)
    ob = u * mixed
    mk, mv = mkv[:, :MEM_WIDTH], mkv[:, MEM_WIDTH:]
    head = lax.broadcasted_iota(jnp.int32, (1, MEM_WIDTH), 1) // MEM_DIM
    om = jnp.zeros((tm, MEM_WIDTH), F32)
    for hd in range(MEM_HEADS):
        hm = (head == hd).astype(F32)
        s = _mm_nt(mq, mk * hm) * MEM_SCALE
        e = jnp.exp(s - lax.stop_gradient(jnp.max(s, axis=-1, keepdims=True)))
        p = e / jnp.sum(e, axis=-1, keepdims=True)
        om = om + _mm(p, mv * hm)
    return jnp.concatenate([oa * _silu(ga), ob * _silu(gb), oc * _silu(gc), om * _silu(gm)], axis=1)


def _post_bwd_fn(dyg, ga, sgu, sgv, gb, gc, mq, gm, oa, oc, mkv, lng, lnb, sgw, biasf):
    _, vjp = jax.vjp(_post_fn, ga, sgu, sgv, gb, gc, mq, gm, oa, oc, mkv, lng, lnb, sgw, biasf)
    dga, dsgu, dsgv, dgb, dgc, dmq, dgm, doa, doc, dmkv, dlng, dlnb, dsgw, dbias = vjp(dyg)
    drows = jnp.stack([_col2row(jnp.sum((doa * oa)[:, h * 128:(h + 1) * 128], axis=1, keepdims=True))
                       for h in range(MLA_HEADS)], axis=0)
    return dga, dsgu, dsgv, dgb, dgc, dmq, dgm, doa, doc, drows, dmkv, dlng, dlnb, dsgw, dbias


def _prep_bwd_fn(dq, dk, dv, cq, ckv, kpe, cc, sa, sb, gq, wuq, gkv, wukv):
    wuq, wukv = wuq.astype(F32), wukv.astype(F32)
    _, vjp = jax.vjp(lambda a, b, c, d, e, f, g: _prep_fn(a, b, c, cc, sa, sb, d, e, f, g), cq, ckv, kpe, gq, wuq, gkv, wukv)
    return vjp((dq, dk, dv))


def _ln_bwd_fn(dxn, z, g, b):
    _, vjp = jax.vjp(_ln, z, g, b)
    dz, dg, db = vjp(dxn)
    return dz, dz, dg, db


def _outproj_fn(yg, x, w, g, b):
    z = ALPHA * x + _dot(yg, w, NN)
    xn = _ln(z, g, b)
    return z, xn, xn


def _loss_fn(xl, tgt):
    err = xl - tgt
    part = 0.5 * jnp.sum(jnp.mean(err * err, axis=-1, keepdims=True), axis=0, keepdims=True)
    return err * (1.0 / D_MODEL), jnp.broadcast_to(part, (1, LANES))


def _mla_fwd(q, k, v, tq, comm=None):
    t = q.shape[0]
    tq = min(tq, t)
    nq = t // tq

    def body(q_ref, k_ref, v_ref, o_ref, lse_ref):
        i = pl.program_id(1)
        qb = q_ref[...]
        r_chunk = lax.broadcasted_iota(jnp.int32, (tq, tq), 0) // CHUNK
        c_chunk = lax.broadcasted_iota(jnp.int32, (tq, tq), 1) // CHUNK
        diag_ok = c_chunk <= r_chunk

        def block(j, carry, masked):
            m, l, acc = carry
            off = pl.multiple_of(j * tq, tq)
            s = _dot(qb, k_ref[pl.ds(off, tq), :], NT) * (MLA_SCALE * LOG2E)
            if masked:
                s = jnp.where(diag_ok, s, NEG)
            m2 = jnp.maximum(m, jnp.max(s, axis=1, keepdims=True))
            p = jnp.exp2(s - m2)
            a = jnp.exp2(m - m2)
            return m2, a * l + jnp.sum(p, axis=1, keepdims=True), a * acc + _dot(p, v_ref[pl.ds(off, tq), :], NN)

        init = (jnp.full((tq, 1), NEG, F32), jnp.zeros((tq, 1), F32), jnp.zeros((tq, MLA_V), F32))
        carry = lax.fori_loop(0, i, lambda j, c: block(j, c, False), init)
        m, l, acc = block(i, carry, True)
        o_ref[...] = acc / l
        lse_ref[...] = _col2row(m + jnp.log2(l))

    (o, lse), received = _comm_call(
        body, "mla_fwd", (MLA_HEADS, nq),
        [pl.BlockSpec((tq, MLA_HEAD_PAD), lambda h, i: (i, h)),
         pl.BlockSpec((t, MLA_HEAD_PAD), lambda h, i: (0, h)),
         pl.BlockSpec((t, MLA_V), lambda h, i: (0, h))],
        [pl.BlockSpec((tq, MLA_V), lambda h, i: (i, h)),
         pl.BlockSpec((None, 8, tq), lambda h, i: (h, 0, i))],
        [jax.ShapeDtypeStruct((t, MLA_WIDTH), F32), jax.ShapeDtypeStruct((MLA_HEADS, 8, t), F32)],
        [q, k, v], ("parallel", "parallel"), comm)
    return o, lse, received


def _mla_bwd(q, k, v, do, lse, drow, tq, comm=None):
    t = q.shape[0]
    tq = min(tq, t)
    nq = t // tq

    def body(k_ref, v_ref, q_ref, do_ref, lse_ref, d_ref, dk_ref, dv_ref, dq_ref):
        j = pl.program_id(1)

        @pl.when(j == 0)
        def _():
            dq_ref[...] = jnp.zeros_like(dq_ref)

        kj, vj = k_ref[...], v_ref[...]
        k_chunk = lax.broadcasted_iota(jnp.int32, (tq, tq), 0) // CHUNK
        q_chunk = lax.broadcasted_iota(jnp.int32, (tq, tq), 1) // CHUNK
        diag_ok = k_chunk <= q_chunk

        def block(i, carry, masked):
            dk, dv = carry
            off = pl.multiple_of(i * tq, tq)
            qi, doi = q_ref[pl.ds(off, tq), :], do_ref[pl.ds(off, tq), :]
            st = _dot(kj, qi, NT) * (MLA_SCALE * LOG2E)
            pt = jnp.exp2(st - lse_ref[0:1, pl.ds(off, tq)])
            if masked:
                pt = jnp.where(diag_ok, pt, 0.0)
            dv = dv + _dot(pt, doi, NN)
            dst = (pt * (_dot(vj, doi, NT) - d_ref[0:1, pl.ds(off, tq)]) * MLA_SCALE).astype(MXU)
            dk = dk + _dot(dst, qi, NN)
            dq_ref[pl.ds(off, tq), :] += _dot(dst, kj, TN)
            return dk, dv

        carry = block(j, (jnp.zeros((tq, MLA_HEAD_PAD), F32), jnp.zeros((tq, MLA_V), F32)), True)
        dk, dv = lax.fori_loop(j + 1, nq, lambda i, c: block(i, c, False), carry)
        dk_ref[...] = dk
        dv_ref[...] = dv

    (dk, dv, dq), received = _comm_call(
        body, "mla_bwd", (MLA_HEADS, nq),
        [pl.BlockSpec((tq, MLA_HEAD_PAD), lambda h, j: (j, h)),
         pl.BlockSpec((tq, MLA_V), lambda h, j: (j, h)),
         pl.BlockSpec((t, MLA_HEAD_PAD), lambda h, j: (0, h)),
         pl.BlockSpec((t, MLA_V), lambda h, j: (0, h)),
         pl.BlockSpec((None, 8, t), lambda h, j: (h, 0, 0)),
         pl.BlockSpec((None, 8, t), lambda h, j: (h, 0, 0))],
        [pl.BlockSpec((tq, MLA_HEAD_PAD), lambda h, j: (j, h)),
         pl.BlockSpec((tq, MLA_V), lambda h, j: (j, h)),
         pl.BlockSpec((t, MLA_HEAD_PAD), lambda h, j: (0, h))],
        [jax.ShapeDtypeStruct((t, MLA_HEADS * MLA_HEAD_PAD), F32), jax.ShapeDtypeStruct((t, MLA_WIDTH), F32),
         jax.ShapeDtypeStruct((t, MLA_HEADS * MLA_HEAD_PAD), F32)],
        [k, v, q, do, lse, drow], ("parallel", "arbitrary"), comm)
    return dq, dk, dv, received


def _sb_logs(z2):
    lb = jnp.minimum(z2, 0.0) - jnp.log2(1.0 + jnp.exp2(jnp.minimum(z2, -z2)))
    return lb, lb - z2


def _tri(n, keep):
    r = lax.broadcasted_iota(jnp.int32, (n, n), 0)
    c = lax.broadcasted_iota(jnp.int32, (n, n), 1)
    return keep(r, c)


def _sb_fwd(hb, tq):
    t = hb.shape[0]
    tq = min(tq, t)
    nq = t // tq
    cq, ck, cv = C_SBQ // SB_DIM, C_SBK // SB_DIM, C_SBV // SB_DIM

    def body(q_ref, k_ref, v_ref, o_ref, tot_ref, first_ref):
        h, i = pl.program_id(0), pl.program_id(1)
        qb = q_ref[...]
        strict = _tri(tq, lambda r, c: c < r)
        later = _tri(min(SCAN, tq), lambda r, c: r > c).astype(MXU)

        def block(j, carry, masked):
            run, acc = carry
            off = pl.multiple_of(j * tq, tq)
            lb, l1 = _sb_logs(_dot(qb, k_ref[pl.ds(off, tq), :], NT) * (SB_SCALE * LOG2E))
            if masked:
                l1 = jnp.where(strict, l1, 0.0)
            after, rowsum = _scan_keys(l1, later, True)
            a = jnp.exp2(lb + after + run)
            if masked:
                a = jnp.where(strict, a, 0.0)
            return run + rowsum, acc + _dot(a, v_ref[pl.ds(off, tq), :], NN)

        carry = block(i, (jnp.zeros((tq, 1), F32), jnp.zeros((tq, SB_DIM), F32)), True)

        def live(state):
            jj, (run, _) = state
            return jnp.logical_and(jj < i, jnp.max(run) > -SB_CUTOFF)

        jj, (run, acc) = lax.while_loop(live, lambda st: (st[0] + 1, block(i - 1 - st[0], st[1], False)),
                                        (jnp.int32(0), carry))
        o_ref[...] = acc
        tot_ref[...] = _col2row(run)
        first_ref[h, i] = i - jj

    return _comm_call(
        body, "sb_fwd", (SB_HEADS, nq),
        [pl.BlockSpec((tq, SB_DIM), lambda h, i: (i, cq + h)),
         pl.BlockSpec((t, SB_DIM), lambda h, i: (0, ck + h)),
         pl.BlockSpec((t, SB_DIM), lambda h, i: (0, cv + h))],
        [pl.BlockSpec((tq, SB_DIM), lambda h, i: (i, h)),
         pl.BlockSpec((None, 8, tq), lambda h, i: (h, 0, i)),
         pl.BlockSpec(memory_space=pltpu.SMEM)],
        [jax.ShapeDtypeStruct((t, SB_WIDTH), F32), jax.ShapeDtypeStruct((SB_HEADS, 8, t), F32),
         jax.ShapeDtypeStruct((SB_HEADS, nq), jnp.int32)],
        [hb, hb, hb], ("arbitrary", "arbitrary"), None)[0]


def _sb_bwd(hb, do, tot, first, tq):
    t = hb.shape[0]
    tq = min(tq, t)
    nq = t // tq
    cq, ck, cv = C_SBQ // SB_DIM, C_SBK // SB_DIM, C_SBV // SB_DIM

    def body(q_ref, do_ref, tot_ref, k_ref, v_ref, first_ref, dq_ref, dk_ref, dv_ref):
        h, i = pl.program_id(0), pl.program_id(1)

        @pl.when(i == 0)
        def _():
            dk_ref[...] = jnp.zeros_like(dk_ref)
            dv_ref[...] = jnp.zeros_like(dv_ref)

        qb, dob = q_ref[...], do_ref[...]
        total = _row2col(tot_ref[0:1, :])
        strict = _tri(tq, lambda r, c: c < r)
        upto = _tri(min(SCAN, tq), lambda r, c: r <= c).astype(MXU)
        before = _tri(min(SCAN, tq), lambda r, c: r < c).astype(MXU)

        def block(j, carry, masked):
            pre, gpre, dq = carry
            off = pl.multiple_of(j * tq, tq)
            kj, vj = k_ref[pl.ds(off, tq), :], v_ref[pl.ds(off, tq), :]
            lb, l1 = _sb_logs(_dot(qb, kj, NT) * (SB_SCALE * LOG2E))
            if masked:
                l1 = jnp.where(strict, l1, 0.0)
            upto_sum, rowsum = _scan_keys(l1, upto, False)
            a = jnp.exp2(lb + (total - (upto_sum + pre)))
            if masked:
                a = jnp.where(strict, a, 0.0)
            g = a * _dot(dob, vj, NT)
            before_sum, grow = _scan_keys(g, before, False)
            beta = jnp.exp2(lb)
            dz = (g * (1.0 - beta) - (before_sum + gpre) * beta) * SB_SCALE
            if masked:
                dz = jnp.where(strict, dz, 0.0)
            dzb = dz.astype(MXU)
            dk_ref[pl.ds(off, tq), :] += _dot(dzb, qb, TN)
            dv_ref[pl.ds(off, tq), :] += _dot(a, dob, TN)
            return pre + rowsum, gpre + grow, dq + _dot(dzb, kj, NN)

        init = (jnp.zeros((tq, 1), F32), jnp.zeros((tq, 1), F32), jnp.zeros((tq, SB_DIM), F32))
        carry = lax.fori_loop(first_ref[h, i], i, lambda j, c: block(j, c, False), init)
        dq_ref[...] = block(i, carry, True)[2].astype(dq_ref.dtype)

    return _comm_call(
        body, "sb_bwd", (SB_HEADS, nq),
        [pl.BlockSpec((tq, SB_DIM), lambda h, i: (i, cq + h)),
         pl.BlockSpec((tq, SB_DIM), lambda h, i: (i, h)),
         pl.BlockSpec((None, 8, tq), lambda h, i: (h, 0, i)),
         pl.BlockSpec((t, SB_DIM), lambda h, i: (0, ck + h)),
         pl.BlockSpec((t, SB_DIM), lambda h, i: (0, cv + h)),
         pl.BlockSpec(memory_space=pltpu.SMEM)],
        [pl.BlockSpec((tq, SB_DIM), lambda h, i: (i, h)),
         pl.BlockSpec((t, SB_DIM), lambda h, i: (0, h)),
         pl.BlockSpec((t, SB_DIM), lambda h, i: (0, h))],
        [jax.ShapeDtypeStruct((t, SB_WIDTH), MXU), jax.ShapeDtypeStruct((t, SB_WIDTH), F32),
         jax.ShapeDtypeStruct((t, SB_WIDTH), F32)],
        [hb, do, tot, hb, hb, first], ("parallel", "arbitrary"), None)[0]


TM = 256
TQ_MLA = 1024
TQ_SB = 512


def _layer_fwd(xf, xb, w, consts, comm=None):
    t = xf.shape[0]
    cc, sa, sb, mem = consts
    h, hb = _matmul("in_proj", xb, w["w_in"], "nt", 512, 1920, D_MODEL, second_dtype=MXU)
    gq, gkv = w["q_norm_g"][None, :], w["kv_norm_g"][None, :]
    q, k, v = _rowcall(
        "mla_prep", _prep_fn, t, TM,
        [(h, 512, C_CQ // 512), (h, 256, C_CKV // 256), (h, 128, C_KPE // 128), (cc, 128, 0), (sa, 128, 0), (sb, 128, 0)],
        [gq, w["w_uq"], gkv, w["w_ukv"]],
        [_ro(t, MLA_HEADS * MLA_HEAD_PAD, MXU, TM), _ro(t, MLA_HEADS * MLA_HEAD_PAD, MXU, TM), _ro(t, MLA_WIDTH, MXU, TM)])
    oa, lse, received = _mla_fwd(q, k, v, TQ_MLA, comm)
    oc, tot, first = _sb_fwd(hb, TQ_SB)
    mkv = _matmul("mem_kv", mem, w["w_mem_kv"], "nn", 256, 512, 2048)
    lng, lnb = w["sg_ln_g"][None, :], w["sg_ln_b"][None, :]
    biasf = jnp.repeat(w["sg_b"].T, SG_CH, axis=1)
    post_rows = [(h, 768, C_GA // 768), (h, 512, C_SGU // 512), (h, 512, C_SGV // 512), (h, 512, C_GB // 512),
                 (h, 512, C_GC // 512), (h, 256, C_MQ // 256), (h, 256, C_GM // 256), (oa, 768, 0), (oc, 512, 0)]
    post_consts = [mkv, lng, lnb, w["sg_w"], biasf]
    (yg,) = _rowcall("mixer_post", lambda *a: (_post_fn(*a),), t, TM, post_rows, post_consts, [_ro(t, D_MODEL, MXU, TM)])
    z, xn, xnb = _rowcall(
        "out_proj_ln", _outproj_fn, t, TM, [(yg, D_MODEL, 0), (xf, D_MODEL, 0)],
        [w["w_out"], w["ln_g"][None, :], w["ln_b"][None, :]],
        [_ro(t, D_MODEL, F32, TM), _ro(t, D_MODEL, F32, TM), _ro(t, D_MODEL, MXU, TM)])
    saved = dict(xb=xb, h=h, hb=hb, q=q, k=k, v=v, oa=oa, lse=lse, oc=oc, tot=tot, first=first, mkv=mkv, yg=yg, z=z,
                 post_rows=post_rows, post_consts=post_consts, gq=gq, gkv=gkv)
    return xn, xnb, saved, received


def _layer_bwd(dxn, w, s, consts, recv=None, early_items=None, late_items=None):
    t = dxn.shape[0]
    cc, sa, sb, mem = consts
    h = s["h"]
    dz, dzb, dlng2, dlnb2 = _rowcall(
        "ln_bwd", _ln_bwd_fn, t, TM, [(dxn, D_MODEL, 0), (s["z"], D_MODEL, 0)],
        [w["ln_g"][None, :], w["ln_b"][None, :]],
        [_ro(t, D_MODEL, F32, TM), _ro(t, D_MODEL, MXU, TM)], [((1, D_MODEL), F32), ((1, D_MODEL), F32)])
    dyg = _matmul("out_proj_dx", dzb, w["w_out"], "nt", 512, 1024, 2048)
    dw_out = _matmul("out_proj_dw", s["yg"], dzb, "tn", 1024, 1024, 2048)
    tm = min(TM, t)
    outs = _rowcall(
        "mixer_post_bwd", _post_bwd_fn, t, TM, [(dyg, D_MODEL, 0)] + s["post_rows"], s["post_consts"],
        [_ro(t, 768, MXU, TM), _ro(t, 512, MXU, TM), _ro(t, 512, MXU, TM), _ro(t, 512, MXU, TM), _ro(t, 512, MXU, TM),
         _ro(t, 256, MXU, TM), _ro(t, 256, MXU, TM), _ro(t, 768, MXU, TM), _ro(t, 512, MXU, TM),
         ((MLA_HEADS, 8, t), F32, (MLA_HEADS, 8, tm), lambda i: (0, 0, i))],
        [((MEM_TOKENS, 2 * MEM_WIDTH), F32), ((1, SG_WIDTH), F32), ((1, SG_WIDTH), F32),
         ((SG_GROUPS, SG_CHUNK, SG_CHUNK), F32), ((SG_CHUNK, SG_WIDTH), F32)])
    dga, dsgu, dsgv, dgb, dgc, dmq, dgm, doa, doc, drow, dmkv, dsg_ln_g, dsg_ln_b, dsgw, dbias = outs
    dw_mem_kv = _matmul("mem_kv_dw", mem, dmkv, "tn", 1024, 512, 256)
    dsbq, dsbk, dsbv = _sb_bwd(s["hb"], doc, s["tot"], s["first"], TQ_SB)
    named = early_items(dict(w_out=dw_out, w_mem_kv=dw_mem_kv)) if early_items else []
    dq, dk, dv, received = _mla_bwd(s["q"], s["k"], s["v"], doa, s["lse"], drow, TQ_MLA,
                                    _Comm([item for _, item in named]) if named else None)
    if named:
        recv.update({n: r for (n, _), r in zip(named, received)})
    wq, wkv = MLA_HEADS * MLA_HEAD_PAD, MLA_WIDTH
    dcq, dckv, dkpe, dgq, dwuq, dgkv, dwukv = _rowcall(
        "mla_prep_bwd", _prep_bwd_fn, t, TM,
        [(dq, wq, 0), (dk, wq, 0), (dv, wkv, 0), (h, 512, C_CQ // 512), (h, 256, C_CKV // 256), (h, 128, C_KPE // 128),
         (cc, 128, 0), (sa, 128, 0), (sb, 128, 0)],
        [s["gq"], w["w_uq"], s["gkv"], w["w_ukv"]],
        [_ro(t, 512, MXU, TM), _ro(t, 256, MXU, TM), _ro(t, 128, MXU, TM)],
        [((1, MLA_Q_RANK), F32), ((wq, MLA_Q_RANK), F32), ((1, MLA_KV_RANK), F32), ((2 * wkv, MLA_KV_RANK), F32)])
    dh = jnp.concatenate([dcq, dckv, dga, dsgu, dsgv, dgb, dsbq, dsbk.astype(MXU), dsbv.astype(MXU), dgc, dmq, dgm, dkpe], axis=1)
    dw_in = _matmul("in_proj_dw", dh, s["xb"], "tn", 1152, 1024, 2048)
    named = late_items(dict(w_in=dw_in, w_uq=dwuq, w_ukv=dwukv)) if late_items else []
    dx, received = _matmul("in_proj_dx", dh, w["w_in"], "nn", 512, 1024, D_IN_PAD, add=dz, add_scale=ALPHA,
                           comm=_Comm([item for _, item in named]) if named else None, with_received=True)
    if named:
        recv.update({n: r for (n, _), r in zip(named, received)})
    grads = dict(w_in=dw_in, q_norm_g=dgq[0], w_uq=dwuq, kv_norm_g=dgkv[0], w_ukv=dwukv, sg_ln_g=dsg_ln_g[0],
                 sg_ln_b=dsg_ln_b[0], sg_w=dsgw, sg_bias=dbias, w_mem_kv=dw_mem_kv, w_out=dw_out, ln_g=dlng2[0], ln_b=dlnb2[0])
    return dx, grads


def _local_step(x, mem, positions, target, depth, weights_of, gather_comm, grad_items, recv):
    t = x.shape[0]
    inv_freq = ROPE_THETA ** (-jnp.arange(0, MLA_ROPE, 2, dtype=F32) / MLA_ROPE)
    invf = jnp.tile(inv_freq, 4)[None, :]
    cc, sa, sb = _rowcall("rope_tables", _rope_tables_fn, t, TM, [(positions[:, None], 1, 0)], [invf],
                          [_ro(t, 128, F32, TM)] * 3)
    consts = (cc, sa, sb, mem)
    xf, xb, saved, weights, received = x, x.astype(MXU), [], [], None
    for l in range(depth):
        weights.append(weights_of(l, received))
        xf, xb, s, received = _layer_fwd(xf, xb, weights[l], consts, gather_comm(l + 1) if l + 1 < depth else None)
        saved.append(s)
    dx, loss_part = _rowcall("loss", _loss_fn, t, TM, [(xf, D_MODEL, 0), (target, D_MODEL, 0)], [],
                             [_ro(t, D_MODEL, F32, TM)], [((1, LANES), F32)])
    grads, recv = [None] * depth, dict(recv or {})
    for l in reversed(range(depth)):
        def early(g, l=l):
            rest = {n: grads[l + 1][n] for n in ("w_in", "w_uq", "w_ukv")} if l + 1 < depth else {}
            return grad_items(l, g, recv) + (grad_items(l + 1, rest, recv) if rest else [])

        def late(g):
            return grad_items(0, g, recv)

        dx, grads[l] = _layer_bwd(dx, weights[l], saved[l], consts, recv, early if grad_items else None,
                                  late if grad_items and l == 0 else None)
    db = jnp.concatenate([g["sg_bias"] for g in grads], axis=0)

    def lane_sums(d):
        return (jnp.concatenate([jnp.broadcast_to(jnp.sum(d[:, g * SG_CH:(g + 1) * SG_CH], axis=1, keepdims=True),
                                                  (d.shape[0], SG_CH)) for g in range(SG_GROUPS)], axis=1),)

    (dbs,) = _rowcall("sg_bias_sum", lane_sums, db.shape[0], SG_CHUNK, [(db, SG_WIDTH, 0)], [], [_ro(db.shape[0], SG_WIDTH, F32, SG_CHUNK)])
    dsgb = dbs[:, ::SG_CH].reshape(depth, SG_CHUNK, SG_GROUPS).transpose(0, 2, 1)
    return loss_part[0, 0], dx, grads, dsgb, recv


SHARD_IN = D_IN // N_DEV
SHARD_IN_PAD = D_IN_PAD // N_DEV
IN_RANGES = ((0, 768), (832, D_IN), (768, 832))


def _in_rows_from_shards(w):
    pieces = []
    for a, b in IN_RANGES:
        while a < b:
            e = min(b, (a // SHARD_IN + 1) * SHARD_IN)
            row = (a // SHARD_IN) * SHARD_IN_PAD + a % SHARD_IN
            pieces.append(w[row:row + e - a])
            a = e
    pieces.append(jnp.zeros((D_IN_PAD - D_IN, w.shape[1]), w.dtype))
    return jnp.concatenate(pieces, axis=0)


def _in_rows_to_shards(g):
    pieces = []
    for d in range(N_DEV):
        a, b = d * SHARD_IN, (d + 1) * SHARD_IN
        while a < b:
            e = min(b, 768 if a < 768 else (832 if a < 832 else D_IN))
            row = a if a < 768 else (C_KPE + a - 768 if a < 832 else a - 64)
            pieces.append(g[row:row + e - a])
            a = e
        pieces.append(jnp.zeros((SHARD_IN_PAD - SHARD_IN, g.shape[1]), g.dtype))
    return jnp.concatenate(pieces, axis=0)


def _uq_rows_pad(w):
    w = w.reshape(MLA_HEADS, MLA_NOPE + MLA_ROPE, w.shape[-1])
    return jnp.pad(w, ((0, 0), (0, MLA_HEAD_PAD - MLA_NOPE - MLA_ROPE), (0, 0))).reshape(MLA_HEADS * MLA_HEAD_PAD, -1)


def _uq_rows_unpad(g):
    return g.reshape(MLA_HEADS, MLA_HEAD_PAD, -1)[:, :MLA_NOPE + MLA_ROPE].reshape(MLA_HEADS * (MLA_NOPE + MLA_ROPE), -1)


def _ukv_rows_split(w):
    return jnp.swapaxes(w.reshape(MLA_HEADS, 2, 128, -1), 0, 1).reshape(MLA_HEADS * 256, -1)


def _ukv_rows_unsplit(g):
    return jnp.swapaxes(g.reshape(2, MLA_HEADS, 128, -1), 0, 1).reshape(MLA_HEADS * 256, -1)


class _Comm:
    def __init__(self, items):
        self.items = items
        self.n = len(items)
        self.pieces = [((s.shape[0] // N_DEV,) + s.shape[1:]) if sc else s.shape for s, sc, _ in items]
        self.inputs = [s for s, _, _ in items] + [d[0] for _, _, d in items if d is not None]
        any_space = pl.BlockSpec(memory_space=pl.ANY)
        self.in_specs = [any_space] * len(self.inputs)
        self.out_specs = [any_space] * self.n
        self.out_shape = [jax.ShapeDtypeStruct((N_DEV,) + tuple(p), s.dtype) if d is None
                          else jax.ShapeDtypeStruct(d[0].shape, d[0].dtype) for p, (s, _, d) in zip(self.pieces, items)]
        self.scratch = [pltpu.SemaphoreType.DMA((N_DEV - 1, self.n)), pltpu.SemaphoreType.DMA((N_DEV - 1, self.n)),
                        pltpu.SemaphoreType.DMA((self.n,))]

    def aliases(self, first_in, first_out):
        passed = [a for a, (_, _, d) in enumerate(self.items) if d is not None]
        return {first_in + self.n + b: first_out + a for b, a in enumerate(passed)}

    def copies(self, in_refs, out_refs, sems):
        send_sems, recv_sems, local_sems = sems
        x, y, c = lax.axis_index("x"), lax.axis_index("y"), lax.axis_index("c")
        me = 4 * x + 2 * y + c

        def block_for(a, p):
            if not self.items[a][1]:
                return in_refs[a]
            n = self.pieces[a][0]
            return in_refs[a].at[pl.ds(pl.multiple_of(p * n, 8), n)]

        def slot(a, dev):
            dst = self.items[a][2]
            return out_refs[a].at[dev] if dst is None else out_refs[a].at[dev, dst[1]]

        starts, recv_waits, send_waits = [], [], []
        for a in range(self.n):
            cp = pltpu.make_async_copy(block_for(a, me), slot(a, me), local_sems.at[a])
            starts.append(cp.start)
            send_waits.append(cp.wait)
        for k in range(1, N_DEV):
            px = 1 - x if (k >> 2) & 1 else x
            py = 1 - y if (k >> 1) & 1 else y
            pc = 1 - c if k & 1 else c
            peer = 4 * px + 2 * py + pc
            for a in range(self.n):
                out = pltpu.make_async_remote_copy(
                    src_ref=block_for(a, peer), dst_ref=slot(a, me), send_sem=send_sems.at[k - 1, a],
                    recv_sem=recv_sems.at[k - 1, a], device_id=(px, py, pc), device_id_type=pl.DeviceIdType.MESH)
                back = pltpu.make_async_remote_copy(
                    src_ref=block_for(a, peer), dst_ref=slot(a, peer), send_sem=send_sems.at[k - 1, a],
                    recv_sem=recv_sems.at[k - 1, a], device_id=(px, py, pc), device_id_type=pl.DeviceIdType.MESH)
                starts.append(out.start)
                send_waits.append(out.wait_send)
                recv_waits.append(back.wait_recv)
        return starts, recv_waits + send_waits


def _comm_call(body, name, grid, in_specs, out_specs, out_shape, operands, semantics, comm, scratch=()):
    n_in, n_out, n_scr = len(operands), len(out_shape), len(scratch)
    if comm is None:
        outs = pl.pallas_call(body, name=name, grid=grid, in_specs=in_specs, out_specs=out_specs, out_shape=out_shape,
                              scratch_shapes=list(scratch), compiler_params=_params(semantics))(*operands)
        return outs, []
    n_ci = len(comm.inputs)

    def fused(*refs):
        ins, c_in = refs[:n_in], refs[n_in:n_in + n_ci]
        outs = refs[n_in + n_ci:n_in + n_ci + n_out]
        c_out = refs[n_in + n_ci + n_out:n_in + n_ci + n_out + comm.n]
        first_scr = n_in + n_ci + n_out + comm.n
        outs = outs + refs[first_scr:first_scr + n_scr]
        starts, waits = comm.copies(c_in, c_out, refs[first_scr + n_scr:])
        first = functools.reduce(jnp.logical_and, [pl.program_id(d) == 0 for d in range(len(grid))])
        last = functools.reduce(jnp.logical_and, [pl.program_id(d) == g - 1 for d, g in enumerate(grid)])

        @pl.when(first)
        def _():
            for start in starts:
                start()

        body(*ins, *outs)

        @pl.when(last)
        def _():
            for wait in waits:
                wait()

    outs = pl.pallas_call(
        fused, name=name, grid=grid, in_specs=list(in_specs) + comm.in_specs, out_specs=list(out_specs) + comm.out_specs,
        out_shape=list(out_shape) + comm.out_shape, scratch_shapes=list(scratch) + comm.scratch,
        input_output_aliases=comm.aliases(n_in, n_out), compiler_params=_params(("arbitrary",) * len(grid)),
    )(*operands, *comm.inputs)
    return outs[:n_out], outs[n_out:]


def _exchange(name, items):
    return _comm_call(lambda: None, name, (1,), [], [], [], [], ("arbitrary",), _Comm(items))[1]


BLOCK_BYTES = 6 << 20


def _row_block(rows, row_bytes):
    if rows * row_bytes <= BLOCK_BYTES or rows % 8:
        return rows
    return max(tb for tb in range(8, rows + 1, 8) if rows % tb == 0 and (tb == 8 or tb * row_bytes <= BLOCK_BYTES))


def _sum_contribs(recv):
    n, rows, width = recv.shape
    tb = _row_block(rows, width * 4 * (n + 1))

    def body(r_ref, g_ref):
        g = r_ref[0]
        for k in range(1, n):
            g = g + r_ref[k]
        g_ref[...] = g

    return pl.pallas_call(
        body, name="sum_grads", grid=(rows // tb,), in_specs=[pl.BlockSpec((n, tb, width), lambda i: (0, i, 0))],
        out_specs=pl.BlockSpec((tb, width), lambda i: (i, 0)), out_shape=jax.ShapeDtypeStruct((rows, width), F32),
        compiler_params=_params(("parallel",)),
    )(recv)


def _adamw(contribs, w, m, v):
    n, rows, width = contribs.shape
    tb = _row_block(rows, width * 4 * (n + 7))
    c1 = 1.0 - ADAM_B1 ** ADAM_STEP
    c2 = 1.0 - ADAM_B2 ** ADAM_STEP

    def body(r_ref, w_ref, m_ref, v_ref, g_ref, d_ref, nm_ref, nv_ref):
        g = r_ref[0]
        for k in range(1, n):
            g = g + r_ref[k]
        nm = ADAM_B1 * m_ref[...] + (1.0 - ADAM_B1) * g
        nv = ADAM_B2 * v_ref[...] + (1.0 - ADAM_B2) * (g * g)
        g_ref[...] = g
        nm_ref[...] = nm
        nv_ref[...] = nv
        d_ref[...] = -ADAM_LR * ((nm / c1) / (jnp.sqrt(nv / c2) + ADAM_EPS) + ADAM_WD * w_ref[...])

    blk = pl.BlockSpec((tb, width), lambda i: (i, 0))
    return pl.pallas_call(
        body, name="adamw", grid=(rows // tb,),
        in_specs=[pl.BlockSpec((n, tb, width), lambda i: (0, i, 0)), blk, blk, blk],
        out_specs=[blk] * 4, out_shape=[jax.ShapeDtypeStruct((rows, width), F32)] * 4,
        compiler_params=_params(("parallel",)),
    )(contribs, w, m, v)


def kernel(x, mem, positions, w_in, q_norm_g, w_uq, kv_norm_g, w_ukv, sg_ln_g, sg_ln_b, sg_w, sg_b, w_mem_k, w_mem_v, w_out, ln_g, ln_b, loss_target, m_w_in, m_q_norm_g, m_w_uq, m_kv_norm_g, m_w_ukv, m_sg_ln_g, m_sg_ln_b, m_sg_w, m_sg_b, m_w_mem_k, m_w_mem_v, m_w_out, m_ln_g, m_ln_b, v_w_in, v_q_norm_g, v_w_uq, v_kv_norm_g, v_w_ukv, v_sg_ln_g, v_sg_ln_b, v_sg_w, v_sg_b, v_w_mem_k, v_w_mem_v, v_w_out, v_ln_g, v_ln_b):
    wl = dict(w_in=w_in, q_norm_g=q_norm_g, w_uq=w_uq, kv_norm_g=kv_norm_g, w_ukv=w_ukv, sg_ln_g=sg_ln_g, sg_ln_b=sg_ln_b,
              sg_w=sg_w, sg_b=sg_b, w_mem_k=w_mem_k, w_mem_v=w_mem_v, w_out=w_out, ln_g=ln_g, ln_b=ln_b)
    ml = dict(w_in=m_w_in, q_norm_g=m_q_norm_g, w_uq=m_w_uq, kv_norm_g=m_kv_norm_g, w_ukv=m_w_ukv, sg_ln_g=m_sg_ln_g,
              sg_ln_b=m_sg_ln_b, sg_w=m_sg_w, sg_b=m_sg_b, w_mem_k=m_w_mem_k, w_mem_v=m_w_mem_v, w_out=m_w_out, ln_g=m_ln_g, ln_b=m_ln_b)
    vl = dict(w_in=v_w_in, q_norm_g=v_q_norm_g, w_uq=v_w_uq, kv_norm_g=v_kv_norm_g, w_ukv=v_w_ukv, sg_ln_g=v_sg_ln_g,
              sg_ln_b=v_sg_ln_b, sg_w=v_sg_w, sg_b=v_sg_b, w_mem_k=v_w_mem_k, w_mem_v=v_w_mem_v, w_out=v_w_out, ln_g=v_ln_g, ln_b=v_ln_b)
    depth = w_in.shape[0]

    w_in_t = jnp.pad(jnp.swapaxes(w_in, 1, 2), ((0, 0), (0, SHARD_IN_PAD - SHARD_IN), (0, 0)))
    shards = [s.astype(MXU) for s in (w_in_t, jnp.swapaxes(w_uq, 1, 2), jnp.swapaxes(w_ukv, 1, 2),
                                      jnp.concatenate([w_mem_k, w_mem_v], axis=-1), w_out)]

    def gather_items(l):
        return [(s[l], False, None) for s in shards]

    first = _exchange("gather_weights", gather_items(0))

    def weights_of(l, received):
        g_in, g_uq, g_ukv, g_mem, g_out = first if l == 0 else received
        return dict(w_in=_in_rows_from_shards(g_in.reshape(D_IN_PAD, D_MODEL)), w_uq=_uq_rows_pad(g_uq.reshape(-1, MLA_Q_RANK)),
                    w_ukv=_ukv_rows_split(g_ukv.reshape(-1, MLA_KV_RANK)), w_mem_kv=g_mem.reshape(D_MODEL, 2 * MEM_WIDTH),
                    w_out=g_out.reshape(D_MODEL, D_MODEL), q_norm_g=q_norm_g[l], kv_norm_g=kv_norm_g[l], sg_ln_g=sg_ln_g[l],
                    sg_ln_b=sg_ln_b[l], sg_w=sg_w[l], sg_b=sg_b[l], ln_g=ln_g[l], ln_b=ln_b[l])

    to_rows = dict(w_in=_in_rows_to_shards, w_uq=_uq_rows_unpad, w_ukv=_ukv_rows_unsplit)

    def grad_items(l, g, recv):
        return [(n, (to_rows.get(n, lambda a: a)(a), True, (recv[n], l))) for n, a in g.items()]

    shard_rows = dict(w_in=(D_IN_PAD, D_MODEL), w_uq=(MLA_HEADS * (MLA_NOPE + MLA_ROPE), MLA_Q_RANK),
                      w_ukv=(MLA_HEADS * 256, MLA_KV_RANK), w_mem_kv=(D_MODEL, 2 * MEM_WIDTH), w_out=(D_MODEL, D_MODEL))
    recv = {n: lax.empty((N_DEV, depth, rows // N_DEV, width), F32) for n, (rows, width) in shard_rows.items()}
    loss_part, grad_x, grads, dsgb, recv = _local_step(
        x[0], mem[0], positions[0], loss_target[0], depth, weights_of, lambda l: _Comm(gather_items(l)), grad_items, recv)
    loss = lax.psum(loss_part, ("x", "y", "c"))
    r_in, r_uq, r_ukv, r_mem, r_out = (recv[n] for n in ("w_in", "w_uq", "w_ukv", "w_mem_kv", "w_out"))

    rep_g = dict(sg_b=dsgb, **{n: jnp.stack([g[n] for g in grads]) for n in REPLICATED if n != "sg_b"})
    rep_sizes = [wl[n].size for n in REPLICATED]
    rep_rows = -(-sum(rep_sizes) // (8 * LANES)) * 8

    def packed(d):
        flat = jnp.concatenate([d[n].reshape(-1) for n in REPLICATED])
        return jnp.pad(flat, (0, rep_rows * LANES - flat.shape[0])).reshape(rep_rows, LANES)

    (r_rep,) = _exchange("exchange_grads", [(packed(rep_g), False, None)])

    def rows2d(a):
        return a.reshape(-1, a.shape[-1])

    def update(name, contribs):
        outs = _adamw(contribs, rows2d(wl[name]), rows2d(ml[name]), rows2d(vl[name]))
        return [o.reshape(wl[name].shape) for o in outs]

    def from_transposed(recv, keep):
        s = _sum_contribs(recv.reshape(N_DEV, -1, recv.shape[-1])).reshape(recv.shape[1:])
        return rows2d(jnp.swapaxes(s[:, :keep], 1, 2))[None]

    res = dict(
        w_in=update("w_in", from_transposed(r_in, SHARD_IN)),
        w_uq=update("w_uq", from_transposed(r_uq, r_uq.shape[2])),
        w_ukv=update("w_ukv", from_transposed(r_ukv, r_ukv.shape[2])),
        w_mem_k=update("w_mem_k", r_mem[..., :MEM_WIDTH].reshape(N_DEV, -1, MEM_WIDTH)),
        w_mem_v=update("w_mem_v", r_mem[..., MEM_WIDTH:].reshape(N_DEV, -1, MEM_WIDTH)),
        w_out=update("w_out", r_out.reshape(N_DEV, -1, D_MODEL)))
    rep_outs = _adamw(r_rep, packed(wl), packed(ml), packed(vl))
    off = 0
    for n, size in zip(REPLICATED, rep_sizes):
        res[n] = [o.reshape(-1)[off:off + size].reshape(wl[n].shape) for o in rep_outs]
        off += size
    return (loss, grad_x[None], *[res[n][0] for n in WEIGHTS], *[res[n][1] for n in WEIGHTS],
            *[res[n][2] for n in WEIGHTS], *[res[n][3] for n in WEIGHTS])
```

```python
import functools
import math

import jax
import jax.numpy as jnp
from jax import lax
from jax.experimental import pallas as pl
from jax.experimental.pallas import tpu as pltpu

F32 = jnp.float32
MXU = jnp.bfloat16

DEPTH_FOR_NORM = 4
D_MODEL = 2048
CHUNK = 64
MLA_HEADS, MLA_NOPE, MLA_ROPE, MLA_V = 6, 128, 64, 128
MLA_Q_RANK, MLA_KV_RANK = 512, 256
MLA_WIDTH = MLA_HEADS * MLA_V
MLA_HEAD_PAD = 256
ROPE_THETA = 10000.0
SG_GROUPS, SG_CH, SG_CHUNK = 4, 128, 128
SG_WIDTH = SG_GROUPS * SG_CH
SB_HEADS, SB_DIM = 4, 128
SB_WIDTH = SB_HEADS * SB_DIM
MEM_TOKENS, MEM_HEADS, MEM_DIM = 256, 4, 64
MEM_WIDTH = MEM_HEADS * MEM_DIM
D_IN = 5696
D_IN_PAD = 5760
ALPHA = (2.0 * DEPTH_FOR_NORM) ** 0.25
LN_EPS = 1e-5
RMS_EPS = 1e-6
MLA_SCALE = 1.0 / math.sqrt(MLA_NOPE + MLA_ROPE)
SB_SCALE = 1.0 / math.sqrt(SB_DIM)
MEM_SCALE = 1.0 / math.sqrt(MEM_DIM)
LOG2E = math.log2(math.e)
LN2 = math.log(2.0)
ADAM_LR, ADAM_B1, ADAM_B2, ADAM_EPS, ADAM_WD, ADAM_STEP = 0.001, 0.9, 0.999, 1e-08, 0.01, 10
N_DEV = 8
LANES = 128
NEG = -1e30
SCAN = 256
SB_CUTOFF = 256.0
VMEM_LIMIT = 56 * 1024 * 1024

C_CQ, C_CKV, C_GA, C_SGU, C_SGV, C_GB = 0, 512, 768, 1536, 2048, 2560
C_SBQ, C_SBK, C_SBV, C_GC, C_MQ, C_GM, C_KPE = 3072, 3584, 4096, 4608, 5120, 5376, 5632

REPLICATED = ("q_norm_g", "kv_norm_g", "sg_ln_g", "sg_ln_b", "sg_w", "sg_b", "ln_g", "ln_b")
WEIGHTS = ("w_in", "q_norm_g", "w_uq", "kv_norm_g", "w_ukv", "sg_ln_g", "sg_ln_b", "sg_w", "sg_b",
           "w_mem_k", "w_mem_v", "w_out", "ln_g", "ln_b")


def _params(sem):
    return pltpu.CompilerParams(dimension_semantics=sem, vmem_limit_bytes=VMEM_LIMIT)


def _dot(a, b, dims):
    return lax.dot_general(a.astype(MXU), b.astype(MXU), (dims, ((), ())), preferred_element_type=F32)


NN, NT, TN = ((1,), (0,)), ((1,), (1,)), ((0,), (0,))


@jax.custom_vjp
def _mm(a, b):
    return _dot(a, b, NN)


def _mm_fwd(a, b):
    return _dot(a, b, NN), (a, b)


def _mm_bwd(res, g):
    a, b = res
    return _dot(g, b, NT), _dot(a, g, TN)


_mm.defvjp(_mm_fwd, _mm_bwd)


@jax.custom_vjp
def _mm_nt(a, b):
    return _dot(a, b, NT)


def _mm_nt_fwd(a, b):
    return _dot(a, b, NT), (a, b)


def _mm_nt_bwd(res, g):
    a, b = res
    return _dot(g, b, NN), _dot(g, a, TN)


_mm_nt.defvjp(_mm_nt_fwd, _mm_nt_bwd)


def _split_dot(x, tri):
    hi = x.astype(MXU)
    lo = (x - hi.astype(F32)).astype(MXU)
    return _dot(hi, tri, NN) + _dot(lo, tri, NN)


def _scan_keys(x, tri, reverse):
    w = tri.shape[0]
    nb = x.shape[1] // w
    outs, carry = [None] * nb, None
    for c in (reversed(range(nb)) if reverse else range(nb)):
        xc = x[:, c * w:(c + 1) * w]
        sc = _split_dot(xc, tri)
        outs[c] = sc if carry is None else sc + carry
        rs = jnp.sum(xc, axis=1, keepdims=True)
        carry = rs if carry is None else carry + rs
    return (outs[0] if nb == 1 else jnp.concatenate(outs, axis=1)), carry


def _col2row(col):
    return jnp.transpose(jnp.broadcast_to(col, (col.shape[0], LANES)))[0:8]


def _row2col(row):
    return jnp.transpose(jnp.broadcast_to(row, (LANES, row.shape[1])))[:, 0:1]


def _matmul(name, a, b, mode, tm, tn, tk, out_dtype=F32, add=None, add_scale=1.0, second_dtype=None,
            comm=None, with_received=False):
    if mode == "nn":
        (m, k), n = a.shape, b.shape[1]
    elif mode == "nt":
        (m, k), n = a.shape, b.shape[0]
    else:
        (k, m), n = a.shape, b.shape[1]
    tm, tn, tk = min(tm, m), min(tn, n), min(tk, k)
    assert m % tm == 0 and n % tn == 0 and k % tk == 0, (name, m, n, k, tm, tn, tk)
    nk = k // tk
    dims = {"nn": NN, "nt": NT, "tn": TN}[mode]
    a_spec = (pl.BlockSpec((tk, tm), lambda j, i, kk: (kk, i)) if mode == "tn"
              else pl.BlockSpec((tm, tk), lambda j, i, kk: (i, kk)))
    b_spec = (pl.BlockSpec((tn, tk), lambda j, i, kk: (j, kk)) if mode == "nt"
              else pl.BlockSpec((tk, tn), lambda j, i, kk: (kk, j)))
    o_spec = pl.BlockSpec((tm, tn), lambda j, i, kk: (i, j))
    in_specs, operands = [a_spec, b_spec], [a, b]
    if add is not None:
        in_specs.append(o_spec)
        operands.append(add)
    out_shape = [jax.ShapeDtypeStruct((m, n), out_dtype)]
    out_specs = [o_spec]
    if second_dtype is not None:
        out_shape.append(jax.ShapeDtypeStruct((m, n), second_dtype))
        out_specs.append(o_spec)

    def body(*refs):
        a_ref, b_ref = refs[0], refs[1]
        add_ref = refs[2] if add is not None else None
        outs = refs[len(operands):-1]
        acc_ref = refs[-1]
        kk = pl.program_id(2)
        if nk == 1:
            r = _dot(a_ref[...], b_ref[...], dims)
            if add_ref is not None:
                r = r + add_scale * add_ref[...]
            outs[0][...] = r.astype(out_dtype)
            if second_dtype is not None:
                outs[1][...] = r.astype(second_dtype)
            return

        @pl.when(kk == 0)
        def _():
            acc_ref[...] = jnp.zeros_like(acc_ref)

        acc_ref[...] += _dot(a_ref[...], b_ref[...], dims)

        @pl.when(kk == nk - 1)
        def _():
            r = acc_ref[...]
            if add_ref is not None:
                r = r + add_scale * add_ref[...]
            outs[0][...] = r.astype(out_dtype)
            if second_dtype is not None:
                outs[1][...] = r.astype(second_dtype)

    res, received = _comm_call(body, name, (n // tn, m // tm, nk), in_specs, out_specs, out_shape, operands,
                               ("parallel", "parallel", "arbitrary"), comm, scratch=[pltpu.VMEM((tm, tn) if nk > 1 else (8, LANES), F32)])
    res = res if second_dtype is not None else res[0]
    return (res, received) if with_received else res


def _accumulate(ref, val, first):
    @pl.when(first)
    def _():
        ref[...] = val

    @pl.when(jnp.logical_not(first))
    def _():
        ref[...] += val


def _rowcall(name, fn, rows, tm, row_in, const_in, row_out, acc_out=()):
    tm = min(tm, rows)
    assert rows % tm == 0
    n_in = len(row_in) + len(const_in)
    in_specs = [pl.BlockSpec((tm, w), lambda i, cb=cb: (i, cb)) for _, w, cb in row_in]
    in_specs += [pl.BlockSpec(c.shape, lambda i, nd=c.ndim: (0,) * nd) for c in const_in]
    out_specs = [pl.BlockSpec(blk, imap) for _, _, blk, imap in row_out]
    out_specs += [pl.BlockSpec(s, lambda i, nd=len(s): (0,) * nd) for s, _ in acc_out]
    out_shape = [jax.ShapeDtypeStruct(s, dt) for s, dt, _, _ in row_out]
    out_shape += [jax.ShapeDtypeStruct(s, dt) for s, dt in acc_out]

    def body(*refs):
        vals = fn(*[r[...] for r in refs[:n_in]])
        outs = refs[n_in:]
        first = pl.program_id(0) == 0
        for k, (_, dt, _, _) in enumerate(row_out):
            outs[k][...] = vals[k].astype(dt)
        for k in range(len(acc_out)):
            _accumulate(outs[len(row_out) + k], vals[len(row_out) + k], first)

    return pl.pallas_call(
        body, name=name, grid=(rows // tm,), in_specs=in_specs, out_specs=out_specs, out_shape=out_shape,
        compiler_params=_params(("arbitrary",) if acc_out else ("parallel",)),
    )(*[a for a, _, _ in row_in], *const_in)


def _ro(rows, width, dtype, tm):
    tm = min(tm, rows)
    return ((rows, width), dtype, (tm, width), lambda i: (i, 0))


def _rms(x, g):
    ms = jnp.mean(x * x, axis=-1, keepdims=True)
    return x * lax.rsqrt(ms + RMS_EPS) * g


def _ln(x, g, b):
    mu = jnp.mean(x, axis=-1, keepdims=True)
    xc = x - mu
    var = jnp.mean(xc * xc, axis=-1, keepdims=True)
    return xc * lax.rsqrt(var + LN_EPS) * g + b


def _gelu(x):
    return 0.5 * x * (1.0 + jnp.tanh(math.sqrt(2.0 / math.pi) * (x + 0.044715 * (x * x * x))))


def _silu(x):
    return x / (1.0 + jnp.exp(-x))


@jax.custom_vjp
def _rope(r, cc, sa, sb):
    return r * cc + pltpu.roll(r, 96, 1) * sa + pltpu.roll(r, 32, 1) * sb


def _rope_fwd(r, cc, sa, sb):
    return _rope(r, cc, sa, sb), (cc, sa, sb)


def _rope_bwd(res, d):
    cc, sa, sb = res
    return d * cc + pltpu.roll(d * sa, 32, 1) + pltpu.roll(d * sb, 96, 1), None, None, None


_rope.defvjp(_rope_fwd, _rope_bwd)


def _rope_tables_fn(pos, invf):
    ang = pos.astype(F32) * invf
    lane = lax.broadcasted_iota(jnp.int32, ang.shape, 1)
    cos, sin = jnp.cos(ang), jnp.sin(ang)
    zero = jnp.zeros_like(ang)
    cc = jnp.where(lane < 64, cos, zero)
    sa = jnp.where(lane < 32, -sin, zero)
    sb = jnp.where((lane >= 32) & (lane < 64), sin, zero)
    return cc, sa, sb


def _prep_fn(cq, ckv, kpe, cc, sa, sb, gq, wuq, gkv, wukv):
    q = _mm_nt(_rms(cq, gq), wuq) * (MLA_SCALE * LOG2E)
    kv = _mm_nt(_rms(ckv, gkv), wukv)
    krot = _rope(kpe, cc, sa, sb)
    qs, ks = [], []
    for h in range(MLA_HEADS):
        o = h * MLA_HEAD_PAD
        qs += [q[:, o:o + 128], _rope(q[:, o + 128:o + 256], cc, sa, sb)]
        ks += [kv[:, h * 128:(h + 1) * 128], krot]
    return jnp.concatenate(qs, axis=1), jnp.concatenate(ks, axis=1), kv[:, MLA_HEADS * 128:]


def _post_fn(ga, sgu, sgv, gb, gc, mq, gm, oa, oc, mkv, lng, lnb, sgw, biasf):
    tm = ga.shape[0]
    u = _gelu(sgu)
    vn = _ln(_gelu(sgv), lng, lnb)
    t_chunk = lax.broadcasted_iota(jnp.int32, (SG_CHUNK, SG_CHUNK), 0) // CHUNK
    s_chunk = lax.broadcasted_iota(jnp.int32, (SG_CHUNK, SG_CHUNK), 1) // CHUNK
    sg_mask = (s_chunk <= t_chunk).astype(F32)
    cols = []
    for g in range(SG_GROUPS):
        wg = sgw[g] * sg_mask
        vg = vn[:, g * SG_CH:(g + 1) * SG_CH]
        cols.append(jnp.concatenate(
            [_mm(wg, vg[n * SG_CHUNK:(n + 1) * SG_CHUNK, :]) for n in range(tm // SG_CHUNK)], axis=0))
    mixed = jnp.concatenate(cols, axis=1) + jnp.concatenate([biasf] * (tm // SG_CHUNK), axis=0)
    ob = u * mixed
    mk, mv = mkv[:, :MEM_WIDTH], mkv[:, MEM_WIDTH:]
    head = lax.broadcasted_iota(jnp.int32, (1, MEM_WIDTH), 1) // MEM_DIM
    om = jnp.zeros((tm, MEM_WIDTH), F32)
    for hd in range(MEM_HEADS):
        hm = (head == hd).astype(F32)
        s = _mm_nt(mq, mk * hm) * MEM_SCALE
        e = jnp.exp(s - lax.stop_gradient(jnp.max(s, axis=-1, keepdims=True)))
        p = e / jnp.sum(e, axis=-1, keepdims=True)
        om = om + _mm(p, mv * hm)
    return jnp.concatenate([oa * _silu(ga), ob * _silu(gb), oc * _silu(gc), om * _silu(gm)], axis=1)


def _post_bwd_fn(dyg, ga, sgu, sgv, gb, gc, mq, gm, oa, oc, mkv, lng, lnb, sgw, biasf):
    _, vjp = jax.vjp(_post_fn, ga, sgu, sgv, gb, gc, mq, gm, oa, oc, mkv, lng, lnb, sgw, biasf)
    dga, dsgu, dsgv, dgb, dgc, dmq, dgm, doa, doc, dmkv, dlng, dlnb, dsgw, dbias = vjp(dyg)
    drows = jnp.stack([_col2row(jnp.sum((doa * oa)[:, h * 128:(h + 1) * 128], axis=1, keepdims=True))
                       for h in range(MLA_HEADS)], axis=0)
    return dga, dsgu, dsgv, dgb, dgc, dmq, dgm, doa, doc, drows, dmkv, dlng, dlnb, dsgw, dbias


def _prep_bwd_fn(dq, dk, dv, cq, ckv, kpe, cc, sa, sb, gq, wuq, gkv, wukv):
    wuq, wukv = wuq.astype(F32), wukv.astype(F32)
    _, vjp = jax.vjp(lambda a, b, c, d, e, f, g: _prep_fn(a, b, c, cc, sa, sb, d, e, f, g), cq, ckv, kpe, gq, wuq, gkv, wukv)
    return vjp((dq, dk, dv))


def _ln_bwd_fn(dxn, z, g, b):
    _, vjp = jax.vjp(_ln, z, g, b)
    dz, dg, db = vjp(dxn)
    return dz, dz, dg, db


def _outproj_fn(yg, x, w, g, b):
    z = ALPHA * x + _dot(yg, w, NN)
    xn = _ln(z, g, b)
    return z, xn, xn


def _loss_fn(xl, tgt):
    err = xl - tgt
    part = 0.5 * jnp.sum(jnp.mean(err * err, axis=-1, keepdims=True), axis=0, keepdims=True)
    return err * (1.0 / D_MODEL), jnp.broadcast_to(part, (1, LANES))


def _mla_fwd(q, k, v, tq, comm=None):
    t = q.shape[0]
    tq = min(tq, t)
    nq = t // tq

    def body(q_ref, k_ref, v_ref, o_ref, lse_ref):
        i = pl.program_id(1)
        qb = q_ref[...]
        r_chunk = lax.broadcasted_iota(jnp.int32, (tq, tq), 0) // CHUNK
        c_chunk = lax.broadcasted_iota(jnp.int32, (tq, tq), 1) // CHUNK
        diag_ok = c_chunk <= r_chunk

        def block(j, carry, masked):
            m, l, acc = carry
            off = pl.multiple_of(j * tq, tq)
            s = _dot(qb, k_ref[pl.ds(off, tq), :], NT)
            if masked:
                s = jnp.where(diag_ok, s, NEG)
            m2 = jnp.maximum(m, jnp.max(s, axis=1, keepdims=True))
            p = jnp.exp2(s - m2)
            a = jnp.exp2(m - m2)
            return m2, a * l + jnp.sum(p, axis=1, keepdims=True), a * acc + _dot(p, v_ref[pl.ds(off, tq), :], NN)

        init = (jnp.full((tq, 1), NEG, F32), jnp.zeros((tq, 1), F32), jnp.zeros((tq, MLA_V), F32))
        carry = lax.fori_loop(0, i, lambda j, c: block(j, c, False), init)
        m, l, acc = block(i, carry, True)
        o_ref[...] = acc / l
        lse_ref[...] = _col2row(m + jnp.log2(l))

    (o, lse), received = _comm_call(
        body, "mla_fwd", (MLA_HEADS, nq),
        [pl.BlockSpec((tq, MLA_HEAD_PAD), lambda h, i: (i, h)),
         pl.BlockSpec((t, MLA_HEAD_PAD), lambda h, i: (0, h)),
         pl.BlockSpec((t, MLA_V), lambda h, i: (0, h))],
        [pl.BlockSpec((tq, MLA_V), lambda h, i: (i, h)),
         pl.BlockSpec((None, 8, tq), lambda h, i: (h, 0, i))],
        [jax.ShapeDtypeStruct((t, MLA_WIDTH), F32), jax.ShapeDtypeStruct((MLA_HEADS, 8, t), F32)],
        [q, k, v], ("parallel", "parallel"), comm)
    return o, lse, received


def _mla_bwd(q, k, v, do, lse, drow, tq, comm=None):
    t = q.shape[0]
    tq = min(tq, t)
    nq = t // tq

    def body(k_ref, v_ref, q_ref, do_ref, lse_ref, d_ref, dk_ref, dv_ref, dq_ref):
        j = pl.program_id(1)

        @pl.when(j == 0)
        def _():
            dq_ref[...] = jnp.zeros_like(dq_ref)

        kj, vj = k_ref[...], v_ref[...]
        k_chunk = lax.broadcasted_iota(jnp.int32, (tq, tq), 0) // CHUNK
        q_chunk = lax.broadcasted_iota(jnp.int32, (tq, tq), 1) // CHUNK
        diag_ok = k_chunk <= q_chunk

        def block(i, carry, masked):
            dk, dv = carry
            off = pl.multiple_of(i * tq, tq)
            qi, doi = q_ref[pl.ds(off, tq), :], do_ref[pl.ds(off, tq), :]
            pt = jnp.exp2(_dot(kj, qi, NT) - lse_ref[0:1, pl.ds(off, tq)])
            if masked:
                pt = jnp.where(diag_ok, pt, 0.0)
            dv = dv + _dot(pt, doi, NN)
            dst = (pt * (_dot(vj, doi, NT) - d_ref[0:1, pl.ds(off, tq)])).astype(MXU)
            dk = dk + _dot(dst, qi, NN)
            dq_ref[pl.ds(off, tq), :] += _dot(dst, kj, TN) * LN2
            return dk, dv

        carry = block(j, (jnp.zeros((tq, MLA_HEAD_PAD), F32), jnp.zeros((tq, MLA_V), F32)), True)
        dk, dv = lax.fori_loop(j + 1, nq, lambda i, c: block(i, c, False), carry)
        dk_ref[...] = dk * LN2
        dv_ref[...] = dv

    (dk, dv, dq), received = _comm_call(
        body, "mla_bwd", (MLA_HEADS, nq),
        [pl.BlockSpec((tq, MLA_HEAD_PAD), lambda h, j: (j, h)),
         pl.BlockSpec((tq, MLA_V), lambda h, j: (j, h)),
         pl.BlockSpec((t, MLA_HEAD_PAD), lambda h, j: (0, h)),
         pl.BlockSpec((t, MLA_V), lambda h, j: (0, h)),
         pl.BlockSpec((None, 8, t), lambda h, j: (h, 0, 0)),
         pl.BlockSpec((None, 8, t), lambda h, j: (h, 0, 0))],
        [pl.BlockSpec((tq, MLA_HEAD_PAD), lambda h, j: (j, h)),
         pl.BlockSpec((tq, MLA_V), lambda h, j: (j, h)),
         pl.BlockSpec((t, MLA_HEAD_PAD), lambda h, j: (0, h))],
        [jax.ShapeDtypeStruct((t, MLA_HEADS * MLA_HEAD_PAD), F32), jax.ShapeDtypeStruct((t, MLA_WIDTH), F32),
         jax.ShapeDtypeStruct((t, MLA_HEADS * MLA_HEAD_PAD), F32)],
        [k, v, q, do, lse, drow], ("parallel", "arbitrary"), comm)
    return dq, dk, dv, received


def _sb_logs(z2):
    lb = jnp.minimum(z2, 0.0) - jnp.log2(1.0 + jnp.exp2(jnp.minimum(z2, -z2)))
    return lb, lb - z2


def _tri(n, keep):
    r = lax.broadcasted_iota(jnp.int32, (n, n), 0)
    c = lax.broadcasted_iota(jnp.int32, (n, n), 1)
    return keep(r, c)


def _sb_fwd(hb, tq):
    t = hb.shape[0]
    tq = min(tq, t)
    nq = t // tq
    cq, ck, cv = C_SBQ // SB_DIM, C_SBK // SB_DIM, C_SBV // SB_DIM

    def body(q_ref, k_ref, v_ref, o_ref, tot_ref, first_ref):
        h, i = pl.program_id(0), pl.program_id(1)
        qb = q_ref[...]
        strict = _tri(tq, lambda r, c: c < r)
        later = _tri(min(SCAN, tq), lambda r, c: r > c).astype(MXU)

        def block(j, carry, masked):
            run, acc = carry
            off = pl.multiple_of(j * tq, tq)
            lb, l1 = _sb_logs(_dot(qb, k_ref[pl.ds(off, tq), :], NT) * (SB_SCALE * LOG2E))
            if masked:
                l1 = jnp.where(strict, l1, 0.0)
            after, rowsum = _scan_keys(l1, later, True)
            a = jnp.exp2(lb + after + run)
            if masked:
                a = jnp.where(strict, a, 0.0)
            return run + rowsum, acc + _dot(a, v_ref[pl.ds(off, tq), :], NN)

        carry = block(i, (jnp.zeros((tq, 1), F32), jnp.zeros((tq, SB_DIM), F32)), True)

        def live(state):
            jj, (run, _) = state
            return jnp.logical_and(jj < i, jnp.max(run) > -SB_CUTOFF)

        jj, (run, acc) = lax.while_loop(live, lambda st: (st[0] + 1, block(i - 1 - st[0], st[1], False)),
                                        (jnp.int32(0), carry))
        o_ref[...] = acc
        tot_ref[...] = _col2row(run)
        first_ref[h, i] = i - jj

    return _comm_call(
        body, "sb_fwd", (SB_HEADS, nq),
        [pl.BlockSpec((tq, SB_DIM), lambda h, i: (i, cq + h)),
         pl.BlockSpec((t, SB_DIM), lambda h, i: (0, ck + h)),
         pl.BlockSpec((t, SB_DIM), lambda h, i: (0, cv + h))],
        [pl.BlockSpec((tq, SB_DIM), lambda h, i: (i, h)),
         pl.BlockSpec((None, 8, tq), lambda h, i: (h, 0, i)),
         pl.BlockSpec(memory_space=pltpu.SMEM)],
        [jax.ShapeDtypeStruct((t, SB_WIDTH), F32), jax.ShapeDtypeStruct((SB_HEADS, 8, t), F32),
         jax.ShapeDtypeStruct((SB_HEADS, nq), jnp.int32)],
        [hb, hb, hb], ("arbitrary", "arbitrary"), None)[0]


def _sb_bwd(hb, do, tot, first, tq):
    t = hb.shape[0]
    tq = min(tq, t)
    nq = t // tq
    cq, ck, cv = C_SBQ // SB_DIM, C_SBK // SB_DIM, C_SBV // SB_DIM

    def body(q_ref, do_ref, tot_ref, k_ref, v_ref, first_ref, dq_ref, dk_ref, dv_ref):
        h, i = pl.program_id(0), pl.program_id(1)

        @pl.when(i == 0)
        def _():
            dk_ref[...] = jnp.zeros_like(dk_ref)
            dv_ref[...] = jnp.zeros_like(dv_ref)

        qb, dob = q_ref[...], do_ref[...]
        total = _row2col(tot_ref[0:1, :])
        strict = _tri(tq, lambda r, c: c < r)
        upto = _tri(min(SCAN, tq), lambda r, c: r <= c).astype(MXU)
        before = _tri(min(SCAN, tq), lambda r, c: r < c).astype(MXU)

        def block(j, carry, masked):
            pre, gpre, dq = carry
            off = pl.multiple_of(j * tq, tq)
            kj, vj = k_ref[pl.ds(off, tq), :], v_ref[pl.ds(off, tq), :]
            lb, l1 = _sb_logs(_dot(qb, kj, NT) * (SB_SCALE * LOG2E))
            if masked:
                l1 = jnp.where(strict, l1, 0.0)
            upto_sum, rowsum = _scan_keys(l1, upto, False)
            a = jnp.exp2(lb + (total - (upto_sum + pre)))
            if masked:
                a = jnp.where(strict, a, 0.0)
            g = a * _dot(dob, vj, NT)
            before_sum, grow = _scan_keys(g, before, False)
            beta = jnp.exp2(lb)
            dz = (g * (1.0 - beta) - (before_sum + gpre) * beta) * SB_SCALE
            if masked:
                dz = jnp.where(strict, dz, 0.0)
            dzb = dz.astype(MXU)
            dk_ref[pl.ds(off, tq), :] += _dot(dzb, qb, TN)
            dv_ref[pl.ds(off, tq), :] += _dot(a, dob, TN)
            return pre + rowsum, gpre + grow, dq + _dot(dzb, kj, NN)

        init = (jnp.zeros((tq, 1), F32), jnp.zeros((tq, 1), F32), jnp.zeros((tq, SB_DIM), F32))
        carry = lax.fori_loop(first_ref[h, i], i, lambda j, c: block(j, c, False), init)
        dq_ref[...] = block(i, carry, True)[2].astype(dq_ref.dtype)

    return _comm_call(
        body, "sb_bwd", (SB_HEADS, nq),
        [pl.BlockSpec((tq, SB_DIM), lambda h, i: (i, cq + h)),
         pl.BlockSpec((tq, SB_DIM), lambda h, i: (i, h)),
         pl.BlockSpec((None, 8, tq), lambda h, i: (h, 0, i)),
         pl.BlockSpec((t, SB_DIM), lambda h, i: (0, ck + h)),
         pl.BlockSpec((t, SB_DIM), lambda h, i: (0, cv + h)),
         pl.BlockSpec(memory_space=pltpu.SMEM)],
        [pl.BlockSpec((tq, SB_DIM), lambda h, i: (i, h)),
         pl.BlockSpec((t, SB_DIM), lambda h, i: (0, h)),
         pl.BlockSpec((t, SB_DIM), lambda h, i: (0, h))],
        [jax.ShapeDtypeStruct((t, SB_WIDTH), MXU), jax.ShapeDtypeStruct((t, SB_WIDTH), F32),
         jax.ShapeDtypeStruct((t, SB_WIDTH), F32)],
        [hb, do, tot, hb, hb, first], ("parallel", "arbitrary"), None)[0]


TM = 256
TQ_MLA = 1024
TQ_SB = 512


def _layer_fwd(xf, xb, w, consts, comm=None, rest=None):
    t = xf.shape[0]
    cc, sa, sb, mem = consts
    (h, hb), arrived = _matmul("in_proj", xb, w["w_in"], "nt", 512, 1920, D_MODEL, second_dtype=MXU,
                               comm=rest[0] if rest else None, with_received=True)
    if rest:
        w = dict(w, **rest[1](arrived))
    gq, gkv = w["q_norm_g"][None, :], w["kv_norm_g"][None, :]
    q, k, v = _rowcall(
        "mla_prep", _prep_fn, t, TM,
        [(h, 512, C_CQ // 512), (h, 256, C_CKV // 256), (h, 128, C_KPE // 128), (cc, 128, 0), (sa, 128, 0), (sb, 128, 0)],
        [gq, w["w_uq"], gkv, w["w_ukv"]],
        [_ro(t, MLA_HEADS * MLA_HEAD_PAD, MXU, TM), _ro(t, MLA_HEADS * MLA_HEAD_PAD, MXU, TM), _ro(t, MLA_WIDTH, MXU, TM)])
    oa, lse, received = _mla_fwd(q, k, v, TQ_MLA, comm)
    oc, tot, first = _sb_fwd(hb, TQ_SB)
    mkv = _matmul("mem_kv", mem, w["w_mem_kv"], "nn", 256, 512, 2048)
    lng, lnb = w["sg_ln_g"][None, :], w["sg_ln_b"][None, :]
    biasf = jnp.repeat(w["sg_b"].T, SG_CH, axis=1)
    post_rows = [(h, 768, C_GA // 768), (h, 512, C_SGU // 512), (h, 512, C_SGV // 512), (h, 512, C_GB // 512),
                 (h, 512, C_GC // 512), (h, 256, C_MQ // 256), (h, 256, C_GM // 256), (oa, 768, 0), (oc, 512, 0)]
    post_consts = [mkv, lng, lnb, w["sg_w"], biasf]
    (yg,) = _rowcall("mixer_post", lambda *a: (_post_fn(*a),), t, TM, post_rows, post_consts, [_ro(t, D_MODEL, MXU, TM)])
    z, xn, xnb = _rowcall(
        "out_proj_ln", _outproj_fn, t, TM, [(yg, D_MODEL, 0), (xf, D_MODEL, 0)],
        [w["w_out"], w["ln_g"][None, :], w["ln_b"][None, :]],
        [_ro(t, D_MODEL, F32, TM), _ro(t, D_MODEL, F32, TM), _ro(t, D_MODEL, MXU, TM)])
    saved = dict(w=w, xb=xb, h=h, hb=hb, q=q, k=k, v=v, oa=oa, lse=lse, oc=oc, tot=tot, first=first, mkv=mkv, yg=yg, z=z,
                 post_rows=post_rows, post_consts=post_consts, gq=gq, gkv=gkv)
    return xn, xnb, saved, received


def _layer_bwd(dxn, w, s, consts, recv=None, early_items=None, late_items=None):
    t = dxn.shape[0]
    cc, sa, sb, mem = consts
    h = s["h"]
    dz, dzb, dlng2, dlnb2 = _rowcall(
        "ln_bwd", _ln_bwd_fn, t, TM, [(dxn, D_MODEL, 0), (s["z"], D_MODEL, 0)],
        [w["ln_g"][None, :], w["ln_b"][None, :]],
        [_ro(t, D_MODEL, F32, TM), _ro(t, D_MODEL, MXU, TM)], [((1, D_MODEL), F32), ((1, D_MODEL), F32)])
    dyg = _matmul("out_proj_dx", dzb, w["w_out"], "nt", 512, 1024, 2048)
    dw_out = _matmul("out_proj_dw", s["yg"], dzb, "tn", 1024, 1024, 2048)
    tm = min(TM, t)
    outs = _rowcall(
        "mixer_post_bwd", _post_bwd_fn, t, TM, [(dyg, D_MODEL, 0)] + s["post_rows"], s["post_consts"],
        [_ro(t, 768, MXU, TM), _ro(t, 512, MXU, TM), _ro(t, 512, MXU, TM), _ro(t, 512, MXU, TM), _ro(t, 512, MXU, TM),
         _ro(t, 256, MXU, TM), _ro(t, 256, MXU, TM), _ro(t, 768, MXU, TM), _ro(t, 512, MXU, TM),
         ((MLA_HEADS, 8, t), F32, (MLA_HEADS, 8, tm), lambda i: (0, 0, i))],
        [((MEM_TOKENS, 2 * MEM_WIDTH), F32), ((1, SG_WIDTH), F32), ((1, SG_WIDTH), F32),
         ((SG_GROUPS, SG_CHUNK, SG_CHUNK), F32), ((SG_CHUNK, SG_WIDTH), F32)])
    dga, dsgu, dsgv, dgb, dgc, dmq, dgm, doa, doc, drow, dmkv, dsg_ln_g, dsg_ln_b, dsgw, dbias = outs
    dw_mem_kv = _matmul("mem_kv_dw", mem, dmkv, "tn", 1024, 512, 256)
    dsbq, dsbk, dsbv = _sb_bwd(s["hb"], doc, s["tot"], s["first"], TQ_SB)
    named = early_items(dict(w_out=dw_out, w_mem_kv=dw_mem_kv)) if early_items else []
    dq, dk, dv, received = _mla_bwd(s["q"], s["k"], s["v"], doa, s["lse"], drow, TQ_MLA,
                                    _Comm([item for _, item in named]) if named else None)
    if named:
        recv.update({n: r for (n, _), r in zip(named, received)})
    wq, wkv = MLA_HEADS * MLA_HEAD_PAD, MLA_WIDTH
    dcq, dckv, dkpe, dgq, dwuq, dgkv, dwukv = _rowcall(
        "mla_prep_bwd", _prep_bwd_fn, t, TM,
        [(dq, wq, 0), (dk, wq, 0), (dv, wkv, 0), (h, 512, C_CQ // 512), (h, 256, C_CKV // 256), (h, 128, C_KPE // 128),
         (cc, 128, 0), (sa, 128, 0), (sb, 128, 0)],
        [s["gq"], w["w_uq"], s["gkv"], w["w_ukv"]],
        [_ro(t, 512, MXU, TM), _ro(t, 256, MXU, TM), _ro(t, 128, MXU, TM)],
        [((1, MLA_Q_RANK), F32), ((wq, MLA_Q_RANK), F32), ((1, MLA_KV_RANK), F32), ((2 * wkv, MLA_KV_RANK), F32)])
    dh = jnp.concatenate([dcq, dckv, dga, dsgu, dsgv, dgb, dsbq, dsbk.astype(MXU), dsbv.astype(MXU), dgc, dmq, dgm, dkpe], axis=1)
    dw_in = _matmul("in_proj_dw", dh, s["xb"], "tn", 1152, 1024, 2048)
    replicated = dict(q_norm_g=dgq[0], kv_norm_g=dgkv[0], sg_ln_g=dsg_ln_g[0], sg_ln_b=dsg_ln_b[0], sg_w=dsgw, sg_bias=dbias,
                      ln_g=dlng2[0], ln_b=dlnb2[0])
    named = late_items(dict(w_in=dw_in, w_uq=dwuq, w_ukv=dwukv), replicated) if late_items else []
    dx, received = _matmul("in_proj_dx", dh, w["w_in"], "nn", 512, 1024, D_IN_PAD, add=dz, add_scale=ALPHA,
                           comm=_Comm([item for _, item in named]) if named else None, with_received=True)
    if named:
        recv.update({n: r for (n, _), r in zip(named, received)})
    return dx, dict(replicated, w_in=dw_in, w_uq=dwuq, w_ukv=dwukv, w_mem_kv=dw_mem_kv, w_out=dw_out)


def _sg_bias_grad(bias_grads):
    db = jnp.concatenate(bias_grads, axis=0)

    def lane_sums(d):
        return (jnp.concatenate([jnp.broadcast_to(jnp.sum(d[:, g * SG_CH:(g + 1) * SG_CH], axis=1, keepdims=True),
                                                  (d.shape[0], SG_CH)) for g in range(SG_GROUPS)], axis=1),)

    (dbs,) = _rowcall("sg_bias_sum", lane_sums, db.shape[0], SG_CHUNK, [(db, SG_WIDTH, 0)], [], [_ro(db.shape[0], SG_WIDTH, F32, SG_CHUNK)])
    return dbs[:, ::SG_CH].reshape(len(bias_grads), SG_CHUNK, SG_GROUPS).transpose(0, 2, 1)


def _local_step(x, mem, positions, target, depth, weights_of, gather_comm, first_rest, grad_items, rep_item, recv):
    t = x.shape[0]
    inv_freq = ROPE_THETA ** (-jnp.arange(0, MLA_ROPE, 2, dtype=F32) / MLA_ROPE)
    invf = jnp.tile(inv_freq, 4)[None, :]
    cc, sa, sb = _rowcall("rope_tables", _rope_tables_fn, t, TM, [(positions[:, None], 1, 0)], [invf],
                          [_ro(t, 128, F32, TM)] * 3)
    consts = (cc, sa, sb, mem)
    xf, xb, saved, received = x, x.astype(MXU), [], None
    for l in range(depth):
        xf, xb, s, received = _layer_fwd(xf, xb, weights_of(l, received), consts, gather_comm(l + 1) if l + 1 < depth else None,
                                         first_rest if l == 0 else None)
        saved.append(s)
    dx, loss_part = _rowcall("loss", _loss_fn, t, TM, [(xf, D_MODEL, 0), (target, D_MODEL, 0)], [],
                             [_ro(t, D_MODEL, F32, TM)], [((1, LANES), F32)])
    grads, recv, sg_b_grad = [None] * depth, dict(recv or {}), []
    for l in reversed(range(depth)):
        def early(g, l=l):
            rest = {n: grads[l + 1][n] for n in ("w_in", "w_uq", "w_ukv")} if l + 1 < depth else {}
            return grad_items(l, g, recv) + (grad_items(l + 1, rest, recv) if rest else [])

        def late(g, replicated):
            per_layer = [replicated] + grads[1:]
            sg_b_grad.append(_sg_bias_grad([r["sg_bias"] for r in per_layer]))
            return grad_items(0, g, recv) + [("replicated", rep_item(per_layer, sg_b_grad[0]))]

        dx, grads[l] = _layer_bwd(dx, saved[l]["w"], saved[l], consts, recv, early if grad_items else None,
                                  late if grad_items and l == 0 else None)
    dsgb = sg_b_grad[0] if sg_b_grad else _sg_bias_grad([g["sg_bias"] for g in grads])
    return loss_part[0, 0], dx, grads, dsgb, recv


SHARD_IN = D_IN // N_DEV
SHARD_IN_PAD = D_IN_PAD // N_DEV
IN_RANGES = ((0, 768), (832, D_IN), (768, 832))


def _in_rows_from_shards(w):
    pieces = []
    for a, b in IN_RANGES:
        while a < b:
            e = min(b, (a // SHARD_IN + 1) * SHARD_IN)
            row = (a // SHARD_IN) * SHARD_IN_PAD + a % SHARD_IN
            pieces.append(w[row:row + e - a])
            a = e
    pieces.append(jnp.zeros((D_IN_PAD - D_IN, w.shape[1]), w.dtype))
    return jnp.concatenate(pieces, axis=0)


def _in_rows_to_shards(g):
    pieces = []
    for d in range(N_DEV):
        a, b = d * SHARD_IN, (d + 1) * SHARD_IN
        while a < b:
            e = min(b, 768 if a < 768 else (832 if a < 832 else D_IN))
            row = a if a < 768 else (C_KPE + a - 768 if a < 832 else a - 64)
            pieces.append(g[row:row + e - a])
            a = e
        pieces.append(jnp.zeros((SHARD_IN_PAD - SHARD_IN, g.shape[1]), g.dtype))
    return jnp.concatenate(pieces, axis=0)


def _uq_rows_pad(w):
    w = w.reshape(MLA_HEADS, MLA_NOPE + MLA_ROPE, w.shape[-1])
    return jnp.pad(w, ((0, 0), (0, MLA_HEAD_PAD - MLA_NOPE - MLA_ROPE), (0, 0))).reshape(MLA_HEADS * MLA_HEAD_PAD, -1)


def _uq_rows_unpad(g):
    return g.reshape(MLA_HEADS, MLA_HEAD_PAD, -1)[:, :MLA_NOPE + MLA_ROPE].reshape(MLA_HEADS * (MLA_NOPE + MLA_ROPE), -1)


def _ukv_rows_split(w):
    return jnp.swapaxes(w.reshape(MLA_HEADS, 2, 128, -1), 0, 1).reshape(MLA_HEADS * 256, -1)


def _ukv_rows_unsplit(g):
    return jnp.swapaxes(g.reshape(2, MLA_HEADS, 128, -1), 0, 1).reshape(MLA_HEADS * 256, -1)


class _Comm:
    def __init__(self, items):
        self.items = items
        self.n = len(items)
        self.pieces = [((s.shape[0] // N_DEV,) + s.shape[1:]) if sc else s.shape for s, sc, _ in items]
        self.inputs = [s for s, _, _ in items] + [d[0] for _, _, d in items if d is not None]
        any_space = pl.BlockSpec(memory_space=pl.ANY)
        self.in_specs = [any_space] * len(self.inputs)
        self.out_specs = [any_space] * self.n
        self.out_shape = [jax.ShapeDtypeStruct((N_DEV,) + tuple(p), s.dtype) if d is None
                          else jax.ShapeDtypeStruct(d[0].shape, d[0].dtype) for p, (s, _, d) in zip(self.pieces, items)]
        self.scratch = [pltpu.SemaphoreType.DMA((N_DEV - 1, self.n)), pltpu.SemaphoreType.DMA((N_DEV - 1, self.n)),
                        pltpu.SemaphoreType.DMA((self.n,))]

    def aliases(self, first_in, first_out):
        passed = [a for a, (_, _, d) in enumerate(self.items) if d is not None]
        return {first_in + self.n + b: first_out + a for b, a in enumerate(passed)}

    def copies(self, in_refs, out_refs, sems):
        send_sems, recv_sems, local_sems = sems
        x, y, c = lax.axis_index("x"), lax.axis_index("y"), lax.axis_index("c")
        me = 4 * x + 2 * y + c

        def block_for(a, p):
            if not self.items[a][1]:
                return in_refs[a]
            n = self.pieces[a][0]
            return in_refs[a].at[pl.ds(pl.multiple_of(p * n, 8), n)]

        def slot(a, dev):
            dst = self.items[a][2]
            return out_refs[a].at[dev] if dst is None else out_refs[a].at[dev, dst[1]]

        starts, recv_waits, send_waits = [], [], []
        for a in range(self.n):
            cp = pltpu.make_async_copy(block_for(a, me), slot(a, me), local_sems.at[a])
            starts.append(cp.start)
            send_waits.append(cp.wait)
        for k in range(1, N_DEV):
            px = 1 - x if (k >> 2) & 1 else x
            py = 1 - y if (k >> 1) & 1 else y
            pc = 1 - c if k & 1 else c
            peer = 4 * px + 2 * py + pc
            for a in range(self.n):
                out = pltpu.make_async_remote_copy(
                    src_ref=block_for(a, peer), dst_ref=slot(a, me), send_sem=send_sems.at[k - 1, a],
                    recv_sem=recv_sems.at[k - 1, a], device_id=(px, py, pc), device_id_type=pl.DeviceIdType.MESH)
                back = pltpu.make_async_remote_copy(
                    src_ref=block_for(a, peer), dst_ref=slot(a, peer), send_sem=send_sems.at[k - 1, a],
                    recv_sem=recv_sems.at[k - 1, a], device_id=(px, py, pc), device_id_type=pl.DeviceIdType.MESH)
                starts.append(out.start)
                send_waits.append(out.wait_send)
                recv_waits.append(back.wait_recv)
        return starts, recv_waits + send_waits


def _comm_call(body, name, grid, in_specs, out_specs, out_shape, operands, semantics, comm, scratch=()):
    n_in, n_out, n_scr = len(operands), len(out_shape), len(scratch)
    if comm is None:
        outs = pl.pallas_call(body, name=name, grid=grid, in_specs=in_specs, out_specs=out_specs, out_shape=out_shape,
                              scratch_shapes=list(scratch), compiler_params=_params(semantics))(*operands)
        return outs, []
    n_ci = len(comm.inputs)

    def fused(*refs):
        ins, c_in = refs[:n_in], refs[n_in:n_in + n_ci]
        outs = refs[n_in + n_ci:n_in + n_ci + n_out]
        c_out = refs[n_in + n_ci + n_out:n_in + n_ci + n_out + comm.n]
        first_scr = n_in + n_ci + n_out + comm.n
        outs = outs + refs[first_scr:first_scr + n_scr]
        starts, waits = comm.copies(c_in, c_out, refs[first_scr + n_scr:])
        first = functools.reduce(jnp.logical_and, [pl.program_id(d) == 0 for d in range(len(grid))])
        last = functools.reduce(jnp.logical_and, [pl.program_id(d) == g - 1 for d, g in enumerate(grid)])

        @pl.when(first)
        def _():
            for start in starts:
                start()

        body(*ins, *outs)

        @pl.when(last)
        def _():
            for wait in waits:
                wait()

    outs = pl.pallas_call(
        fused, name=name, grid=grid, in_specs=list(in_specs) + comm.in_specs, out_specs=list(out_specs) + comm.out_specs,
        out_shape=list(out_shape) + comm.out_shape, scratch_shapes=list(scratch) + comm.scratch,
        input_output_aliases=comm.aliases(n_in, n_out), compiler_params=_params(("arbitrary",) * len(grid)),
    )(*operands, *comm.inputs)
    return outs[:n_out], outs[n_out:]


def _exchange(name, items):
    return _comm_call(lambda: None, name, (1,), [], [], [], [], ("arbitrary",), _Comm(items))[1]


BLOCK_BYTES = 6 << 20


def _row_block(rows, row_bytes):
    if rows * row_bytes <= BLOCK_BYTES or rows % 8:
        return rows
    return max(tb for tb in range(8, rows + 1, 8) if rows % tb == 0 and (tb == 8 or tb * row_bytes <= BLOCK_BYTES))


def _sum_contribs(recv):
    n, rows, width = recv.shape
    tb = _row_block(rows, width * 4 * (n + 1))

    def body(r_ref, g_ref):
        g = r_ref[0]
        for k in range(1, n):
            g = g + r_ref[k]
        g_ref[...] = g

    return pl.pallas_call(
        body, name="sum_grads", grid=(rows // tb,), in_specs=[pl.BlockSpec((n, tb, width), lambda i: (0, i, 0))],
        out_specs=pl.BlockSpec((tb, width), lambda i: (i, 0)), out_shape=jax.ShapeDtypeStruct((rows, width), F32),
        compiler_params=_params(("parallel",)),
    )(recv)


def _adamw(contribs, w, m, v):
    n, rows, width = contribs.shape
    tb = _row_block(rows, width * 4 * (n + 7))
    c1 = 1.0 - ADAM_B1 ** ADAM_STEP
    c2 = 1.0 - ADAM_B2 ** ADAM_STEP

    def body(r_ref, w_ref, m_ref, v_ref, g_ref, d_ref, nm_ref, nv_ref):
        g = r_ref[0]
        for k in range(1, n):
            g = g + r_ref[k]
        nm = ADAM_B1 * m_ref[...] + (1.0 - ADAM_B1) * g
        nv = ADAM_B2 * v_ref[...] + (1.0 - ADAM_B2) * (g * g)
        g_ref[...] = g
        nm_ref[...] = nm
        nv_ref[...] = nv
        d_ref[...] = -ADAM_LR * ((nm / c1) / (jnp.sqrt(nv / c2) + ADAM_EPS) + ADAM_WD * w_ref[...])

    blk = pl.BlockSpec((tb, width), lambda i: (i, 0))
    return pl.pallas_call(
        body, name="adamw", grid=(rows // tb,),
        in_specs=[pl.BlockSpec((n, tb, width), lambda i: (0, i, 0)), blk, blk, blk],
        out_specs=[blk] * 4, out_shape=[jax.ShapeDtypeStruct((rows, width), F32)] * 4,
        compiler_params=_params(("parallel",)),
    )(contribs, w, m, v)


def kernel(x, mem, positions, w_in, q_norm_g, w_uq, kv_norm_g, w_ukv, sg_ln_g, sg_ln_b, sg_w, sg_b, w_mem_k, w_mem_v, w_out, ln_g, ln_b, loss_target, m_w_in, m_q_norm_g, m_w_uq, m_kv_norm_g, m_w_ukv, m_sg_ln_g, m_sg_ln_b, m_sg_w, m_sg_b, m_w_mem_k, m_w_mem_v, m_w_out, m_ln_g, m_ln_b, v_w_in, v_q_norm_g, v_w_uq, v_kv_norm_g, v_w_ukv, v_sg_ln_g, v_sg_ln_b, v_sg_w, v_sg_b, v_w_mem_k, v_w_mem_v, v_w_out, v_ln_g, v_ln_b):
    wl = dict(w_in=w_in, q_norm_g=q_norm_g, w_uq=w_uq, kv_norm_g=kv_norm_g, w_ukv=w_ukv, sg_ln_g=sg_ln_g, sg_ln_b=sg_ln_b,
              sg_w=sg_w, sg_b=sg_b, w_mem_k=w_mem_k, w_mem_v=w_mem_v, w_out=w_out, ln_g=ln_g, ln_b=ln_b)
    ml = dict(w_in=m_w_in, q_norm_g=m_q_norm_g, w_uq=m_w_uq, kv_norm_g=m_kv_norm_g, w_ukv=m_w_ukv, sg_ln_g=m_sg_ln_g,
              sg_ln_b=m_sg_ln_b, sg_w=m_sg_w, sg_b=m_sg_b, w_mem_k=m_w_mem_k, w_mem_v=m_w_mem_v, w_out=m_w_out, ln_g=m_ln_g, ln_b=m_ln_b)
    vl = dict(w_in=v_w_in, q_norm_g=v_q_norm_g, w_uq=v_w_uq, kv_norm_g=v_kv_norm_g, w_ukv=v_w_ukv, sg_ln_g=v_sg_ln_g,
              sg_ln_b=v_sg_ln_b, sg_w=v_sg_w, sg_b=v_sg_b, w_mem_k=v_w_mem_k, w_mem_v=v_w_mem_v, w_out=v_w_out, ln_g=v_ln_g, ln_b=v_ln_b)
    depth = w_in.shape[0]

    w_in_t = jnp.pad(jnp.swapaxes(w_in, 1, 2), ((0, 0), (0, SHARD_IN_PAD - SHARD_IN), (0, 0)))
    shards = [s.astype(MXU) for s in (w_in_t, jnp.swapaxes(w_uq, 1, 2), jnp.swapaxes(w_ukv, 1, 2),
                                      jnp.concatenate([w_mem_k, w_mem_v], axis=-1), w_out)]

    def gather_items(l):
        return [(s[l], False, None) for s in shards]

    (first_in,) = _exchange("gather_weights", gather_items(0)[:1])

    def in_rows(g_in):
        return _in_rows_from_shards(g_in.reshape(D_IN_PAD, D_MODEL))

    def other_weights(g_uq, g_ukv, g_mem, g_out):
        return dict(w_uq=_uq_rows_pad(g_uq.reshape(-1, MLA_Q_RANK)), w_ukv=_ukv_rows_split(g_ukv.reshape(-1, MLA_KV_RANK)),
                    w_mem_kv=g_mem.reshape(D_MODEL, 2 * MEM_WIDTH), w_out=g_out.reshape(D_MODEL, D_MODEL))

    def weights_of(l, received):
        w = dict(q_norm_g=q_norm_g[l], kv_norm_g=kv_norm_g[l], sg_ln_g=sg_ln_g[l], sg_ln_b=sg_ln_b[l], sg_w=sg_w[l],
                 sg_b=sg_b[l], ln_g=ln_g[l], ln_b=ln_b[l])
        if l == 0:
            return dict(w, w_in=in_rows(first_in))
        return dict(w, w_in=in_rows(received[0]), **other_weights(*received[1:]))

    first_rest = (_Comm(gather_items(0)[1:]), lambda received: other_weights(*received))

    to_rows = dict(w_in=_in_rows_to_shards, w_uq=_uq_rows_unpad, w_ukv=_ukv_rows_unsplit)

    def grad_items(l, g, recv):
        return [(n, (to_rows.get(n, lambda a: a)(a), True, (recv[n], l))) for n, a in g.items()]

    shard_rows = dict(w_in=(D_IN_PAD, D_MODEL), w_uq=(MLA_HEADS * (MLA_NOPE + MLA_ROPE), MLA_Q_RANK),
                      w_ukv=(MLA_HEADS * 256, MLA_KV_RANK), w_mem_kv=(D_MODEL, 2 * MEM_WIDTH), w_out=(D_MODEL, D_MODEL))
    recv = {n: lax.empty((N_DEV, depth, rows // N_DEV, width), F32) for n, (rows, width) in shard_rows.items()}
    rep_sizes = [wl[n].size for n in REPLICATED]
    rep_rows = -(-sum(rep_sizes) // (8 * LANES)) * 8

    def packed(d):
        flat = jnp.concatenate([d[n].reshape(-1) for n in REPLICATED])
        return jnp.pad(flat, (0, rep_rows * LANES - flat.shape[0])).reshape(rep_rows, LANES)

    def rep_item(per_layer, dsgb):
        rep_g = dict(sg_b=dsgb, **{n: jnp.stack([g[n] for g in per_layer]) for n in REPLICATED if n != "sg_b"})
        return (packed(rep_g), False, None)

    loss_part, grad_x, grads, dsgb, recv = _local_step(
        x[0], mem[0], positions[0], loss_target[0], depth, weights_of, lambda l: _Comm(gather_items(l)), first_rest,
        grad_items, rep_item, recv)
    loss = lax.psum(loss_part, ("x", "y", "c"))
    r_in, r_uq, r_ukv, r_mem, r_out, r_rep = (recv[n] for n in ("w_in", "w_uq", "w_ukv", "w_mem_kv", "w_out", "replicated"))

    def rows2d(a):
        return a.reshape(-1, a.shape[-1])

    def update(name, contribs):
        outs = _adamw(contribs, rows2d(wl[name]), rows2d(ml[name]), rows2d(vl[name]))
        return [o.reshape(wl[name].shape) for o in outs]

    def from_transposed(recv, keep):
        s = _sum_contribs(recv.reshape(N_DEV, -1, recv.shape[-1])).reshape(recv.shape[1:])
        return rows2d(jnp.swapaxes(s[:, :keep], 1, 2))[None]

    res = dict(
        w_in=update("w_in", from_transposed(r_in, SHARD_IN)),
        w_uq=update("w_uq", from_transposed(r_uq, r_uq.shape[2])),
        w_ukv=update("w_ukv", from_transposed(r_ukv, r_ukv.shape[2])),
        w_mem_k=update("w_mem_k", r_mem[..., :MEM_WIDTH].reshape(N_DEV, -1, MEM_WIDTH)),
        w_mem_v=update("w_mem_v", r_mem[..., MEM_WIDTH:].reshape(N_DEV, -1, MEM_WIDTH)),
        w_out=update("w_out", r_out.reshape(N_DEV, -1, D_MODEL)))
    rep_outs = _adamw(r_rep, packed(wl), packed(ml), packed(vl))
    off = 0
    for n, size in zip(REPLICATED, rep_sizes):
        res[n] = [o.reshape(-1)[off:off + size].reshape(wl[n].shape) for o in rep_outs]
        off += size
    return (loss, grad_x[None], *[res[n][0] for n in WEIGHTS], *[res[n][1] for n in WEIGHTS],
            *[res[n][2] for n in WEIGHTS], *[res[n][3] for n in WEIGHTS])
```

```python
import functools
import math

import jax
import jax.numpy as jnp
from jax import lax
from jax.experimental import pallas as pl
from jax.experimental.pallas import tpu as pltpu

F32 = jnp.float32
MXU = jnp.bfloat16

DEPTH_FOR_NORM = 4
D_MODEL = 2048
CHUNK = 64
MLA_HEADS, MLA_NOPE, MLA_ROPE, MLA_V = 6, 128, 64, 128
MLA_Q_RANK, MLA_KV_RANK = 512, 256
MLA_WIDTH = MLA_HEADS * MLA_V
MLA_HEAD_PAD = 256
ROPE_THETA = 10000.0
SG_GROUPS, SG_CH, SG_CHUNK = 4, 128, 128
SG_WIDTH = SG_GROUPS * SG_CH
SB_HEADS, SB_DIM = 4, 128
SB_WIDTH = SB_HEADS * SB_DIM
MEM_TOKENS, MEM_HEADS, MEM_DIM = 256, 4, 64
MEM_WIDTH = MEM_HEADS * MEM_DIM
D_IN = 5696
D_IN_PAD = 5760
ALPHA = (2.0 * DEPTH_FOR_NORM) ** 0.25
LN_EPS = 1e-5
RMS_EPS = 1e-6
MLA_SCALE = 1.0 / math.sqrt(MLA_NOPE + MLA_ROPE)
SB_SCALE = 1.0 / math.sqrt(SB_DIM)
MEM_SCALE = 1.0 / math.sqrt(MEM_DIM)
LOG2E = math.log2(math.e)
LN2 = math.log(2.0)
ADAM_LR, ADAM_B1, ADAM_B2, ADAM_EPS, ADAM_WD, ADAM_STEP = 0.001, 0.9, 0.999, 1e-08, 0.01, 10
N_DEV = 8
LANES = 128
NEG = -1e30
SCAN = 256
SB_CUTOFF = 256.0
VMEM_LIMIT = 56 * 1024 * 1024

C_CQ, C_CKV, C_GA, C_SGU, C_SGV, C_GB = 0, 512, 768, 1536, 2048, 2560
C_SBQ, C_SBK, C_SBV, C_GC, C_MQ, C_GM, C_KPE = 3072, 3584, 4096, 4608, 5120, 5376, 5632

REPLICATED = ("q_norm_g", "kv_norm_g", "sg_ln_g", "sg_ln_b", "sg_w", "sg_b", "ln_g", "ln_b")
WEIGHTS = ("w_in", "q_norm_g", "w_uq", "kv_norm_g", "w_ukv", "sg_ln_g", "sg_ln_b", "sg_w", "sg_b",
           "w_mem_k", "w_mem_v", "w_out", "ln_g", "ln_b")


def _params(sem):
    return pltpu.CompilerParams(dimension_semantics=sem, vmem_limit_bytes=VMEM_LIMIT)


def _dot(a, b, dims):
    return lax.dot_general(a.astype(MXU), b.astype(MXU), (dims, ((), ())), preferred_element_type=F32)


NN, NT, TN = ((1,), (0,)), ((1,), (1,)), ((0,), (0,))


@jax.custom_vjp
def _mm(a, b):
    return _dot(a, b, NN)


def _mm_fwd(a, b):
    return _dot(a, b, NN), (a, b)


def _mm_bwd(res, g):
    a, b = res
    return _dot(g, b, NT), _dot(a, g, TN)


_mm.defvjp(_mm_fwd, _mm_bwd)


@jax.custom_vjp
def _mm_nt(a, b):
    return _dot(a, b, NT)


def _mm_nt_fwd(a, b):
    return _dot(a, b, NT), (a, b)


def _mm_nt_bwd(res, g):
    a, b = res
    return _dot(g, b, NN), _dot(g, a, TN)


_mm_nt.defvjp(_mm_nt_fwd, _mm_nt_bwd)


def _split_dot(x, tri):
    hi = x.astype(MXU)
    lo = (x - hi.astype(F32)).astype(MXU)
    return _dot(hi, tri, NN) + _dot(lo, tri, NN)


def _scan_keys(x, tri, reverse):
    w = tri.shape[0]
    nb = x.shape[1] // w
    outs, carry = [None] * nb, None
    for c in (reversed(range(nb)) if reverse else range(nb)):
        xc = x[:, c * w:(c + 1) * w]
        sc = _split_dot(xc, tri)
        outs[c] = sc if carry is None else sc + carry
        rs = jnp.sum(xc, axis=1, keepdims=True)
        carry = rs if carry is None else carry + rs
    return (outs[0] if nb == 1 else jnp.concatenate(outs, axis=1)), carry


def _col2row(col):
    return jnp.transpose(jnp.broadcast_to(col, (col.shape[0], LANES)))[0:8]


def _row2col(row):
    return jnp.transpose(jnp.broadcast_to(row, (LANES, row.shape[1])))[:, 0:1]


def _matmul(name, a, b, mode, tm, tn, tk, out_dtype=F32, add=None, add_scale=1.0, second_dtype=None,
            comm=None, with_received=False):
    if mode == "nn":
        (m, k), n = a.shape, b.shape[1]
    elif mode == "nt":
        (m, k), n = a.shape, b.shape[0]
    else:
        (k, m), n = a.shape, b.shape[1]
    tm, tn, tk = min(tm, m), min(tn, n), min(tk, k)
    assert m % tm == 0 and n % tn == 0 and k % tk == 0, (name, m, n, k, tm, tn, tk)
    nk = k // tk
    dims = {"nn": NN, "nt": NT, "tn": TN}[mode]
    a_spec = (pl.BlockSpec((tk, tm), lambda j, i, kk: (kk, i)) if mode == "tn"
              else pl.BlockSpec((tm, tk), lambda j, i, kk: (i, kk)))
    b_spec = (pl.BlockSpec((tn, tk), lambda j, i, kk: (j, kk)) if mode == "nt"
              else pl.BlockSpec((tk, tn), lambda j, i, kk: (kk, j)))
    o_spec = pl.BlockSpec((tm, tn), lambda j, i, kk: (i, j))
    in_specs, operands = [a_spec, b_spec], [a, b]
    if add is not None:
        in_specs.append(o_spec)
        operands.append(add)
    out_shape = [jax.ShapeDtypeStruct((m, n), out_dtype)]
    out_specs = [o_spec]
    if second_dtype is not None:
        out_shape.append(jax.ShapeDtypeStruct((m, n), second_dtype))
        out_specs.append(o_spec)

    def body(*refs):
        a_ref, b_ref = refs[0], refs[1]
        add_ref = refs[2] if add is not None else None
        outs = refs[len(operands):-1]
        acc_ref = refs[-1]
        kk = pl.program_id(2)
        if nk == 1:
            r = _dot(a_ref[...], b_ref[...], dims)
            if add_ref is not None:
                r = r + add_scale * add_ref[...]
            outs[0][...] = r.astype(out_dtype)
            if second_dtype is not None:
                outs[1][...] = r.astype(second_dtype)
            return

        @pl.when(kk == 0)
        def _():
            acc_ref[...] = jnp.zeros_like(acc_ref)

        acc_ref[...] += _dot(a_ref[...], b_ref[...], dims)

        @pl.when(kk == nk - 1)
        def _():
            r = acc_ref[...]
            if add_ref is not None:
                r = r + add_scale * add_ref[...]
            outs[0][...] = r.astype(out_dtype)
            if second_dtype is not None:
                outs[1][...] = r.astype(second_dtype)

    res, received = _comm_call(body, name, (n // tn, m // tm, nk), in_specs, out_specs, out_shape, operands,
                               ("parallel", "parallel", "arbitrary"), comm, scratch=[pltpu.VMEM((tm, tn) if nk > 1 else (8, LANES), F32)])
    res = res if second_dtype is not None else res[0]
    return (res, received) if with_received else res


def _accumulate(ref, val, first):
    @pl.when(first)
    def _():
        ref[...] = val

    @pl.when(jnp.logical_not(first))
    def _():
        ref[...] += val


def _rowcall(name, fn, rows, tm, row_in, const_in, row_out, acc_out=()):
    tm = min(tm, rows)
    assert rows % tm == 0
    n_in = len(row_in) + len(const_in)
    in_specs = [pl.BlockSpec((tm, w), lambda i, cb=cb: (i, cb)) for _, w, cb in row_in]
    in_specs += [pl.BlockSpec(c.shape, lambda i, nd=c.ndim: (0,) * nd) for c in const_in]
    out_specs = [pl.BlockSpec(blk, imap) for _, _, blk, imap in row_out]
    out_specs += [pl.BlockSpec(s, lambda i, nd=len(s): (0,) * nd) for s, _ in acc_out]
    out_shape = [jax.ShapeDtypeStruct(s, dt) for s, dt, _, _ in row_out]
    out_shape += [jax.ShapeDtypeStruct(s, dt) for s, dt in acc_out]

    def body(*refs):
        vals = fn(*[r[...] for r in refs[:n_in]])
        outs = refs[n_in:]
        first = pl.program_id(0) == 0
        for k, (_, dt, _, _) in enumerate(row_out):
            outs[k][...] = vals[k].astype(dt)
        for k in range(len(acc_out)):
            _accumulate(outs[len(row_out) + k], vals[len(row_out) + k], first)

    return pl.pallas_call(
        body, name=name, grid=(rows // tm,), in_specs=in_specs, out_specs=out_specs, out_shape=out_shape,
        compiler_params=_params(("arbitrary",) if acc_out else ("parallel",)),
    )(*[a for a, _, _ in row_in], *const_in)


def _ro(rows, width, dtype, tm):
    tm = min(tm, rows)
    return ((rows, width), dtype, (tm, width), lambda i: (i, 0))


def _rms(x, g):
    ms = jnp.mean(x * x, axis=-1, keepdims=True)
    return x * lax.rsqrt(ms + RMS_EPS) * g


def _ln(x, g, b):
    mu = jnp.mean(x, axis=-1, keepdims=True)
    xc = x - mu
    var = jnp.mean(xc * xc, axis=-1, keepdims=True)
    return xc * lax.rsqrt(var + LN_EPS) * g + b


def _gelu(x):
    return 0.5 * x * (1.0 + jnp.tanh(math.sqrt(2.0 / math.pi) * (x + 0.044715 * (x * x * x))))


def _silu(x):
    return x / (1.0 + jnp.exp(-x))


@jax.custom_vjp
def _rope(r, cc, sa, sb):
    return r * cc + pltpu.roll(r, 96, 1) * sa + pltpu.roll(r, 32, 1) * sb


def _rope_fwd(r, cc, sa, sb):
    return _rope(r, cc, sa, sb), (cc, sa, sb)


def _rope_bwd(res, d):
    cc, sa, sb = res
    return d * cc + pltpu.roll(d * sa, 32, 1) + pltpu.roll(d * sb, 96, 1), None, None, None


_rope.defvjp(_rope_fwd, _rope_bwd)


def _rope_tables_fn(pos, invf):
    ang = pos.astype(F32) * invf
    lane = lax.broadcasted_iota(jnp.int32, ang.shape, 1)
    cos, sin = jnp.cos(ang), jnp.sin(ang)
    zero = jnp.zeros_like(ang)
    cc = jnp.where(lane < 64, cos, zero)
    sa = jnp.where(lane < 32, -sin, zero)
    sb = jnp.where((lane >= 32) & (lane < 64), sin, zero)
    return cc, sa, sb


def _prep_fn(cq, ckv, kpe, cc, sa, sb, gq, wuq, gkv, wukv):
    q = _mm_nt(_rms(cq, gq), wuq) * (MLA_SCALE * LOG2E)
    kv = _mm_nt(_rms(ckv, gkv), wukv)
    krot = _rope(kpe, cc, sa, sb)
    qs, ks = [], []
    for h in range(MLA_HEADS):
        o = h * MLA_HEAD_PAD
        qs += [q[:, o:o + 128], _rope(q[:, o + 128:o + 256], cc, sa, sb)]
        ks += [kv[:, h * 128:(h + 1) * 128], krot]
    return jnp.concatenate(qs, axis=1), jnp.concatenate(ks, axis=1), kv[:, MLA_HEADS * 128:]


def _post_fn(ga, sgu, sgv, gb, gc, mq, gm, oa, oc, mkv, lng, lnb, sgw, biasf):
    tm = ga.shape[0]
    u = _gelu(sgu)
    vn = _ln(_gelu(sgv), lng, lnb)
    t_chunk = lax.broadcasted_iota(jnp.int32, (SG_CHUNK, SG_CHUNK), 0) // CHUNK
    s_chunk = lax.broadcasted_iota(jnp.int32, (SG_CHUNK, SG_CHUNK), 1) // CHUNK
    sg_mask = (s_chunk <= t_chunk).astype(F32)
    cols = []
    for g in range(SG_GROUPS):
        wg = sgw[g] * sg_mask
        vg = vn[:, g * SG_CH:(g + 1) * SG_CH]
        cols.append(jnp.concatenate(
            [_mm(wg, vg[n * SG_CHUNK:(n + 1) * SG_CHUNK, :]) for n in range(tm // SG_CHUNK)], axis=0))
    mixed = jnp.concatenate(cols, axis=1) + jnp.concatenate([biasf] * (tm // SG_CHUNK), axis=0)
    ob = u * mixed
    mk, mv = mkv[:, :MEM_WIDTH], mkv[:, MEM_WIDTH:]
    head = lax.broadcasted_iota(jnp.int32, (1, MEM_WIDTH), 1) // MEM_DIM
    om = jnp.zeros((tm, MEM_WIDTH), F32)
    for hd in range(MEM_HEADS):
        hm = (head == hd).astype(F32)
        s = _mm_nt(mq, mk * hm) * MEM_SCALE
        e = jnp.exp(s - lax.stop_gradient(jnp.max(s, axis=-1, keepdims=True)))
        p = e / jnp.sum(e, axis=-1, keepdims=True)
        om = om + _mm(p, mv * hm)
    return jnp.concatenate([oa * _silu(ga), ob * _silu(gb), oc * _silu(gc), om * _silu(gm)], axis=1)


def _post_bwd_fn(dyg, ga, sgu, sgv, gb, gc, mq, gm, oa, oc, mkv, lng, lnb, sgw, biasf):
    _, vjp = jax.vjp(_post_fn, ga, sgu, sgv, gb, gc, mq, gm, oa, oc, mkv, lng, lnb, sgw, biasf)
    dga, dsgu, dsgv, dgb, dgc, dmq, dgm, doa, doc, dmkv, dlng, dlnb, dsgw, dbias = vjp(dyg)
    drows = jnp.stack([_col2row(jnp.sum((doa * oa)[:, h * 128:(h + 1) * 128], axis=1, keepdims=True))
                       for h in range(MLA_HEADS)], axis=0)
    return dga, dsgu, dsgv, dgb, dgc, dmq, dgm, doa, doc, drows, dmkv, dlng, dlnb, dsgw, dbias


def _prep_bwd_fn(dq, dk, dv, cq, ckv, kpe, cc, sa, sb, gq, wuq, gkv, wukv):
    wuq, wukv = wuq.astype(F32), wukv.astype(F32)
    _, vjp = jax.vjp(lambda a, b, c, d, e, f, g: _prep_fn(a, b, c, cc, sa, sb, d, e, f, g), cq, ckv, kpe, gq, wuq, gkv, wukv)
    return vjp((dq, dk, dv))


def _ln_bwd_fn(dxn, z, g, b):
    _, vjp = jax.vjp(_ln, z, g, b)
    dz, dg, db = vjp(dxn)
    return dz, dz, dg, db


def _outproj_fn(yg, x, w, g, b):
    z = ALPHA * x + _dot(yg, w, NN)
    xn = _ln(z, g, b)
    return z, xn, xn


def _loss_fn(xl, tgt):
    err = xl - tgt
    part = 0.5 * jnp.sum(jnp.mean(err * err, axis=-1, keepdims=True), axis=0, keepdims=True)
    return err * (1.0 / D_MODEL), jnp.broadcast_to(part, (1, LANES))


def _mla_fwd(q, k, v, tq, comm=None):
    t = q.shape[0]
    tq = min(tq, t)
    nq = t // tq

    def body(q_ref, k_ref, v_ref, o_ref, lse_ref):
        i = pl.program_id(1)
        qb = q_ref[...]
        r_chunk = lax.broadcasted_iota(jnp.int32, (tq, tq), 0) // CHUNK
        c_chunk = lax.broadcasted_iota(jnp.int32, (tq, tq), 1) // CHUNK
        diag_ok = c_chunk <= r_chunk

        def block(j, carry, masked):
            m, l, acc = carry
            off = pl.multiple_of(j * tq, tq)
            s = _dot(qb, k_ref[pl.ds(off, tq), :], NT)
            if masked:
                s = jnp.where(diag_ok, s, NEG)
            m2 = jnp.maximum(m, jnp.max(s, axis=1, keepdims=True))
            p = jnp.exp2(s - m2)
            a = jnp.exp2(m - m2)
            return m2, a * l + jnp.sum(p, axis=1, keepdims=True), a * acc + _dot(p, v_ref[pl.ds(off, tq), :], NN)

        init = (jnp.full((tq, 1), NEG, F32), jnp.zeros((tq, 1), F32), jnp.zeros((tq, MLA_V), F32))
        carry = lax.fori_loop(0, i, lambda j, c: block(j, c, False), init)
        m, l, acc = block(i, carry, True)
        o_ref[...] = acc / l
        lse_ref[...] = _col2row(m + jnp.log2(l))

    (o, lse), received = _comm_call(
        body, "mla_fwd", (MLA_HEADS, nq),
        [pl.BlockSpec((tq, MLA_HEAD_PAD), lambda h, i: (i, h)),
         pl.BlockSpec((t, MLA_HEAD_PAD), lambda h, i: (0, h)),
         pl.BlockSpec((t, MLA_V), lambda h, i: (0, h))],
        [pl.BlockSpec((tq, MLA_V), lambda h, i: (i, h)),
         pl.BlockSpec((None, 8, tq), lambda h, i: (h, 0, i))],
        [jax.ShapeDtypeStruct((t, MLA_WIDTH), F32), jax.ShapeDtypeStruct((MLA_HEADS, 8, t), F32)],
        [q, k, v], ("parallel", "parallel"), comm)
    return o, lse, received


def _mla_bwd(q, k, v, do, lse, drow, tq, comm=None):
    t = q.shape[0]
    tq = min(tq, t)
    nq = t // tq

    def body(k_ref, v_ref, q_ref, do_ref, lse_ref, d_ref, dk_ref, dv_ref, dq_ref):
        j = pl.program_id(1)

        @pl.when(j == 0)
        def _():
            dq_ref[...] = jnp.zeros_like(dq_ref)

        kj, vj = k_ref[...], v_ref[...]
        k_chunk = lax.broadcasted_iota(jnp.int32, (tq, tq), 0) // CHUNK
        q_chunk = lax.broadcasted_iota(jnp.int32, (tq, tq), 1) // CHUNK
        diag_ok = k_chunk <= q_chunk

        def block(i, carry, masked):
            dk, dv = carry
            off = pl.multiple_of(i * tq, tq)
            qi, doi = q_ref[pl.ds(off, tq), :], do_ref[pl.ds(off, tq), :]
            pt = jnp.exp2(_dot(kj, qi, NT) - lse_ref[0:1, pl.ds(off, tq)])
            if masked:
                pt = jnp.where(diag_ok, pt, 0.0)
            dv = dv + _dot(pt, doi, NN)
            dst = (pt * (_dot(vj, doi, NT) - d_ref[0:1, pl.ds(off, tq)])).astype(MXU)
            dk = dk + _dot(dst, qi, NN)
            dq_ref[pl.ds(off, tq), :] += _dot(dst, kj, TN) * LN2
            return dk, dv

        carry = block(j, (jnp.zeros((tq, MLA_HEAD_PAD), F32), jnp.zeros((tq, MLA_V), F32)), True)
        dk, dv = lax.fori_loop(j + 1, nq, lambda i, c: block(i, c, False), carry)
        dk_ref[...] = dk * LN2
        dv_ref[...] = dv

    (dk, dv, dq), received = _comm_call(
        body, "mla_bwd", (MLA_HEADS, nq),
        [pl.BlockSpec((tq, MLA_HEAD_PAD), lambda h, j: (j, h)),
         pl.BlockSpec((tq, MLA_V), lambda h, j: (j, h)),
         pl.BlockSpec((t, MLA_HEAD_PAD), lambda h, j: (0, h)),
         pl.BlockSpec((t, MLA_V), lambda h, j: (0, h)),
         pl.BlockSpec((None, 8, t), lambda h, j: (h, 0, 0)),
         pl.BlockSpec((None, 8, t), lambda h, j: (h, 0, 0))],
        [pl.BlockSpec((tq, MLA_HEAD_PAD), lambda h, j: (j, h)),
         pl.BlockSpec((tq, MLA_V), lambda h, j: (j, h)),
         pl.BlockSpec((t, MLA_HEAD_PAD), lambda h, j: (0, h))],
        [jax.ShapeDtypeStruct((t, MLA_HEADS * MLA_HEAD_PAD), F32), jax.ShapeDtypeStruct((t, MLA_WIDTH), F32),
         jax.ShapeDtypeStruct((t, MLA_HEADS * MLA_HEAD_PAD), F32)],
        [k, v, q, do, lse, drow], ("parallel", "arbitrary"), comm)
    return dq, dk, dv, received


def _sb_logs(z2):
    lb = jnp.minimum(z2, 0.0) - jnp.log2(1.0 + jnp.exp2(jnp.minimum(z2, -z2)))
    return lb, lb - z2


def _tri(n, keep):
    r = lax.broadcasted_iota(jnp.int32, (n, n), 0)
    c = lax.broadcasted_iota(jnp.int32, (n, n), 1)
    return keep(r, c)


def _sb_fwd(hb, tq):
    t = hb.shape[0]
    tq = min(tq, t)
    nq = t // tq
    cq, ck, cv = C_SBQ // SB_DIM, C_SBK // SB_DIM, C_SBV // SB_DIM

    def body(q_ref, k_ref, v_ref, o_ref, tot_ref, first_ref):
        h, i = pl.program_id(0), pl.program_id(1)
        qb = q_ref[...]
        strict = _tri(tq, lambda r, c: c < r)
        later = _tri(min(SCAN, tq), lambda r, c: r > c).astype(MXU)

        def block(j, carry, masked):
            run, acc = carry
            off = pl.multiple_of(j * tq, tq)
            lb, l1 = _sb_logs(_dot(qb, k_ref[pl.ds(off, tq), :], NT) * (SB_SCALE * LOG2E))
            if masked:
                l1 = jnp.where(strict, l1, 0.0)
            after, rowsum = _scan_keys(l1, later, True)
            a = jnp.exp2(lb + after + run)
            if masked:
                a = jnp.where(strict, a, 0.0)
            return run + rowsum, acc + _dot(a, v_ref[pl.ds(off, tq), :], NN)

        carry = block(i, (jnp.zeros((tq, 1), F32), jnp.zeros((tq, SB_DIM), F32)), True)

        def live(state):
            jj, (run, _) = state
            return jnp.logical_and(jj < i, jnp.max(run) > -SB_CUTOFF)

        jj, (run, acc) = lax.while_loop(live, lambda st: (st[0] + 1, block(i - 1 - st[0], st[1], False)),
                                        (jnp.int32(0), carry))
        o_ref[...] = acc
        tot_ref[...] = _col2row(run)
        first_ref[h, i] = i - jj

    return _comm_call(
        body, "sb_fwd", (SB_HEADS, nq),
        [pl.BlockSpec((tq, SB_DIM), lambda h, i: (i, cq + h)),
         pl.BlockSpec((t, SB_DIM), lambda h, i: (0, ck + h)),
         pl.BlockSpec((t, SB_DIM), lambda h, i: (0, cv + h))],
        [pl.BlockSpec((tq, SB_DIM), lambda h, i: (i, h)),
         pl.BlockSpec((None, 8, tq), lambda h, i: (h, 0, i)),
         pl.BlockSpec(memory_space=pltpu.SMEM)],
        [jax.ShapeDtypeStruct((t, SB_WIDTH), F32), jax.ShapeDtypeStruct((SB_HEADS, 8, t), F32),
         jax.ShapeDtypeStruct((SB_HEADS, nq), jnp.int32)],
        [hb, hb, hb], ("arbitrary", "arbitrary"), None)[0]


def _sb_bwd(hb, do, tot, first, tq):
    t = hb.shape[0]
    tq = min(tq, t)
    nq = t // tq
    cq, ck, cv = C_SBQ // SB_DIM, C_SBK // SB_DIM, C_SBV // SB_DIM

    def body(q_ref, do_ref, tot_ref, k_ref, v_ref, first_ref, dq_ref, dk_ref, dv_ref):
        h, i = pl.program_id(0), pl.program_id(1)

        @pl.when(i == 0)
        def _():
            dk_ref[...] = jnp.zeros_like(dk_ref)
            dv_ref[...] = jnp.zeros_like(dv_ref)

        qb, dob = q_ref[...], do_ref[...]
        total = _row2col(tot_ref[0:1, :])
        strict = _tri(tq, lambda r, c: c < r)
        upto = _tri(min(SCAN, tq), lambda r, c: r <= c).astype(MXU)
        before = _tri(min(SCAN, tq), lambda r, c: r < c).astype(MXU)

        def block(j, carry, masked):
            pre, gpre, dq = carry
            off = pl.multiple_of(j * tq, tq)
            kj, vj = k_ref[pl.ds(off, tq), :], v_ref[pl.ds(off, tq), :]
            lb, l1 = _sb_logs(_dot(qb, kj, NT) * (SB_SCALE * LOG2E))
            if masked:
                l1 = jnp.where(strict, l1, 0.0)
            upto_sum, rowsum = _scan_keys(l1, upto, False)
            a = jnp.exp2(lb + (total - (upto_sum + pre)))
            if masked:
                a = jnp.where(strict, a, 0.0)
            g = a * _dot(dob, vj, NT)
            before_sum, grow = _scan_keys(g, before, False)
            beta = jnp.exp2(lb)
            dz = (g * (1.0 - beta) - (before_sum + gpre) * beta) * SB_SCALE
            if masked:
                dz = jnp.where(strict, dz, 0.0)
            dzb = dz.astype(MXU)
            dk_ref[pl.ds(off, tq), :] += _dot(dzb, qb, TN)
            dv_ref[pl.ds(off, tq), :] += _dot(a, dob, TN)
            return pre + rowsum, gpre + grow, dq + _dot(dzb, kj, NN)

        init = (jnp.zeros((tq, 1), F32), jnp.zeros((tq, 1), F32), jnp.zeros((tq, SB_DIM), F32))
        carry = lax.fori_loop(first_ref[h, i], i, lambda j, c: block(j, c, False), init)
        dq_ref[...] = block(i, carry, True)[2].astype(dq_ref.dtype)

    return _comm_call(
        body, "sb_bwd", (SB_HEADS, nq),
        [pl.BlockSpec((tq, SB_DIM), lambda h, i: (i, cq + h)),
         pl.BlockSpec((tq, SB_DIM), lambda h, i: (i, h)),
         pl.BlockSpec((None, 8, tq), lambda h, i: (h, 0, i)),
         pl.BlockSpec((t, SB_DIM), lambda h, i: (0, ck + h)),
         pl.BlockSpec((t, SB_DIM), lambda h, i: (0, cv + h)),
         pl.BlockSpec(memory_space=pltpu.SMEM)],
        [pl.BlockSpec((tq, SB_DIM), lambda h, i: (i, h)),
         pl.BlockSpec((t, SB_DIM), lambda h, i: (0, h)),
         pl.BlockSpec((t, SB_DIM), lambda h, i: (0, h))],
        [jax.ShapeDtypeStruct((t, SB_WIDTH), MXU), jax.ShapeDtypeStruct((t, SB_WIDTH), F32),
         jax.ShapeDtypeStruct((t, SB_WIDTH), F32)],
        [hb, do, tot, hb, hb, first], ("parallel", "arbitrary"), None)[0]


TM = 256
TQ_MLA = 1024
TQ_SB = 512


def _layer_fwd(xf, xb, w, consts, comm=None, rest=None):
    t = xf.shape[0]
    cc, sa, sb, mem = consts
    (h, hb), arrived = _matmul("in_proj", xb, w["w_in"], "nt", 512, 1920, D_MODEL, second_dtype=MXU,
                               comm=rest[0] if rest else None, with_received=True)
    if rest:
        w = dict(w, **rest[1](arrived))
    gq, gkv = w["q_norm_g"][None, :], w["kv_norm_g"][None, :]
    q, k, v = _rowcall(
        "mla_prep", _prep_fn, t, TM,
        [(h, 512, C_CQ // 512), (h, 256, C_CKV // 256), (h, 128, C_KPE // 128), (cc, 128, 0), (sa, 128, 0), (sb, 128, 0)],
        [gq, w["w_uq"], gkv, w["w_ukv"]],
        [_ro(t, MLA_HEADS * MLA_HEAD_PAD, MXU, TM), _ro(t, MLA_HEADS * MLA_HEAD_PAD, MXU, TM), _ro(t, MLA_WIDTH, MXU, TM)])
    oa, lse, received = _mla_fwd(q, k, v, TQ_MLA, comm)
    oc, tot, first = _sb_fwd(hb, TQ_SB)
    mkv = _matmul("mem_kv", mem, w["w_mem_kv"], "nn", 256, 512, 2048)
    lng, lnb = w["sg_ln_g"][None, :], w["sg_ln_b"][None, :]
    biasf = jnp.repeat(w["sg_b"].T, SG_CH, axis=1)
    post_rows = [(h, 768, C_GA // 768), (h, 512, C_SGU // 512), (h, 512, C_SGV // 512), (h, 512, C_GB // 512),
                 (h, 512, C_GC // 512), (h, 256, C_MQ // 256), (h, 256, C_GM // 256), (oa, 768, 0), (oc, 512, 0)]
    post_consts = [mkv, lng, lnb, w["sg_w"], biasf]
    (yg,) = _rowcall("mixer_post", lambda *a: (_post_fn(*a),), t, TM, post_rows, post_consts, [_ro(t, D_MODEL, MXU, TM)])
    z, xn, xnb = _rowcall(
        "out_proj_ln", _outproj_fn, t, TM, [(yg, D_MODEL, 0), (xf, D_MODEL, 0)],
        [w["w_out"], w["ln_g"][None, :], w["ln_b"][None, :]],
        [_ro(t, D_MODEL, F32, TM), _ro(t, D_MODEL, F32, TM), _ro(t, D_MODEL, MXU, TM)])
    saved = dict(w=w, xb=xb, h=h, hb=hb, q=q, k=k, v=v, oa=oa, lse=lse, oc=oc, tot=tot, first=first, mkv=mkv, yg=yg, z=z,
                 post_rows=post_rows, post_consts=post_consts, gq=gq, gkv=gkv)
    return xn, xnb, saved, received


def _layer_bwd(dxn, w, s, consts, recv=None, early_items=None, late_items=None):
    t = dxn.shape[0]
    cc, sa, sb, mem = consts
    h = s["h"]
    dz, dzb, dlng2, dlnb2 = _rowcall(
        "ln_bwd", _ln_bwd_fn, t, TM, [(dxn, D_MODEL, 0), (s["z"], D_MODEL, 0)],
        [w["ln_g"][None, :], w["ln_b"][None, :]],
        [_ro(t, D_MODEL, F32, TM), _ro(t, D_MODEL, MXU, TM)], [((1, D_MODEL), F32), ((1, D_MODEL), F32)])
    dyg = _matmul("out_proj_dx", dzb, w["w_out"], "nt", 512, 1024, 2048)
    dw_out = _matmul("out_proj_dw", s["yg"], dzb, "tn", 1024, 1024, 2048)
    tm = min(TM, t)
    outs = _rowcall(
        "mixer_post_bwd", _post_bwd_fn, t, TM, [(dyg, D_MODEL, 0)] + s["post_rows"], s["post_consts"],
        [_ro(t, 768, MXU, TM), _ro(t, 512, MXU, TM), _ro(t, 512, MXU, TM), _ro(t, 512, MXU, TM), _ro(t, 512, MXU, TM),
         _ro(t, 256, MXU, TM), _ro(t, 256, MXU, TM), _ro(t, 768, MXU, TM), _ro(t, 512, MXU, TM),
         ((MLA_HEADS, 8, t), F32, (MLA_HEADS, 8, tm), lambda i: (0, 0, i))],
        [((MEM_TOKENS, 2 * MEM_WIDTH), F32), ((1, SG_WIDTH), F32), ((1, SG_WIDTH), F32),
         ((SG_GROUPS, SG_CHUNK, SG_CHUNK), F32), ((SG_CHUNK, SG_WIDTH), F32)])
    dga, dsgu, dsgv, dgb, dgc, dmq, dgm, doa, doc, drow, dmkv, dsg_ln_g, dsg_ln_b, dsgw, dbias = outs
    dw_mem_kv = _matmul("mem_kv_dw", mem, dmkv, "tn", 1024, 512, 256)
    dsbq, dsbk, dsbv = _sb_bwd(s["hb"], doc, s["tot"], s["first"], TQ_SB)
    named = early_items(dict(w_out=dw_out, w_mem_kv=dw_mem_kv)) if early_items else []
    dq, dk, dv, received = _mla_bwd(s["q"], s["k"], s["v"], doa, s["lse"], drow, TQ_MLA,
                                    _Comm([item for _, item in named]) if named else None)
    if named:
        recv.update({n: r for (n, _), r in zip(named, received)})
    wq, wkv = MLA_HEADS * MLA_HEAD_PAD, MLA_WIDTH
    dcq, dckv, dkpe, dgq, dwuq, dgkv, dwukv = _rowcall(
        "mla_prep_bwd", _prep_bwd_fn, t, TM,
        [(dq, wq, 0), (dk, wq, 0), (dv, wkv, 0), (h, 512, C_CQ // 512), (h, 256, C_CKV // 256), (h, 128, C_KPE // 128),
         (cc, 128, 0), (sa, 128, 0), (sb, 128, 0)],
        [s["gq"], w["w_uq"], s["gkv"], w["w_ukv"]],
        [_ro(t, 512, MXU, TM), _ro(t, 256, MXU, TM), _ro(t, 128, MXU, TM)],
        [((1, MLA_Q_RANK), F32), ((wq, MLA_Q_RANK), F32), ((1, MLA_KV_RANK), F32), ((2 * wkv, MLA_KV_RANK), F32)])
    dh = jnp.concatenate([dcq, dckv, dga, dsgu, dsgv, dgb, dsbq, dsbk.astype(MXU), dsbv.astype(MXU), dgc, dmq, dgm, dkpe], axis=1)
    dw_in = _matmul("in_proj_dw", dh, s["xb"], "tn", 1152, 1024, 2048)
    replicated = dict(q_norm_g=dgq[0], kv_norm_g=dgkv[0], sg_ln_g=dsg_ln_g[0], sg_ln_b=dsg_ln_b[0], sg_w=dsgw, sg_bias=dbias,
                      ln_g=dlng2[0], ln_b=dlnb2[0])
    named = late_items(dict(w_in=dw_in, w_uq=dwuq, w_ukv=dwukv), replicated) if late_items else []
    dx, received = _matmul("in_proj_dx", dh, w["w_in"], "nn", 512, 1024, D_IN_PAD, add=dz, add_scale=ALPHA,
                           comm=_Comm([item for _, item in named]) if named else None, with_received=True)
    if named:
        recv.update({n: r for (n, _), r in zip(named, received)})
    return dx, dict(replicated, w_in=dw_in, w_uq=dwuq, w_ukv=dwukv, w_mem_kv=dw_mem_kv, w_out=dw_out)


def _sg_bias_grad(bias_grads):
    db = jnp.concatenate(bias_grads, axis=0)

    def lane_sums(d):
        return (jnp.concatenate([jnp.broadcast_to(jnp.sum(d[:, g * SG_CH:(g + 1) * SG_CH], axis=1, keepdims=True),
                                                  (d.shape[0], SG_CH)) for g in range(SG_GROUPS)], axis=1),)

    (dbs,) = _rowcall("sg_bias_sum", lane_sums, db.shape[0], SG_CHUNK, [(db, SG_WIDTH, 0)], [], [_ro(db.shape[0], SG_WIDTH, F32, SG_CHUNK)])
    return dbs[:, ::SG_CH].reshape(len(bias_grads), SG_CHUNK, SG_GROUPS).transpose(0, 2, 1)


def _local_step(x, mem, positions, target, depth, weights_of, gather_comm, first_rest, grad_items, rep_item, recv):
    t = x.shape[0]
    inv_freq = ROPE_THETA ** (-jnp.arange(0, MLA_ROPE, 2, dtype=F32) / MLA_ROPE)
    invf = jnp.tile(inv_freq, 4)[None, :]
    cc, sa, sb = _rowcall("rope_tables", _rope_tables_fn, t, TM, [(positions[:, None], 1, 0)], [invf],
                          [_ro(t, 128, F32, TM)] * 3)
    consts = (cc, sa, sb, mem)
    xf, xb, saved, received = x, x.astype(MXU), [], None
    for l in range(depth):
        xf, xb, s, received = _layer_fwd(xf, xb, weights_of(l, received), consts, gather_comm(l + 1) if l + 1 < depth else None,
                                         first_rest if l == 0 else None)
        saved.append(s)
    dx, loss_part = _rowcall("loss", _loss_fn, t, TM, [(xf, D_MODEL, 0), (target, D_MODEL, 0)], [],
                             [_ro(t, D_MODEL, F32, TM)], [((1, LANES), F32)])
    grads, recv, sg_b_grad = [None] * depth, dict(recv or {}), []
    for l in reversed(range(depth)):
        def early(g, l=l):
            rest = {n: grads[l + 1][n] for n in ("w_in", "w_uq", "w_ukv")} if l + 1 < depth else {}
            return grad_items(l, g, recv) + (grad_items(l + 1, rest, recv) if rest else [])

        def late(g, replicated):
            per_layer = [replicated] + grads[1:]
            sg_b_grad.append(_sg_bias_grad([r["sg_bias"] for r in per_layer]))
            return grad_items(0, g, recv) + [("replicated", rep_item(per_layer, sg_b_grad[0]))]

        dx, grads[l] = _layer_bwd(dx, saved[l]["w"], saved[l], consts, recv, early if grad_items else None,
                                  late if grad_items and l == 0 else None)
    dsgb = sg_b_grad[0] if sg_b_grad else _sg_bias_grad([g["sg_bias"] for g in grads])
    return loss_part[0, 0], dx, grads, dsgb, recv


SHARD_IN = D_IN // N_DEV
SHARD_IN_PAD = D_IN_PAD // N_DEV
IN_RANGES = ((0, 768), (832, D_IN), (768, 832))


def _in_rows_from_shards(w):
    pieces = []
    for a, b in IN_RANGES:
        while a < b:
            e = min(b, (a // SHARD_IN + 1) * SHARD_IN)
            row = (a // SHARD_IN) * SHARD_IN_PAD + a % SHARD_IN
            pieces.append(w[row:row + e - a])
            a = e
    pieces.append(jnp.zeros((D_IN_PAD - D_IN, w.shape[1]), w.dtype))
    return jnp.concatenate(pieces, axis=0)


def _in_rows_to_shards(g):
    pieces = []
    for d in range(N_DEV):
        a, b = d * SHARD_IN, (d + 1) * SHARD_IN
        while a < b:
            e = min(b, 768 if a < 768 else (832 if a < 832 else D_IN))
            row = a if a < 768 else (C_KPE + a - 768 if a < 832 else a - 64)
            pieces.append(g[row:row + e - a])
            a = e
        pieces.append(jnp.zeros((SHARD_IN_PAD - SHARD_IN, g.shape[1]), g.dtype))
    return jnp.concatenate(pieces, axis=0)


def _uq_rows_pad(w):
    w = w.reshape(MLA_HEADS, MLA_NOPE + MLA_ROPE, w.shape[-1])
    return jnp.pad(w, ((0, 0), (0, MLA_HEAD_PAD - MLA_NOPE - MLA_ROPE), (0, 0))).reshape(MLA_HEADS * MLA_HEAD_PAD, -1)


def _uq_rows_unpad(g):
    return g.reshape(MLA_HEADS, MLA_HEAD_PAD, -1)[:, :MLA_NOPE + MLA_ROPE].reshape(MLA_HEADS * (MLA_NOPE + MLA_ROPE), -1)


def _ukv_rows_split(w):
    return jnp.swapaxes(w.reshape(MLA_HEADS, 2, 128, -1), 0, 1).reshape(MLA_HEADS * 256, -1)


def _ukv_rows_unsplit(g):
    return jnp.swapaxes(g.reshape(2, MLA_HEADS, 128, -1), 0, 1).reshape(MLA_HEADS * 256, -1)


class _Comm:
    def __init__(self, items):
        self.items = items
        self.n = len(items)
        self.pieces = [((s.shape[0] // N_DEV,) + s.shape[1:]) if sc else s.shape for s, sc, _ in items]
        self.inputs = [s for s, _, _ in items] + [d[0] for _, _, d in items if d is not None]
        any_space = pl.BlockSpec(memory_space=pl.ANY)
        self.in_specs = [any_space] * len(self.inputs)
        self.out_specs = [any_space] * self.n
        self.out_shape = [jax.ShapeDtypeStruct((N_DEV,) + tuple(p), s.dtype) if d is None
                          else jax.ShapeDtypeStruct(d[0].shape, d[0].dtype) for p, (s, _, d) in zip(self.pieces, items)]
        self.scratch = [pltpu.SemaphoreType.DMA((N_DEV - 1, self.n)), pltpu.SemaphoreType.DMA((N_DEV - 1, self.n)),
                        pltpu.SemaphoreType.DMA((self.n,))]

    def aliases(self, first_in, first_out):
        passed = [a for a, (_, _, d) in enumerate(self.items) if d is not None]
        return {first_in + self.n + b: first_out + a for b, a in enumerate(passed)}

    def copies(self, in_refs, out_refs, sems):
        send_sems, recv_sems, local_sems = sems
        x, y, c = lax.axis_index("x"), lax.axis_index("y"), lax.axis_index("c")
        me = 4 * x + 2 * y + c

        def block_for(a, p):
            if not self.items[a][1]:
                return in_refs[a]
            n = self.pieces[a][0]
            return in_refs[a].at[pl.ds(pl.multiple_of(p * n, SUBLANES_16BIT), n)]

        def slot(a, dev):
            dst = self.items[a][2]
            return out_refs[a].at[dev] if dst is None else out_refs[a].at[dev, dst[1]]

        starts, recv_waits, send_waits = [], [], []
        for a in range(self.n):
            cp = pltpu.make_async_copy(block_for(a, me), slot(a, me), local_sems.at[a])
            starts.append(cp.start)
            send_waits.append(cp.wait)
        for k in range(1, N_DEV):
            px = 1 - x if (k >> 2) & 1 else x
            py = 1 - y if (k >> 1) & 1 else y
            pc = 1 - c if k & 1 else c
            peer = 4 * px + 2 * py + pc
            for a in range(self.n):
                out = pltpu.make_async_remote_copy(
                    src_ref=block_for(a, peer), dst_ref=slot(a, me), send_sem=send_sems.at[k - 1, a],
                    recv_sem=recv_sems.at[k - 1, a], device_id=(px, py, pc), device_id_type=pl.DeviceIdType.MESH)
                back = pltpu.make_async_remote_copy(
                    src_ref=block_for(a, peer), dst_ref=slot(a, peer), send_sem=send_sems.at[k - 1, a],
                    recv_sem=recv_sems.at[k - 1, a], device_id=(px, py, pc), device_id_type=pl.DeviceIdType.MESH)
                starts.append(out.start)
                send_waits.append(out.wait_send)
                recv_waits.append(back.wait_recv)
        return starts, recv_waits + send_waits


def _comm_call(body, name, grid, in_specs, out_specs, out_shape, operands, semantics, comm, scratch=()):
    n_in, n_out, n_scr = len(operands), len(out_shape), len(scratch)
    if comm is None:
        outs = pl.pallas_call(body, name=name, grid=grid, in_specs=in_specs, out_specs=out_specs, out_shape=out_shape,
                              scratch_shapes=list(scratch), compiler_params=_params(semantics))(*operands)
        return outs, []
    n_ci = len(comm.inputs)

    def fused(*refs):
        ins, c_in = refs[:n_in], refs[n_in:n_in + n_ci]
        outs = refs[n_in + n_ci:n_in + n_ci + n_out]
        c_out = refs[n_in + n_ci + n_out:n_in + n_ci + n_out + comm.n]
        first_scr = n_in + n_ci + n_out + comm.n
        outs = outs + refs[first_scr:first_scr + n_scr]
        starts, waits = comm.copies(c_in, c_out, refs[first_scr + n_scr:])
        first = functools.reduce(jnp.logical_and, [pl.program_id(d) == 0 for d in range(len(grid))])
        last = functools.reduce(jnp.logical_and, [pl.program_id(d) == g - 1 for d, g in enumerate(grid)])

        @pl.when(first)
        def _():
            for start in starts:
                start()

        body(*ins, *outs)

        @pl.when(last)
        def _():
            for wait in waits:
                wait()

    outs = pl.pallas_call(
        fused, name=name, grid=grid, in_specs=list(in_specs) + comm.in_specs, out_specs=list(out_specs) + comm.out_specs,
        out_shape=list(out_shape) + comm.out_shape, scratch_shapes=list(scratch) + comm.scratch,
        input_output_aliases=comm.aliases(n_in, n_out), compiler_params=_params(("arbitrary",) * len(grid)),
    )(*operands, *comm.inputs)
    return outs[:n_out], outs[n_out:]


def _exchange(name, items):
    return _comm_call(lambda: None, name, (1,), [], [], [], [], ("arbitrary",), _Comm(items))[1]


BLOCK_BYTES = 6 << 20


SUBLANES_16BIT = 16


def _row_block(rows, row_bytes):
    step = SUBLANES_16BIT
    if rows * row_bytes <= BLOCK_BYTES or rows % step:
        return rows
    return max(tb for tb in range(step, rows + 1, step) if rows % tb == 0 and (tb == step or tb * row_bytes <= BLOCK_BYTES))


def _sum_contribs(recv):
    n, rows, width = recv.shape
    tb = _row_block(rows, width * 4 * (n + 1))

    def body(r_ref, g_ref):
        g = r_ref[0].astype(F32)
        for k in range(1, n):
            g = g + r_ref[k].astype(F32)
        g_ref[...] = g

    return pl.pallas_call(
        body, name="sum_grads", grid=(rows // tb,), in_specs=[pl.BlockSpec((n, tb, width), lambda i: (0, i, 0))],
        out_specs=pl.BlockSpec((tb, width), lambda i: (i, 0)), out_shape=jax.ShapeDtypeStruct((rows, width), F32),
        compiler_params=_params(("parallel",)),
    )(recv)


def _adamw(contribs, w, m, v):
    n, rows, width = contribs.shape
    tb = _row_block(rows, width * 4 * (n + 7))
    c1 = 1.0 - ADAM_B1 ** ADAM_STEP
    c2 = 1.0 - ADAM_B2 ** ADAM_STEP

    def body(r_ref, w_ref, m_ref, v_ref, g_ref, d_ref, nm_ref, nv_ref):
        g = r_ref[0].astype(F32)
        for k in range(1, n):
            g = g + r_ref[k].astype(F32)
        nm = ADAM_B1 * m_ref[...] + (1.0 - ADAM_B1) * g
        nv = ADAM_B2 * v_ref[...] + (1.0 - ADAM_B2) * (g * g)
        g_ref[...] = g
        nm_ref[...] = nm
        nv_ref[...] = nv
        d_ref[...] = -ADAM_LR * ((nm / c1) / (jnp.sqrt(nv / c2) + ADAM_EPS) + ADAM_WD * w_ref[...])

    blk = pl.BlockSpec((tb, width), lambda i: (i, 0))
    return pl.pallas_call(
        body, name="adamw", grid=(rows // tb,),
        in_specs=[pl.BlockSpec((n, tb, width), lambda i: (0, i, 0)), blk, blk, blk],
        out_specs=[blk] * 4, out_shape=[jax.ShapeDtypeStruct((rows, width), F32)] * 4,
        compiler_params=_params(("parallel",)),
    )(contribs, w, m, v)


def kernel(x, mem, positions, w_in, q_norm_g, w_uq, kv_norm_g, w_ukv, sg_ln_g, sg_ln_b, sg_w, sg_b, w_mem_k, w_mem_v, w_out, ln_g, ln_b, loss_target, m_w_in, m_q_norm_g, m_w_uq, m_kv_norm_g, m_w_ukv, m_sg_ln_g, m_sg_ln_b, m_sg_w, m_sg_b, m_w_mem_k, m_w_mem_v, m_w_out, m_ln_g, m_ln_b, v_w_in, v_q_norm_g, v_w_uq, v_kv_norm_g, v_w_ukv, v_sg_ln_g, v_sg_ln_b, v_sg_w, v_sg_b, v_w_mem_k, v_w_mem_v, v_w_out, v_ln_g, v_ln_b):
    wl = dict(w_in=w_in, q_norm_g=q_norm_g, w_uq=w_uq, kv_norm_g=kv_norm_g, w_ukv=w_ukv, sg_ln_g=sg_ln_g, sg_ln_b=sg_ln_b,
              sg_w=sg_w, sg_b=sg_b, w_mem_k=w_mem_k, w_mem_v=w_mem_v, w_out=w_out, ln_g=ln_g, ln_b=ln_b)
    ml = dict(w_in=m_w_in, q_norm_g=m_q_norm_g, w_uq=m_w_uq, kv_norm_g=m_kv_norm_g, w_ukv=m_w_ukv, sg_ln_g=m_sg_ln_g,
              sg_ln_b=m_sg_ln_b, sg_w=m_sg_w, sg_b=m_sg_b, w_mem_k=m_w_mem_k, w_mem_v=m_w_mem_v, w_out=m_w_out, ln_g=m_ln_g, ln_b=m_ln_b)
    vl = dict(w_in=v_w_in, q_norm_g=v_q_norm_g, w_uq=v_w_uq, kv_norm_g=v_kv_norm_g, w_ukv=v_w_ukv, sg_ln_g=v_sg_ln_g,
              sg_ln_b=v_sg_ln_b, sg_w=v_sg_w, sg_b=v_sg_b, w_mem_k=v_w_mem_k, w_mem_v=v_w_mem_v, w_out=v_w_out, ln_g=v_ln_g, ln_b=v_ln_b)
    depth = w_in.shape[0]

    w_in_t = jnp.pad(jnp.swapaxes(w_in, 1, 2), ((0, 0), (0, SHARD_IN_PAD - SHARD_IN), (0, 0)))
    shards = [s.astype(MXU) for s in (w_in_t, jnp.swapaxes(w_uq, 1, 2), jnp.swapaxes(w_ukv, 1, 2),
                                      jnp.concatenate([w_mem_k, w_mem_v], axis=-1), w_out)]

    def gather_items(l):
        return [(s[l], False, None) for s in shards]

    (first_in,) = _exchange("gather_weights", gather_items(0)[:1])

    def in_rows(g_in):
        return _in_rows_from_shards(g_in.reshape(D_IN_PAD, D_MODEL))

    def other_weights(g_uq, g_ukv, g_mem, g_out):
        return dict(w_uq=_uq_rows_pad(g_uq.reshape(-1, MLA_Q_RANK)), w_ukv=_ukv_rows_split(g_ukv.reshape(-1, MLA_KV_RANK)),
                    w_mem_kv=g_mem.reshape(D_MODEL, 2 * MEM_WIDTH), w_out=g_out.reshape(D_MODEL, D_MODEL))

    def weights_of(l, received):
        w = dict(q_norm_g=q_norm_g[l], kv_norm_g=kv_norm_g[l], sg_ln_g=sg_ln_g[l], sg_ln_b=sg_ln_b[l], sg_w=sg_w[l],
                 sg_b=sg_b[l], ln_g=ln_g[l], ln_b=ln_b[l])
        if l == 0:
            return dict(w, w_in=in_rows(first_in))
        return dict(w, w_in=in_rows(received[0]), **other_weights(*received[1:]))

    first_rest = (_Comm(gather_items(0)[1:]), lambda received: other_weights(*received))

    to_rows = dict(w_in=_in_rows_to_shards, w_uq=_uq_rows_unpad, w_ukv=_ukv_rows_unsplit)

    def grad_items(l, g, recv):
        return [(n, (to_rows.get(n, lambda a: a)(a).astype(MXU), True, (recv[n], l))) for n, a in g.items()]

    shard_rows = dict(w_in=(D_IN_PAD, D_MODEL), w_uq=(MLA_HEADS * (MLA_NOPE + MLA_ROPE), MLA_Q_RANK),
                      w_ukv=(MLA_HEADS * 256, MLA_KV_RANK), w_mem_kv=(D_MODEL, 2 * MEM_WIDTH), w_out=(D_MODEL, D_MODEL))
    recv = {n: lax.empty((N_DEV, depth, rows // N_DEV, width), MXU) for n, (rows, width) in shard_rows.items()}
    rep_sizes = [wl[n].size for n in REPLICATED]
    rep_rows = -(-sum(rep_sizes) // (SUBLANES_16BIT * LANES)) * SUBLANES_16BIT

    def packed(d):
        flat = jnp.concatenate([d[n].reshape(-1) for n in REPLICATED])
        return jnp.pad(flat, (0, rep_rows * LANES - flat.shape[0])).reshape(rep_rows, LANES)

    def rep_item(per_layer, dsgb):
        rep_g = dict(sg_b=dsgb, **{n: jnp.stack([g[n] for g in per_layer]) for n in REPLICATED if n != "sg_b"})
        return (packed(rep_g), False, None)

    loss_part, grad_x, grads, dsgb, recv = _local_step(
        x[0], mem[0], positions[0], loss_target[0], depth, weights_of, lambda l: _Comm(gather_items(l)), first_rest,
        grad_items, rep_item, recv)
    loss = lax.psum(loss_part, ("x", "y", "c"))
    r_in, r_uq, r_ukv, r_mem, r_out, r_rep = (recv[n] for n in ("w_in", "w_uq", "w_ukv", "w_mem_kv", "w_out", "replicated"))

    def rows2d(a):
        return a.reshape(-1, a.shape[-1])

    def update(name, contribs):
        outs = _adamw(contribs, rows2d(wl[name]), rows2d(ml[name]), rows2d(vl[name]))
        return [o.reshape(wl[name].shape) for o in outs]

    def from_transposed(recv, keep):
        s = _sum_contribs(recv.reshape(N_DEV, -1, recv.shape[-1])).reshape(recv.shape[1:])
        return rows2d(jnp.swapaxes(s[:, :keep], 1, 2))[None]

    res = dict(
        w_in=update("w_in", from_transposed(r_in, SHARD_IN)),
        w_uq=update("w_uq", from_transposed(r_uq, r_uq.shape[2])),
        w_ukv=update("w_ukv", from_transposed(r_ukv, r_ukv.shape[2])),
        w_mem_k=update("w_mem_k", r_mem[..., :MEM_WIDTH].reshape(N_DEV, -1, MEM_WIDTH)),
        w_mem_v=update("w_mem_v", r_mem[..., MEM_WIDTH:].reshape(N_DEV, -1, MEM_WIDTH)),
        w_out=update("w_out", r_out.reshape(N_DEV, -1, D_MODEL)))
    rep_outs = _adamw(r_rep, packed(wl), packed(ml), packed(vl))
    off = 0
    for n, size in zip(REPLICATED, rep_sizes):
        res[n] = [o.reshape(-1)[off:off + size].reshape(wl[n].shape) for o in rep_outs]
        off += size
    return (loss, grad_x[None], *[res[n][0] for n in WEIGHTS], *[res[n][1] for n in WEIGHTS],
            *[res[n][2] for n in WEIGHTS], *[res[n][3] for n in WEIGHTS])
```

```python
import functools
import math

import jax
import jax.numpy as jnp
from jax import lax
from jax.experimental import pallas as pl
from jax.experimental.pallas import tpu as pltpu

F32 = jnp.float32
MXU = jnp.bfloat16
GRAD_WIRE = jnp.bfloat16

DEPTH_FOR_NORM = 4
D_MODEL = 2048
CHUNK = 64
MLA_HEADS, MLA_NOPE, MLA_ROPE, MLA_V = 6, 128, 64, 128
MLA_Q_RANK, MLA_KV_RANK = 512, 256
MLA_WIDTH = MLA_HEADS * MLA_V
MLA_HEAD_PAD = 256
ROPE_THETA = 10000.0
SG_GROUPS, SG_CH, SG_CHUNK = 4, 128, 128
SG_WIDTH = SG_GROUPS * SG_CH
SB_HEADS, SB_DIM = 4, 128
SB_WIDTH = SB_HEADS * SB_DIM
MEM_TOKENS, MEM_HEADS, MEM_DIM = 256, 4, 64
MEM_WIDTH = MEM_HEADS * MEM_DIM
D_IN = 5696
D_IN_PAD = 5760
ALPHA = (2.0 * DEPTH_FOR_NORM) ** 0.25
LN_EPS = 1e-5
RMS_EPS = 1e-6
MLA_SCALE = 1.0 / math.sqrt(MLA_NOPE + MLA_ROPE)
SB_SCALE = 1.0 / math.sqrt(SB_DIM)
MEM_SCALE = 1.0 / math.sqrt(MEM_DIM)
LOG2E = math.log2(math.e)
LN2 = math.log(2.0)
ADAM_LR, ADAM_B1, ADAM_B2, ADAM_EPS, ADAM_WD, ADAM_STEP = 0.001, 0.9, 0.999, 1e-08, 0.01, 10
N_DEV = 8
LANES = 128
NEG = -1e30
SCAN = 256
SB_CUTOFF = 256.0
VMEM_LIMIT = 56 * 1024 * 1024

C_CQ, C_CKV, C_GA, C_SGU, C_SGV, C_GB = 0, 512, 768, 1536, 2048, 2560
C_SBQ, C_SBK, C_SBV, C_GC, C_MQ, C_GM, C_KPE = 3072, 3584, 4096, 4608, 5120, 5376, 5632

REPLICATED = ("q_norm_g", "kv_norm_g", "sg_ln_g", "sg_ln_b", "sg_w", "sg_b", "ln_g", "ln_b")
WEIGHTS = ("w_in", "q_norm_g", "w_uq", "kv_norm_g", "w_ukv", "sg_ln_g", "sg_ln_b", "sg_w", "sg_b",
           "w_mem_k", "w_mem_v", "w_out", "ln_g", "ln_b")


def _params(sem):
    return pltpu.CompilerParams(dimension_semantics=sem, vmem_limit_bytes=VMEM_LIMIT)


def _dot(a, b, dims):
    return lax.dot_general(a.astype(MXU), b.astype(MXU), (dims, ((), ())), preferred_element_type=F32)


NN, NT, TN = ((1,), (0,)), ((1,), (1,)), ((0,), (0,))


@jax.custom_vjp
def _mm(a, b):
    return _dot(a, b, NN)


def _mm_fwd(a, b):
    return _dot(a, b, NN), (a, b)


def _mm_bwd(res, g):
    a, b = res
    return _dot(g, b, NT), _dot(a, g, TN)


_mm.defvjp(_mm_fwd, _mm_bwd)


@jax.custom_vjp
def _mm_nt(a, b):
    return _dot(a, b, NT)


def _mm_nt_fwd(a, b):
    return _dot(a, b, NT), (a, b)


def _mm_nt_bwd(res, g):
    a, b = res
    return _dot(g, b, NN), _dot(g, a, TN)


_mm_nt.defvjp(_mm_nt_fwd, _mm_nt_bwd)


def _split_dot(x, tri):
    hi = x.astype(MXU)
    lo = (x - hi.astype(F32)).astype(MXU)
    return _dot(hi, tri, NN) + _dot(lo, tri, NN)


def _scan_keys(x, tri, reverse):
    w = tri.shape[0]
    nb = x.shape[1] // w
    outs, carry = [None] * nb, None
    for c in (reversed(range(nb)) if reverse else range(nb)):
        xc = x[:, c * w:(c + 1) * w]
        sc = _split_dot(xc, tri)
        outs[c] = sc if carry is None else sc + carry
        rs = jnp.sum(xc, axis=1, keepdims=True)
        carry = rs if carry is None else carry + rs
    return (outs[0] if nb == 1 else jnp.concatenate(outs, axis=1)), carry


def _col2row(col):
    return jnp.transpose(jnp.broadcast_to(col, (col.shape[0], LANES)))[0:8]


def _row2col(row):
    return jnp.transpose(jnp.broadcast_to(row, (LANES, row.shape[1])))[:, 0:1]


def _matmul(name, a, b, mode, tm, tn, tk, out_dtype=F32, add=None, add_scale=1.0, second_dtype=None,
            second_scale=None, comm=None, with_received=False):
    if mode == "nn":
        (m, k), n = a.shape, b.shape[1]
    elif mode == "nt":
        (m, k), n = a.shape, b.shape[0]
    else:
        (k, m), n = a.shape, b.shape[1]
    tm, tn, tk = min(tm, m), min(tn, n), min(tk, k)
    assert m % tm == 0 and n % tn == 0 and k % tk == 0, (name, m, n, k, tm, tn, tk)
    nk = k // tk
    dims = {"nn": NN, "nt": NT, "tn": TN}[mode]
    a_spec = (pl.BlockSpec((tk, tm), lambda j, i, kk: (kk, i)) if mode == "tn"
              else pl.BlockSpec((tm, tk), lambda j, i, kk: (i, kk)))
    b_spec = (pl.BlockSpec((tn, tk), lambda j, i, kk: (j, kk)) if mode == "nt"
              else pl.BlockSpec((tk, tn), lambda j, i, kk: (kk, j)))
    o_spec = pl.BlockSpec((tm, tn), lambda j, i, kk: (i, j))
    in_specs, operands = [a_spec, b_spec], [a, b]
    if add is not None:
        in_specs.append(o_spec)
        operands.append(add)
    if second_scale is not None:
        in_specs.append(pl.BlockSpec((1, tn), lambda j, i, kk: (0, j)))
        operands.append(second_scale)
    out_shape = [jax.ShapeDtypeStruct((m, n), out_dtype)]
    out_specs = [o_spec]
    if second_dtype is not None:
        out_shape.append(jax.ShapeDtypeStruct((m, n), second_dtype))
        out_specs.append(o_spec)

    def body(*refs):
        a_ref, b_ref = refs[0], refs[1]
        extra = list(refs[2:len(operands)])
        add_ref = extra.pop(0) if add is not None else None
        scale_ref = extra.pop(0) if second_scale is not None else None
        outs = refs[len(operands):-1]
        acc_ref = refs[-1]
        kk = pl.program_id(2)

        def finish(r):
            if add_ref is not None:
                r = r + add_scale * add_ref[...]
            outs[0][...] = r.astype(out_dtype)
            if second_dtype is not None:
                outs[1][...] = (r if scale_ref is None else r * scale_ref[...]).astype(second_dtype)

        if nk == 1:
            finish(_dot(a_ref[...], b_ref[...], dims))
            return

        @pl.when(kk == 0)
        def _():
            acc_ref[...] = jnp.zeros_like(acc_ref)

        acc_ref[...] += _dot(a_ref[...], b_ref[...], dims)

        @pl.when(kk == nk - 1)
        def _():
            finish(acc_ref[...])

    res, received = _comm_call(body, name, (n // tn, m // tm, nk), in_specs, out_specs, out_shape, operands,
                               ("parallel", "parallel", "arbitrary"), comm, scratch=[pltpu.VMEM((tm, tn) if nk > 1 else (8, LANES), F32)])
    res = res if second_dtype is not None else res[0]
    return (res, received) if with_received else res


def _accumulate(ref, val, first):
    @pl.when(first)
    def _():
        ref[...] = val

    @pl.when(jnp.logical_not(first))
    def _():
        ref[...] += val


def _rowcall(name, fn, rows, tm, row_in, const_in, row_out, acc_out=()):
    tm = min(tm, rows)
    assert rows % tm == 0
    n_in = len(row_in) + len(const_in)
    in_specs = [pl.BlockSpec((tm, w), lambda i, cb=cb: (i, cb)) for _, w, cb in row_in]
    in_specs += [pl.BlockSpec(c.shape, lambda i, nd=c.ndim: (0,) * nd) for c in const_in]
    out_specs = [pl.BlockSpec(blk, imap) for _, _, blk, imap in row_out]
    out_specs += [pl.BlockSpec(s, lambda i, nd=len(s): (0,) * nd) for s, _ in acc_out]
    out_shape = [jax.ShapeDtypeStruct(s, dt) for s, dt, _, _ in row_out]
    out_shape += [jax.ShapeDtypeStruct(s, dt) for s, dt in acc_out]

    def body(*refs):
        vals = fn(*[r[...] for r in refs[:n_in]])
        outs = refs[n_in:]
        first = pl.program_id(0) == 0
        for k, (_, dt, _, _) in enumerate(row_out):
            outs[k][...] = vals[k].astype(dt)
        for k in range(len(acc_out)):
            _accumulate(outs[len(row_out) + k], vals[len(row_out) + k], first)

    return pl.pallas_call(
        body, name=name, grid=(rows // tm,), in_specs=in_specs, out_specs=out_specs, out_shape=out_shape,
        compiler_params=_params(("arbitrary",) if acc_out else ("parallel",)),
    )(*[a for a, _, _ in row_in], *const_in)


def _ro(rows, width, dtype, tm):
    tm = min(tm, rows)
    return ((rows, width), dtype, (tm, width), lambda i: (i, 0))


def _rms(x, g):
    ms = jnp.mean(x * x, axis=-1, keepdims=True)
    return x * lax.rsqrt(ms + RMS_EPS) * g


def _ln(x, g, b):
    mu = jnp.mean(x, axis=-1, keepdims=True)
    xc = x - mu
    var = jnp.mean(xc * xc, axis=-1, keepdims=True)
    return xc * lax.rsqrt(var + LN_EPS) * g + b


def _gelu(x):
    return 0.5 * x * (1.0 + jnp.tanh(math.sqrt(2.0 / math.pi) * (x + 0.044715 * (x * x * x))))


def _silu(x):
    return x / (1.0 + jnp.exp(-x))


@jax.custom_vjp
def _rope(r, cc, sa, sb):
    return r * cc + pltpu.roll(r, 96, 1) * sa + pltpu.roll(r, 32, 1) * sb


def _rope_fwd(r, cc, sa, sb):
    return _rope(r, cc, sa, sb), (cc, sa, sb)


def _rope_bwd(res, d):
    cc, sa, sb = res
    return d * cc + pltpu.roll(d * sa, 32, 1) + pltpu.roll(d * sb, 96, 1), None, None, None


_rope.defvjp(_rope_fwd, _rope_bwd)


def _rope_tables_fn(pos, invf):
    ang = pos.astype(F32) * invf
    lane = lax.broadcasted_iota(jnp.int32, ang.shape, 1)
    cos, sin = jnp.cos(ang), jnp.sin(ang)
    zero = jnp.zeros_like(ang)
    cc = jnp.where(lane < 64, cos, zero)
    sa = jnp.where(lane < 32, -sin, zero)
    sb = jnp.where((lane >= 32) & (lane < 64), sin, zero)
    return cc, sa, sb


def _prep_fn(cq, ckv, kpe, cc, sa, sb, gq, wuq, gkv, wukv):
    q = _mm_nt(_rms(cq, gq), wuq) * (MLA_SCALE * LOG2E)
    kv = _mm_nt(_rms(ckv, gkv), wukv)
    krot = _rope(kpe, cc, sa, sb)
    qs, ks = [], []
    for h in range(MLA_HEADS):
        o = h * MLA_HEAD_PAD
        qs += [q[:, o:o + 128], _rope(q[:, o + 128:o + 256], cc, sa, sb)]
        ks += [kv[:, h * 128:(h + 1) * 128], krot]
    return jnp.concatenate(qs, axis=1), jnp.concatenate(ks, axis=1), kv[:, MLA_HEADS * 128:]


def _post_fn(ga, sgu, sgv, gb, gc, mq, gm, oa, oc, mkv, lng, lnb, sgw, biasf):
    tm = ga.shape[0]
    u = _gelu(sgu)
    vn = _ln(_gelu(sgv), lng, lnb)
    t_chunk = lax.broadcasted_iota(jnp.int32, (SG_CHUNK, SG_CHUNK), 0) // CHUNK
    s_chunk = lax.broadcasted_iota(jnp.int32, (SG_CHUNK, SG_CHUNK), 1) // CHUNK
    sg_mask = (s_chunk <= t_chunk).astype(F32)
    cols = []
    for g in range(SG_GROUPS):
        wg = sgw[g] * sg_mask
        vg = vn[:, g * SG_CH:(g + 1) * SG_CH]
        cols.append(jnp.concatenate(
            [_mm(wg, vg[n * SG_CHUNK:(n + 1) * SG_CHUNK, :]) for n in range(tm // SG_CHUNK)], axis=0))
    mixed = jnp.concatenate(cols, axis=1) + jnp.concatenate([biasf] * (tm // SG_CHUNK), axis=0)
    ob = u * mixed
    mk, mv = mkv[:, :MEM_WIDTH], mkv[:, MEM_WIDTH:]
    head = lax.broadcasted_iota(jnp.int32, (1, MEM_WIDTH), 1) // MEM_DIM
    om = jnp.zeros((tm, MEM_WIDTH), F32)
    for hd in range(MEM_HEADS):
        hm = (head == hd).astype(F32)
        s = _mm_nt(mq, mk * hm) * MEM_SCALE
        e = jnp.exp(s - lax.stop_gradient(jnp.max(s, axis=-1, keepdims=True)))
        p = e / jnp.sum(e, axis=-1, keepdims=True)
        om = om + _mm(p, mv * hm)
    return jnp.concatenate([oa * _silu(ga), ob * _silu(gb), oc * _silu(gc), om * _silu(gm)], axis=1)


def _post_bwd_fn(dyg, ga, sgu, sgv, gb, gc, mq, gm, oa, oc, mkv, lng, lnb, sgw, biasf):
    _, vjp = jax.vjp(_post_fn, ga, sgu, sgv, gb, gc, mq, gm, oa, oc, mkv, lng, lnb, sgw, biasf)
    dga, dsgu, dsgv, dgb, dgc, dmq, dgm, doa, doc, dmkv, dlng, dlnb, dsgw, dbias = vjp(dyg)
    drows = jnp.stack([_col2row(jnp.sum((doa * oa)[:, h * 128:(h + 1) * 128], axis=1, keepdims=True))
                       for h in range(MLA_HEADS)], axis=0)
    return dga, dsgu, dsgv, dgb, dgc, dmq, dgm, doa, doc, drows, dmkv, dlng, dlnb, dsgw, dbias


def _prep_bwd_fn(dq, dk, dv, cq, ckv, kpe, cc, sa, sb, gq, wuq, gkv, wukv):
    wuq, wukv = wuq.astype(F32), wukv.astype(F32)
    _, vjp = jax.vjp(lambda a, b, c, d, e, f, g: _prep_fn(a, b, c, cc, sa, sb, d, e, f, g), cq, ckv, kpe, gq, wuq, gkv, wukv)
    return vjp((dq, dk, dv))


def _ln_bwd_fn(dxn, z, g, b):
    _, vjp = jax.vjp(_ln, z, g, b)
    dz, dg, db = vjp(dxn)
    return dz, dz, dg, db


def _outproj_fn(yg, x, w, g, b):
    z = ALPHA * x + _dot(yg, w, NN)
    xn = _ln(z, g, b)
    return z, xn, xn


def _loss_fn(xl, tgt):
    err = xl - tgt
    part = 0.5 * jnp.sum(jnp.mean(err * err, axis=-1, keepdims=True), axis=0, keepdims=True)
    return err * (1.0 / D_MODEL), jnp.broadcast_to(part, (1, LANES))


def _mla_fwd(q, k, v, tq, comm=None):
    t = q.shape[0]
    tq = min(tq, t)
    nq = t // tq

    def body(q_ref, k_ref, v_ref, o_ref, lse_ref):
        i = pl.program_id(1)
        qb = q_ref[...]
        r_chunk = lax.broadcasted_iota(jnp.int32, (tq, tq), 0) // CHUNK
        c_chunk = lax.broadcasted_iota(jnp.int32, (tq, tq), 1) // CHUNK
        diag_ok = c_chunk <= r_chunk

        def block(j, carry, masked):
            m, l, acc = carry
            off = pl.multiple_of(j * tq, tq)
            s = _dot(qb, k_ref[pl.ds(off, tq), :], NT)
            if masked:
                s = jnp.where(diag_ok, s, NEG)
            m2 = jnp.maximum(m, jnp.max(s, axis=1, keepdims=True))
            p = jnp.exp2(s - m2)
            a = jnp.exp2(m - m2)
            return m2, a * l + jnp.sum(p, axis=1, keepdims=True), a * acc + _dot(p, v_ref[pl.ds(off, tq), :], NN)

        init = (jnp.full((tq, 1), NEG, F32), jnp.zeros((tq, 1), F32), jnp.zeros((tq, MLA_V), F32))
        carry = lax.fori_loop(0, i, lambda j, c: block(j, c, False), init)
        m, l, acc = block(i, carry, True)
        o_ref[...] = acc / l
        lse_ref[...] = _col2row(m + jnp.log2(l))

    (o, lse), received = _comm_call(
        body, "mla_fwd", (MLA_HEADS, nq),
        [pl.BlockSpec((tq, MLA_HEAD_PAD), lambda h, i: (i, h)),
         pl.BlockSpec((t, MLA_HEAD_PAD), lambda h, i: (0, h)),
         pl.BlockSpec((t, MLA_V), lambda h, i: (0, h))],
        [pl.BlockSpec((tq, MLA_V), lambda h, i: (i, h)),
         pl.BlockSpec((None, 8, tq), lambda h, i: (h, 0, i))],
        [jax.ShapeDtypeStruct((t, MLA_WIDTH), F32), jax.ShapeDtypeStruct((MLA_HEADS, 8, t), F32)],
        [q, k, v], ("parallel", "parallel"), comm)
    return o, lse, received


def _mla_bwd(q, k, v, do, lse, drow, tq, comm=None):
    t = q.shape[0]
    tq = min(tq, t)
    nq = t // tq

    def body(k_ref, v_ref, q_ref, do_ref, lse_ref, d_ref, dk_ref, dv_ref, dq_ref):
        j = pl.program_id(1)

        @pl.when(j == 0)
        def _():
            dq_ref[...] = jnp.zeros_like(dq_ref)

        kj, vj = k_ref[...], v_ref[...]
        k_chunk = lax.broadcasted_iota(jnp.int32, (tq, tq), 0) // CHUNK
        q_chunk = lax.broadcasted_iota(jnp.int32, (tq, tq), 1) // CHUNK
        diag_ok = k_chunk <= q_chunk

        def block(i, carry, masked):
            dk, dv = carry
            off = pl.multiple_of(i * tq, tq)
            qi, doi = q_ref[pl.ds(off, tq), :], do_ref[pl.ds(off, tq), :]
            pt = jnp.exp2(_dot(kj, qi, NT) - lse_ref[0:1, pl.ds(off, tq)])
            if masked:
                pt = jnp.where(diag_ok, pt, 0.0)
            dv = dv + _dot(pt, doi, NN)
            dst = (pt * (_dot(vj, doi, NT) - d_ref[0:1, pl.ds(off, tq)])).astype(MXU)
            dk = dk + _dot(dst, qi, NN)
            dq_ref[pl.ds(off, tq), :] += _dot(dst, kj, TN) * LN2
            return dk, dv

        carry = block(j, (jnp.zeros((tq, MLA_HEAD_PAD), F32), jnp.zeros((tq, MLA_V), F32)), True)
        dk, dv = lax.fori_loop(j + 1, nq, lambda i, c: block(i, c, False), carry)
        dk_ref[...] = dk * LN2
        dv_ref[...] = dv

    (dk, dv, dq), received = _comm_call(
        body, "mla_bwd", (MLA_HEADS, nq),
        [pl.BlockSpec((tq, MLA_HEAD_PAD), lambda h, j: (j, h)),
         pl.BlockSpec((tq, MLA_V), lambda h, j: (j, h)),
         pl.BlockSpec((t, MLA_HEAD_PAD), lambda h, j: (0, h)),
         pl.BlockSpec((t, MLA_V), lambda h, j: (0, h)),
         pl.BlockSpec((None, 8, t), lambda h, j: (h, 0, 0)),
         pl.BlockSpec((None, 8, t), lambda h, j: (h, 0, 0))],
        [pl.BlockSpec((tq, MLA_HEAD_PAD), lambda h, j: (j, h)),
         pl.BlockSpec((tq, MLA_V), lambda h, j: (j, h)),
         pl.BlockSpec((t, MLA_HEAD_PAD), lambda h, j: (0, h))],
        [jax.ShapeDtypeStruct((t, MLA_HEADS * MLA_HEAD_PAD), F32), jax.ShapeDtypeStruct((t, MLA_WIDTH), F32),
         jax.ShapeDtypeStruct((t, MLA_HEADS * MLA_HEAD_PAD), F32)],
        [k, v, q, do, lse, drow], ("parallel", "arbitrary"), comm)
    return dq, dk, dv, received


def _sb_logs(z2):
    lb = jnp.minimum(z2, 0.0) - jnp.log2(1.0 + jnp.exp2(jnp.minimum(z2, -z2)))
    return lb, lb - z2


def _tri(n, keep):
    r = lax.broadcasted_iota(jnp.int32, (n, n), 0)
    c = lax.broadcasted_iota(jnp.int32, (n, n), 1)
    return keep(r, c)


def _sb_fwd(hb, tq):
    t = hb.shape[0]
    tq = min(tq, t)
    nq = t // tq
    cq, ck, cv = C_SBQ // SB_DIM, C_SBK // SB_DIM, C_SBV // SB_DIM

    def body(q_ref, k_ref, v_ref, o_ref, tot_ref, first_ref):
        h, i = pl.program_id(0), pl.program_id(1)
        qb = q_ref[...]
        strict = _tri(tq, lambda r, c: c < r)
        later = _tri(min(SCAN, tq), lambda r, c: r > c).astype(MXU)

        def block(j, carry, masked):
            run, acc = carry
            off = pl.multiple_of(j * tq, tq)
            lb, l1 = _sb_logs(_dot(qb, k_ref[pl.ds(off, tq), :], NT))
            if masked:
                l1 = jnp.where(strict, l1, 0.0)
            after, rowsum = _scan_keys(l1, later, True)
            a = jnp.exp2(lb + after + run)
            if masked:
                a = jnp.where(strict, a, 0.0)
            return run + rowsum, acc + _dot(a, v_ref[pl.ds(off, tq), :], NN)

        carry = block(i, (jnp.zeros((tq, 1), F32), jnp.zeros((tq, SB_DIM), F32)), True)

        def live(state):
            jj, (run, _) = state
            return jnp.logical_and(jj < i, jnp.max(run) > -SB_CUTOFF)

        jj, (run, acc) = lax.while_loop(live, lambda st: (st[0] + 1, block(i - 1 - st[0], st[1], False)),
                                        (jnp.int32(0), carry))
        o_ref[...] = acc
        tot_ref[...] = _col2row(run)
        first_ref[h, i] = i - jj

    return _comm_call(
        body, "sb_fwd", (SB_HEADS, nq),
        [pl.BlockSpec((tq, SB_DIM), lambda h, i: (i, cq + h)),
         pl.BlockSpec((t, SB_DIM), lambda h, i: (0, ck + h)),
         pl.BlockSpec((t, SB_DIM), lambda h, i: (0, cv + h))],
        [pl.BlockSpec((tq, SB_DIM), lambda h, i: (i, h)),
         pl.BlockSpec((None, 8, tq), lambda h, i: (h, 0, i)),
         pl.BlockSpec(memory_space=pltpu.SMEM)],
        [jax.ShapeDtypeStruct((t, SB_WIDTH), F32), jax.ShapeDtypeStruct((SB_HEADS, 8, t), F32),
         jax.ShapeDtypeStruct((SB_HEADS, nq), jnp.int32)],
        [hb, hb, hb], ("arbitrary", "arbitrary"), None)[0]


def _sb_bwd(hb, do, tot, first, tq):
    t = hb.shape[0]
    tq = min(tq, t)
    nq = t // tq
    cq, ck, cv = C_SBQ // SB_DIM, C_SBK // SB_DIM, C_SBV // SB_DIM

    def body(q_ref, do_ref, tot_ref, k_ref, v_ref, first_ref, dq_ref, dk_ref, dv_ref):
        h, i = pl.program_id(0), pl.program_id(1)

        @pl.when(i == 0)
        def _():
            dk_ref[...] = jnp.zeros_like(dk_ref)
            dv_ref[...] = jnp.zeros_like(dv_ref)

        qb, dob = q_ref[...], do_ref[...]
        total = _row2col(tot_ref[0:1, :])
        strict = _tri(tq, lambda r, c: c < r)
        upto = _tri(min(SCAN, tq), lambda r, c: r <= c).astype(MXU)
        before = _tri(min(SCAN, tq), lambda r, c: r < c).astype(MXU)

        def block(j, carry, masked):
            pre, gpre, dq = carry
            off = pl.multiple_of(j * tq, tq)
            kj, vj = k_ref[pl.ds(off, tq), :], v_ref[pl.ds(off, tq), :]
            lb, l1 = _sb_logs(_dot(qb, kj, NT))
            if masked:
                l1 = jnp.where(strict, l1, 0.0)
            upto_sum, rowsum = _scan_keys(l1, upto, False)
            a = jnp.exp2(lb + (total - (upto_sum + pre)))
            if masked:
                a = jnp.where(strict, a, 0.0)
            g = a * _dot(dob, vj, NT)
            before_sum, grow = _scan_keys(g, before, False)
            beta = jnp.exp2(lb)
            dz = g * (1.0 - beta) - (before_sum + gpre) * beta
            if masked:
                dz = jnp.where(strict, dz, 0.0)
            dzb = dz.astype(MXU)
            dk_ref[pl.ds(off, tq), :] += _dot(dzb, qb, TN) * LN2
            dv_ref[pl.ds(off, tq), :] += _dot(a, dob, TN)
            return pre + rowsum, gpre + grow, dq + _dot(dzb, kj, NN)

        init = (jnp.zeros((tq, 1), F32), jnp.zeros((tq, 1), F32), jnp.zeros((tq, SB_DIM), F32))
        carry = lax.fori_loop(first_ref[h, i], i, lambda j, c: block(j, c, False), init)
        dq_ref[...] = (block(i, carry, True)[2] * SB_SCALE).astype(dq_ref.dtype)

    return _comm_call(
        body, "sb_bwd", (SB_HEADS, nq),
        [pl.BlockSpec((tq, SB_DIM), lambda h, i: (i, cq + h)),
         pl.BlockSpec((tq, SB_DIM), lambda h, i: (i, h)),
         pl.BlockSpec((None, 8, tq), lambda h, i: (h, 0, i)),
         pl.BlockSpec((t, SB_DIM), lambda h, i: (0, ck + h)),
         pl.BlockSpec((t, SB_DIM), lambda h, i: (0, cv + h)),
         pl.BlockSpec(memory_space=pltpu.SMEM)],
        [pl.BlockSpec((tq, SB_DIM), lambda h, i: (i, h)),
         pl.BlockSpec((t, SB_DIM), lambda h, i: (0, h)),
         pl.BlockSpec((t, SB_DIM), lambda h, i: (0, h))],
        [jax.ShapeDtypeStruct((t, SB_WIDTH), MXU), jax.ShapeDtypeStruct((t, SB_WIDTH), F32),
         jax.ShapeDtypeStruct((t, SB_WIDTH), F32)],
        [hb, do, tot, hb, hb, first], ("parallel", "arbitrary"), None)[0]


TM = 256
TQ_MLA = 1024
TQ_SB = 512


def _layer_fwd(xf, xb, w, consts, comm=None, rest=None):
    t = xf.shape[0]
    cc, sa, sb, mem = consts
    col_scale = jnp.ones((1, D_IN_PAD), F32).at[:, C_SBQ:C_SBQ + SB_WIDTH].set(SB_SCALE * LOG2E)
    (h, hb), arrived = _matmul("in_proj", xb, w["w_in"], "nt", 512, 1920, D_MODEL, second_dtype=MXU, second_scale=col_scale,
                               comm=rest[0] if rest else None, with_received=True)
    if rest:
        w = dict(w, **rest[1](arrived))
    gq, gkv = w["q_norm_g"][None, :], w["kv_norm_g"][None, :]
    q, k, v = _rowcall(
        "mla_prep", _prep_fn, t, TM,
        [(h, 512, C_CQ // 512), (h, 256, C_CKV // 256), (h, 128, C_KPE // 128), (cc, 128, 0), (sa, 128, 0), (sb, 128, 0)],
        [gq, w["w_uq"], gkv, w["w_ukv"]],
        [_ro(t, MLA_HEADS * MLA_HEAD_PAD, MXU, TM), _ro(t, MLA_HEADS * MLA_HEAD_PAD, MXU, TM), _ro(t, MLA_WIDTH, MXU, TM)])
    oa, lse, received = _mla_fwd(q, k, v, TQ_MLA, comm)
    oc, tot, first = _sb_fwd(hb, TQ_SB)
    mkv = _matmul("mem_kv", mem, w["w_mem_kv"], "nn", 256, 512, 2048)
    lng, lnb = w["sg_ln_g"][None, :], w["sg_ln_b"][None, :]
    biasf = jnp.repeat(w["sg_b"].T, SG_CH, axis=1)
    post_rows = [(h, 768, C_GA // 768), (h, 512, C_SGU // 512), (h, 512, C_SGV // 512), (h, 512, C_GB // 512),
                 (h, 512, C_GC // 512), (h, 256, C_MQ // 256), (h, 256, C_GM // 256), (oa, 768, 0), (oc, 512, 0)]
    post_consts = [mkv, lng, lnb, w["sg_w"], biasf]
    (yg,) = _rowcall("mixer_post", lambda *a: (_post_fn(*a),), t, TM, post_rows, post_consts, [_ro(t, D_MODEL, MXU, TM)])
    z, xn, xnb = _rowcall(
        "out_proj_ln", _outproj_fn, t, TM, [(yg, D_MODEL, 0), (xf, D_MODEL, 0)],
        [w["w_out"], w["ln_g"][None, :], w["ln_b"][None, :]],
        [_ro(t, D_MODEL, F32, TM), _ro(t, D_MODEL, F32, TM), _ro(t, D_MODEL, MXU, TM)])
    saved = dict(w=w, xb=xb, h=h, hb=hb, q=q, k=k, v=v, oa=oa, lse=lse, oc=oc, tot=tot, first=first, mkv=mkv, yg=yg, z=z,
                 post_rows=post_rows, post_consts=post_consts, gq=gq, gkv=gkv)
    return xn, xnb, saved, received


def _layer_bwd(dxn, w, s, consts, recv=None, early_items=None, late_items=None):
    t = dxn.shape[0]
    cc, sa, sb, mem = consts
    h = s["h"]
    dz, dzb, dlng2, dlnb2 = _rowcall(
        "ln_bwd", _ln_bwd_fn, t, TM, [(dxn, D_MODEL, 0), (s["z"], D_MODEL, 0)],
        [w["ln_g"][None, :], w["ln_b"][None, :]],
        [_ro(t, D_MODEL, F32, TM), _ro(t, D_MODEL, MXU, TM)], [((1, D_MODEL), F32), ((1, D_MODEL), F32)])
    dyg = _matmul("out_proj_dx", dzb, w["w_out"], "nt", 512, 1024, 2048)
    dw_out = _matmul("out_proj_dw", s["yg"], dzb, "tn", 1024, 1024, 2048, out_dtype=GRAD_WIRE)
    tm = min(TM, t)
    outs = _rowcall(
        "mixer_post_bwd", _post_bwd_fn, t, TM, [(dyg, D_MODEL, 0)] + s["post_rows"], s["post_consts"],
        [_ro(t, 768, MXU, TM), _ro(t, 512, MXU, TM), _ro(t, 512, MXU, TM), _ro(t, 512, MXU, TM), _ro(t, 512, MXU, TM),
         _ro(t, 256, MXU, TM), _ro(t, 256, MXU, TM), _ro(t, 768, MXU, TM), _ro(t, 512, MXU, TM),
         ((MLA_HEADS, 8, t), F32, (MLA_HEADS, 8, tm), lambda i: (0, 0, i))],
        [((MEM_TOKENS, 2 * MEM_WIDTH), F32), ((1, SG_WIDTH), F32), ((1, SG_WIDTH), F32),
         ((SG_GROUPS, SG_CHUNK, SG_CHUNK), F32), ((SG_CHUNK, SG_WIDTH), F32)])
    dga, dsgu, dsgv, dgb, dgc, dmq, dgm, doa, doc, drow, dmkv, dsg_ln_g, dsg_ln_b, dsgw, dbias = outs
    dw_mem_kv = _matmul("mem_kv_dw", mem, dmkv, "tn", 1024, 512, 256, out_dtype=GRAD_WIRE)
    dsbq, dsbk, dsbv = _sb_bwd(s["hb"], doc, s["tot"], s["first"], TQ_SB)
    named = early_items(dict(w_out=dw_out, w_mem_kv=dw_mem_kv)) if early_items else []
    dq, dk, dv, received = _mla_bwd(s["q"], s["k"], s["v"], doa, s["lse"], drow, TQ_MLA,
                                    _Comm([item for _, item in named]) if named else None)
    if named:
        recv.update({n: r for (n, _), r in zip(named, received)})
    wq, wkv = MLA_HEADS * MLA_HEAD_PAD, MLA_WIDTH
    dcq, dckv, dkpe, dgq, dwuq, dgkv, dwukv = _rowcall(
        "mla_prep_bwd", _prep_bwd_fn, t, TM,
        [(dq, wq, 0), (dk, wq, 0), (dv, wkv, 0), (h, 512, C_CQ // 512), (h, 256, C_CKV // 256), (h, 128, C_KPE // 128),
         (cc, 128, 0), (sa, 128, 0), (sb, 128, 0)],
        [s["gq"], w["w_uq"], s["gkv"], w["w_ukv"]],
        [_ro(t, 512, MXU, TM), _ro(t, 256, MXU, TM), _ro(t, 128, MXU, TM)],
        [((1, MLA_Q_RANK), F32), ((wq, MLA_Q_RANK), F32), ((1, MLA_KV_RANK), F32), ((2 * wkv, MLA_KV_RANK), F32)])
    dh = jnp.concatenate([dcq, dckv, dga, dsgu, dsgv, dgb, dsbq, dsbk.astype(MXU), dsbv.astype(MXU), dgc, dmq, dgm, dkpe], axis=1)
    dw_in = _matmul("in_proj_dw", dh, s["xb"], "tn", 1152, 1024, 2048, out_dtype=GRAD_WIRE)
    replicated = dict(q_norm_g=dgq[0], kv_norm_g=dgkv[0], sg_ln_g=dsg_ln_g[0], sg_ln_b=dsg_ln_b[0], sg_w=dsgw, sg_bias=dbias,
                      ln_g=dlng2[0], ln_b=dlnb2[0])
    named = late_items(dict(w_in=dw_in, w_uq=dwuq, w_ukv=dwukv), replicated) if late_items else []
    dx, received = _matmul("in_proj_dx", dh, w["w_in"], "nn", 512, 1024, D_IN_PAD, add=dz, add_scale=ALPHA,
                           comm=_Comm([item for _, item in named]) if named else None, with_received=True)
    if named:
        recv.update({n: r for (n, _), r in zip(named, received)})
    return dx, dict(replicated, w_in=dw_in, w_uq=dwuq, w_ukv=dwukv, w_mem_kv=dw_mem_kv, w_out=dw_out)


def _sg_bias_grad(bias_grads):
    db = jnp.concatenate(bias_grads, axis=0)

    def lane_sums(d):
        return (jnp.concatenate([jnp.broadcast_to(jnp.sum(d[:, g * SG_CH:(g + 1) * SG_CH], axis=1, keepdims=True),
                                                  (d.shape[0], SG_CH)) for g in range(SG_GROUPS)], axis=1),)

    (dbs,) = _rowcall("sg_bias_sum", lane_sums, db.shape[0], SG_CHUNK, [(db, SG_WIDTH, 0)], [], [_ro(db.shape[0], SG_WIDTH, F32, SG_CHUNK)])
    return dbs[:, ::SG_CH].reshape(len(bias_grads), SG_CHUNK, SG_GROUPS).transpose(0, 2, 1)


def _local_step(x, mem, positions, target, depth, weights_of, gather_comm, first_rest, grad_items, rep_item, recv):
    t = x.shape[0]
    inv_freq = ROPE_THETA ** (-jnp.arange(0, MLA_ROPE, 2, dtype=F32) / MLA_ROPE)
    invf = jnp.tile(inv_freq, 4)[None, :]
    cc, sa, sb = _rowcall("rope_tables", _rope_tables_fn, t, TM, [(positions[:, None], 1, 0)], [invf],
                          [_ro(t, 128, F32, TM)] * 3)
    consts = (cc, sa, sb, mem)
    xf, xb, saved, received = x, x.astype(MXU), [], None
    for l in range(depth):
        xf, xb, s, received = _layer_fwd(xf, xb, weights_of(l, received), consts, gather_comm(l + 1) if l + 1 < depth else None,
                                         first_rest if l == 0 else None)
        saved.append(s)
    dx, loss_part = _rowcall("loss", _loss_fn, t, TM, [(xf, D_MODEL, 0), (target, D_MODEL, 0)], [],
                             [_ro(t, D_MODEL, F32, TM)], [((1, LANES), F32)])
    grads, recv, sg_b_grad = [None] * depth, dict(recv or {}), []
    for l in reversed(range(depth)):
        def early(g, l=l):
            rest = {n: grads[l + 1][n] for n in ("w_in", "w_uq", "w_ukv")} if l + 1 < depth else {}
            return grad_items(l, g, recv) + (grad_items(l + 1, rest, recv) if rest else [])

        def late(g, replicated):
            per_layer = [replicated] + grads[1:]
            sg_b_grad.append(_sg_bias_grad([r["sg_bias"] for r in per_layer]))
            return grad_items(0, g, recv) + [("replicated", rep_item(per_layer, sg_b_grad[0]))]

        dx, grads[l] = _layer_bwd(dx, saved[l]["w"], saved[l], consts, recv, early if grad_items else None,
                                  late if grad_items and l == 0 else None)
    dsgb = sg_b_grad[0] if sg_b_grad else _sg_bias_grad([g["sg_bias"] for g in grads])
    return loss_part[0, 0], dx, grads, dsgb, recv


SHARD_IN = D_IN // N_DEV
SHARD_IN_PAD = D_IN_PAD // N_DEV
IN_RANGES = ((0, 768), (832, D_IN), (768, 832))


def _in_rows_from_shards(w):
    pieces = []
    for a, b in IN_RANGES:
        while a < b:
            e = min(b, (a // SHARD_IN + 1) * SHARD_IN)
            row = (a // SHARD_IN) * SHARD_IN_PAD + a % SHARD_IN
            pieces.append(w[row:row + e - a])
            a = e
    pieces.append(jnp.zeros((D_IN_PAD - D_IN, w.shape[1]), w.dtype))
    return jnp.concatenate(pieces, axis=0)


def _in_rows_to_shards(g):
    pieces = []
    for d in range(N_DEV):
        a, b = d * SHARD_IN, (d + 1) * SHARD_IN
        while a < b:
            e = min(b, 768 if a < 768 else (832 if a < 832 else D_IN))
            row = a if a < 768 else (C_KPE + a - 768 if a < 832 else a - 64)
            pieces.append(g[row:row + e - a])
            a = e
        pieces.append(jnp.zeros((SHARD_IN_PAD - SHARD_IN, g.shape[1]), g.dtype))
    return jnp.concatenate(pieces, axis=0)


def _uq_rows_pad(w):
    w = w.reshape(MLA_HEADS, MLA_NOPE + MLA_ROPE, w.shape[-1])
    return jnp.pad(w, ((0, 0), (0, MLA_HEAD_PAD - MLA_NOPE - MLA_ROPE), (0, 0))).reshape(MLA_HEADS * MLA_HEAD_PAD, -1)


def _uq_rows_unpad(g):
    return g.reshape(MLA_HEADS, MLA_HEAD_PAD, -1)[:, :MLA_NOPE + MLA_ROPE].reshape(MLA_HEADS * (MLA_NOPE + MLA_ROPE), -1)


def _ukv_rows_split(w):
    return jnp.swapaxes(w.reshape(MLA_HEADS, 2, 128, -1), 0, 1).reshape(MLA_HEADS * 256, -1)


def _ukv_rows_unsplit(g):
    return jnp.swapaxes(g.reshape(2, MLA_HEADS, 128, -1), 0, 1).reshape(MLA_HEADS * 256, -1)


class _Comm:
    def __init__(self, items):
        self.items = items
        self.n = len(items)
        self.pieces = [((s.shape[0] // N_DEV,) + s.shape[1:]) if sc else s.shape for s, sc, _ in items]
        self.inputs = [s for s, _, _ in items] + [d[0] for _, _, d in items if d is not None]
        any_space = pl.BlockSpec(memory_space=pl.ANY)
        self.in_specs = [any_space] * len(self.inputs)
        self.out_specs = [any_space] * self.n
        self.out_shape = [jax.ShapeDtypeStruct((N_DEV,) + tuple(p), s.dtype) if d is None
                          else jax.ShapeDtypeStruct(d[0].shape, d[0].dtype) for p, (s, _, d) in zip(self.pieces, items)]
        self.scratch = [pltpu.SemaphoreType.DMA((N_DEV - 1, self.n)), pltpu.SemaphoreType.DMA((N_DEV - 1, self.n)),
                        pltpu.SemaphoreType.DMA((self.n,))]

    def aliases(self, first_in, first_out):
        passed = [a for a, (_, _, d) in enumerate(self.items) if d is not None]
        return {first_in + self.n + b: first_out + a for b, a in enumerate(passed)}

    def copies(self, in_refs, out_refs, sems):
        send_sems, recv_sems, local_sems = sems
        x, y, c = lax.axis_index("x"), lax.axis_index("y"), lax.axis_index("c")
        me = 4 * x + 2 * y + c

        def block_for(a, p):
            if not self.items[a][1]:
                return in_refs[a]
            n = self.pieces[a][0]
            return in_refs[a].at[pl.ds(pl.multiple_of(p * n, SUBLANES_16BIT), n)]

        def slot(a, dev):
            dst = self.items[a][2]
            return out_refs[a].at[dev] if dst is None else out_refs[a].at[dev, dst[1]]

        starts, recv_waits, send_waits = [], [], []
        for a in range(self.n):
            cp = pltpu.make_async_copy(block_for(a, me), slot(a, me), local_sems.at[a])
            starts.append(cp.start)
            send_waits.append(cp.wait)
        for k in range(1, N_DEV):
            px = 1 - x if (k >> 2) & 1 else x
            py = 1 - y if (k >> 1) & 1 else y
            pc = 1 - c if k & 1 else c
            peer = 4 * px + 2 * py + pc
            for a in range(self.n):
                out = pltpu.make_async_remote_copy(
                    src_ref=block_for(a, peer), dst_ref=slot(a, me), send_sem=send_sems.at[k - 1, a],
                    recv_sem=recv_sems.at[k - 1, a], device_id=(px, py, pc), device_id_type=pl.DeviceIdType.MESH)
                back = pltpu.make_async_remote_copy(
                    src_ref=block_for(a, peer), dst_ref=slot(a, peer), send_sem=send_sems.at[k - 1, a],
                    recv_sem=recv_sems.at[k - 1, a], device_id=(px, py, pc), device_id_type=pl.DeviceIdType.MESH)
                starts.append(out.start)
                send_waits.append(out.wait_send)
                recv_waits.append(back.wait_recv)
        return starts, recv_waits + send_waits


def _comm_call(body, name, grid, in_specs, out_specs, out_shape, operands, semantics, comm, scratch=()):
    n_in, n_out, n_scr = len(operands), len(out_shape), len(scratch)
    if comm is None:
        outs = pl.pallas_call(body, name=name, grid=grid, in_specs=in_specs, out_specs=out_specs, out_shape=out_shape,
                              scratch_shapes=list(scratch), compiler_params=_params(semantics))(*operands)
        return outs, []
    n_ci = len(comm.inputs)

    def fused(*refs):
        ins, c_in = refs[:n_in], refs[n_in:n_in + n_ci]
        outs = refs[n_in + n_ci:n_in + n_ci + n_out]
        c_out = refs[n_in + n_ci + n_out:n_in + n_ci + n_out + comm.n]
        first_scr = n_in + n_ci + n_out + comm.n
        outs = outs + refs[first_scr:first_scr + n_scr]
        starts, waits = comm.copies(c_in, c_out, refs[first_scr + n_scr:])
        first = functools.reduce(jnp.logical_and, [pl.program_id(d) == 0 for d in range(len(grid))])
        last = functools.reduce(jnp.logical_and, [pl.program_id(d) == g - 1 for d, g in enumerate(grid)])

        @pl.when(first)
        def _():
            for start in starts:
                start()

        body(*ins, *outs)

        @pl.when(last)
        def _():
            for wait in waits:
                wait()

    outs = pl.pallas_call(
        fused, name=name, grid=grid, in_specs=list(in_specs) + comm.in_specs, out_specs=list(out_specs) + comm.out_specs,
        out_shape=list(out_shape) + comm.out_shape, scratch_shapes=list(scratch) + comm.scratch,
        input_output_aliases=comm.aliases(n_in, n_out), compiler_params=_params(("arbitrary",) * len(grid)),
    )(*operands, *comm.inputs)
    return outs[:n_out], outs[n_out:]


def _exchange(name, items):
    return _comm_call(lambda: None, name, (1,), [], [], [], [], ("arbitrary",), _Comm(items))[1]


BLOCK_BYTES = 6 << 20


SUBLANES_16BIT = 16


def _row_block(rows, row_bytes):
    step = SUBLANES_16BIT
    if rows * row_bytes <= BLOCK_BYTES or rows % step:
        return rows
    return max(tb for tb in range(step, rows + 1, step) if rows % tb == 0 and (tb == step or tb * row_bytes <= BLOCK_BYTES))


def _sum_contribs(recv):
    n, rows, width = recv.shape
    tb = _row_block(rows, width * 4 * (n + 1))

    def body(r_ref, g_ref):
        g = r_ref[0].astype(F32)
        for k in range(1, n):
            g = g + r_ref[k].astype(F32)
        g_ref[...] = g

    return pl.pallas_call(
        body, name="sum_grads", grid=(rows // tb,), in_specs=[pl.BlockSpec((n, tb, width), lambda i: (0, i, 0))],
        out_specs=pl.BlockSpec((tb, width), lambda i: (i, 0)), out_shape=jax.ShapeDtypeStruct((rows, width), F32),
        compiler_params=_params(("parallel",)),
    )(recv)


def _adamw(contribs, w, m, v):
    n, rows, width = contribs.shape
    tb = _row_block(rows, width * 4 * (n + 7))
    c1 = 1.0 - ADAM_B1 ** ADAM_STEP
    c2 = 1.0 - ADAM_B2 ** ADAM_STEP

    def body(r_ref, w_ref, m_ref, v_ref, g_ref, d_ref, nm_ref, nv_ref):
        g = r_ref[0].astype(F32)
        for k in range(1, n):
            g = g + r_ref[k].astype(F32)
        nm = ADAM_B1 * m_ref[...] + (1.0 - ADAM_B1) * g
        nv = ADAM_B2 * v_ref[...] + (1.0 - ADAM_B2) * (g * g)
        g_ref[...] = g
        nm_ref[...] = nm
        nv_ref[...] = nv
        d_ref[...] = -ADAM_LR * ((nm / c1) / (jnp.sqrt(nv / c2) + ADAM_EPS) + ADAM_WD * w_ref[...])

    blk = pl.BlockSpec((tb, width), lambda i: (i, 0))
    return pl.pallas_call(
        body, name="adamw", grid=(rows // tb,),
        in_specs=[pl.BlockSpec((n, tb, width), lambda i: (0, i, 0)), blk, blk, blk],
        out_specs=[blk] * 4, out_shape=[jax.ShapeDtypeStruct((rows, width), F32)] * 4,
        compiler_params=_params(("parallel",)),
    )(contribs, w, m, v)


def kernel(x, mem, positions, w_in, q_norm_g, w_uq, kv_norm_g, w_ukv, sg_ln_g, sg_ln_b, sg_w, sg_b, w_mem_k, w_mem_v, w_out, ln_g, ln_b, loss_target, m_w_in, m_q_norm_g, m_w_uq, m_kv_norm_g, m_w_ukv, m_sg_ln_g, m_sg_ln_b, m_sg_w, m_sg_b, m_w_mem_k, m_w_mem_v, m_w_out, m_ln_g, m_ln_b, v_w_in, v_q_norm_g, v_w_uq, v_kv_norm_g, v_w_ukv, v_sg_ln_g, v_sg_ln_b, v_sg_w, v_sg_b, v_w_mem_k, v_w_mem_v, v_w_out, v_ln_g, v_ln_b):
    wl = dict(w_in=w_in, q_norm_g=q_norm_g, w_uq=w_uq, kv_norm_g=kv_norm_g, w_ukv=w_ukv, sg_ln_g=sg_ln_g, sg_ln_b=sg_ln_b,
              sg_w=sg_w, sg_b=sg_b, w_mem_k=w_mem_k, w_mem_v=w_mem_v, w_out=w_out, ln_g=ln_g, ln_b=ln_b)
    ml = dict(w_in=m_w_in, q_norm_g=m_q_norm_g, w_uq=m_w_uq, kv_norm_g=m_kv_norm_g, w_ukv=m_w_ukv, sg_ln_g=m_sg_ln_g,
              sg_ln_b=m_sg_ln_b, sg_w=m_sg_w, sg_b=m_sg_b, w_mem_k=m_w_mem_k, w_mem_v=m_w_mem_v, w_out=m_w_out, ln_g=m_ln_g, ln_b=m_ln_b)
    vl = dict(w_in=v_w_in, q_norm_g=v_q_norm_g, w_uq=v_w_uq, kv_norm_g=v_kv_norm_g, w_ukv=v_w_ukv, sg_ln_g=v_sg_ln_g,
              sg_ln_b=v_sg_ln_b, sg_w=v_sg_w, sg_b=v_sg_b, w_mem_k=v_w_mem_k, w_mem_v=v_w_mem_v, w_out=v_w_out, ln_g=v_ln_g, ln_b=v_ln_b)
    depth = w_in.shape[0]

    w_in_t = jnp.pad(jnp.swapaxes(w_in, 1, 2), ((0, 0), (0, SHARD_IN_PAD - SHARD_IN), (0, 0)))
    shards = [s.astype(MXU) for s in (w_in_t, jnp.swapaxes(w_uq, 1, 2), jnp.swapaxes(w_ukv, 1, 2),
                                      jnp.concatenate([w_mem_k, w_mem_v], axis=-1), w_out)]

    def gather_items(l):
        return [(s[l], False, None) for s in shards]

    (first_in,) = _exchange("gather_weights", gather_items(0)[:1])

    def in_rows(g_in):
        return _in_rows_from_shards(g_in.reshape(D_IN_PAD, D_MODEL))

    def other_weights(g_uq, g_ukv, g_mem, g_out):
        return dict(w_uq=_uq_rows_pad(g_uq.reshape(-1, MLA_Q_RANK)), w_ukv=_ukv_rows_split(g_ukv.reshape(-1, MLA_KV_RANK)),
                    w_mem_kv=g_mem.reshape(D_MODEL, 2 * MEM_WIDTH), w_out=g_out.reshape(D_MODEL, D_MODEL))

    def weights_of(l, received):
        w = dict(q_norm_g=q_norm_g[l], kv_norm_g=kv_norm_g[l], sg_ln_g=sg_ln_g[l], sg_ln_b=sg_ln_b[l], sg_w=sg_w[l],
                 sg_b=sg_b[l], ln_g=ln_g[l], ln_b=ln_b[l])
        if l == 0:
            return dict(w, w_in=in_rows(first_in))
        return dict(w, w_in=in_rows(received[0]), **other_weights(*received[1:]))

    first_rest = (_Comm(gather_items(0)[1:]), lambda received: other_weights(*received))

    to_rows = dict(w_in=_in_rows_to_shards, w_uq=_uq_rows_unpad, w_ukv=_ukv_rows_unsplit)

    def grad_items(l, g, recv):
        return [(n, (to_rows.get(n, lambda a: a)(a).astype(GRAD_WIRE), True, (recv[n], l))) for n, a in g.items()]

    shard_rows = dict(w_in=(D_IN_PAD, D_MODEL), w_uq=(MLA_HEADS * (MLA_NOPE + MLA_ROPE), MLA_Q_RANK),
                      w_ukv=(MLA_HEADS * 256, MLA_KV_RANK), w_mem_kv=(D_MODEL, 2 * MEM_WIDTH), w_out=(D_MODEL, D_MODEL))
    recv = {n: lax.empty((N_DEV, depth, rows // N_DEV, width), GRAD_WIRE) for n, (rows, width) in shard_rows.items()}
    rep_sizes = [wl[n].size for n in REPLICATED]
    rep_rows = -(-sum(rep_sizes) // (SUBLANES_16BIT * LANES)) * SUBLANES_16BIT

    def packed(d):
        flat = jnp.concatenate([d[n].reshape(-1) for n in REPLICATED])
        return jnp.pad(flat, (0, rep_rows * LANES - flat.shape[0])).reshape(rep_rows, LANES)

    def rep_item(per_layer, dsgb):
        rep_g = dict(sg_b=dsgb, **{n: jnp.stack([g[n] for g in per_layer]) for n in REPLICATED if n != "sg_b"})
        return (packed(rep_g), False, None)

    loss_part, grad_x, grads, dsgb, recv = _local_step(
        x[0], mem[0], positions[0], loss_target[0], depth, weights_of, lambda l: _Comm(gather_items(l)), first_rest,
        grad_items, rep_item, recv)
    loss = lax.psum(loss_part, ("x", "y", "c"))
    r_in, r_uq, r_ukv, r_mem, r_out, r_rep = (recv[n] for n in ("w_in", "w_uq", "w_ukv", "w_mem_kv", "w_out", "replicated"))

    def rows2d(a):
        return a.reshape(-1, a.shape[-1])

    def update(name, contribs):
        outs = _adamw(contribs, rows2d(wl[name]), rows2d(ml[name]), rows2d(vl[name]))
        return [o.reshape(wl[name].shape) for o in outs]

    def from_transposed(recv, keep):
        s = _sum_contribs(recv.reshape(N_DEV, -1, recv.shape[-1])).reshape(recv.shape[1:])
        return rows2d(jnp.swapaxes(s[:, :keep], 1, 2))[None]

    res = dict(
        w_in=update("w_in", from_transposed(r_in, SHARD_IN)),
        w_uq=update("w_uq", from_transposed(r_uq, r_uq.shape[2])),
        w_ukv=update("w_ukv", from_transposed(r_ukv, r_ukv.shape[2])),
        w_mem_k=update("w_mem_k", r_mem[..., :MEM_WIDTH].reshape(N_DEV, -1, MEM_WIDTH)),
        w_mem_v=update("w_mem_v", r_mem[..., MEM_WIDTH:].reshape(N_DEV, -1, MEM_WIDTH)),
        w_out=update("w_out", r_out.reshape(N_DEV, -1, D_MODEL)))
    rep_outs = _adamw(r_rep, packed(wl), packed(ml), packed(vl))
    off = 0
    for n, size in zip(REPLICATED, rep_sizes):
        res[n] = [o.reshape(-1)[off:off + size].reshape(wl[n].shape) for o in rep_outs]
        off += size
    return (loss, grad_x[None], *[res[n][0] for n in WEIGHTS], *[res[n][1] for n in WEIGHTS],
            *[res[n][2] for n in WEIGHTS], *[res[n][3] for n in WEIGHTS])
```

```python
import functools
import math

import jax
import jax.numpy as jnp
from jax import lax
from jax.experimental import pallas as pl
from jax.experimental.pallas import tpu as pltpu

F32 = jnp.float32
MXU = jnp.bfloat16
GRAD_WIRE = jnp.bfloat16

DEPTH_FOR_NORM = 4
D_MODEL = 2048
CHUNK = 64
MLA_HEADS, MLA_NOPE, MLA_ROPE, MLA_V = 6, 128, 64, 128
MLA_Q_RANK, MLA_KV_RANK = 512, 256
MLA_WIDTH = MLA_HEADS * MLA_V
MLA_HEAD_PAD = 256
ROPE_THETA = 10000.0
SG_GROUPS, SG_CH, SG_CHUNK = 4, 128, 128
SG_WIDTH = SG_GROUPS * SG_CH
SB_HEADS, SB_DIM = 4, 128
SB_WIDTH = SB_HEADS * SB_DIM
MEM_TOKENS, MEM_HEADS, MEM_DIM = 256, 4, 64
MEM_WIDTH = MEM_HEADS * MEM_DIM
D_IN = 5696
D_IN_PAD = 5760
ALPHA = (2.0 * DEPTH_FOR_NORM) ** 0.25
LN_EPS = 1e-5
RMS_EPS = 1e-6
MLA_SCALE = 1.0 / math.sqrt(MLA_NOPE + MLA_ROPE)
SB_SCALE = 1.0 / math.sqrt(SB_DIM)
MEM_SCALE = 1.0 / math.sqrt(MEM_DIM)
LOG2E = math.log2(math.e)
LN2 = math.log(2.0)
ADAM_LR, ADAM_B1, ADAM_B2, ADAM_EPS, ADAM_WD, ADAM_STEP = 0.001, 0.9, 0.999, 1e-08, 0.01, 10
N_DEV = 8
LANES = 128
NEG = -1e30
SCAN = 256
SB_CUTOFF = 256.0
VMEM_LIMIT = 56 * 1024 * 1024

C_CQ, C_CKV, C_GA, C_SGU, C_SGV, C_GB = 0, 512, 768, 1536, 2048, 2560
C_SBQ, C_SBK, C_SBV, C_GC, C_MQ, C_GM, C_KPE = 3072, 3584, 4096, 4608, 5120, 5376, 5632

REPLICATED = ("q_norm_g", "kv_norm_g", "sg_ln_g", "sg_ln_b", "sg_w", "sg_b", "ln_g", "ln_b")
WEIGHTS = ("w_in", "q_norm_g", "w_uq", "kv_norm_g", "w_ukv", "sg_ln_g", "sg_ln_b", "sg_w", "sg_b",
           "w_mem_k", "w_mem_v", "w_out", "ln_g", "ln_b")


def _params(sem):
    return pltpu.CompilerParams(dimension_semantics=sem, vmem_limit_bytes=VMEM_LIMIT)


def _dot(a, b, dims):
    return lax.dot_general(a.astype(MXU), b.astype(MXU), (dims, ((), ())), preferred_element_type=F32)


NN, NT, TN = ((1,), (0,)), ((1,), (1,)), ((0,), (0,))


@jax.custom_vjp
def _mm(a, b):
    return _dot(a, b, NN)


def _mm_fwd(a, b):
    return _dot(a, b, NN), (a, b)


def _mm_bwd(res, g):
    a, b = res
    return _dot(g, b, NT), _dot(a, g, TN)


_mm.defvjp(_mm_fwd, _mm_bwd)


@jax.custom_vjp
def _mm_nt(a, b):
    return _dot(a, b, NT)


def _mm_nt_fwd(a, b):
    return _dot(a, b, NT), (a, b)


def _mm_nt_bwd(res, g):
    a, b = res
    return _dot(g, b, NN), _dot(g, a, TN)


_mm_nt.defvjp(_mm_nt_fwd, _mm_nt_bwd)


def _split_dot(x, tri):
    hi = x.astype(MXU)
    lo = (x - hi.astype(F32)).astype(MXU)
    return _dot(hi, tri, NN) + _dot(lo, tri, NN)


def _scan_keys(x, tri, reverse):
    w = tri.shape[0]
    nb = x.shape[1] // w
    outs, carry = [None] * nb, None
    for c in (reversed(range(nb)) if reverse else range(nb)):
        xc = x[:, c * w:(c + 1) * w]
        sc = _split_dot(xc, tri)
        outs[c] = sc if carry is None else sc + carry
        rs = jnp.sum(xc, axis=1, keepdims=True)
        carry = rs if carry is None else carry + rs
    return (outs[0] if nb == 1 else jnp.concatenate(outs, axis=1)), carry


def _col2row(col):
    return jnp.transpose(jnp.broadcast_to(col, (col.shape[0], LANES)))[0:8]


def _row2col(row):
    return jnp.transpose(jnp.broadcast_to(row, (LANES, row.shape[1])))[:, 0:1]


def _matmul(name, a, b, mode, tm, tn, tk, out_dtype=F32, add=None, add_scale=1.0, second_dtype=None,
            second_scale=None, comm=None, with_received=False):
    if mode == "nn":
        (m, k), n = a.shape, b.shape[1]
    elif mode == "nt":
        (m, k), n = a.shape, b.shape[0]
    else:
        (k, m), n = a.shape, b.shape[1]
    tm, tn, tk = min(tm, m), min(tn, n), min(tk, k)
    assert m % tm == 0 and n % tn == 0 and k % tk == 0, (name, m, n, k, tm, tn, tk)
    nk = k // tk
    dims = {"nn": NN, "nt": NT, "tn": TN}[mode]
    a_spec = (pl.BlockSpec((tk, tm), lambda j, i, kk: (kk, i)) if mode == "tn"
              else pl.BlockSpec((tm, tk), lambda j, i, kk: (i, kk)))
    b_spec = (pl.BlockSpec((tn, tk), lambda j, i, kk: (j, kk)) if mode == "nt"
              else pl.BlockSpec((tk, tn), lambda j, i, kk: (kk, j)))
    o_spec = pl.BlockSpec((tm, tn), lambda j, i, kk: (i, j))
    in_specs, operands = [a_spec, b_spec], [a, b]
    if add is not None:
        in_specs.append(o_spec)
        operands.append(add)
    if second_scale is not None:
        in_specs.append(pl.BlockSpec((1, tn), lambda j, i, kk: (0, j)))
        operands.append(second_scale)
    out_shape = [jax.ShapeDtypeStruct((m, n), out_dtype)]
    out_specs = [o_spec]
    if second_dtype is not None:
        out_shape.append(jax.ShapeDtypeStruct((m, n), second_dtype))
        out_specs.append(o_spec)

    def body(*refs):
        a_ref, b_ref = refs[0], refs[1]
        extra = list(refs[2:len(operands)])
        add_ref = extra.pop(0) if add is not None else None
        scale_ref = extra.pop(0) if second_scale is not None else None
        outs = refs[len(operands):-1]
        acc_ref = refs[-1]
        kk = pl.program_id(2)

        def finish(r):
            if add_ref is not None:
                r = r + add_scale * add_ref[...]
            outs[0][...] = r.astype(out_dtype)
            if second_dtype is not None:
                outs[1][...] = (r if scale_ref is None else r * scale_ref[...]).astype(second_dtype)

        if nk == 1:
            finish(_dot(a_ref[...], b_ref[...], dims))
            return

        @pl.when(kk == 0)
        def _():
            acc_ref[...] = jnp.zeros_like(acc_ref)

        acc_ref[...] += _dot(a_ref[...], b_ref[...], dims)

        @pl.when(kk == nk - 1)
        def _():
            finish(acc_ref[...])

    res, received = _comm_call(body, name, (n // tn, m // tm, nk), in_specs, out_specs, out_shape, operands,
                               ("parallel", "parallel", "arbitrary"), comm, scratch=[pltpu.VMEM((tm, tn) if nk > 1 else (8, LANES), F32)])
    res = res if second_dtype is not None else res[0]
    return (res, received) if with_received else res


def _accumulate(ref, val, first):
    @pl.when(first)
    def _():
        ref[...] = val

    @pl.when(jnp.logical_not(first))
    def _():
        ref[...] += val


def _rowcall(name, fn, rows, tm, row_in, const_in, row_out, acc_out=()):
    tm = min(tm, rows)
    assert rows % tm == 0
    n_in = len(row_in) + len(const_in)
    in_specs = [pl.BlockSpec((tm, w), lambda i, cb=cb: (i, cb)) for _, w, cb in row_in]
    in_specs += [pl.BlockSpec(c.shape, lambda i, nd=c.ndim: (0,) * nd) for c in const_in]
    out_specs = [pl.BlockSpec(blk, imap) for _, _, blk, imap in row_out]
    out_specs += [pl.BlockSpec(s, lambda i, nd=len(s): (0,) * nd) for s, _ in acc_out]
    out_shape = [jax.ShapeDtypeStruct(s, dt) for s, dt, _, _ in row_out]
    out_shape += [jax.ShapeDtypeStruct(s, dt) for s, dt in acc_out]

    def body(*refs):
        vals = fn(*[r[...] for r in refs[:n_in]])
        outs = refs[n_in:]
        first = pl.program_id(0) == 0
        for k, (_, dt, _, _) in enumerate(row_out):
            outs[k][...] = vals[k].astype(dt)
        for k in range(len(acc_out)):
            _accumulate(outs[len(row_out) + k], vals[len(row_out) + k], first)

    return pl.pallas_call(
        body, name=name, grid=(rows // tm,), in_specs=in_specs, out_specs=out_specs, out_shape=out_shape,
        compiler_params=_params(("arbitrary",) if acc_out else ("parallel",)),
    )(*[a for a, _, _ in row_in], *const_in)


def _ro(rows, width, dtype, tm):
    tm = min(tm, rows)
    return ((rows, width), dtype, (tm, width), lambda i: (i, 0))


def _rms(x, g):
    ms = jnp.mean(x * x, axis=-1, keepdims=True)
    return x * lax.rsqrt(ms + RMS_EPS) * g


def _ln(x, g, b):
    mu = jnp.mean(x, axis=-1, keepdims=True)
    xc = x - mu
    var = jnp.mean(xc * xc, axis=-1, keepdims=True)
    return xc * lax.rsqrt(var + LN_EPS) * g + b


def _gelu(x):
    return 0.5 * x * (1.0 + jnp.tanh(math.sqrt(2.0 / math.pi) * (x + 0.044715 * (x * x * x))))


def _silu(x):
    return x / (1.0 + jnp.exp(-x))


@jax.custom_vjp
def _rope(r, cc, sa, sb):
    return r * cc + pltpu.roll(r, 96, 1) * sa + pltpu.roll(r, 32, 1) * sb


def _rope_fwd(r, cc, sa, sb):
    return _rope(r, cc, sa, sb), (cc, sa, sb)


def _rope_bwd(res, d):
    cc, sa, sb = res
    return d * cc + pltpu.roll(d * sa, 32, 1) + pltpu.roll(d * sb, 96, 1), None, None, None


_rope.defvjp(_rope_fwd, _rope_bwd)


def _rope_tables_fn(pos, invf):
    ang = pos.astype(F32) * invf
    lane = lax.broadcasted_iota(jnp.int32, ang.shape, 1)
    cos, sin = jnp.cos(ang), jnp.sin(ang)
    zero = jnp.zeros_like(ang)
    cc = jnp.where(lane < 64, cos, zero)
    sa = jnp.where(lane < 32, -sin, zero)
    sb = jnp.where((lane >= 32) & (lane < 64), sin, zero)
    return cc, sa, sb


def _prep_fn(cq, ckv, kpe, cc, sa, sb, gq, wuq, gkv, wukv):
    q = _mm_nt(_rms(cq, gq), wuq) * (MLA_SCALE * LOG2E)
    kv = _mm_nt(_rms(ckv, gkv), wukv)
    krot = _rope(kpe, cc, sa, sb)
    qs, ks = [], []
    for h in range(MLA_HEADS):
        o = h * MLA_HEAD_PAD
        qs += [q[:, o:o + 128], _rope(q[:, o + 128:o + 256], cc, sa, sb)]
        ks += [kv[:, h * 128:(h + 1) * 128], krot]
    return jnp.concatenate(qs, axis=1), jnp.concatenate(ks, axis=1), kv[:, MLA_HEADS * 128:]


def _post_fn(ga, sgu, sgv, gb, gc, mq, gm, oa, oc, mkv, lng, lnb, sgw, biasf):
    tm = ga.shape[0]
    u = _gelu(sgu)
    vn = _ln(_gelu(sgv), lng, lnb)
    t_chunk = lax.broadcasted_iota(jnp.int32, (SG_CHUNK, SG_CHUNK), 0) // CHUNK
    s_chunk = lax.broadcasted_iota(jnp.int32, (SG_CHUNK, SG_CHUNK), 1) // CHUNK
    sg_mask = (s_chunk <= t_chunk).astype(F32)
    cols = []
    for g in range(SG_GROUPS):
        wg = sgw[g] * sg_mask
        vg = vn[:, g * SG_CH:(g + 1) * SG_CH]
        cols.append(jnp.concatenate(
            [_mm(wg, vg[n * SG_CHUNK:(n + 1) * SG_CHUNK, :]) for n in range(tm // SG_CHUNK)], axis=0))
    mixed = jnp.concatenate(cols, axis=1) + jnp.concatenate([biasf] * (tm // SG_CHUNK), axis=0)
    ob = u * mixed
    mk, mv = mkv[:, :MEM_WIDTH], mkv[:, MEM_WIDTH:]
    head = lax.broadcasted_iota(jnp.int32, (1, MEM_WIDTH), 1) // MEM_DIM
    om = jnp.zeros((tm, MEM_WIDTH), F32)
    for hd in range(MEM_HEADS):
        hm = (head == hd).astype(F32)
        s = _mm_nt(mq, mk * hm) * MEM_SCALE
        e = jnp.exp(s - lax.stop_gradient(jnp.max(s, axis=-1, keepdims=True)))
        p = e / jnp.sum(e, axis=-1, keepdims=True)
        om = om + _mm(p, mv * hm)
    return jnp.concatenate([oa * _silu(ga), ob * _silu(gb), oc * _silu(gc), om * _silu(gm)], axis=1)


def _post_bwd_fn(dyg, ga, sgu, sgv, gb, gc, mq, gm, oa, oc, mkv, lng, lnb, sgw, biasf):
    _, vjp = jax.vjp(_post_fn, ga, sgu, sgv, gb, gc, mq, gm, oa, oc, mkv, lng, lnb, sgw, biasf)
    dga, dsgu, dsgv, dgb, dgc, dmq, dgm, doa, doc, dmkv, dlng, dlnb, dsgw, dbias = vjp(dyg)
    drows = jnp.stack([_col2row(jnp.sum((doa * oa)[:, h * 128:(h + 1) * 128], axis=1, keepdims=True))
                       for h in range(MLA_HEADS)], axis=0)
    return dga, dsgu, dsgv, dgb, dgc, dmq, dgm, doa, doc, drows, dmkv, dlng, dlnb, dsgw, dbias


def _prep_bwd_fn(dq, dk, dv, cq, ckv, kpe, cc, sa, sb, gq, wuq, gkv, wukv):
    wuq, wukv = wuq.astype(F32), wukv.astype(F32)
    _, vjp = jax.vjp(lambda a, b, c, d, e, f, g: _prep_fn(a, b, c, cc, sa, sb, d, e, f, g), cq, ckv, kpe, gq, wuq, gkv, wukv)
    return vjp((dq, dk, dv))


def _ln_bwd_fn(dxn, z, g, b):
    _, vjp = jax.vjp(_ln, z, g, b)
    dz, dg, db = vjp(dxn)
    return dz, dz, dg, db


def _outproj_fn(yg, x, w, g, b):
    z = ALPHA * x + _dot(yg, w, NN)
    xn = _ln(z, g, b)
    return z, xn, xn


def _loss_fn(xl, tgt):
    err = xl - tgt
    part = 0.5 * jnp.sum(jnp.mean(err * err, axis=-1, keepdims=True), axis=0, keepdims=True)
    return err * (1.0 / D_MODEL), jnp.broadcast_to(part, (1, LANES))


def _mla_fwd(q, k, v, tq, comm=None):
    t = q.shape[0]
    tq = min(tq, t)
    nq = t // tq

    def body(q_ref, k_ref, v_ref, o_ref, lse_ref):
        i = pl.program_id(1)
        qb = q_ref[...]
        r_chunk = lax.broadcasted_iota(jnp.int32, (tq, tq), 0) // CHUNK
        c_chunk = lax.broadcasted_iota(jnp.int32, (tq, tq), 1) // CHUNK
        diag_ok = c_chunk <= r_chunk

        def block(j, carry, masked):
            m, l, acc = carry
            off = pl.multiple_of(j * tq, tq)
            s = _dot(qb, k_ref[pl.ds(off, tq), :], NT)
            if masked:
                s = jnp.where(diag_ok, s, NEG)
            m2 = jnp.maximum(m, jnp.max(s, axis=1, keepdims=True))
            p = jnp.exp2(s - m2)
            a = jnp.exp2(m - m2)
            return m2, a * l + jnp.sum(p, axis=1, keepdims=True), a * acc + _dot(p, v_ref[pl.ds(off, tq), :], NN)

        init = (jnp.full((tq, 1), NEG, F32), jnp.zeros((tq, 1), F32), jnp.zeros((tq, MLA_V), F32))
        carry = lax.fori_loop(0, i, lambda j, c: block(j, c, False), init)
        m, l, acc = block(i, carry, True)
        o_ref[...] = acc / l
        lse_ref[...] = _col2row(m + jnp.log2(l))

    (o, lse), received = _comm_call(
        body, "mla_fwd", (MLA_HEADS, nq),
        [pl.BlockSpec((tq, MLA_HEAD_PAD), lambda h, i: (i, h)),
         pl.BlockSpec((t, MLA_HEAD_PAD), lambda h, i: (0, h)),
         pl.BlockSpec((t, MLA_V), lambda h, i: (0, h))],
        [pl.BlockSpec((tq, MLA_V), lambda h, i: (i, h)),
         pl.BlockSpec((None, 8, tq), lambda h, i: (h, 0, i))],
        [jax.ShapeDtypeStruct((t, MLA_WIDTH), F32), jax.ShapeDtypeStruct((MLA_HEADS, 8, t), F32)],
        [q, k, v], ("parallel", "parallel"), comm)
    return o, lse, received


def _mla_bwd(q, k, v, do, lse, drow, tq, comm=None):
    t = q.shape[0]
    tq = min(tq, t)
    nq = t // tq

    def body(k_ref, v_ref, q_ref, do_ref, lse_ref, d_ref, dk_ref, dv_ref, dq_ref):
        j = pl.program_id(1)

        @pl.when(j == 0)
        def _():
            dq_ref[...] = jnp.zeros_like(dq_ref)

        kj, vj = k_ref[...], v_ref[...]
        k_chunk = lax.broadcasted_iota(jnp.int32, (tq, tq), 0) // CHUNK
        q_chunk = lax.broadcasted_iota(jnp.int32, (tq, tq), 1) // CHUNK
        diag_ok = k_chunk <= q_chunk

        def block(i, carry, masked):
            dk, dv = carry
            off = pl.multiple_of(i * tq, tq)
            qi, doi = q_ref[pl.ds(off, tq), :], do_ref[pl.ds(off, tq), :]
            pt = jnp.exp2(_dot(kj, qi, NT) - lse_ref[0:1, pl.ds(off, tq)])
            if masked:
                pt = jnp.where(diag_ok, pt, 0.0)
            dv = dv + _dot(pt, doi, NN)
            dst = (pt * (_dot(vj, doi, NT) - d_ref[0:1, pl.ds(off, tq)])).astype(MXU)
            dk = dk + _dot(dst, qi, NN)
            dq_ref[pl.ds(off, tq), :] += _dot(dst, kj, TN) * LN2
            return dk, dv

        carry = block(j, (jnp.zeros((tq, MLA_HEAD_PAD), F32), jnp.zeros((tq, MLA_V), F32)), True)
        dk, dv = lax.fori_loop(j + 1, nq, lambda i, c: block(i, c, False), carry)
        dk_ref[...] = dk * LN2
        dv_ref[...] = dv

    (dk, dv, dq), received = _comm_call(
        body, "mla_bwd", (MLA_HEADS, nq),
        [pl.BlockSpec((tq, MLA_HEAD_PAD), lambda h, j: (j, h)),
         pl.BlockSpec((tq, MLA_V), lambda h, j: (j, h)),
         pl.BlockSpec((t, MLA_HEAD_PAD), lambda h, j: (0, h)),
         pl.BlockSpec((t, MLA_V), lambda h, j: (0, h)),
         pl.BlockSpec((None, 8, t), lambda h, j: (h, 0, 0)),
         pl.BlockSpec((None, 8, t), lambda h, j: (h, 0, 0))],
        [pl.BlockSpec((tq, MLA_HEAD_PAD), lambda h, j: (j, h)),
         pl.BlockSpec((tq, MLA_V), lambda h, j: (j, h)),
         pl.BlockSpec((t, MLA_HEAD_PAD), lambda h, j: (0, h))],
        [jax.ShapeDtypeStruct((t, MLA_HEADS * MLA_HEAD_PAD), F32), jax.ShapeDtypeStruct((t, MLA_WIDTH), F32),
         jax.ShapeDtypeStruct((t, MLA_HEADS * MLA_HEAD_PAD), F32)],
        [k, v, q, do, lse, drow], ("parallel", "arbitrary"), comm)
    return dq, dk, dv, received


def _sb_logs(z2):
    lb = jnp.minimum(z2, 0.0) - jnp.log2(1.0 + jnp.exp2(jnp.minimum(z2, -z2)))
    return lb, lb - z2


def _tri(n, keep):
    r = lax.broadcasted_iota(jnp.int32, (n, n), 0)
    c = lax.broadcasted_iota(jnp.int32, (n, n), 1)
    return keep(r, c)


def _sb_fwd(hb, tq):
    t = hb.shape[0]
    tq = min(tq, t)
    nq, tk = t // tq, min(SCAN, tq)
    cq, ck, cv = C_SBQ // SB_DIM, C_SBK // SB_DIM, C_SBV // SB_DIM

    def body(q_ref, k_ref, v_ref, o_ref, tot_ref, visited_ref):
        h, i = pl.program_id(0), pl.program_id(1)
        qb = q_ref[...]
        strict = _tri(tq, lambda r, c: c < r)
        later = _tri(tk, lambda r, c: r > c).astype(MXU)

        def block(off, n, carry, masked):
            run, acc = carry
            lb, l1 = _sb_logs(_dot(qb, k_ref[pl.ds(off, n), :], NT))
            if masked:
                l1 = jnp.where(strict, l1, 0.0)
            after, rowsum = _scan_keys(l1, later, True)
            a = jnp.exp2(lb + after + run)
            if masked:
                a = jnp.where(strict, a, 0.0)
            return run + rowsum, acc + _dot(a, v_ref[pl.ds(off, n), :], NN)

        diag = pl.multiple_of(i * tq, tq)
        carry = block(diag, tq, (jnp.zeros((tq, 1), F32), jnp.zeros((tq, SB_DIM), F32)), True)

        def live(state):
            jj, (run, _) = state
            return jnp.logical_and(jj < i * (tq // tk), jnp.max(run) > -SB_CUTOFF)

        def back(state):
            jj, c = state
            return jj + 1, block(pl.multiple_of(diag - (jj + 1) * tk, tk), tk, c, False)

        jj, (run, acc) = lax.while_loop(live, back, (jnp.int32(0), carry))
        o_ref[...] = acc
        tot_ref[...] = _col2row(run)
        visited_ref[h, i] = jj

    return _comm_call(
        body, "sb_fwd", (SB_HEADS, nq),
        [pl.BlockSpec((tq, SB_DIM), lambda h, i: (i, cq + h)),
         pl.BlockSpec((t, SB_DIM), lambda h, i: (0, ck + h)),
         pl.BlockSpec((t, SB_DIM), lambda h, i: (0, cv + h))],
        [pl.BlockSpec((tq, SB_DIM), lambda h, i: (i, h)),
         pl.BlockSpec((None, 8, tq), lambda h, i: (h, 0, i)),
         pl.BlockSpec(memory_space=pltpu.SMEM)],
        [jax.ShapeDtypeStruct((t, SB_WIDTH), F32), jax.ShapeDtypeStruct((SB_HEADS, 8, t), F32),
         jax.ShapeDtypeStruct((SB_HEADS, nq), jnp.int32)],
        [hb, hb, hb], ("arbitrary", "arbitrary"), None)[0]


def _sb_bwd(hb, do, tot, visited, tq):
    t = hb.shape[0]
    tq = min(tq, t)
    nq, tk = t // tq, min(SCAN, tq)
    cq, ck, cv = C_SBQ // SB_DIM, C_SBK // SB_DIM, C_SBV // SB_DIM

    def body(q_ref, do_ref, tot_ref, k_ref, v_ref, visited_ref, dq_ref, dk_ref, dv_ref):
        h, i = pl.program_id(0), pl.program_id(1)

        @pl.when(i == 0)
        def _():
            dk_ref[...] = jnp.zeros_like(dk_ref)
            dv_ref[...] = jnp.zeros_like(dv_ref)

        qb, dob = q_ref[...], do_ref[...]
        total = _row2col(tot_ref[0:1, :])
        strict = _tri(tq, lambda r, c: c < r)
        upto = _tri(tk, lambda r, c: r <= c).astype(MXU)
        before = _tri(tk, lambda r, c: r < c).astype(MXU)

        def block(off, n, carry, masked):
            pre, gpre, dq = carry
            kj, vj = k_ref[pl.ds(off, n), :], v_ref[pl.ds(off, n), :]
            lb, l1 = _sb_logs(_dot(qb, kj, NT))
            if masked:
                l1 = jnp.where(strict, l1, 0.0)
            upto_sum, rowsum = _scan_keys(l1, upto, False)
            a = jnp.exp2(lb + (total - (upto_sum + pre)))
            if masked:
                a = jnp.where(strict, a, 0.0)
            g = a * _dot(dob, vj, NT)
            before_sum, grow = _scan_keys(g, before, False)
            beta = jnp.exp2(lb)
            dz = g * (1.0 - beta) - (before_sum + gpre) * beta
            if masked:
                dz = jnp.where(strict, dz, 0.0)
            dzb = dz.astype(MXU)
            dk_ref[pl.ds(off, n), :] += _dot(dzb, qb, TN) * LN2
            dv_ref[pl.ds(off, n), :] += _dot(a, dob, TN)
            return pre + rowsum, gpre + grow, dq + _dot(dzb, kj, NN)

        diag = pl.multiple_of(i * tq, tq)
        start = diag - visited_ref[h, i] * tk
        init = (jnp.zeros((tq, 1), F32), jnp.zeros((tq, 1), F32), jnp.zeros((tq, SB_DIM), F32))
        carry = lax.fori_loop(0, visited_ref[h, i],
                              lambda jj, c: block(pl.multiple_of(start + jj * tk, tk), tk, c, False), init)
        dq_ref[...] = (block(diag, tq, carry, True)[2] * SB_SCALE).astype(dq_ref.dtype)

    return _comm_call(
        body, "sb_bwd", (SB_HEADS, nq),
        [pl.BlockSpec((tq, SB_DIM), lambda h, i: (i, cq + h)),
         pl.BlockSpec((tq, SB_DIM), lambda h, i: (i, h)),
         pl.BlockSpec((None, 8, tq), lambda h, i: (h, 0, i)),
         pl.BlockSpec((t, SB_DIM), lambda h, i: (0, ck + h)),
         pl.BlockSpec((t, SB_DIM), lambda h, i: (0, cv + h)),
         pl.BlockSpec(memory_space=pltpu.SMEM)],
        [pl.BlockSpec((tq, SB_DIM), lambda h, i: (i, h)),
         pl.BlockSpec((t, SB_DIM), lambda h, i: (0, h)),
         pl.BlockSpec((t, SB_DIM), lambda h, i: (0, h))],
        [jax.ShapeDtypeStruct((t, SB_WIDTH), MXU), jax.ShapeDtypeStruct((t, SB_WIDTH), F32),
         jax.ShapeDtypeStruct((t, SB_WIDTH), F32)],
        [hb, do, tot, hb, hb, visited], ("parallel", "arbitrary"), None)[0]


TM = 256
TQ_MLA = 1024
TQ_SB = 512


def _layer_fwd(xf, xb, w, consts, comm=None, rest=None):
    t = xf.shape[0]
    cc, sa, sb, mem = consts
    col_scale = jnp.ones((1, D_IN_PAD), F32).at[:, C_SBQ:C_SBQ + SB_WIDTH].set(SB_SCALE * LOG2E)
    (h, hb), arrived = _matmul("in_proj", xb, w["w_in"], "nt", 512, 1920, D_MODEL, second_dtype=MXU, second_scale=col_scale,
                               comm=rest[0] if rest else None, with_received=True)
    if rest:
        w = dict(w, **rest[1](arrived))
    gq, gkv = w["q_norm_g"][None, :], w["kv_norm_g"][None, :]
    q, k, v = _rowcall(
        "mla_prep", _prep_fn, t, TM,
        [(h, 512, C_CQ // 512), (h, 256, C_CKV // 256), (h, 128, C_KPE // 128), (cc, 128, 0), (sa, 128, 0), (sb, 128, 0)],
        [gq, w["w_uq"], gkv, w["w_ukv"]],
        [_ro(t, MLA_HEADS * MLA_HEAD_PAD, MXU, TM), _ro(t, MLA_HEADS * MLA_HEAD_PAD, MXU, TM), _ro(t, MLA_WIDTH, MXU, TM)])
    oa, lse, received = _mla_fwd(q, k, v, TQ_MLA, comm)
    oc, tot, visited = _sb_fwd(hb, TQ_SB)
    mkv = _matmul("mem_kv", mem, w["w_mem_kv"], "nn", 256, 512, 2048)
    lng, lnb = w["sg_ln_g"][None, :], w["sg_ln_b"][None, :]
    biasf = jnp.repeat(w["sg_b"].T, SG_CH, axis=1)
    post_rows = [(h, 768, C_GA // 768), (h, 512, C_SGU // 512), (h, 512, C_SGV // 512), (h, 512, C_GB // 512),
                 (h, 512, C_GC // 512), (h, 256, C_MQ // 256), (h, 256, C_GM // 256), (oa, 768, 0), (oc, 512, 0)]
    post_consts = [mkv, lng, lnb, w["sg_w"], biasf]
    (yg,) = _rowcall("mixer_post", lambda *a: (_post_fn(*a),), t, TM, post_rows, post_consts, [_ro(t, D_MODEL, MXU, TM)])
    z, xn, xnb = _rowcall(
        "out_proj_ln", _outproj_fn, t, TM, [(yg, D_MODEL, 0), (xf, D_MODEL, 0)],
        [w["w_out"], w["ln_g"][None, :], w["ln_b"][None, :]],
        [_ro(t, D_MODEL, F32, TM), _ro(t, D_MODEL, F32, TM), _ro(t, D_MODEL, MXU, TM)])
    saved = dict(w=w, xb=xb, h=h, hb=hb, q=q, k=k, v=v, oa=oa, lse=lse, oc=oc, tot=tot, visited=visited, mkv=mkv, yg=yg, z=z,
                 post_rows=post_rows, post_consts=post_consts, gq=gq, gkv=gkv)
    return xn, xnb, saved, received


def _layer_bwd(dxn, w, s, consts, recv=None, early_items=None, late_items=None):
    t = dxn.shape[0]
    cc, sa, sb, mem = consts
    h = s["h"]
    dz, dzb, dlng2, dlnb2 = _rowcall(
        "ln_bwd", _ln_bwd_fn, t, TM, [(dxn, D_MODEL, 0), (s["z"], D_MODEL, 0)],
        [w["ln_g"][None, :], w["ln_b"][None, :]],
        [_ro(t, D_MODEL, F32, TM), _ro(t, D_MODEL, MXU, TM)], [((1, D_MODEL), F32), ((1, D_MODEL), F32)])
    dyg = _matmul("out_proj_dx", dzb, w["w_out"], "nt", 512, 1024, 2048)
    dw_out = _matmul("out_proj_dw", s["yg"], dzb, "tn", 1024, 1024, 2048, out_dtype=GRAD_WIRE)
    tm = min(TM, t)
    outs = _rowcall(
        "mixer_post_bwd", _post_bwd_fn, t, TM, [(dyg, D_MODEL, 0)] + s["post_rows"], s["post_consts"],
        [_ro(t, 768, MXU, TM), _ro(t, 512, MXU, TM), _ro(t, 512, MXU, TM), _ro(t, 512, MXU, TM), _ro(t, 512, MXU, TM),
         _ro(t, 256, MXU, TM), _ro(t, 256, MXU, TM), _ro(t, 768, MXU, TM), _ro(t, 512, MXU, TM),
         ((MLA_HEADS, 8, t), F32, (MLA_HEADS, 8, tm), lambda i: (0, 0, i))],
        [((MEM_TOKENS, 2 * MEM_WIDTH), F32), ((1, SG_WIDTH), F32), ((1, SG_WIDTH), F32),
         ((SG_GROUPS, SG_CHUNK, SG_CHUNK), F32), ((SG_CHUNK, SG_WIDTH), F32)])
    dga, dsgu, dsgv, dgb, dgc, dmq, dgm, doa, doc, drow, dmkv, dsg_ln_g, dsg_ln_b, dsgw, dbias = outs
    dw_mem_kv = _matmul("mem_kv_dw", mem, dmkv, "tn", 1024, 512, 256, out_dtype=GRAD_WIRE)
    dsbq, dsbk, dsbv = _sb_bwd(s["hb"], doc, s["tot"], s["visited"], TQ_SB)
    named = early_items(dict(w_out=dw_out, w_mem_kv=dw_mem_kv)) if early_items else []
    dq, dk, dv, received = _mla_bwd(s["q"], s["k"], s["v"], doa, s["lse"], drow, TQ_MLA,
                                    _Comm([item for _, item in named]) if named else None)
    if named:
        recv.update({n: r for (n, _), r in zip(named, received)})
    wq, wkv = MLA_HEADS * MLA_HEAD_PAD, MLA_WIDTH
    dcq, dckv, dkpe, dgq, dwuq, dgkv, dwukv = _rowcall(
        "mla_prep_bwd", _prep_bwd_fn, t, TM,
        [(dq, wq, 0), (dk, wq, 0), (dv, wkv, 0), (h, 512, C_CQ // 512), (h, 256, C_CKV // 256), (h, 128, C_KPE // 128),
         (cc, 128, 0), (sa, 128, 0), (sb, 128, 0)],
        [s["gq"], w["w_uq"], s["gkv"], w["w_ukv"]],
        [_ro(t, 512, MXU, TM), _ro(t, 256, MXU, TM), _ro(t, 128, MXU, TM)],
        [((1, MLA_Q_RANK), F32), ((wq, MLA_Q_RANK), F32), ((1, MLA_KV_RANK), F32), ((2 * wkv, MLA_KV_RANK), F32)])
    dh = jnp.concatenate([dcq, dckv, dga, dsgu, dsgv, dgb, dsbq, dsbk.astype(MXU), dsbv.astype(MXU), dgc, dmq, dgm, dkpe], axis=1)
    dw_in = _matmul("in_proj_dw", dh, s["xb"], "tn", 1152, 1024, 2048, out_dtype=GRAD_WIRE)
    replicated = dict(q_norm_g=dgq[0], kv_norm_g=dgkv[0], sg_ln_g=dsg_ln_g[0], sg_ln_b=dsg_ln_b[0], sg_w=dsgw, sg_bias=dbias,
                      ln_g=dlng2[0], ln_b=dlnb2[0])
    named = late_items(dict(w_in=dw_in, w_uq=dwuq, w_ukv=dwukv), replicated) if late_items else []
    dx, received = _matmul("in_proj_dx", dh, w["w_in"], "nn", 512, 1024, D_IN_PAD, add=dz, add_scale=ALPHA,
                           comm=_Comm([item for _, item in named]) if named else None, with_received=True)
    if named:
        recv.update({n: r for (n, _), r in zip(named, received)})
    return dx, dict(replicated, w_in=dw_in, w_uq=dwuq, w_ukv=dwukv, w_mem_kv=dw_mem_kv, w_out=dw_out)


def _sg_bias_grad(bias_grads):
    db = jnp.concatenate(bias_grads, axis=0)

    def lane_sums(d):
        return (jnp.concatenate([jnp.broadcast_to(jnp.sum(d[:, g * SG_CH:(g + 1) * SG_CH], axis=1, keepdims=True),
                                                  (d.shape[0], SG_CH)) for g in range(SG_GROUPS)], axis=1),)

    (dbs,) = _rowcall("sg_bias_sum", lane_sums, db.shape[0], SG_CHUNK, [(db, SG_WIDTH, 0)], [], [_ro(db.shape[0], SG_WIDTH, F32, SG_CHUNK)])
    return dbs[:, ::SG_CH].reshape(len(bias_grads), SG_CHUNK, SG_GROUPS).transpose(0, 2, 1)


def _local_step(x, mem, positions, target, depth, weights_of, gather_comm, first_rest, grad_items, rep_item, recv):
    t = x.shape[0]
    inv_freq = ROPE_THETA ** (-jnp.arange(0, MLA_ROPE, 2, dtype=F32) / MLA_ROPE)
    invf = jnp.tile(inv_freq, 4)[None, :]
    cc, sa, sb = _rowcall("rope_tables", _rope_tables_fn, t, TM, [(positions[:, None], 1, 0)], [invf],
                          [_ro(t, 128, F32, TM)] * 3)
    consts = (cc, sa, sb, mem)
    xf, xb, saved, received = x, x.astype(MXU), [], None
    for l in range(depth):
        xf, xb, s, received = _layer_fwd(xf, xb, weights_of(l, received), consts, gather_comm(l + 1) if l + 1 < depth else None,
                                         first_rest if l == 0 else None)
        saved.append(s)
    dx, loss_part = _rowcall("loss", _loss_fn, t, TM, [(xf, D_MODEL, 0), (target, D_MODEL, 0)], [],
                             [_ro(t, D_MODEL, F32, TM)], [((1, LANES), F32)])
    grads, recv, sg_b_grad = [None] * depth, dict(recv or {}), []
    for l in reversed(range(depth)):
        def early(g, l=l):
            rest = {n: grads[l + 1][n] for n in ("w_in", "w_uq", "w_ukv")} if l + 1 < depth else {}
            return grad_items(l, g, recv) + (grad_items(l + 1, rest, recv) if rest else [])

        def late(g, replicated):
            per_layer = [replicated] + grads[1:]
            sg_b_grad.append(_sg_bias_grad([r["sg_bias"] for r in per_layer]))
            return grad_items(0, g, recv) + [("replicated", rep_item(per_layer, sg_b_grad[0]))]

        dx, grads[l] = _layer_bwd(dx, saved[l]["w"], saved[l], consts, recv, early if grad_items else None,
                                  late if grad_items and l == 0 else None)
    dsgb = sg_b_grad[0] if sg_b_grad else _sg_bias_grad([g["sg_bias"] for g in grads])
    return loss_part[0, 0], dx, grads, dsgb, recv


SHARD_IN = D_IN // N_DEV
SHARD_IN_PAD = D_IN_PAD // N_DEV
IN_RANGES = ((0, 768), (832, D_IN), (768, 832))


def _in_rows_from_shards(w):
    pieces = []
    for a, b in IN_RANGES:
        while a < b:
            e = min(b, (a // SHARD_IN + 1) * SHARD_IN)
            row = (a // SHARD_IN) * SHARD_IN_PAD + a % SHARD_IN
            pieces.append(w[row:row + e - a])
            a = e
    pieces.append(jnp.zeros((D_IN_PAD - D_IN, w.shape[1]), w.dtype))
    return jnp.concatenate(pieces, axis=0)


def _in_rows_to_shards(g):
    pieces = []
    for d in range(N_DEV):
        a, b = d * SHARD_IN, (d + 1) * SHARD_IN
        while a < b:
            e = min(b, 768 if a < 768 else (832 if a < 832 else D_IN))
            row = a if a < 768 else (C_KPE + a - 768 if a < 832 else a - 64)
            pieces.append(g[row:row + e - a])
            a = e
        pieces.append(jnp.zeros((SHARD_IN_PAD - SHARD_IN, g.shape[1]), g.dtype))
    return jnp.concatenate(pieces, axis=0)


def _uq_rows_pad(w):
    w = w.reshape(MLA_HEADS, MLA_NOPE + MLA_ROPE, w.shape[-1])
    return jnp.pad(w, ((0, 0), (0, MLA_HEAD_PAD - MLA_NOPE - MLA_ROPE), (0, 0))).reshape(MLA_HEADS * MLA_HEAD_PAD, -1)


def _uq_rows_unpad(g):
    return g.reshape(MLA_HEADS, MLA_HEAD_PAD, -1)[:, :MLA_NOPE + MLA_ROPE].reshape(MLA_HEADS * (MLA_NOPE + MLA_ROPE), -1)


def _ukv_rows_split(w):
    return jnp.swapaxes(w.reshape(MLA_HEADS, 2, 128, -1), 0, 1).reshape(MLA_HEADS * 256, -1)


def _ukv_rows_unsplit(g):
    return jnp.swapaxes(g.reshape(2, MLA_HEADS, 128, -1), 0, 1).reshape(MLA_HEADS * 256, -1)


class _Comm:
    def __init__(self, items):
        self.items = items
        self.n = len(items)
        self.pieces = [((s.shape[0] // N_DEV,) + s.shape[1:]) if sc else s.shape for s, sc, _ in items]
        self.inputs = [s for s, _, _ in items] + [d[0] for _, _, d in items if d is not None]
        any_space = pl.BlockSpec(memory_space=pl.ANY)
        self.in_specs = [any_space] * len(self.inputs)
        self.out_specs = [any_space] * self.n
        self.out_shape = [jax.ShapeDtypeStruct((N_DEV,) + tuple(p), s.dtype) if d is None
                          else jax.ShapeDtypeStruct(d[0].shape, d[0].dtype) for p, (s, _, d) in zip(self.pieces, items)]
        self.scratch = [pltpu.SemaphoreType.DMA((N_DEV - 1, self.n)), pltpu.SemaphoreType.DMA((N_DEV - 1, self.n)),
                        pltpu.SemaphoreType.DMA((self.n,))]

    def aliases(self, first_in, first_out):
        passed = [a for a, (_, _, d) in enumerate(self.items) if d is not None]
        return {first_in + self.n + b: first_out + a for b, a in enumerate(passed)}

    def copies(self, in_refs, out_refs, sems):
        send_sems, recv_sems, local_sems = sems
        x, y, c = lax.axis_index("x"), lax.axis_index("y"), lax.axis_index("c")
        me = 4 * x + 2 * y + c

        def block_for(a, p):
            if not self.items[a][1]:
                return in_refs[a]
            n = self.pieces[a][0]
            return in_refs[a].at[pl.ds(pl.multiple_of(p * n, SUBLANES_16BIT), n)]

        def slot(a, dev):
            dst = self.items[a][2]
            return out_refs[a].at[dev] if dst is None else out_refs[a].at[dev, dst[1]]

        starts, recv_waits, send_waits = [], [], []
        for a in range(self.n):
            cp = pltpu.make_async_copy(block_for(a, me), slot(a, me), local_sems.at[a])
            starts.append(cp.start)
            send_waits.append(cp.wait)
        for k in range(1, N_DEV):
            px = 1 - x if (k >> 2) & 1 else x
            py = 1 - y if (k >> 1) & 1 else y
            pc = 1 - c if k & 1 else c
            peer = 4 * px + 2 * py + pc
            for a in range(self.n):
                out = pltpu.make_async_remote_copy(
                    src_ref=block_for(a, peer), dst_ref=slot(a, me), send_sem=send_sems.at[k - 1, a],
                    recv_sem=recv_sems.at[k - 1, a], device_id=(px, py, pc), device_id_type=pl.DeviceIdType.MESH)
                back = pltpu.make_async_remote_copy(
                    src_ref=block_for(a, peer), dst_ref=slot(a, peer), send_sem=send_sems.at[k - 1, a],
                    recv_sem=recv_sems.at[k - 1, a], device_id=(px, py, pc), device_id_type=pl.DeviceIdType.MESH)
                starts.append(out.start)
                send_waits.append(out.wait_send)
                recv_waits.append(back.wait_recv)
        return starts, recv_waits + send_waits


def _comm_call(body, name, grid, in_specs, out_specs, out_shape, operands, semantics, comm, scratch=()):
    n_in, n_out, n_scr = len(operands), len(out_shape), len(scratch)
    if comm is None:
        outs = pl.pallas_call(body, name=name, grid=grid, in_specs=in_specs, out_specs=out_specs, out_shape=out_shape,
                              scratch_shapes=list(scratch), compiler_params=_params(semantics))(*operands)
        return outs, []
    n_ci = len(comm.inputs)

    def fused(*refs):
        ins, c_in = refs[:n_in], refs[n_in:n_in + n_ci]
        outs = refs[n_in + n_ci:n_in + n_ci + n_out]
        c_out = refs[n_in + n_ci + n_out:n_in + n_ci + n_out + comm.n]
        first_scr = n_in + n_ci + n_out + comm.n
        outs = outs + refs[first_scr:first_scr + n_scr]
        starts, waits = comm.copies(c_in, c_out, refs[first_scr + n_scr:])
        first = functools.reduce(jnp.logical_and, [pl.program_id(d) == 0 for d in range(len(grid))])
        last = functools.reduce(jnp.logical_and, [pl.program_id(d) == g - 1 for d, g in enumerate(grid)])

        @pl.when(first)
        def _():
            for start in starts:
                start()

        body(*ins, *outs)

        @pl.when(last)
        def _():
            for wait in waits:
                wait()

    outs = pl.pallas_call(
        fused, name=name, grid=grid, in_specs=list(in_specs) + comm.in_specs, out_specs=list(out_specs) + comm.out_specs,
        out_shape=list(out_shape) + comm.out_shape, scratch_shapes=list(scratch) + comm.scratch,
        input_output_aliases=comm.aliases(n_in, n_out), compiler_params=_params(("arbitrary",) * len(grid)),
    )(*operands, *comm.inputs)
    return outs[:n_out], outs[n_out:]


def _exchange(name, items):
    return _comm_call(lambda: None, name, (1,), [], [], [], [], ("arbitrary",), _Comm(items))[1]


BLOCK_BYTES = 6 << 20


SUBLANES_16BIT = 16


def _row_block(rows, row_bytes):
    step = SUBLANES_16BIT
    if rows * row_bytes <= BLOCK_BYTES or rows % step:
        return rows
    return max(tb for tb in range(step, rows + 1, step) if rows % tb == 0 and (tb == step or tb * row_bytes <= BLOCK_BYTES))


def _sum_contribs(recv):
    n, rows, width = recv.shape
    tb = _row_block(rows, width * 4 * (n + 1))

    def body(r_ref, g_ref):
        g = r_ref[0].astype(F32)
        for k in range(1, n):
            g = g + r_ref[k].astype(F32)
        g_ref[...] = g

    return pl.pallas_call(
        body, name="sum_grads", grid=(rows // tb,), in_specs=[pl.BlockSpec((n, tb, width), lambda i: (0, i, 0))],
        out_specs=pl.BlockSpec((tb, width), lambda i: (i, 0)), out_shape=jax.ShapeDtypeStruct((rows, width), F32),
        compiler_params=_params(("parallel",)),
    )(recv)


def _adamw(contribs, w, m, v):
    n, rows, width = contribs.shape
    tb = _row_block(rows, width * 4 * (n + 7))
    c1 = 1.0 - ADAM_B1 ** ADAM_STEP
    c2 = 1.0 - ADAM_B2 ** ADAM_STEP

    def body(r_ref, w_ref, m_ref, v_ref, g_ref, d_ref, nm_ref, nv_ref):
        g = r_ref[0].astype(F32)
        for k in range(1, n):
            g = g + r_ref[k].astype(F32)
        nm = ADAM_B1 * m_ref[...] + (1.0 - ADAM_B1) * g
        nv = ADAM_B2 * v_ref[...] + (1.0 - ADAM_B2) * (g * g)
        g_ref[...] = g
        nm_ref[...] = nm
        nv_ref[...] = nv
        d_ref[...] = -ADAM_LR * ((nm / c1) / (jnp.sqrt(nv / c2) + ADAM_EPS) + ADAM_WD * w_ref[...])

    blk = pl.BlockSpec((tb, width), lambda i: (i, 0))
    return pl.pallas_call(
        body, name="adamw", grid=(rows // tb,),
        in_specs=[pl.BlockSpec((n, tb, width), lambda i: (0, i, 0)), blk, blk, blk],
        out_specs=[blk] * 4, out_shape=[jax.ShapeDtypeStruct((rows, width), F32)] * 4,
        compiler_params=_params(("parallel",)),
    )(contribs, w, m, v)


def kernel(x, mem, positions, w_in, q_norm_g, w_uq, kv_norm_g, w_ukv, sg_ln_g, sg_ln_b, sg_w, sg_b, w_mem_k, w_mem_v, w_out, ln_g, ln_b, loss_target, m_w_in, m_q_norm_g, m_w_uq, m_kv_norm_g, m_w_ukv, m_sg_ln_g, m_sg_ln_b, m_sg_w, m_sg_b, m_w_mem_k, m_w_mem_v, m_w_out, m_ln_g, m_ln_b, v_w_in, v_q_norm_g, v_w_uq, v_kv_norm_g, v_w_ukv, v_sg_ln_g, v_sg_ln_b, v_sg_w, v_sg_b, v_w_mem_k, v_w_mem_v, v_w_out, v_ln_g, v_ln_b):
    wl = dict(w_in=w_in, q_norm_g=q_norm_g, w_uq=w_uq, kv_norm_g=kv_norm_g, w_ukv=w_ukv, sg_ln_g=sg_ln_g, sg_ln_b=sg_ln_b,
              sg_w=sg_w, sg_b=sg_b, w_mem_k=w_mem_k, w_mem_v=w_mem_v, w_out=w_out, ln_g=ln_g, ln_b=ln_b)
    ml = dict(w_in=m_w_in, q_norm_g=m_q_norm_g, w_uq=m_w_uq, kv_norm_g=m_kv_norm_g, w_ukv=m_w_ukv, sg_ln_g=m_sg_ln_g,
              sg_ln_b=m_sg_ln_b, sg_w=m_sg_w, sg_b=m_sg_b, w_mem_k=m_w_mem_k, w_mem_v=m_w_mem_v, w_out=m_w_out, ln_g=m_ln_g, ln_b=m_ln_b)
    vl = dict(w_in=v_w_in, q_norm_g=v_q_norm_g, w_uq=v_w_uq, kv_norm_g=v_kv_norm_g, w_ukv=v_w_ukv, sg_ln_g=v_sg_ln_g,
              sg_ln_b=v_sg_ln_b, sg_w=v_sg_w, sg_b=v_sg_b, w_mem_k=v_w_mem_k, w_mem_v=v_w_mem_v, w_out=v_w_out, ln_g=v_ln_g, ln_b=v_ln_b)
    depth = w_in.shape[0]

    w_in_t = jnp.pad(jnp.swapaxes(w_in, 1, 2), ((0, 0), (0, SHARD_IN_PAD - SHARD_IN), (0, 0)))
    shards = [s.astype(MXU) for s in (w_in_t, jnp.swapaxes(w_uq, 1, 2), jnp.swapaxes(w_ukv, 1, 2),
                                      jnp.concatenate([w_mem_k, w_mem_v], axis=-1), w_out)]

    def gather_items(l):
        return [(s[l], False, None) for s in shards]

    (first_in,) = _exchange("gather_weights", gather_items(0)[:1])

    def in_rows(g_in):
        return _in_rows_from_shards(g_in.reshape(D_IN_PAD, D_MODEL))

    def other_weights(g_uq, g_ukv, g_mem, g_out):
        return dict(w_uq=_uq_rows_pad(g_uq.reshape(-1, MLA_Q_RANK)), w_ukv=_ukv_rows_split(g_ukv.reshape(-1, MLA_KV_RANK)),
                    w_mem_kv=g_mem.reshape(D_MODEL, 2 * MEM_WIDTH), w_out=g_out.reshape(D_MODEL, D_MODEL))

    def weights_of(l, received):
        w = dict(q_norm_g=q_norm_g[l], kv_norm_g=kv_norm_g[l], sg_ln_g=sg_ln_g[l], sg_ln_b=sg_ln_b[l], sg_w=sg_w[l],
                 sg_b=sg_b[l], ln_g=ln_g[l], ln_b=ln_b[l])
        if l == 0:
            return dict(w, w_in=in_rows(first_in))
        return dict(w, w_in=in_rows(received[0]), **other_weights(*received[1:]))

    first_rest = (_Comm(gather_items(0)[1:]), lambda received: other_weights(*received))

    to_rows = dict(w_in=_in_rows_to_shards, w_uq=_uq_rows_unpad, w_ukv=_ukv_rows_unsplit)

    def grad_items(l, g, recv):
        return [(n, (to_rows.get(n, lambda a: a)(a).astype(GRAD_WIRE), True, (recv[n], l))) for n, a in g.items()]

    shard_rows = dict(w_in=(D_IN_PAD, D_MODEL), w_uq=(MLA_HEADS * (MLA_NOPE + MLA_ROPE), MLA_Q_RANK),
                      w_ukv=(MLA_HEADS * 256, MLA_KV_RANK), w_mem_kv=(D_MODEL, 2 * MEM_WIDTH), w_out=(D_MODEL, D_MODEL))
    recv = {n: lax.empty((N_DEV, depth, rows // N_DEV, width), GRAD_WIRE) for n, (rows, width) in shard_rows.items()}
    rep_sizes = [wl[n].size for n in REPLICATED]
    rep_rows = -(-sum(rep_sizes) // (SUBLANES_16BIT * LANES)) * SUBLANES_16BIT

    def packed(d):
        flat = jnp.concatenate([d[n].reshape(-1) for n in REPLICATED])
        return jnp.pad(flat, (0, rep_rows * LANES - flat.shape[0])).reshape(rep_rows, LANES)

    def rep_item(per_layer, dsgb):
        rep_g = dict(sg_b=dsgb, **{n: jnp.stack([g[n] for g in per_layer]) for n in REPLICATED if n != "sg_b"})
        return (packed(rep_g), False, None)

    loss_part, grad_x, grads, dsgb, recv = _local_step(
        x[0], mem[0], positions[0], loss_target[0], depth, weights_of, lambda l: _Comm(gather_items(l)), first_rest,
        grad_items, rep_item, recv)
    loss = lax.psum(loss_part, ("x", "y", "c"))
    r_in, r_uq, r_ukv, r_mem, r_out, r_rep = (recv[n] for n in ("w_in", "w_uq", "w_ukv", "w_mem_kv", "w_out", "replicated"))

    def rows2d(a):
        return a.reshape(-1, a.shape[-1])

    def update(name, contribs):
        outs = _adamw(contribs, rows2d(wl[name]), rows2d(ml[name]), rows2d(vl[name]))
        return [o.reshape(wl[name].shape) for o in outs]

    def from_transposed(recv, keep):
        s = _sum_contribs(recv.reshape(N_DEV, -1, recv.shape[-1])).reshape(recv.shape[1:])
        return rows2d(jnp.swapaxes(s[:, :keep], 1, 2))[None]

    res = dict(
        w_in=update("w_in", from_transposed(r_in, SHARD_IN)),
        w_uq=update("w_uq", from_transposed(r_uq, r_uq.shape[2])),
        w_ukv=update("w_ukv", from_transposed(r_ukv, r_ukv.shape[2])),
        w_mem_k=update("w_mem_k", r_mem[..., :MEM_WIDTH].reshape(N_DEV, -1, MEM_WIDTH)),
        w_mem_v=update("w_mem_v", r_mem[..., MEM_WIDTH:].reshape(N_DEV, -1, MEM_WIDTH)),
        w_out=update("w_out", r_out.reshape(N_DEV, -1, D_MODEL)))
    rep_outs = _adamw(r_rep, packed(wl), packed(ml), packed(vl))
    off = 0
    for n, size in zip(REPLICATED, rep_sizes):
        res[n] = [o.reshape(-1)[off:off + size].reshape(wl[n].shape) for o in rep_outs]
        off += size
    return (loss, grad_x[None], *[res[n][0] for n in WEIGHTS], *[res[n][1] for n in WEIGHTS],
            *[res[n][2] for n in WEIGHTS], *[res[n][3] for n in WEIGHTS])
```

```python
import functools
import math

import jax
import jax.numpy as jnp
from jax import lax
from jax.experimental import pallas as pl
from jax.experimental.pallas import tpu as pltpu

F32 = jnp.float32
MXU = jnp.bfloat16
GRAD_WIRE = jnp.bfloat16

DEPTH_FOR_NORM = 4
D_MODEL = 2048
CHUNK = 64
MLA_HEADS, MLA_NOPE, MLA_ROPE, MLA_V = 6, 128, 64, 128
MLA_Q_RANK, MLA_KV_RANK = 512, 256
MLA_WIDTH = MLA_HEADS * MLA_V
MLA_HEAD_PAD = 256
ROPE_THETA = 10000.0
SG_GROUPS, SG_CH, SG_CHUNK = 4, 128, 128
SG_WIDTH = SG_GROUPS * SG_CH
SB_HEADS, SB_DIM = 4, 128
SB_WIDTH = SB_HEADS * SB_DIM
MEM_TOKENS, MEM_HEADS, MEM_DIM = 256, 4, 64
MEM_WIDTH = MEM_HEADS * MEM_DIM
D_IN = 5696
D_IN_PAD = 5760
ALPHA = (2.0 * DEPTH_FOR_NORM) ** 0.25
LN_EPS = 1e-5
RMS_EPS = 1e-6
MLA_SCALE = 1.0 / math.sqrt(MLA_NOPE + MLA_ROPE)
SB_SCALE = 1.0 / math.sqrt(SB_DIM)
MEM_SCALE = 1.0 / math.sqrt(MEM_DIM)
LOG2E = math.log2(math.e)
LN2 = math.log(2.0)
ADAM_LR, ADAM_B1, ADAM_B2, ADAM_EPS, ADAM_WD, ADAM_STEP = 0.001, 0.9, 0.999, 1e-08, 0.01, 10
N_DEV = 8
LANES = 128
NEG = -1e30
SCAN = 256
SB_CUTOFF = 256.0
VMEM_LIMIT = 56 * 1024 * 1024

C_CQ, C_CKV, C_GA, C_SGU, C_SGV, C_GB = 0, 512, 768, 1536, 2048, 2560
C_SBQ, C_SBK, C_SBV, C_GC, C_MQ, C_GM, C_KPE = 3072, 3584, 4096, 4608, 5120, 5376, 5632

REPLICATED = ("q_norm_g", "kv_norm_g", "sg_ln_g", "sg_ln_b", "sg_w", "sg_b", "ln_g", "ln_b")
WEIGHTS = ("w_in", "q_norm_g", "w_uq", "kv_norm_g", "w_ukv", "sg_ln_g", "sg_ln_b", "sg_w", "sg_b",
           "w_mem_k", "w_mem_v", "w_out", "ln_g", "ln_b")


def _params(sem):
    return pltpu.CompilerParams(dimension_semantics=sem, vmem_limit_bytes=VMEM_LIMIT)


def _dot(a, b, dims):
    return lax.dot_general(a.astype(MXU), b.astype(MXU), (dims, ((), ())), preferred_element_type=F32)


NN, NT, TN = ((1,), (0,)), ((1,), (1,)), ((0,), (0,))


@jax.custom_vjp
def _mm(a, b):
    return _dot(a, b, NN)


def _mm_fwd(a, b):
    return _dot(a, b, NN), (a, b)


def _mm_bwd(res, g):
    a, b = res
    return _dot(g, b, NT), _dot(a, g, TN)


_mm.defvjp(_mm_fwd, _mm_bwd)


@jax.custom_vjp
def _mm_nt(a, b):
    return _dot(a, b, NT)


def _mm_nt_fwd(a, b):
    return _dot(a, b, NT), (a, b)


def _mm_nt_bwd(res, g):
    a, b = res
    return _dot(g, b, NN), _dot(g, a, TN)


_mm_nt.defvjp(_mm_nt_fwd, _mm_nt_bwd)


def _split_dot(x, tri):
    hi = x.astype(MXU)
    lo = (x - hi.astype(F32)).astype(MXU)
    return _dot(hi, tri, NN) + _dot(lo, tri, NN)


def _scan_keys(x, tri, reverse):
    w = tri.shape[0]
    nb = x.shape[1] // w
    outs, carry = [None] * nb, None
    for c in (reversed(range(nb)) if reverse else range(nb)):
        xc = x[:, c * w:(c + 1) * w]
        sc = _split_dot(xc, tri)
        outs[c] = sc if carry is None else sc + carry
        rs = jnp.sum(xc, axis=1, keepdims=True)
        carry = rs if carry is None else carry + rs
    return (outs[0] if nb == 1 else jnp.concatenate(outs, axis=1)), carry


def _col2row(col):
    return jnp.transpose(jnp.broadcast_to(col, (col.shape[0], LANES)))[0:8]


def _row2col(row):
    return jnp.transpose(jnp.broadcast_to(row, (LANES, row.shape[1])))[:, 0:1]


def _matmul(name, a, b, mode, tm, tn, tk, out_dtype=F32, add=None, add_scale=1.0, second_dtype=None,
            second_scale=None, comm=None, with_received=False):
    if mode == "nn":
        (m, k), n = a.shape, b.shape[1]
    elif mode == "nt":
        (m, k), n = a.shape, b.shape[0]
    else:
        (k, m), n = a.shape, b.shape[1]
    tm, tn, tk = min(tm, m), min(tn, n), min(tk, k)
    assert m % tm == 0 and n % tn == 0 and k % tk == 0, (name, m, n, k, tm, tn, tk)
    nk = k // tk
    dims = {"nn": NN, "nt": NT, "tn": TN}[mode]
    a_spec = (pl.BlockSpec((tk, tm), lambda j, i, kk: (kk, i)) if mode == "tn"
              else pl.BlockSpec((tm, tk), lambda j, i, kk: (i, kk)))
    b_spec = (pl.BlockSpec((tn, tk), lambda j, i, kk: (j, kk)) if mode == "nt"
              else pl.BlockSpec((tk, tn), lambda j, i, kk: (kk, j)))
    o_spec = pl.BlockSpec((tm, tn), lambda j, i, kk: (i, j))
    in_specs, operands = [a_spec, b_spec], [a, b]
    if add is not None:
        in_specs.append(o_spec)
        operands.append(add)
    if second_scale is not None:
        in_specs.append(pl.BlockSpec((1, tn), lambda j, i, kk: (0, j)))
        operands.append(second_scale)
    out_shape = [jax.ShapeDtypeStruct((m, n), out_dtype)]
    out_specs = [o_spec]
    if second_dtype is not None:
        out_shape.append(jax.ShapeDtypeStruct((m, n), second_dtype))
        out_specs.append(o_spec)

    def body(*refs):
        a_ref, b_ref = refs[0], refs[1]
        extra = list(refs[2:len(operands)])
        add_ref = extra.pop(0) if add is not None else None
        scale_ref = extra.pop(0) if second_scale is not None else None
        outs = refs[len(operands):-1]
        acc_ref = refs[-1]
        kk = pl.program_id(2)

        def finish(r):
            if add_ref is not None:
                r = r + add_scale * add_ref[...]
            outs[0][...] = r.astype(out_dtype)
            if second_dtype is not None:
                outs[1][...] = (r if scale_ref is None else r * scale_ref[...]).astype(second_dtype)

        if nk == 1:
            finish(_dot(a_ref[...], b_ref[...], dims))
            return

        @pl.when(kk == 0)
        def _():
            acc_ref[...] = jnp.zeros_like(acc_ref)

        acc_ref[...] += _dot(a_ref[...], b_ref[...], dims)

        @pl.when(kk == nk - 1)
        def _():
            finish(acc_ref[...])

    res, received = _comm_call(body, name, (n // tn, m // tm, nk), in_specs, out_specs, out_shape, operands,
                               ("parallel", "parallel", "arbitrary"), comm, scratch=[pltpu.VMEM((tm, tn) if nk > 1 else (8, LANES), F32)])
    res = res if second_dtype is not None else res[0]
    return (res, received) if with_received else res


def _accumulate(ref, val, first):
    @pl.when(first)
    def _():
        ref[...] = val

    @pl.when(jnp.logical_not(first))
    def _():
        ref[...] += val


def _rowcall(name, fn, rows, tm, row_in, const_in, row_out, acc_out=()):
    tm = min(tm, rows)
    assert rows % tm == 0
    n_in = len(row_in) + len(const_in)
    in_specs = [pl.BlockSpec((tm, w), lambda i, cb=cb: (i, cb)) for _, w, cb in row_in]
    in_specs += [pl.BlockSpec(c.shape, lambda i, nd=c.ndim: (0,) * nd) for c in const_in]
    out_specs = [pl.BlockSpec(blk, imap) for _, _, blk, imap in row_out]
    out_specs += [pl.BlockSpec(s, lambda i, nd=len(s): (0,) * nd) for s, _ in acc_out]
    out_shape = [jax.ShapeDtypeStruct(s, dt) for s, dt, _, _ in row_out]
    out_shape += [jax.ShapeDtypeStruct(s, dt) for s, dt in acc_out]

    def body(*refs):
        vals = fn(*[r[...] for r in refs[:n_in]])
        outs = refs[n_in:]
        first = pl.program_id(0) == 0
        for k, (_, dt, _, _) in enumerate(row_out):
            outs[k][...] = vals[k].astype(dt)
        for k in range(len(acc_out)):
            _accumulate(outs[len(row_out) + k], vals[len(row_out) + k], first)

    return pl.pallas_call(
        body, name=name, grid=(rows // tm,), in_specs=in_specs, out_specs=out_specs, out_shape=out_shape,
        compiler_params=_params(("arbitrary",) if acc_out else ("parallel",)),
    )(*[a for a, _, _ in row_in], *const_in)


def _ro(rows, width, dtype, tm):
    tm = min(tm, rows)
    return ((rows, width), dtype, (tm, width), lambda i: (i, 0))


def _rms(x, g):
    ms = jnp.mean(x * x, axis=-1, keepdims=True)
    return x * lax.rsqrt(ms + RMS_EPS) * g


def _ln(x, g, b):
    mu = jnp.mean(x, axis=-1, keepdims=True)
    xc = x - mu
    var = jnp.mean(xc * xc, axis=-1, keepdims=True)
    return xc * lax.rsqrt(var + LN_EPS) * g + b


def _gelu(x):
    return 0.5 * x * (1.0 + jnp.tanh(math.sqrt(2.0 / math.pi) * (x + 0.044715 * (x * x * x))))


def _silu(x):
    return x / (1.0 + jnp.exp(-x))


@jax.custom_vjp
def _rope(r, cc, sa, sb):
    return r * cc + pltpu.roll(r, 96, 1) * sa + pltpu.roll(r, 32, 1) * sb


def _rope_fwd(r, cc, sa, sb):
    return _rope(r, cc, sa, sb), (cc, sa, sb)


def _rope_bwd(res, d):
    cc, sa, sb = res
    return d * cc + pltpu.roll(d * sa, 32, 1) + pltpu.roll(d * sb, 96, 1), None, None, None


_rope.defvjp(_rope_fwd, _rope_bwd)


def _rope_tables_fn(pos, invf):
    ang = pos.astype(F32) * invf
    lane = lax.broadcasted_iota(jnp.int32, ang.shape, 1)
    cos, sin = jnp.cos(ang), jnp.sin(ang)
    zero = jnp.zeros_like(ang)
    cc = jnp.where(lane < 64, cos, zero)
    sa = jnp.where(lane < 32, -sin, zero)
    sb = jnp.where((lane >= 32) & (lane < 64), sin, zero)
    return cc, sa, sb


def _prep_fn(cq, ckv, kpe, cc, sa, sb, gq, wuq, gkv, wukv):
    q = _mm_nt(_rms(cq, gq), wuq) * (MLA_SCALE * LOG2E)
    kv = _mm_nt(_rms(ckv, gkv), wukv)
    krot = _rope(kpe, cc, sa, sb)
    qs, ks = [], []
    for h in range(MLA_HEADS):
        o = h * MLA_HEAD_PAD
        qs += [q[:, o:o + 128], _rope(q[:, o + 128:o + 256], cc, sa, sb)]
        ks += [kv[:, h * 128:(h + 1) * 128], krot]
    return jnp.concatenate(qs, axis=1), jnp.concatenate(ks, axis=1), kv[:, MLA_HEADS * 128:]


def _post_fn(ga, sgu, sgv, gb, gc, mq, gm, oa, oc, mkv, lng, lnb, sgw, biasf):
    tm = ga.shape[0]
    u = _gelu(sgu)
    vn = _ln(_gelu(sgv), lng, lnb)
    t_chunk = lax.broadcasted_iota(jnp.int32, (SG_CHUNK, SG_CHUNK), 0) // CHUNK
    s_chunk = lax.broadcasted_iota(jnp.int32, (SG_CHUNK, SG_CHUNK), 1) // CHUNK
    sg_mask = (s_chunk <= t_chunk).astype(F32)
    cols = []
    for g in range(SG_GROUPS):
        wg = sgw[g] * sg_mask
        vg = vn[:, g * SG_CH:(g + 1) * SG_CH]
        cols.append(jnp.concatenate(
            [_mm(wg, vg[n * SG_CHUNK:(n + 1) * SG_CHUNK, :]) for n in range(tm // SG_CHUNK)], axis=0))
    mixed = jnp.concatenate(cols, axis=1) + jnp.concatenate([biasf] * (tm // SG_CHUNK), axis=0)
    ob = u * mixed
    mk, mv = mkv[:, :MEM_WIDTH], mkv[:, MEM_WIDTH:]
    head = lax.broadcasted_iota(jnp.int32, (1, MEM_WIDTH), 1) // MEM_DIM
    om = jnp.zeros((tm, MEM_WIDTH), F32)
    for hd in range(MEM_HEADS):
        hm = (head == hd).astype(F32)
        s = _mm_nt(mq, mk * hm) * MEM_SCALE
        e = jnp.exp(s - lax.stop_gradient(jnp.max(s, axis=-1, keepdims=True)))
        p = e / jnp.sum(e, axis=-1, keepdims=True)
        om = om + _mm(p, mv * hm)
    return jnp.concatenate([oa * _silu(ga), ob * _silu(gb), oc * _silu(gc), om * _silu(gm)], axis=1)


def _post_bwd_fn(dyg, ga, sgu, sgv, gb, gc, mq, gm, oa, oc, mkv, lng, lnb, sgw, biasf):
    _, vjp = jax.vjp(_post_fn, ga, sgu, sgv, gb, gc, mq, gm, oa, oc, mkv, lng, lnb, sgw, biasf)
    dga, dsgu, dsgv, dgb, dgc, dmq, dgm, doa, doc, dmkv, dlng, dlnb, dsgw, dbias = vjp(dyg)
    drows = jnp.stack([_col2row(jnp.sum((doa * oa)[:, h * 128:(h + 1) * 128], axis=1, keepdims=True))
                       for h in range(MLA_HEADS)], axis=0)
    return dga, dsgu, dsgv, dgb, dgc, dmq, dgm, doa, doc, drows, dmkv, dlng, dlnb, dsgw, dbias


def _prep_bwd_fn(dq, dk, dv, cq, ckv, kpe, cc, sa, sb, gq, wuq, gkv, wukv):
    wuq, wukv = wuq.astype(F32), wukv.astype(F32)
    _, vjp = jax.vjp(lambda a, b, c, d, e, f, g: _prep_fn(a, b, c, cc, sa, sb, d, e, f, g), cq, ckv, kpe, gq, wuq, gkv, wukv)
    return vjp((dq, dk, dv))


def _ln_bwd_fn(dxn, z, g, b):
    _, vjp = jax.vjp(_ln, z, g, b)
    dz, dg, db = vjp(dxn)
    return dz, dz, dg, db


def _outproj_fn(yg, x, w, g, b):
    z = ALPHA * x + _dot(yg, w, NN)
    xn = _ln(z, g, b)
    return z, xn, xn


def _loss_fn(xl, tgt):
    err = xl - tgt
    part = 0.5 * jnp.sum(jnp.mean(err * err, axis=-1, keepdims=True), axis=0, keepdims=True)
    return err * (1.0 / D_MODEL), jnp.broadcast_to(part, (1, LANES))


def _mla_fwd(q, k, v, tq, comm=None):
    t = q.shape[0]
    tq = min(tq, t)
    nq = t // tq

    def body(q_ref, k_ref, v_ref, o_ref, lse_ref):
        i = pl.program_id(1)
        qb = q_ref[...]
        r_chunk = lax.broadcasted_iota(jnp.int32, (tq, tq), 0) // CHUNK
        c_chunk = lax.broadcasted_iota(jnp.int32, (tq, tq), 1) // CHUNK
        diag_ok = c_chunk <= r_chunk

        def block(j, carry, masked):
            m, l, acc = carry
            off = pl.multiple_of(j * tq, tq)
            s = _dot(qb, k_ref[pl.ds(off, tq), :], NT)
            if masked:
                s = jnp.where(diag_ok, s, NEG)
            m2 = jnp.maximum(m, jnp.max(s, axis=1, keepdims=True))
            p = jnp.exp2(s - m2)
            a = jnp.exp2(m - m2)
            return m2, a * l + jnp.sum(p, axis=1, keepdims=True), a * acc + _dot(p, v_ref[pl.ds(off, tq), :], NN)

        init = (jnp.full((tq, 1), NEG, F32), jnp.zeros((tq, 1), F32), jnp.zeros((tq, MLA_V), F32))
        carry = lax.fori_loop(0, i, lambda j, c: block(j, c, False), init)
        m, l, acc = block(i, carry, True)
        o_ref[...] = acc / l
        lse_ref[...] = _col2row(m + jnp.log2(l))

    (o, lse), received = _comm_call(
        body, "mla_fwd", (MLA_HEADS, nq),
        [pl.BlockSpec((tq, MLA_HEAD_PAD), lambda h, i: (i, h)),
         pl.BlockSpec((t, MLA_HEAD_PAD), lambda h, i: (0, h)),
         pl.BlockSpec((t, MLA_V), lambda h, i: (0, h))],
        [pl.BlockSpec((tq, MLA_V), lambda h, i: (i, h)),
         pl.BlockSpec((None, 8, tq), lambda h, i: (h, 0, i))],
        [jax.ShapeDtypeStruct((t, MLA_WIDTH), F32), jax.ShapeDtypeStruct((MLA_HEADS, 8, t), F32)],
        [q, k, v], ("parallel", "parallel"), comm)
    return o, lse, received


def _mla_bwd(q, k, v, do, lse, drow, tq, comm=None):
    t = q.shape[0]
    tq = min(tq, t)
    nq = t // tq

    def body(k_ref, v_ref, q_ref, do_ref, lse_ref, d_ref, dk_ref, dv_ref, dq_ref):
        j = pl.program_id(1)

        @pl.when(j == 0)
        def _():
            dq_ref[...] = jnp.zeros_like(dq_ref)

        kj, vj = k_ref[...], v_ref[...]
        k_chunk = lax.broadcasted_iota(jnp.int32, (tq, tq), 0) // CHUNK
        q_chunk = lax.broadcasted_iota(jnp.int32, (tq, tq), 1) // CHUNK
        diag_ok = k_chunk <= q_chunk

        def block(i, carry, masked):
            dk, dv = carry
            off = pl.multiple_of(i * tq, tq)
            qi, doi = q_ref[pl.ds(off, tq), :], do_ref[pl.ds(off, tq), :]
            pt = jnp.exp2(_dot(kj, qi, NT) - lse_ref[0:1, pl.ds(off, tq)])
            if masked:
                pt = jnp.where(diag_ok, pt, 0.0)
            dv = dv + _dot(pt, doi, NN)
            dst = (pt * (_dot(vj, doi, NT) - d_ref[0:1, pl.ds(off, tq)])).astype(MXU)
            dk = dk + _dot(dst, qi, NN)
            dq_ref[pl.ds(off, tq), :] += _dot(dst, kj, TN) * LN2
            return dk, dv

        carry = block(j, (jnp.zeros((tq, MLA_HEAD_PAD), F32), jnp.zeros((tq, MLA_V), F32)), True)
        dk, dv = lax.fori_loop(j + 1, nq, lambda i, c: block(i, c, False), carry)
        dk_ref[...] = dk * LN2
        dv_ref[...] = dv

    (dk, dv, dq), received = _comm_call(
        body, "mla_bwd", (MLA_HEADS, nq),
        [pl.BlockSpec((tq, MLA_HEAD_PAD), lambda h, j: (j, h)),
         pl.BlockSpec((tq, MLA_V), lambda h, j: (j, h)),
         pl.BlockSpec((t, MLA_HEAD_PAD), lambda h, j: (0, h)),
         pl.BlockSpec((t, MLA_V), lambda h, j: (0, h)),
         pl.BlockSpec((None, 8, t), lambda h, j: (h, 0, 0)),
         pl.BlockSpec((None, 8, t), lambda h, j: (h, 0, 0))],
        [pl.BlockSpec((tq, MLA_HEAD_PAD), lambda h, j: (j, h)),
         pl.BlockSpec((tq, MLA_V), lambda h, j: (j, h)),
         pl.BlockSpec((t, MLA_HEAD_PAD), lambda h, j: (0, h))],
        [jax.ShapeDtypeStruct((t, MLA_HEADS * MLA_HEAD_PAD), F32), jax.ShapeDtypeStruct((t, MLA_WIDTH), F32),
         jax.ShapeDtypeStruct((t, MLA_HEADS * MLA_HEAD_PAD), F32)],
        [k, v, q, do, lse, drow], ("parallel", "arbitrary"), comm)
    return dq, dk, dv, received


def _sb_logs(z2):
    lb = jnp.minimum(z2, 0.0) - jnp.log2(1.0 + jnp.exp2(jnp.minimum(z2, -z2)))
    return lb, lb - z2


def _tri(n, keep):
    r = lax.broadcasted_iota(jnp.int32, (n, n), 0)
    c = lax.broadcasted_iota(jnp.int32, (n, n), 1)
    return keep(r, c)


def _sb_fwd(hb, tq):
    t = hb.shape[0]
    tq = min(tq, t)
    nq, tk = t // tq, min(SCAN, tq)
    cq, ck, cv = C_SBQ // SB_DIM, C_SBK // SB_DIM, C_SBV // SB_DIM

    def body(q_ref, k_ref, v_ref, o_ref, tot_ref, visited_ref):
        h, i = pl.program_id(0), pl.program_id(1)
        qb = q_ref[...]
        strict = _tri(tq, lambda r, c: c < r)
        later = _tri(tk, lambda r, c: r > c).astype(MXU)

        def block(off, n, carry, masked):
            run, acc = carry
            lb, l1 = _sb_logs(_dot(qb, k_ref[pl.ds(off, n), :], NT))
            if masked:
                l1 = jnp.where(strict, l1, 0.0)
            after, rowsum = _scan_keys(l1, later, True)
            a = jnp.exp2(lb + after + run)
            if masked:
                a = jnp.where(strict, a, 0.0)
            return run + rowsum, acc + _dot(a, v_ref[pl.ds(off, n), :], NN)

        diag = pl.multiple_of(i * tq, tq)
        carry = block(diag, tq, (jnp.zeros((tq, 1), F32), jnp.zeros((tq, SB_DIM), F32)), True)

        def live(state):
            jj, (run, _) = state
            return jnp.logical_and(jj < i * (tq // tk), jnp.max(run) > -SB_CUTOFF)

        def back(state):
            jj, c = state
            return jj + 1, block(pl.multiple_of(diag - (jj + 1) * tk, tk), tk, c, False)

        jj, (run, acc) = lax.while_loop(live, back, (jnp.int32(0), carry))
        o_ref[...] = acc
        tot_ref[...] = _col2row(run)
        visited_ref[h, i] = jj

    return _comm_call(
        body, "sb_fwd", (SB_HEADS, nq),
        [pl.BlockSpec((tq, SB_DIM), lambda h, i: (i, cq + h)),
         pl.BlockSpec((t, SB_DIM), lambda h, i: (0, ck + h)),
         pl.BlockSpec((t, SB_DIM), lambda h, i: (0, cv + h))],
        [pl.BlockSpec((tq, SB_DIM), lambda h, i: (i, h)),
         pl.BlockSpec((None, 8, tq), lambda h, i: (h, 0, i)),
         pl.BlockSpec(memory_space=pltpu.SMEM)],
        [jax.ShapeDtypeStruct((t, SB_WIDTH), F32), jax.ShapeDtypeStruct((SB_HEADS, 8, t), F32),
         jax.ShapeDtypeStruct((SB_HEADS, nq), jnp.int32)],
        [hb, hb, hb], ("arbitrary", "arbitrary"), None)[0]


def _sb_bwd(hb, do, tot, visited, tq):
    t = hb.shape[0]
    tq = min(tq, t)
    nq, tk = t // tq, min(SCAN, tq)
    cq, ck, cv = C_SBQ // SB_DIM, C_SBK // SB_DIM, C_SBV // SB_DIM

    def body(q_ref, do_ref, tot_ref, k_ref, v_ref, visited_ref, dq_ref, dk_ref, dv_ref):
        h, i = pl.program_id(0), pl.program_id(1)

        @pl.when(i == 0)
        def _():
            dk_ref[...] = jnp.zeros_like(dk_ref)
            dv_ref[...] = jnp.zeros_like(dv_ref)

        qb, dob = q_ref[...], do_ref[...]
        total = _row2col(tot_ref[0:1, :])
        strict = _tri(tq, lambda r, c: c < r)
        upto = _tri(tk, lambda r, c: r <= c).astype(MXU)
        before = _tri(tk, lambda r, c: r < c).astype(MXU)

        def block(off, n, carry, masked):
            pre, gpre, dq = carry
            kj, vj = k_ref[pl.ds(off, n), :], v_ref[pl.ds(off, n), :]
            lb, l1 = _sb_logs(_dot(qb, kj, NT))
            if masked:
                l1 = jnp.where(strict, l1, 0.0)
            upto_sum, rowsum = _scan_keys(l1, upto, False)
            a = jnp.exp2(lb + (total - (upto_sum + pre)))
            if masked:
                a = jnp.where(strict, a, 0.0)
            g = a * _dot(dob, vj, NT)
            before_sum, grow = _scan_keys(g, before, False)
            beta = jnp.exp2(lb)
            dz = g * (1.0 - beta) - (before_sum + gpre) * beta
            if masked:
                dz = jnp.where(strict, dz, 0.0)
            dzb = dz.astype(MXU)
            dk_ref[pl.ds(off, n), :] += _dot(dzb, qb, TN) * LN2
            dv_ref[pl.ds(off, n), :] += _dot(a, dob, TN)
            return pre + rowsum, gpre + grow, dq + _dot(dzb, kj, NN)

        diag = pl.multiple_of(i * tq, tq)
        start = diag - visited_ref[h, i] * tk
        init = (jnp.zeros((tq, 1), F32), jnp.zeros((tq, 1), F32), jnp.zeros((tq, SB_DIM), F32))
        carry = lax.fori_loop(0, visited_ref[h, i],
                              lambda jj, c: block(pl.multiple_of(start + jj * tk, tk), tk, c, False), init)
        dq_ref[...] = (block(diag, tq, carry, True)[2] * SB_SCALE).astype(dq_ref.dtype)

    return _comm_call(
        body, "sb_bwd", (SB_HEADS, nq),
        [pl.BlockSpec((tq, SB_DIM), lambda h, i: (i, cq + h)),
         pl.BlockSpec((tq, SB_DIM), lambda h, i: (i, h)),
         pl.BlockSpec((None, 8, tq), lambda h, i: (h, 0, i)),
         pl.BlockSpec((t, SB_DIM), lambda h, i: (0, ck + h)),
         pl.BlockSpec((t, SB_DIM), lambda h, i: (0, cv + h)),
         pl.BlockSpec(memory_space=pltpu.SMEM)],
        [pl.BlockSpec((tq, SB_DIM), lambda h, i: (i, h)),
         pl.BlockSpec((t, SB_DIM), lambda h, i: (0, h)),
         pl.BlockSpec((t, SB_DIM), lambda h, i: (0, h))],
        [jax.ShapeDtypeStruct((t, SB_WIDTH), MXU), jax.ShapeDtypeStruct((t, SB_WIDTH), F32),
         jax.ShapeDtypeStruct((t, SB_WIDTH), F32)],
        [hb, do, tot, hb, hb, visited], ("parallel", "arbitrary"), None)[0]


TM = 256
TQ_MLA_FWD = 2048
TQ_MLA_BWD = 1024
TQ_SB = 512


def _layer_fwd(xf, xb, w, consts, comm=None, rest=None):
    t = xf.shape[0]
    cc, sa, sb, mem = consts
    col_scale = jnp.ones((1, D_IN_PAD), F32).at[:, C_SBQ:C_SBQ + SB_WIDTH].set(SB_SCALE * LOG2E)
    (h, hb), arrived = _matmul("in_proj", xb, w["w_in"], "nt", 512, 1920, D_MODEL, second_dtype=MXU, second_scale=col_scale,
                               comm=rest[0] if rest else None, with_received=True)
    if rest:
        w = dict(w, **rest[1](arrived))
    gq, gkv = w["q_norm_g"][None, :], w["kv_norm_g"][None, :]
    q, k, v = _rowcall(
        "mla_prep", _prep_fn, t, TM,
        [(h, 512, C_CQ // 512), (h, 256, C_CKV // 256), (h, 128, C_KPE // 128), (cc, 128, 0), (sa, 128, 0), (sb, 128, 0)],
        [gq, w["w_uq"], gkv, w["w_ukv"]],
        [_ro(t, MLA_HEADS * MLA_HEAD_PAD, MXU, TM), _ro(t, MLA_HEADS * MLA_HEAD_PAD, MXU, TM), _ro(t, MLA_WIDTH, MXU, TM)])
    oa, lse, received = _mla_fwd(q, k, v, TQ_MLA_FWD, comm)
    oc, tot, visited = _sb_fwd(hb, TQ_SB)
    mkv = _matmul("mem_kv", mem, w["w_mem_kv"], "nn", 256, 512, 2048)
    lng, lnb = w["sg_ln_g"][None, :], w["sg_ln_b"][None, :]
    biasf = jnp.repeat(w["sg_b"].T, SG_CH, axis=1)
    post_rows = [(h, 768, C_GA // 768), (h, 512, C_SGU // 512), (h, 512, C_SGV // 512), (h, 512, C_GB // 512),
                 (h, 512, C_GC // 512), (h, 256, C_MQ // 256), (h, 256, C_GM // 256), (oa, 768, 0), (oc, 512, 0)]
    post_consts = [mkv, lng, lnb, w["sg_w"], biasf]
    (yg,) = _rowcall("mixer_post", lambda *a: (_post_fn(*a),), t, TM, post_rows, post_consts, [_ro(t, D_MODEL, MXU, TM)])
    z, xn, xnb = _rowcall(
        "out_proj_ln", _outproj_fn, t, TM, [(yg, D_MODEL, 0), (xf, D_MODEL, 0)],
        [w["w_out"], w["ln_g"][None, :], w["ln_b"][None, :]],
        [_ro(t, D_MODEL, F32, TM), _ro(t, D_MODEL, F32, TM), _ro(t, D_MODEL, MXU, TM)])
    saved = dict(w=w, xb=xb, h=h, hb=hb, q=q, k=k, v=v, oa=oa, lse=lse, oc=oc, tot=tot, visited=visited, mkv=mkv, yg=yg, z=z,
                 post_rows=post_rows, post_consts=post_consts, gq=gq, gkv=gkv)
    return xn, xnb, saved, received


def _layer_bwd(dxn, w, s, consts, recv=None, early_items=None, late_items=None):
    t = dxn.shape[0]
    cc, sa, sb, mem = consts
    h = s["h"]
    dz, dzb, dlng2, dlnb2 = _rowcall(
        "ln_bwd", _ln_bwd_fn, t, TM, [(dxn, D_MODEL, 0), (s["z"], D_MODEL, 0)],
        [w["ln_g"][None, :], w["ln_b"][None, :]],
        [_ro(t, D_MODEL, F32, TM), _ro(t, D_MODEL, MXU, TM)], [((1, D_MODEL), F32), ((1, D_MODEL), F32)])
    dyg = _matmul("out_proj_dx", dzb, w["w_out"], "nt", 512, 1024, 2048)
    dw_out = _matmul("out_proj_dw", s["yg"], dzb, "tn", 1024, 1024, 2048, out_dtype=GRAD_WIRE)
    tm = min(TM, t)
    outs = _rowcall(
        "mixer_post_bwd", _post_bwd_fn, t, TM, [(dyg, D_MODEL, 0)] + s["post_rows"], s["post_consts"],
        [_ro(t, 768, MXU, TM), _ro(t, 512, MXU, TM), _ro(t, 512, MXU, TM), _ro(t, 512, MXU, TM), _ro(t, 512, MXU, TM),
         _ro(t, 256, MXU, TM), _ro(t, 256, MXU, TM), _ro(t, 768, MXU, TM), _ro(t, 512, MXU, TM),
         ((MLA_HEADS, 8, t), F32, (MLA_HEADS, 8, tm), lambda i: (0, 0, i))],
        [((MEM_TOKENS, 2 * MEM_WIDTH), F32), ((1, SG_WIDTH), F32), ((1, SG_WIDTH), F32),
         ((SG_GROUPS, SG_CHUNK, SG_CHUNK), F32), ((SG_CHUNK, SG_WIDTH), F32)])
    dga, dsgu, dsgv, dgb, dgc, dmq, dgm, doa, doc, drow, dmkv, dsg_ln_g, dsg_ln_b, dsgw, dbias = outs
    dw_mem_kv = _matmul("mem_kv_dw", mem, dmkv, "tn", 1024, 512, 256, out_dtype=GRAD_WIRE)
    dsbq, dsbk, dsbv = _sb_bwd(s["hb"], doc, s["tot"], s["visited"], TQ_SB)
    named = early_items(dict(w_out=dw_out, w_mem_kv=dw_mem_kv)) if early_items else []
    dq, dk, dv, received = _mla_bwd(s["q"], s["k"], s["v"], doa, s["lse"], drow, TQ_MLA_BWD,
                                    _Comm([item for _, item in named]) if named else None)
    if named:
        recv.update({n: r for (n, _), r in zip(named, received)})
    wq, wkv = MLA_HEADS * MLA_HEAD_PAD, MLA_WIDTH
    dcq, dckv, dkpe, dgq, dwuq, dgkv, dwukv = _rowcall(
        "mla_prep_bwd", _prep_bwd_fn, t, TM,
        [(dq, wq, 0), (dk, wq, 0), (dv, wkv, 0), (h, 512, C_CQ // 512), (h, 256, C_CKV // 256), (h, 128, C_KPE // 128),
         (cc, 128, 0), (sa, 128, 0), (sb, 128, 0)],
        [s["gq"], w["w_uq"], s["gkv"], w["w_ukv"]],
        [_ro(t, 512, MXU, TM), _ro(t, 256, MXU, TM), _ro(t, 128, MXU, TM)],
        [((1, MLA_Q_RANK), F32), ((wq, MLA_Q_RANK), F32), ((1, MLA_KV_RANK), F32), ((2 * wkv, MLA_KV_RANK), F32)])
    dh = jnp.concatenate([dcq, dckv, dga, dsgu, dsgv, dgb, dsbq, dsbk.astype(MXU), dsbv.astype(MXU), dgc, dmq, dgm, dkpe], axis=1)
    dw_in = _matmul("in_proj_dw", dh, s["xb"], "tn", 1152, 1024, 2048, out_dtype=GRAD_WIRE)
    replicated = dict(q_norm_g=dgq[0], kv_norm_g=dgkv[0], sg_ln_g=dsg_ln_g[0], sg_ln_b=dsg_ln_b[0], sg_w=dsgw, sg_bias=dbias,
                      ln_g=dlng2[0], ln_b=dlnb2[0])
    named = late_items(dict(w_in=dw_in, w_uq=dwuq, w_ukv=dwukv), replicated) if late_items else []
    dx, received = _matmul("in_proj_dx", dh, w["w_in"], "nn", 512, 1024, D_IN_PAD, add=dz, add_scale=ALPHA,
                           comm=_Comm([item for _, item in named]) if named else None, with_received=True)
    if named:
        recv.update({n: r for (n, _), r in zip(named, received)})
    return dx, dict(replicated, w_in=dw_in, w_uq=dwuq, w_ukv=dwukv, w_mem_kv=dw_mem_kv, w_out=dw_out)


def _sg_bias_grad(bias_grads):
    db = jnp.concatenate(bias_grads, axis=0)

    def lane_sums(d):
        return (jnp.concatenate([jnp.broadcast_to(jnp.sum(d[:, g * SG_CH:(g + 1) * SG_CH], axis=1, keepdims=True),
                                                  (d.shape[0], SG_CH)) for g in range(SG_GROUPS)], axis=1),)

    (dbs,) = _rowcall("sg_bias_sum", lane_sums, db.shape[0], SG_CHUNK, [(db, SG_WIDTH, 0)], [], [_ro(db.shape[0], SG_WIDTH, F32, SG_CHUNK)])
    return dbs[:, ::SG_CH].reshape(len(bias_grads), SG_CHUNK, SG_GROUPS).transpose(0, 2, 1)


def _local_step(x, mem, positions, target, depth, weights_of, gather_comm, first_rest, grad_items, rep_item, recv):
    t = x.shape[0]
    inv_freq = ROPE_THETA ** (-jnp.arange(0, MLA_ROPE, 2, dtype=F32) / MLA_ROPE)
    invf = jnp.tile(inv_freq, 4)[None, :]
    cc, sa, sb = _rowcall("rope_tables", _rope_tables_fn, t, TM, [(positions[:, None], 1, 0)], [invf],
                          [_ro(t, 128, F32, TM)] * 3)
    consts = (cc, sa, sb, mem)
    xf, xb, saved, received = x, x.astype(MXU), [], None
    for l in range(depth):
        xf, xb, s, received = _layer_fwd(xf, xb, weights_of(l, received), consts, gather_comm(l + 1) if l + 1 < depth else None,
                                         first_rest if l == 0 else None)
        saved.append(s)
    dx, loss_part = _rowcall("loss", _loss_fn, t, TM, [(xf, D_MODEL, 0), (target, D_MODEL, 0)], [],
                             [_ro(t, D_MODEL, F32, TM)], [((1, LANES), F32)])
    grads, recv, sg_b_grad = [None] * depth, dict(recv or {}), []
    for l in reversed(range(depth)):
        def early(g, l=l):
            rest = {n: grads[l + 1][n] for n in ("w_in", "w_uq", "w_ukv")} if l + 1 < depth else {}
            return grad_items(l, g, recv) + (grad_items(l + 1, rest, recv) if rest else [])

        def late(g, replicated):
            per_layer = [replicated] + grads[1:]
            sg_b_grad.append(_sg_bias_grad([r["sg_bias"] for r in per_layer]))
            return grad_items(0, g, recv) + [("replicated", rep_item(per_layer, sg_b_grad[0]))]

        dx, grads[l] = _layer_bwd(dx, saved[l]["w"], saved[l], consts, recv, early if grad_items else None,
                                  late if grad_items and l == 0 else None)
    dsgb = sg_b_grad[0] if sg_b_grad else _sg_bias_grad([g["sg_bias"] for g in grads])
    return loss_part[0, 0], dx, grads, dsgb, recv


SHARD_IN = D_IN // N_DEV
SHARD_IN_PAD = D_IN_PAD // N_DEV
IN_RANGES = ((0, 768), (832, D_IN), (768, 832))


def _in_rows_from_shards(w):
    pieces = []
    for a, b in IN_RANGES:
        while a < b:
            e = min(b, (a // SHARD_IN + 1) * SHARD_IN)
            row = (a // SHARD_IN) * SHARD_IN_PAD + a % SHARD_IN
            pieces.append(w[row:row + e - a])
            a = e
    pieces.append(jnp.zeros((D_IN_PAD - D_IN, w.shape[1]), w.dtype))
    return jnp.concatenate(pieces, axis=0)


def _in_rows_to_shards(g):
    pieces = []
    for d in range(N_DEV):
        a, b = d * SHARD_IN, (d + 1) * SHARD_IN
        while a < b:
            e = min(b, 768 if a < 768 else (832 if a < 832 else D_IN))
            row = a if a < 768 else (C_KPE + a - 768 if a < 832 else a - 64)
            pieces.append(g[row:row + e - a])
            a = e
        pieces.append(jnp.zeros((SHARD_IN_PAD - SHARD_IN, g.shape[1]), g.dtype))
    return jnp.concatenate(pieces, axis=0)


def _uq_rows_pad(w):
    w = w.reshape(MLA_HEADS, MLA_NOPE + MLA_ROPE, w.shape[-1])
    return jnp.pad(w, ((0, 0), (0, MLA_HEAD_PAD - MLA_NOPE - MLA_ROPE), (0, 0))).reshape(MLA_HEADS * MLA_HEAD_PAD, -1)


def _uq_rows_unpad(g):
    return g.reshape(MLA_HEADS, MLA_HEAD_PAD, -1)[:, :MLA_NOPE + MLA_ROPE].reshape(MLA_HEADS * (MLA_NOPE + MLA_ROPE), -1)


def _ukv_rows_split(w):
    return jnp.swapaxes(w.reshape(MLA_HEADS, 2, 128, -1), 0, 1).reshape(MLA_HEADS * 256, -1)


def _ukv_rows_unsplit(g):
    return jnp.swapaxes(g.reshape(2, MLA_HEADS, 128, -1), 0, 1).reshape(MLA_HEADS * 256, -1)


class _Comm:
    def __init__(self, items):
        self.items = items
        self.n = len(items)
        self.pieces = [((s.shape[0] // N_DEV,) + s.shape[1:]) if sc else s.shape for s, sc, _ in items]
        self.inputs = [s for s, _, _ in items] + [d[0] for _, _, d in items if d is not None]
        any_space = pl.BlockSpec(memory_space=pl.ANY)
        self.in_specs = [any_space] * len(self.inputs)
        self.out_specs = [any_space] * self.n
        self.out_shape = [jax.ShapeDtypeStruct((N_DEV,) + tuple(p), s.dtype) if d is None
                          else jax.ShapeDtypeStruct(d[0].shape, d[0].dtype) for p, (s, _, d) in zip(self.pieces, items)]
        self.scratch = [pltpu.SemaphoreType.DMA((N_DEV - 1, self.n)), pltpu.SemaphoreType.DMA((N_DEV - 1, self.n)),
                        pltpu.SemaphoreType.DMA((self.n,))]

    def aliases(self, first_in, first_out):
        passed = [a for a, (_, _, d) in enumerate(self.items) if d is not None]
        return {first_in + self.n + b: first_out + a for b, a in enumerate(passed)}

    def copies(self, in_refs, out_refs, sems):
        send_sems, recv_sems, local_sems = sems
        x, y, c = lax.axis_index("x"), lax.axis_index("y"), lax.axis_index("c")
        me = 4 * x + 2 * y + c

        def block_for(a, p):
            if not self.items[a][1]:
                return in_refs[a]
            n = self.pieces[a][0]
            return in_refs[a].at[pl.ds(pl.multiple_of(p * n, SUBLANES_16BIT), n)]

        def slot(a, dev):
            dst = self.items[a][2]
            return out_refs[a].at[dev] if dst is None else out_refs[a].at[dev, dst[1]]

        starts, recv_waits, send_waits = [], [], []
        for a in range(self.n):
            cp = pltpu.make_async_copy(block_for(a, me), slot(a, me), local_sems.at[a])
            starts.append(cp.start)
            send_waits.append(cp.wait)
        for k in range(1, N_DEV):
            px = 1 - x if (k >> 2) & 1 else x
            py = 1 - y if (k >> 1) & 1 else y
            pc = 1 - c if k & 1 else c
            peer = 4 * px + 2 * py + pc
            for a in range(self.n):
                out = pltpu.make_async_remote_copy(
                    src_ref=block_for(a, peer), dst_ref=slot(a, me), send_sem=send_sems.at[k - 1, a],
                    recv_sem=recv_sems.at[k - 1, a], device_id=(px, py, pc), device_id_type=pl.DeviceIdType.MESH)
                back = pltpu.make_async_remote_copy(
                    src_ref=block_for(a, peer), dst_ref=slot(a, peer), send_sem=send_sems.at[k - 1, a],
                    recv_sem=recv_sems.at[k - 1, a], device_id=(px, py, pc), device_id_type=pl.DeviceIdType.MESH)
                starts.append(out.start)
                send_waits.append(out.wait_send)
                recv_waits.append(back.wait_recv)
        return starts, recv_waits + send_waits


def _comm_call(body, name, grid, in_specs, out_specs, out_shape, operands, semantics, comm, scratch=()):
    n_in, n_out, n_scr = len(operands), len(out_shape), len(scratch)
    if comm is None:
        outs = pl.pallas_call(body, name=name, grid=grid, in_specs=in_specs, out_specs=out_specs, out_shape=out_shape,
                              scratch_shapes=list(scratch), compiler_params=_params(semantics))(*operands)
        return outs, []
    n_ci = len(comm.inputs)

    def fused(*refs):
        ins, c_in = refs[:n_in], refs[n_in:n_in + n_ci]
        outs = refs[n_in + n_ci:n_in + n_ci + n_out]
        c_out = refs[n_in + n_ci + n_out:n_in + n_ci + n_out + comm.n]
        first_scr = n_in + n_ci + n_out + comm.n
        outs = outs + refs[first_scr:first_scr + n_scr]
        starts, waits = comm.copies(c_in, c_out, refs[first_scr + n_scr:])
        first = functools.reduce(jnp.logical_and, [pl.program_id(d) == 0 for d in range(len(grid))])
        last = functools.reduce(jnp.logical_and, [pl.program_id(d) == g - 1 for d, g in enumerate(grid)])

        @pl.when(first)
        def _():
            for start in starts:
                start()

        body(*ins, *outs)

        @pl.when(last)
        def _():
            for wait in waits:
                wait()

    outs = pl.pallas_call(
        fused, name=name, grid=grid, in_specs=list(in_specs) + comm.in_specs, out_specs=list(out_specs) + comm.out_specs,
        out_shape=list(out_shape) + comm.out_shape, scratch_shapes=list(scratch) + comm.scratch,
        input_output_aliases=comm.aliases(n_in, n_out), compiler_params=_params(("arbitrary",) * len(grid)),
    )(*operands, *comm.inputs)
    return outs[:n_out], outs[n_out:]


def _exchange(name, items):
    return _comm_call(lambda: None, name, (1,), [], [], [], [], ("arbitrary",), _Comm(items))[1]


BLOCK_BYTES = 6 << 20


SUBLANES_16BIT = 16


def _row_block(rows, row_bytes):
    step = SUBLANES_16BIT
    if rows * row_bytes <= BLOCK_BYTES or rows % step:
        return rows
    return max(tb for tb in range(step, rows + 1, step) if rows % tb == 0 and (tb == step or tb * row_bytes <= BLOCK_BYTES))


def _sum_contribs(recv):
    n, rows, width = recv.shape
    tb = _row_block(rows, width * 4 * (n + 1))

    def body(r_ref, g_ref):
        g = r_ref[0].astype(F32)
        for k in range(1, n):
            g = g + r_ref[k].astype(F32)
        g_ref[...] = g

    return pl.pallas_call(
        body, name="sum_grads", grid=(rows // tb,), in_specs=[pl.BlockSpec((n, tb, width), lambda i: (0, i, 0))],
        out_specs=pl.BlockSpec((tb, width), lambda i: (i, 0)), out_shape=jax.ShapeDtypeStruct((rows, width), F32),
        compiler_params=_params(("parallel",)),
    )(recv)


def _adamw(contribs, w, m, v):
    n, rows, width = contribs.shape
    tb = _row_block(rows, width * 4 * (n + 7))
    c1 = 1.0 - ADAM_B1 ** ADAM_STEP
    c2 = 1.0 - ADAM_B2 ** ADAM_STEP

    def body(r_ref, w_ref, m_ref, v_ref, g_ref, d_ref, nm_ref, nv_ref):
        g = r_ref[0].astype(F32)
        for k in range(1, n):
            g = g + r_ref[k].astype(F32)
        nm = ADAM_B1 * m_ref[...] + (1.0 - ADAM_B1) * g
        nv = ADAM_B2 * v_ref[...] + (1.0 - ADAM_B2) * (g * g)
        g_ref[...] = g
        nm_ref[...] = nm
        nv_ref[...] = nv
        d_ref[...] = -ADAM_LR * ((nm / c1) / (jnp.sqrt(nv / c2) + ADAM_EPS) + ADAM_WD * w_ref[...])

    blk = pl.BlockSpec((tb, width), lambda i: (i, 0))
    return pl.pallas_call(
        body, name="adamw", grid=(rows // tb,),
        in_specs=[pl.BlockSpec((n, tb, width), lambda i: (0, i, 0)), blk, blk, blk],
        out_specs=[blk] * 4, out_shape=[jax.ShapeDtypeStruct((rows, width), F32)] * 4,
        compiler_params=_params(("parallel",)),
    )(contribs, w, m, v)


def kernel(x, mem, positions, w_in, q_norm_g, w_uq, kv_norm_g, w_ukv, sg_ln_g, sg_ln_b, sg_w, sg_b, w_mem_k, w_mem_v, w_out, ln_g, ln_b, loss_target, m_w_in, m_q_norm_g, m_w_uq, m_kv_norm_g, m_w_ukv, m_sg_ln_g, m_sg_ln_b, m_sg_w, m_sg_b, m_w_mem_k, m_w_mem_v, m_w_out, m_ln_g, m_ln_b, v_w_in, v_q_norm_g, v_w_uq, v_kv_norm_g, v_w_ukv, v_sg_ln_g, v_sg_ln_b, v_sg_w, v_sg_b, v_w_mem_k, v_w_mem_v, v_w_out, v_ln_g, v_ln_b):
    wl = dict(w_in=w_in, q_norm_g=q_norm_g, w_uq=w_uq, kv_norm_g=kv_norm_g, w_ukv=w_ukv, sg_ln_g=sg_ln_g, sg_ln_b=sg_ln_b,
              sg_w=sg_w, sg_b=sg_b, w_mem_k=w_mem_k, w_mem_v=w_mem_v, w_out=w_out, ln_g=ln_g, ln_b=ln_b)
    ml = dict(w_in=m_w_in, q_norm_g=m_q_norm_g, w_uq=m_w_uq, kv_norm_g=m_kv_norm_g, w_ukv=m_w_ukv, sg_ln_g=m_sg_ln_g,
              sg_ln_b=m_sg_ln_b, sg_w=m_sg_w, sg_b=m_sg_b, w_mem_k=m_w_mem_k, w_mem_v=m_w_mem_v, w_out=m_w_out, ln_g=m_ln_g, ln_b=m_ln_b)
    vl = dict(w_in=v_w_in, q_norm_g=v_q_norm_g, w_uq=v_w_uq, kv_norm_g=v_kv_norm_g, w_ukv=v_w_ukv, sg_ln_g=v_sg_ln_g,
              sg_ln_b=v_sg_ln_b, sg_w=v_sg_w, sg_b=v_sg_b, w_mem_k=v_w_mem_k, w_mem_v=v_w_mem_v, w_out=v_w_out, ln_g=v_ln_g, ln_b=v_ln_b)
    depth = w_in.shape[0]

    w_in_t = jnp.pad(jnp.swapaxes(w_in, 1, 2), ((0, 0), (0, SHARD_IN_PAD - SHARD_IN), (0, 0)))
    shards = [s.astype(MXU) for s in (w_in_t, jnp.swapaxes(w_uq, 1, 2), jnp.swapaxes(w_ukv, 1, 2),
                                      jnp.concatenate([w_mem_k, w_mem_v], axis=-1), w_out)]

    def gather_items(l):
        return [(s[l], False, None) for s in shards]

    (first_in,) = _exchange("gather_weights", gather_items(0)[:1])

    def in_rows(g_in):
        return _in_rows_from_shards(g_in.reshape(D_IN_PAD, D_MODEL))

    def other_weights(g_uq, g_ukv, g_mem, g_out):
        return dict(w_uq=_uq_rows_pad(g_uq.reshape(-1, MLA_Q_RANK)), w_ukv=_ukv_rows_split(g_ukv.reshape(-1, MLA_KV_RANK)),
                    w_mem_kv=g_mem.reshape(D_MODEL, 2 * MEM_WIDTH), w_out=g_out.reshape(D_MODEL, D_MODEL))

    def weights_of(l, received):
        w = dict(q_norm_g=q_norm_g[l], kv_norm_g=kv_norm_g[l], sg_ln_g=sg_ln_g[l], sg_ln_b=sg_ln_b[l], sg_w=sg_w[l],
                 sg_b=sg_b[l], ln_g=ln_g[l], ln_b=ln_b[l])
        if l == 0:
            return dict(w, w_in=in_rows(first_in))
        return dict(w, w_in=in_rows(received[0]), **other_weights(*received[1:]))

    first_rest = (_Comm(gather_items(0)[1:]), lambda received: other_weights(*received))

    to_rows = dict(w_in=_in_rows_to_shards, w_uq=_uq_rows_unpad, w_ukv=_ukv_rows_unsplit)

    def grad_items(l, g, recv):
        return [(n, (to_rows.get(n, lambda a: a)(a).astype(GRAD_WIRE), True, (recv[n], l))) for n, a in g.items()]

    shard_rows = dict(w_in=(D_IN_PAD, D_MODEL), w_uq=(MLA_HEADS * (MLA_NOPE + MLA_ROPE), MLA_Q_RANK),
                      w_ukv=(MLA_HEADS * 256, MLA_KV_RANK), w_mem_kv=(D_MODEL, 2 * MEM_WIDTH), w_out=(D_MODEL, D_MODEL))
    recv = {n: lax.empty((N_DEV, depth, rows // N_DEV, width), GRAD_WIRE) for n, (rows, width) in shard_rows.items()}
    rep_sizes = [wl[n].size for n in REPLICATED]
    rep_rows = -(-sum(rep_sizes) // (SUBLANES_16BIT * LANES)) * SUBLANES_16BIT

    def packed(d):
        flat = jnp.concatenate([d[n].reshape(-1) for n in REPLICATED])
        return jnp.pad(flat, (0, rep_rows * LANES - flat.shape[0])).reshape(rep_rows, LANES)

    def rep_item(per_layer, dsgb):
        rep_g = dict(sg_b=dsgb, **{n: jnp.stack([g[n] for g in per_layer]) for n in REPLICATED if n != "sg_b"})
        return (packed(rep_g), False, None)

    loss_part, grad_x, grads, dsgb, recv = _local_step(
        x[0], mem[0], positions[0], loss_target[0], depth, weights_of, lambda l: _Comm(gather_items(l)), first_rest,
        grad_items, rep_item, recv)
    loss = lax.psum(loss_part, ("x", "y", "c"))
    r_in, r_uq, r_ukv, r_mem, r_out, r_rep = (recv[n] for n in ("w_in", "w_uq", "w_ukv", "w_mem_kv", "w_out", "replicated"))

    def rows2d(a):
        return a.reshape(-1, a.shape[-1])

    def update(name, contribs):
        outs = _adamw(contribs, rows2d(wl[name]), rows2d(ml[name]), rows2d(vl[name]))
        return [o.reshape(wl[name].shape) for o in outs]

    def from_transposed(recv, keep):
        s = _sum_contribs(recv.reshape(N_DEV, -1, recv.shape[-1])).reshape(recv.shape[1:])
        return rows2d(jnp.swapaxes(s[:, :keep], 1, 2))[None]

    res = dict(
        w_in=update("w_in", from_transposed(r_in, SHARD_IN)),
        w_uq=update("w_uq", from_transposed(r_uq, r_uq.shape[2])),
        w_ukv=update("w_ukv", from_transposed(r_ukv, r_ukv.shape[2])),
        w_mem_k=update("w_mem_k", r_mem[..., :MEM_WIDTH].reshape(N_DEV, -1, MEM_WIDTH)),
        w_mem_v=update("w_mem_v", r_mem[..., MEM_WIDTH:].reshape(N_DEV, -1, MEM_WIDTH)),
        w_out=update("w_out", r_out.reshape(N_DEV, -1, D_MODEL)))
    rep_outs = _adamw(r_rep, packed(wl), packed(ml), packed(vl))
    off = 0
    for n, size in zip(REPLICATED, rep_sizes):
        res[n] = [o.reshape(-1)[off:off + size].reshape(wl[n].shape) for o in rep_outs]
        off += size
    return (loss, grad_x[None], *[res[n][0] for n in WEIGHTS], *[res[n][1] for n in WEIGHTS],
            *[res[n][2] for n in WEIGHTS], *[res[n][3] for n in WEIGHTS])
```

```python
import functools
import math

import jax
import jax.numpy as jnp
from jax import lax
from jax.experimental import pallas as pl
from jax.experimental.pallas import tpu as pltpu

F32 = jnp.float32
MXU = jnp.bfloat16
GRAD_WIRE = jnp.bfloat16

DEPTH_FOR_NORM = 4
D_MODEL = 2048
CHUNK = 64
MLA_HEADS, MLA_NOPE, MLA_ROPE, MLA_V = 6, 128, 64, 128
MLA_Q_RANK, MLA_KV_RANK = 512, 256
MLA_WIDTH = MLA_HEADS * MLA_V
MLA_HEAD_PAD = 256
ROPE_THETA = 10000.0
SG_GROUPS, SG_CH, SG_CHUNK = 4, 128, 128
SG_WIDTH = SG_GROUPS * SG_CH
SB_HEADS, SB_DIM = 4, 128
SB_WIDTH = SB_HEADS * SB_DIM
MEM_TOKENS, MEM_HEADS, MEM_DIM = 256, 4, 64
MEM_WIDTH = MEM_HEADS * MEM_DIM
D_IN = 5696
D_IN_PAD = 5760
ALPHA = (2.0 * DEPTH_FOR_NORM) ** 0.25
LN_EPS = 1e-5
RMS_EPS = 1e-6
MLA_SCALE = 1.0 / math.sqrt(MLA_NOPE + MLA_ROPE)
SB_SCALE = 1.0 / math.sqrt(SB_DIM)
MEM_SCALE = 1.0 / math.sqrt(MEM_DIM)
LOG2E = math.log2(math.e)
LN2 = math.log(2.0)
ADAM_LR, ADAM_B1, ADAM_B2, ADAM_EPS, ADAM_WD, ADAM_STEP = 0.001, 0.9, 0.999, 1e-08, 0.01, 10
N_DEV = 8
LANES = 128
NEG = -1e30
SCAN = 256
SB_CUTOFF = 256.0
VMEM_LIMIT = 56 * 1024 * 1024

C_CQ, C_CKV, C_GA, C_SGU, C_SGV, C_GB = 0, 512, 768, 1536, 2048, 2560
C_SBQ, C_SBK, C_SBV, C_GC, C_MQ, C_GM, C_KPE = 3072, 3584, 4096, 4608, 5120, 5376, 5632

REPLICATED = ("q_norm_g", "kv_norm_g", "sg_ln_g", "sg_ln_b", "sg_w", "sg_b", "ln_g", "ln_b")
WEIGHTS = ("w_in", "q_norm_g", "w_uq", "kv_norm_g", "w_ukv", "sg_ln_g", "sg_ln_b", "sg_w", "sg_b",
           "w_mem_k", "w_mem_v", "w_out", "ln_g", "ln_b")


def _params(sem):
    return pltpu.CompilerParams(dimension_semantics=sem, vmem_limit_bytes=VMEM_LIMIT)


def _dot(a, b, dims):
    return lax.dot_general(a.astype(MXU), b.astype(MXU), (dims, ((), ())), preferred_element_type=F32)


NN, NT, TN = ((1,), (0,)), ((1,), (1,)), ((0,), (0,))


@jax.custom_vjp
def _mm(a, b):
    return _dot(a, b, NN)


def _mm_fwd(a, b):
    return _dot(a, b, NN), (a, b)


def _mm_bwd(res, g):
    a, b = res
    return _dot(g, b, NT), _dot(a, g, TN)


_mm.defvjp(_mm_fwd, _mm_bwd)


@jax.custom_vjp
def _mm_nt(a, b):
    return _dot(a, b, NT)


def _mm_nt_fwd(a, b):
    return _dot(a, b, NT), (a, b)


def _mm_nt_bwd(res, g):
    a, b = res
    return _dot(g, b, NN), _dot(g, a, TN)


_mm_nt.defvjp(_mm_nt_fwd, _mm_nt_bwd)


def _split_dot(x, tri):
    hi = x.astype(MXU)
    lo = (x - hi.astype(F32)).astype(MXU)
    return _dot(hi, tri, NN) + _dot(lo, tri, NN)


def _scan_keys(x, tri, reverse):
    w = tri.shape[0]
    nb = x.shape[1] // w
    outs, carry = [None] * nb, None
    for c in (reversed(range(nb)) if reverse else range(nb)):
        xc = x[:, c * w:(c + 1) * w]
        sc = _split_dot(xc, tri)
        outs[c] = sc if carry is None else sc + carry
        rs = jnp.sum(xc, axis=1, keepdims=True)
        carry = rs if carry is None else carry + rs
    return (outs[0] if nb == 1 else jnp.concatenate(outs, axis=1)), carry


def _col2row(col):
    return jnp.transpose(jnp.broadcast_to(col, (col.shape[0], LANES)))[0:8]


def _row2col(row):
    return jnp.transpose(jnp.broadcast_to(row, (LANES, row.shape[1])))[:, 0:1]


def _matmul(name, a, b, mode, tm, tn, tk, out_dtype=F32, add=None, add_scale=1.0, second_dtype=None,
            second_scale=None, comm=None, with_received=False):
    if mode == "nn":
        (m, k), n = a.shape, b.shape[1]
    elif mode == "nt":
        (m, k), n = a.shape, b.shape[0]
    else:
        (k, m), n = a.shape, b.shape[1]
    tm, tn, tk = min(tm, m), min(tn, n), min(tk, k)
    assert m % tm == 0 and n % tn == 0 and k % tk == 0, (name, m, n, k, tm, tn, tk)
    nk = k // tk
    dims = {"nn": NN, "nt": NT, "tn": TN}[mode]
    a_spec = (pl.BlockSpec((tk, tm), lambda j, i, kk: (kk, i)) if mode == "tn"
              else pl.BlockSpec((tm, tk), lambda j, i, kk: (i, kk)))
    b_spec = (pl.BlockSpec((tn, tk), lambda j, i, kk: (j, kk)) if mode == "nt"
              else pl.BlockSpec((tk, tn), lambda j, i, kk: (kk, j)))
    o_spec = pl.BlockSpec((tm, tn), lambda j, i, kk: (i, j))
    in_specs, operands = [a_spec, b_spec], [a, b]
    if add is not None:
        in_specs.append(o_spec)
        operands.append(add)
    if second_scale is not None:
        in_specs.append(pl.BlockSpec((1, tn), lambda j, i, kk: (0, j)))
        operands.append(second_scale)
    out_shape = [jax.ShapeDtypeStruct((m, n), out_dtype)]
    out_specs = [o_spec]
    if second_dtype is not None:
        out_shape.append(jax.ShapeDtypeStruct((m, n), second_dtype))
        out_specs.append(o_spec)

    def body(*refs):
        a_ref, b_ref = refs[0], refs[1]
        extra = list(refs[2:len(operands)])
        add_ref = extra.pop(0) if add is not None else None
        scale_ref = extra.pop(0) if second_scale is not None else None
        outs = refs[len(operands):-1]
        acc_ref = refs[-1]
        kk = pl.program_id(2)

        def finish(r):
            if add_ref is not None:
                r = r + add_scale * add_ref[...]
            outs[0][...] = r.astype(out_dtype)
            if second_dtype is not None:
                outs[1][...] = (r if scale_ref is None else r * scale_ref[...]).astype(second_dtype)

        if nk == 1:
            finish(_dot(a_ref[...], b_ref[...], dims))
            return

        @pl.when(kk == 0)
        def _():
            acc_ref[...] = jnp.zeros_like(acc_ref)

        acc_ref[...] += _dot(a_ref[...], b_ref[...], dims)

        @pl.when(kk == nk - 1)
        def _():
            finish(acc_ref[...])

    res, received = _comm_call(body, name, (n // tn, m // tm, nk), in_specs, out_specs, out_shape, operands,
                               ("parallel", "parallel", "arbitrary"), comm, scratch=[pltpu.VMEM((tm, tn) if nk > 1 else (8, LANES), F32)])
    res = res if second_dtype is not None else res[0]
    return (res, received) if with_received else res


def _accumulate(ref, val, first):
    @pl.when(first)
    def _():
        ref[...] = val

    @pl.when(jnp.logical_not(first))
    def _():
        ref[...] += val


def _rowcall(name, fn, rows, tm, row_in, const_in, row_out, acc_out=()):
    tm = min(tm, rows)
    assert rows % tm == 0
    n_in = len(row_in) + len(const_in)
    in_specs = [pl.BlockSpec((tm, w), lambda i, cb=cb: (i, cb)) for _, w, cb in row_in]
    in_specs += [pl.BlockSpec(c.shape, lambda i, nd=c.ndim: (0,) * nd) for c in const_in]
    out_specs = [pl.BlockSpec(blk, imap) for _, _, blk, imap in row_out]
    out_specs += [pl.BlockSpec(s, lambda i, nd=len(s): (0,) * nd) for s, _ in acc_out]
    out_shape = [jax.ShapeDtypeStruct(s, dt) for s, dt, _, _ in row_out]
    out_shape += [jax.ShapeDtypeStruct(s, dt) for s, dt in acc_out]

    def body(*refs):
        vals = fn(*[r[...] for r in refs[:n_in]])
        outs = refs[n_in:]
        first = pl.program_id(0) == 0
        for k, (_, dt, _, _) in enumerate(row_out):
            outs[k][...] = vals[k].astype(dt)
        for k in range(len(acc_out)):
            _accumulate(outs[len(row_out) + k], vals[len(row_out) + k], first)

    return pl.pallas_call(
        body, name=name, grid=(rows // tm,), in_specs=in_specs, out_specs=out_specs, out_shape=out_shape,
        compiler_params=_params(("arbitrary",) if acc_out else ("parallel",)),
    )(*[a for a, _, _ in row_in], *const_in)


def _ro(rows, width, dtype, tm):
    tm = min(tm, rows)
    return ((rows, width), dtype, (tm, width), lambda i: (i, 0))


def _rms(x, g):
    ms = jnp.mean(x * x, axis=-1, keepdims=True)
    return x * lax.rsqrt(ms + RMS_EPS) * g


def _ln(x, g, b):
    mu = jnp.mean(x, axis=-1, keepdims=True)
    xc = x - mu
    var = jnp.mean(xc * xc, axis=-1, keepdims=True)
    return xc * lax.rsqrt(var + LN_EPS) * g + b


def _gelu(x):
    return 0.5 * x * (1.0 + jnp.tanh(math.sqrt(2.0 / math.pi) * (x + 0.044715 * (x * x * x))))


def _silu(x):
    return x / (1.0 + jnp.exp(-x))


@jax.custom_vjp
def _rope(r, cc, sa, sb):
    return r * cc + pltpu.roll(r, 96, 1) * sa + pltpu.roll(r, 32, 1) * sb


def _rope_fwd(r, cc, sa, sb):
    return _rope(r, cc, sa, sb), (cc, sa, sb)


def _rope_bwd(res, d):
    cc, sa, sb = res
    return d * cc + pltpu.roll(d * sa, 32, 1) + pltpu.roll(d * sb, 96, 1), None, None, None


_rope.defvjp(_rope_fwd, _rope_bwd)


def _rope_tables_fn(pos, invf):
    ang = pos.astype(F32) * invf
    lane = lax.broadcasted_iota(jnp.int32, ang.shape, 1)
    cos, sin = jnp.cos(ang), jnp.sin(ang)
    zero = jnp.zeros_like(ang)
    cc = jnp.where(lane < 64, cos, zero)
    sa = jnp.where(lane < 32, -sin, zero)
    sb = jnp.where((lane >= 32) & (lane < 64), sin, zero)
    return cc, sa, sb


def _prep_fn(cq, ckv, kpe, cc, sa, sb, gq, wuq, gkv, wukv):
    q = _mm_nt(_rms(cq, gq), wuq) * (MLA_SCALE * LOG2E)
    kv = _mm_nt(_rms(ckv, gkv), wukv)
    krot = _rope(kpe, cc, sa, sb)
    qs, ks = [], []
    for h in range(MLA_HEADS):
        o = h * MLA_HEAD_PAD
        qs += [q[:, o:o + 128], _rope(q[:, o + 128:o + 256], cc, sa, sb)]
        ks += [kv[:, h * 128:(h + 1) * 128], krot]
    return jnp.concatenate(qs, axis=1), jnp.concatenate(ks, axis=1), kv[:, MLA_HEADS * 128:]


def _post_fn(ga, sgu, sgv, gb, gc, mq, gm, oa, oc, mkv, lng, lnb, sgw, biasf):
    tm = ga.shape[0]
    u = _gelu(sgu)
    vn = _ln(_gelu(sgv), lng, lnb)
    t_chunk = lax.broadcasted_iota(jnp.int32, (SG_CHUNK, SG_CHUNK), 0) // CHUNK
    s_chunk = lax.broadcasted_iota(jnp.int32, (SG_CHUNK, SG_CHUNK), 1) // CHUNK
    sg_mask = (s_chunk <= t_chunk).astype(F32)
    cols = []
    for g in range(SG_GROUPS):
        wg = sgw[g] * sg_mask
        vg = vn[:, g * SG_CH:(g + 1) * SG_CH]
        cols.append(jnp.concatenate(
            [_mm(wg, vg[n * SG_CHUNK:(n + 1) * SG_CHUNK, :]) for n in range(tm // SG_CHUNK)], axis=0))
    mixed = jnp.concatenate(cols, axis=1) + jnp.concatenate([biasf] * (tm // SG_CHUNK), axis=0)
    ob = u * mixed
    mk, mv = mkv[:, :MEM_WIDTH], mkv[:, MEM_WIDTH:]
    head = lax.broadcasted_iota(jnp.int32, (1, MEM_WIDTH), 1) // MEM_DIM
    om = jnp.zeros((tm, MEM_WIDTH), F32)
    for hd in range(MEM_HEADS):
        hm = (head == hd).astype(F32)
        s = _mm_nt(mq, mk * hm) * MEM_SCALE
        e = jnp.exp(s - lax.stop_gradient(jnp.max(s, axis=-1, keepdims=True)))
        p = e / jnp.sum(e, axis=-1, keepdims=True)
        om = om + _mm(p, mv * hm)
    return jnp.concatenate([oa * _silu(ga), ob * _silu(gb), oc * _silu(gc), om * _silu(gm)], axis=1)


def _post_bwd_fn(dyg, ga, sgu, sgv, gb, gc, mq, gm, oa, oc, mkv, lng, lnb, sgw, biasf):
    _, vjp = jax.vjp(_post_fn, ga, sgu, sgv, gb, gc, mq, gm, oa, oc, mkv, lng, lnb, sgw, biasf)
    dga, dsgu, dsgv, dgb, dgc, dmq, dgm, doa, doc, dmkv, dlng, dlnb, dsgw, dbias = vjp(dyg)
    drows = jnp.stack([_col2row(jnp.sum((doa * oa)[:, h * 128:(h + 1) * 128], axis=1, keepdims=True))
                       for h in range(MLA_HEADS)], axis=0)
    return dga, dsgu, dsgv, dgb, dgc, dmq, dgm, doa, doc, drows, dmkv, dlng, dlnb, dsgw, dbias


def _prep_bwd_fn(dq, dk, dv, cq, ckv, kpe, cc, sa, sb, gq, wuq, gkv, wukv):
    wuq, wukv = wuq.astype(F32), wukv.astype(F32)
    _, vjp = jax.vjp(lambda a, b, c, d, e, f, g: _prep_fn(a, b, c, cc, sa, sb, d, e, f, g), cq, ckv, kpe, gq, wuq, gkv, wukv)
    return vjp((dq, dk, dv))


def _ln_bwd_fn(dxn, z, g, b):
    _, vjp = jax.vjp(_ln, z, g, b)
    dz, dg, db = vjp(dxn)
    return dz, dz, dg, db


def _outproj_fn(yg, x, w, g, b):
    z = ALPHA * x + _dot(yg, w, NN)
    xn = _ln(z, g, b)
    return z, xn, xn


def _loss_fn(xl, tgt):
    err = xl - tgt
    part = 0.5 * jnp.sum(jnp.mean(err * err, axis=-1, keepdims=True), axis=0, keepdims=True)
    return err * (1.0 / D_MODEL), jnp.broadcast_to(part, (1, LANES))


def _mla_fwd(q, k, v, tq, comm=None):
    t = q.shape[0]
    tq = min(tq, t)
    nq = t // tq

    def body(q_ref, k_ref, v_ref, o_ref, lse_ref):
        i = pl.program_id(1)
        qb = q_ref[...]
        r_chunk = lax.broadcasted_iota(jnp.int32, (tq, tq), 0) // CHUNK
        c_chunk = lax.broadcasted_iota(jnp.int32, (tq, tq), 1) // CHUNK
        diag_ok = c_chunk <= r_chunk

        def block(j, carry, masked):
            m, l, acc = carry
            off = pl.multiple_of(j * tq, tq)
            s = _dot(qb, k_ref[pl.ds(off, tq), :], NT)
            if masked:
                s = jnp.where(diag_ok, s, NEG)
            m2 = jnp.maximum(m, jnp.max(s, axis=1, keepdims=True))
            p = jnp.exp2(s - m2)
            a = jnp.exp2(m - m2)
            return m2, a * l + jnp.sum(p, axis=1, keepdims=True), a * acc + _dot(p, v_ref[pl.ds(off, tq), :], NN)

        init = (jnp.full((tq, 1), NEG, F32), jnp.zeros((tq, 1), F32), jnp.zeros((tq, MLA_V), F32))
        carry = lax.fori_loop(0, i, lambda j, c: block(j, c, False), init)
        m, l, acc = block(i, carry, True)
        o_ref[...] = acc / l
        lse_ref[...] = _col2row(m + jnp.log2(l))

    (o, lse), received = _comm_call(
        body, "mla_fwd", (MLA_HEADS, nq),
        [pl.BlockSpec((tq, MLA_HEAD_PAD), lambda h, i: (i, h)),
         pl.BlockSpec((t, MLA_HEAD_PAD), lambda h, i: (0, h)),
         pl.BlockSpec((t, MLA_V), lambda h, i: (0, h))],
        [pl.BlockSpec((tq, MLA_V), lambda h, i: (i, h)),
         pl.BlockSpec((None, 8, tq), lambda h, i: (h, 0, i))],
        [jax.ShapeDtypeStruct((t, MLA_WIDTH), F32), jax.ShapeDtypeStruct((MLA_HEADS, 8, t), F32)],
        [q, k, v], ("parallel", "parallel"), comm)
    return o, lse, received


def _mla_bwd(q, k, v, do, lse, drow, tq, comm=None):
    t = q.shape[0]
    tq = min(tq, t)
    nq = t // tq

    def body(k_ref, v_ref, q_ref, do_ref, lse_ref, d_ref, dk_ref, dv_ref, dq_ref):
        j = pl.program_id(1)

        @pl.when(j == 0)
        def _():
            dq_ref[...] = jnp.zeros_like(dq_ref)

        kj, vj = k_ref[...], v_ref[...]
        k_chunk = lax.broadcasted_iota(jnp.int32, (tq, tq), 0) // CHUNK
        q_chunk = lax.broadcasted_iota(jnp.int32, (tq, tq), 1) // CHUNK
        diag_ok = k_chunk <= q_chunk

        def block(i, carry, masked):
            dk, dv = carry
            off = pl.multiple_of(i * tq, tq)
            qi, doi = q_ref[pl.ds(off, tq), :], do_ref[pl.ds(off, tq), :]
            pt = jnp.exp2(_dot(kj, qi, NT) - lse_ref[0:1, pl.ds(off, tq)])
            if masked:
                pt = jnp.where(diag_ok, pt, 0.0)
            dv = dv + _dot(pt, doi, NN)
            dst = (pt * (_dot(vj, doi, NT) - d_ref[0:1, pl.ds(off, tq)])).astype(MXU)
            dk = dk + _dot(dst, qi, NN)
            dq_ref[pl.ds(off, tq), :] += _dot(dst, kj, TN) * LN2
            return dk, dv

        carry = block(j, (jnp.zeros((tq, MLA_HEAD_PAD), F32), jnp.zeros((tq, MLA_V), F32)), True)
        dk, dv = lax.fori_loop(j + 1, nq, lambda i, c: block(i, c, False), carry)
        dk_ref[...] = dk * LN2
        dv_ref[...] = dv

    (dk, dv, dq), received = _comm_call(
        body, "mla_bwd", (MLA_HEADS, nq),
        [pl.BlockSpec((tq, MLA_HEAD_PAD), lambda h, j: (j, h)),
         pl.BlockSpec((tq, MLA_V), lambda h, j: (j, h)),
         pl.BlockSpec((t, MLA_HEAD_PAD), lambda h, j: (0, h)),
         pl.BlockSpec((t, MLA_V), lambda h, j: (0, h)),
         pl.BlockSpec((None, 8, t), lambda h, j: (h, 0, 0)),
         pl.BlockSpec((None, 8, t), lambda h, j: (h, 0, 0))],
        [pl.BlockSpec((tq, MLA_HEAD_PAD), lambda h, j: (j, h)),
         pl.BlockSpec((tq, MLA_V), lambda h, j: (j, h)),
         pl.BlockSpec((t, MLA_HEAD_PAD), lambda h, j: (0, h))],
        [jax.ShapeDtypeStruct((t, MLA_HEADS * MLA_HEAD_PAD), F32), jax.ShapeDtypeStruct((t, MLA_WIDTH), F32),
         jax.ShapeDtypeStruct((t, MLA_HEADS * MLA_HEAD_PAD), F32)],
        [k, v, q, do, lse, drow], ("parallel", "arbitrary"), comm)
    return dq, dk, dv, received


def _sb_logs(z2):
    lb = jnp.minimum(z2, 0.0) - jnp.log2(1.0 + jnp.exp2(jnp.minimum(z2, -z2)))
    return lb, lb - z2


def _tri(n, keep):
    r = lax.broadcasted_iota(jnp.int32, (n, n), 0)
    c = lax.broadcasted_iota(jnp.int32, (n, n), 1)
    return keep(r, c)


def _sb_fwd(hb, tq):
    t = hb.shape[0]
    tq = min(tq, t)
    nq, tk, hq = t // tq, min(SCAN, tq), tq // 2
    cq, ck, cv = C_SBQ // SB_DIM, C_SBK // SB_DIM, C_SBV // SB_DIM

    def body(q_ref, k_ref, v_ref, o_ref, tot_ref, visited_ref):
        h, i = pl.program_id(0), pl.program_id(1)
        qb = q_ref[...]
        strict = _tri(tq, lambda r, c: c < r)
        later = _tri(tk, lambda r, c: r > c).astype(MXU)

        def block(qv, off, n, carry, masked):
            run, acc = carry
            lb, l1 = _sb_logs(_dot(qv, k_ref[pl.ds(off, n), :], NT))
            if masked:
                l1 = jnp.where(strict, l1, 0.0)
            after, rowsum = _scan_keys(l1, later, True)
            a = jnp.exp2(lb + after + run)
            if masked:
                a = jnp.where(strict, a, 0.0)
            return run + rowsum, acc + _dot(a, v_ref[pl.ds(off, n), :], NN)

        diag = pl.multiple_of(i * tq, tq)
        carry = block(qb, diag, tq, (jnp.zeros((tq, 1), F32), jnp.zeros((tq, SB_DIM), F32)), True)

        def back(qv, first_row):
            def live(state):
                jj, (run, _) = state
                return jnp.logical_and(jj < i * (tq // tk), jnp.max(run[first_row:]) > -SB_CUTOFF)

            def step(state):
                jj, c = state
                return jj + 1, block(qv, pl.multiple_of(diag - (jj + 1) * tk, tk), tk, c, False)

            return live, step

        n_all, carry = lax.while_loop(*back(qb, hq), (jnp.int32(0), carry))
        n_any, upper = lax.while_loop(*back(qb[:hq], 0), (n_all, tuple(c[:hq] for c in carry)))
        run, acc = (jnp.concatenate([u, c[hq:]], axis=0) for u, c in zip(upper, carry))
        o_ref[...] = acc
        tot_ref[...] = _col2row(run)
        visited_ref[h, i] = n_all
        visited_ref[SB_HEADS + h, i] = n_any

    return _comm_call(
        body, "sb_fwd", (SB_HEADS, nq),
        [pl.BlockSpec((tq, SB_DIM), lambda h, i: (i, cq + h)),
         pl.BlockSpec((t, SB_DIM), lambda h, i: (0, ck + h)),
         pl.BlockSpec((t, SB_DIM), lambda h, i: (0, cv + h))],
        [pl.BlockSpec((tq, SB_DIM), lambda h, i: (i, h)),
         pl.BlockSpec((None, 8, tq), lambda h, i: (h, 0, i)),
         pl.BlockSpec(memory_space=pltpu.SMEM)],
        [jax.ShapeDtypeStruct((t, SB_WIDTH), F32), jax.ShapeDtypeStruct((SB_HEADS, 8, t), F32),
         jax.ShapeDtypeStruct((2 * SB_HEADS, nq), jnp.int32)],
        [hb, hb, hb], ("arbitrary", "arbitrary"), None)[0]


def _sb_bwd(hb, do, tot, visited, tq):
    t = hb.shape[0]
    tq = min(tq, t)
    nq, tk, hq = t // tq, min(SCAN, tq), tq // 2
    cq, ck, cv = C_SBQ // SB_DIM, C_SBK // SB_DIM, C_SBV // SB_DIM

    def body(q_ref, do_ref, tot_ref, k_ref, v_ref, visited_ref, dq_ref, dk_ref, dv_ref):
        h, i = pl.program_id(0), pl.program_id(1)

        @pl.when(i == 0)
        def _():
            dk_ref[...] = jnp.zeros_like(dk_ref)
            dv_ref[...] = jnp.zeros_like(dv_ref)

        qb, dob = q_ref[...], do_ref[...]
        total = _row2col(tot_ref[0:1, :])
        strict = _tri(tq, lambda r, c: c < r)
        upto = _tri(tk, lambda r, c: r <= c).astype(MXU)
        before = _tri(tk, lambda r, c: r < c).astype(MXU)

        def block(rows, off, n, carry, masked):
            pre, gpre, dq = carry
            qv, dov, tot_v = qb[:rows], dob[:rows], total[:rows]
            kj, vj = k_ref[pl.ds(off, n), :], v_ref[pl.ds(off, n), :]
            lb, l1 = _sb_logs(_dot(qv, kj, NT))
            if masked:
                l1 = jnp.where(strict, l1, 0.0)
            upto_sum, rowsum = _scan_keys(l1, upto, False)
            a = jnp.exp2(lb + (tot_v - (upto_sum + pre)))
            if masked:
                a = jnp.where(strict, a, 0.0)
            g = a * _dot(dov, vj, NT)
            before_sum, grow = _scan_keys(g, before, False)
            beta = jnp.exp2(lb)
            dz = g * (1.0 - beta) - (before_sum + gpre) * beta
            if masked:
                dz = jnp.where(strict, dz, 0.0)
            dzb = dz.astype(MXU)
            dk_ref[pl.ds(off, n), :] += _dot(dzb, qv, TN) * LN2
            dv_ref[pl.ds(off, n), :] += _dot(a, dov, TN)
            return pre + rowsum, gpre + grow, dq + _dot(dzb, kj, NN)

        diag = pl.multiple_of(i * tq, tq)
        n_all, n_any = visited_ref[h, i], visited_ref[SB_HEADS + h, i]
        start = diag - n_any * tk

        def earlier(rows, first, count, carry):
            return lax.fori_loop(
                0, count, lambda jj, c: block(rows, pl.multiple_of(start + (first + jj) * tk, tk), tk, c, False), carry)

        def zeros(rows):
            return jnp.zeros((rows, 1), F32), jnp.zeros((rows, 1), F32), jnp.zeros((rows, SB_DIM), F32)

        upper = earlier(hq, 0, n_any - n_all, zeros(hq))
        carry = tuple(jnp.concatenate([u, z], axis=0) for u, z in zip(upper, zeros(tq - hq)))
        carry = earlier(tq, n_any - n_all, n_all, carry)
        dq_ref[...] = (block(tq, diag, tq, carry, True)[2] * SB_SCALE).astype(dq_ref.dtype)

    return _comm_call(
        body, "sb_bwd", (SB_HEADS, nq),
        [pl.BlockSpec((tq, SB_DIM), lambda h, i: (i, cq + h)),
         pl.BlockSpec((tq, SB_DIM), lambda h, i: (i, h)),
         pl.BlockSpec((None, 8, tq), lambda h, i: (h, 0, i)),
         pl.BlockSpec((t, SB_DIM), lambda h, i: (0, ck + h)),
         pl.BlockSpec((t, SB_DIM), lambda h, i: (0, cv + h)),
         pl.BlockSpec(memory_space=pltpu.SMEM)],
        [pl.BlockSpec((tq, SB_DIM), lambda h, i: (i, h)),
         pl.BlockSpec((t, SB_DIM), lambda h, i: (0, h)),
         pl.BlockSpec((t, SB_DIM), lambda h, i: (0, h))],
        [jax.ShapeDtypeStruct((t, SB_WIDTH), MXU), jax.ShapeDtypeStruct((t, SB_WIDTH), F32),
         jax.ShapeDtypeStruct((t, SB_WIDTH), F32)],
        [hb, do, tot, hb, hb, visited], ("parallel", "arbitrary"), None)[0]


TM = 256
TQ_MLA_FWD = 2048
TQ_MLA_BWD = 1024
TQ_SB = 512


def _layer_fwd(xf, xb, w, consts, comm=None, rest=None):
    t = xf.shape[0]
    cc, sa, sb, mem = consts
    col_scale = jnp.ones((1, D_IN_PAD), F32).at[:, C_SBQ:C_SBQ + SB_WIDTH].set(SB_SCALE * LOG2E)
    (h, hb), arrived = _matmul("in_proj", xb, w["w_in"], "nt", 512, 1920, D_MODEL, second_dtype=MXU, second_scale=col_scale,
                               comm=rest[0] if rest else None, with_received=True)
    if rest:
        w = dict(w, **rest[1](arrived))
    gq, gkv = w["q_norm_g"][None, :], w["kv_norm_g"][None, :]
    q, k, v = _rowcall(
        "mla_prep", _prep_fn, t, TM,
        [(h, 512, C_CQ // 512), (h, 256, C_CKV // 256), (h, 128, C_KPE // 128), (cc, 128, 0), (sa, 128, 0), (sb, 128, 0)],
        [gq, w["w_uq"], gkv, w["w_ukv"]],
        [_ro(t, MLA_HEADS * MLA_HEAD_PAD, MXU, TM), _ro(t, MLA_HEADS * MLA_HEAD_PAD, MXU, TM), _ro(t, MLA_WIDTH, MXU, TM)])
    oa, lse, received = _mla_fwd(q, k, v, TQ_MLA_FWD, comm)
    oc, tot, visited = _sb_fwd(hb, TQ_SB)
    mkv = _matmul("mem_kv", mem, w["w_mem_kv"], "nn", 256, 512, 2048)
    lng, lnb = w["sg_ln_g"][None, :], w["sg_ln_b"][None, :]
    biasf = jnp.repeat(w["sg_b"].T, SG_CH, axis=1)
    post_rows = [(h, 768, C_GA // 768), (h, 512, C_SGU // 512), (h, 512, C_SGV // 512), (h, 512, C_GB // 512),
                 (h, 512, C_GC // 512), (h, 256, C_MQ // 256), (h, 256, C_GM // 256), (oa, 768, 0), (oc, 512, 0)]
    post_consts = [mkv, lng, lnb, w["sg_w"], biasf]
    (yg,) = _rowcall("mixer_post", lambda *a: (_post_fn(*a),), t, TM, post_rows, post_consts, [_ro(t, D_MODEL, MXU, TM)])
    z, xn, xnb = _rowcall(
        "out_proj_ln", _outproj_fn, t, TM, [(yg, D_MODEL, 0), (xf, D_MODEL, 0)],
        [w["w_out"], w["ln_g"][None, :], w["ln_b"][None, :]],
        [_ro(t, D_MODEL, F32, TM), _ro(t, D_MODEL, F32, TM), _ro(t, D_MODEL, MXU, TM)])
    saved = dict(w=w, xb=xb, h=h, hb=hb, q=q, k=k, v=v, oa=oa, lse=lse, oc=oc, tot=tot, visited=visited, mkv=mkv, yg=yg, z=z,
                 post_rows=post_rows, post_consts=post_consts, gq=gq, gkv=gkv)
    return xn, xnb, saved, received


def _layer_bwd(dxn, w, s, consts, recv=None, early_items=None, late_items=None):
    t = dxn.shape[0]
    cc, sa, sb, mem = consts
    h = s["h"]
    dz, dzb, dlng2, dlnb2 = _rowcall(
        "ln_bwd", _ln_bwd_fn, t, TM, [(dxn, D_MODEL, 0), (s["z"], D_MODEL, 0)],
        [w["ln_g"][None, :], w["ln_b"][None, :]],
        [_ro(t, D_MODEL, F32, TM), _ro(t, D_MODEL, MXU, TM)], [((1, D_MODEL), F32), ((1, D_MODEL), F32)])
    dyg = _matmul("out_proj_dx", dzb, w["w_out"], "nt", 512, 1024, 2048)
    dw_out = _matmul("out_proj_dw", s["yg"], dzb, "tn", 1024, 1024, 2048, out_dtype=GRAD_WIRE)
    tm = min(TM, t)
    outs = _rowcall(
        "mixer_post_bwd", _post_bwd_fn, t, TM, [(dyg, D_MODEL, 0)] + s["post_rows"], s["post_consts"],
        [_ro(t, 768, MXU, TM), _ro(t, 512, MXU, TM), _ro(t, 512, MXU, TM), _ro(t, 512, MXU, TM), _ro(t, 512, MXU, TM),
         _ro(t, 256, MXU, TM), _ro(t, 256, MXU, TM), _ro(t, 768, MXU, TM), _ro(t, 512, MXU, TM),
         ((MLA_HEADS, 8, t), F32, (MLA_HEADS, 8, tm), lambda i: (0, 0, i))],
        [((MEM_TOKENS, 2 * MEM_WIDTH), F32), ((1, SG_WIDTH), F32), ((1, SG_WIDTH), F32),
         ((SG_GROUPS, SG_CHUNK, SG_CHUNK), F32), ((SG_CHUNK, SG_WIDTH), F32)])
    dga, dsgu, dsgv, dgb, dgc, dmq, dgm, doa, doc, drow, dmkv, dsg_ln_g, dsg_ln_b, dsgw, dbias = outs
    dw_mem_kv = _matmul("mem_kv_dw", mem, dmkv, "tn", 1024, 512, 256, out_dtype=GRAD_WIRE)
    dsbq, dsbk, dsbv = _sb_bwd(s["hb"], doc, s["tot"], s["visited"], TQ_SB)
    named = early_items(dict(w_out=dw_out, w_mem_kv=dw_mem_kv)) if early_items else []
    dq, dk, dv, received = _mla_bwd(s["q"], s["k"], s["v"], doa, s["lse"], drow, TQ_MLA_BWD,
                                    _Comm([item for _, item in named]) if named else None)
    if named:
        recv.update({n: r for (n, _), r in zip(named, received)})
    wq, wkv = MLA_HEADS * MLA_HEAD_PAD, MLA_WIDTH
    dcq, dckv, dkpe, dgq, dwuq, dgkv, dwukv = _rowcall(
        "mla_prep_bwd", _prep_bwd_fn, t, TM,
        [(dq, wq, 0), (dk, wq, 0), (dv, wkv, 0), (h, 512, C_CQ // 512), (h, 256, C_CKV // 256), (h, 128, C_KPE // 128),
         (cc, 128, 0), (sa, 128, 0), (sb, 128, 0)],
        [s["gq"], w["w_uq"], s["gkv"], w["w_ukv"]],
        [_ro(t, 512, MXU, TM), _ro(t, 256, MXU, TM), _ro(t, 128, MXU, TM)],
        [((1, MLA_Q_RANK), F32), ((wq, MLA_Q_RANK), F32), ((1, MLA_KV_RANK), F32), ((2 * wkv, MLA_KV_RANK), F32)])
    dh = jnp.concatenate([dcq, dckv, dga, dsgu, dsgv, dgb, dsbq, dsbk.astype(MXU), dsbv.astype(MXU), dgc, dmq, dgm, dkpe], axis=1)
    dw_in = _matmul("in_proj_dw", dh, s["xb"], "tn", 1152, 1024, 2048, out_dtype=GRAD_WIRE)
    replicated = dict(q_norm_g=dgq[0], kv_norm_g=dgkv[0], sg_ln_g=dsg_ln_g[0], sg_ln_b=dsg_ln_b[0], sg_w=dsgw, sg_bias=dbias,
                      ln_g=dlng2[0], ln_b=dlnb2[0])
    named = late_items(dict(w_in=dw_in, w_uq=dwuq, w_ukv=dwukv), replicated) if late_items else []
    dx, received = _matmul("in_proj_dx", dh, w["w_in"], "nn", 512, 1024, D_IN_PAD, add=dz, add_scale=ALPHA,
                           comm=_Comm([item for _, item in named]) if named else None, with_received=True)
    if named:
        recv.update({n: r for (n, _), r in zip(named, received)})
    return dx, dict(replicated, w_in=dw_in, w_uq=dwuq, w_ukv=dwukv, w_mem_kv=dw_mem_kv, w_out=dw_out)


def _sg_bias_grad(bias_grads):
    db = jnp.concatenate(bias_grads, axis=0)

    def lane_sums(d):
        return (jnp.concatenate([jnp.broadcast_to(jnp.sum(d[:, g * SG_CH:(g + 1) * SG_CH], axis=1, keepdims=True),
                                                  (d.shape[0], SG_CH)) for g in range(SG_GROUPS)], axis=1),)

    (dbs,) = _rowcall("sg_bias_sum", lane_sums, db.shape[0], SG_CHUNK, [(db, SG_WIDTH, 0)], [], [_ro(db.shape[0], SG_WIDTH, F32, SG_CHUNK)])
    return dbs[:, ::SG_CH].reshape(len(bias_grads), SG_CHUNK, SG_GROUPS).transpose(0, 2, 1)


def _local_step(x, mem, positions, target, depth, weights_of, gather_comm, first_rest, grad_items, rep_item, recv):
    t = x.shape[0]
    inv_freq = ROPE_THETA ** (-jnp.arange(0, MLA_ROPE, 2, dtype=F32) / MLA_ROPE)
    invf = jnp.tile(inv_freq, 4)[None, :]
    cc, sa, sb = _rowcall("rope_tables", _rope_tables_fn, t, TM, [(positions[:, None], 1, 0)], [invf],
                          [_ro(t, 128, F32, TM)] * 3)
    consts = (cc, sa, sb, mem)
    xf, xb, saved, received = x, x.astype(MXU), [], None
    for l in range(depth):
        xf, xb, s, received = _layer_fwd(xf, xb, weights_of(l, received), consts, gather_comm(l + 1) if l + 1 < depth else None,
                                         first_rest if l == 0 else None)
        saved.append(s)
    dx, loss_part = _rowcall("loss", _loss_fn, t, TM, [(xf, D_MODEL, 0), (target, D_MODEL, 0)], [],
                             [_ro(t, D_MODEL, F32, TM)], [((1, LANES), F32)])
    grads, recv, sg_b_grad = [None] * depth, dict(recv or {}), []
    for l in reversed(range(depth)):
        def early(g, l=l):
            rest = {n: grads[l + 1][n] for n in ("w_in", "w_uq", "w_ukv")} if l + 1 < depth else {}
            return grad_items(l, g, recv) + (grad_items(l + 1, rest, recv) if rest else [])

        def late(g, replicated):
            per_layer = [replicated] + grads[1:]
            sg_b_grad.append(_sg_bias_grad([r["sg_bias"] for r in per_layer]))
            return grad_items(0, g, recv) + [("replicated", rep_item(per_layer, sg_b_grad[0]))]

        dx, grads[l] = _layer_bwd(dx, saved[l]["w"], saved[l], consts, recv, early if grad_items else None,
                                  late if grad_items and l == 0 else None)
    dsgb = sg_b_grad[0] if sg_b_grad else _sg_bias_grad([g["sg_bias"] for g in grads])
    return loss_part[0, 0], dx, grads, dsgb, recv


SHARD_IN = D_IN // N_DEV
SHARD_IN_PAD = D_IN_PAD // N_DEV
IN_RANGES = ((0, 768), (832, D_IN), (768, 832))


def _in_rows_from_shards(w):
    pieces = []
    for a, b in IN_RANGES:
        while a < b:
            e = min(b, (a // SHARD_IN + 1) * SHARD_IN)
            row = (a // SHARD_IN) * SHARD_IN_PAD + a % SHARD_IN
            pieces.append(w[row:row + e - a])
            a = e
    pieces.append(jnp.zeros((D_IN_PAD - D_IN, w.shape[1]), w.dtype))
    return jnp.concatenate(pieces, axis=0)


def _in_rows_to_shards(g):
    pieces = []
    for d in range(N_DEV):
        a, b = d * SHARD_IN, (d + 1) * SHARD_IN
        while a < b:
            e = min(b, 768 if a < 768 else (832 if a < 832 else D_IN))
            row = a if a < 768 else (C_KPE + a - 768 if a < 832 else a - 64)
            pieces.append(g[row:row + e - a])
            a = e
        pieces.append(jnp.zeros((SHARD_IN_PAD - SHARD_IN, g.shape[1]), g.dtype))
    return jnp.concatenate(pieces, axis=0)


def _uq_rows_pad(w):
    w = w.reshape(MLA_HEADS, MLA_NOPE + MLA_ROPE, w.shape[-1])
    return jnp.pad(w, ((0, 0), (0, MLA_HEAD_PAD - MLA_NOPE - MLA_ROPE), (0, 0))).reshape(MLA_HEADS * MLA_HEAD_PAD, -1)


def _uq_rows_unpad(g):
    return g.reshape(MLA_HEADS, MLA_HEAD_PAD, -1)[:, :MLA_NOPE + MLA_ROPE].reshape(MLA_HEADS * (MLA_NOPE + MLA_ROPE), -1)


def _ukv_rows_split(w):
    return jnp.swapaxes(w.reshape(MLA_HEADS, 2, 128, -1), 0, 1).reshape(MLA_HEADS * 256, -1)


def _ukv_rows_unsplit(g):
    return jnp.swapaxes(g.reshape(2, MLA_HEADS, 128, -1), 0, 1).reshape(MLA_HEADS * 256, -1)


class _Comm:
    def __init__(self, items):
        self.items = items
        self.n = len(items)
        self.pieces = [((s.shape[0] // N_DEV,) + s.shape[1:]) if sc else s.shape for s, sc, _ in items]
        self.inputs = [s for s, _, _ in items] + [d[0] for _, _, d in items if d is not None]
        any_space = pl.BlockSpec(memory_space=pl.ANY)
        self.in_specs = [any_space] * len(self.inputs)
        self.out_specs = [any_space] * self.n
        self.out_shape = [jax.ShapeDtypeStruct((N_DEV,) + tuple(p), s.dtype) if d is None
                          else jax.ShapeDtypeStruct(d[0].shape, d[0].dtype) for p, (s, _, d) in zip(self.pieces, items)]
        self.scratch = [pltpu.SemaphoreType.DMA((N_DEV - 1, self.n)), pltpu.SemaphoreType.DMA((N_DEV - 1, self.n)),
                        pltpu.SemaphoreType.DMA((self.n,))]

    def aliases(self, first_in, first_out):
        passed = [a for a, (_, _, d) in enumerate(self.items) if d is not None]
        return {first_in + self.n + b: first_out + a for b, a in enumerate(passed)}

    def copies(self, in_refs, out_refs, sems):
        send_sems, recv_sems, local_sems = sems
        x, y, c = lax.axis_index("x"), lax.axis_index("y"), lax.axis_index("c")
        me = 4 * x + 2 * y + c

        def block_for(a, p):
            if not self.items[a][1]:
                return in_refs[a]
            n = self.pieces[a][0]
            return in_refs[a].at[pl.ds(pl.multiple_of(p * n, SUBLANES_16BIT), n)]

        def slot(a, dev):
            dst = self.items[a][2]
            return out_refs[a].at[dev] if dst is None else out_refs[a].at[dev, dst[1]]

        starts, recv_waits, send_waits = [], [], []
        for a in range(self.n):
            cp = pltpu.make_async_copy(block_for(a, me), slot(a, me), local_sems.at[a])
            starts.append(cp.start)
            send_waits.append(cp.wait)
        for k in range(1, N_DEV):
            px = 1 - x if (k >> 2) & 1 else x
            py = 1 - y if (k >> 1) & 1 else y
            pc = 1 - c if k & 1 else c
            peer = 4 * px + 2 * py + pc
            for a in range(self.n):
                out = pltpu.make_async_remote_copy(
                    src_ref=block_for(a, peer), dst_ref=slot(a, me), send_sem=send_sems.at[k - 1, a],
                    recv_sem=recv_sems.at[k - 1, a], device_id=(px, py, pc), device_id_type=pl.DeviceIdType.MESH)
                back = pltpu.make_async_remote_copy(
                    src_ref=block_for(a, peer), dst_ref=slot(a, peer), send_sem=send_sems.at[k - 1, a],
                    recv_sem=recv_sems.at[k - 1, a], device_id=(px, py, pc), device_id_type=pl.DeviceIdType.MESH)
                starts.append(out.start)
                send_waits.append(out.wait_send)
                recv_waits.append(back.wait_recv)
        return starts, recv_waits + send_waits


def _comm_call(body, name, grid, in_specs, out_specs, out_shape, operands, semantics, comm, scratch=()):
    n_in, n_out, n_scr = len(operands), len(out_shape), len(scratch)
    if comm is None:
        outs = pl.pallas_call(body, name=name, grid=grid, in_specs=in_specs, out_specs=out_specs, out_shape=out_shape,
                              scratch_shapes=list(scratch), compiler_params=_params(semantics))(*operands)
        return outs, []
    n_ci = len(comm.inputs)

    def fused(*refs):
        ins, c_in = refs[:n_in], refs[n_in:n_in + n_ci]
        outs = refs[n_in + n_ci:n_in + n_ci + n_out]
        c_out = refs[n_in + n_ci + n_out:n_in + n_ci + n_out + comm.n]
        first_scr = n_in + n_ci + n_out + comm.n
        outs = outs + refs[first_scr:first_scr + n_scr]
        starts, waits = comm.copies(c_in, c_out, refs[first_scr + n_scr:])
        first = functools.reduce(jnp.logical_and, [pl.program_id(d) == 0 for d in range(len(grid))])
        last = functools.reduce(jnp.logical_and, [pl.program_id(d) == g - 1 for d, g in enumerate(grid)])

        @pl.when(first)
        def _():
            for start in starts:
                start()

        body(*ins, *outs)

        @pl.when(last)
        def _():
            for wait in waits:
                wait()

    outs = pl.pallas_call(
        fused, name=name, grid=grid, in_specs=list(in_specs) + comm.in_specs, out_specs=list(out_specs) + comm.out_specs,
        out_shape=list(out_shape) + comm.out_shape, scratch_shapes=list(scratch) + comm.scratch,
        input_output_aliases=comm.aliases(n_in, n_out), compiler_params=_params(("arbitrary",) * len(grid)),
    )(*operands, *comm.inputs)
    return outs[:n_out], outs[n_out:]


def _exchange(name, items):
    return _comm_call(lambda: None, name, (1,), [], [], [], [], ("arbitrary",), _Comm(items))[1]


BLOCK_BYTES = 6 << 20


SUBLANES_16BIT = 16


def _row_block(rows, row_bytes):
    step = SUBLANES_16BIT
    if rows * row_bytes <= BLOCK_BYTES or rows % step:
        return rows
    return max(tb for tb in range(step, rows + 1, step) if rows % tb == 0 and (tb == step or tb * row_bytes <= BLOCK_BYTES))


def _sum_contribs(recv):
    n, rows, width = recv.shape
    tb = _row_block(rows, width * 4 * (n + 1))

    def body(r_ref, g_ref):
        g = r_ref[0].astype(F32)
        for k in range(1, n):
            g = g + r_ref[k].astype(F32)
        g_ref[...] = g

    return pl.pallas_call(
        body, name="sum_grads", grid=(rows // tb,), in_specs=[pl.BlockSpec((n, tb, width), lambda i: (0, i, 0))],
        out_specs=pl.BlockSpec((tb, width), lambda i: (i, 0)), out_shape=jax.ShapeDtypeStruct((rows, width), F32),
        compiler_params=_params(("parallel",)),
    )(recv)


def _adamw(contribs, w, m, v):
    n, rows, width = contribs.shape
    tb = _row_block(rows, width * 4 * (n + 7))
    c1 = 1.0 - ADAM_B1 ** ADAM_STEP
    c2 = 1.0 - ADAM_B2 ** ADAM_STEP

    def body(r_ref, w_ref, m_ref, v_ref, g_ref, d_ref, nm_ref, nv_ref):
        g = r_ref[0].astype(F32)
        for k in range(1, n):
            g = g + r_ref[k].astype(F32)
        nm = ADAM_B1 * m_ref[...] + (1.0 - ADAM_B1) * g
        nv = ADAM_B2 * v_ref[...] + (1.0 - ADAM_B2) * (g * g)
        g_ref[...] = g
        nm_ref[...] = nm
        nv_ref[...] = nv
        d_ref[...] = -ADAM_LR * ((nm / c1) / (jnp.sqrt(nv / c2) + ADAM_EPS) + ADAM_WD * w_ref[...])

    blk = pl.BlockSpec((tb, width), lambda i: (i, 0))
    return pl.pallas_call(
        body, name="adamw", grid=(rows // tb,),
        in_specs=[pl.BlockSpec((n, tb, width), lambda i: (0, i, 0)), blk, blk, blk],
        out_specs=[blk] * 4, out_shape=[jax.ShapeDtypeStruct((rows, width), F32)] * 4,
        compiler_params=_params(("parallel",)),
    )(contribs, w, m, v)


def kernel(x, mem, positions, w_in, q_norm_g, w_uq, kv_norm_g, w_ukv, sg_ln_g, sg_ln_b, sg_w, sg_b, w_mem_k, w_mem_v, w_out, ln_g, ln_b, loss_target, m_w_in, m_q_norm_g, m_w_uq, m_kv_norm_g, m_w_ukv, m_sg_ln_g, m_sg_ln_b, m_sg_w, m_sg_b, m_w_mem_k, m_w_mem_v, m_w_out, m_ln_g, m_ln_b, v_w_in, v_q_norm_g, v_w_uq, v_kv_norm_g, v_w_ukv, v_sg_ln_g, v_sg_ln_b, v_sg_w, v_sg_b, v_w_mem_k, v_w_mem_v, v_w_out, v_ln_g, v_ln_b):
    wl = dict(w_in=w_in, q_norm_g=q_norm_g, w_uq=w_uq, kv_norm_g=kv_norm_g, w_ukv=w_ukv, sg_ln_g=sg_ln_g, sg_ln_b=sg_ln_b,
              sg_w=sg_w, sg_b=sg_b, w_mem_k=w_mem_k, w_mem_v=w_mem_v, w_out=w_out, ln_g=ln_g, ln_b=ln_b)
    ml = dict(w_in=m_w_in, q_norm_g=m_q_norm_g, w_uq=m_w_uq, kv_norm_g=m_kv_norm_g, w_ukv=m_w_ukv, sg_ln_g=m_sg_ln_g,
              sg_ln_b=m_sg_ln_b, sg_w=m_sg_w, sg_b=m_sg_b, w_mem_k=m_w_mem_k, w_mem_v=m_w_mem_v, w_out=m_w_out, ln_g=m_ln_g, ln_b=m_ln_b)
    vl = dict(w_in=v_w_in, q_norm_g=v_q_norm_g, w_uq=v_w_uq, kv_norm_g=v_kv_norm_g, w_ukv=v_w_ukv, sg_ln_g=v_sg_ln_g,
              sg_ln_b=v_sg_ln_b, sg_w=v_sg_w, sg_b=v_sg_b, w_mem_k=v_w_mem_k, w_mem_v=v_w_mem_v, w_out=v_w_out, ln_g=v_ln_g, ln_b=v_ln_b)
    depth = w_in.shape[0]

    w_in_t = jnp.pad(jnp.swapaxes(w_in, 1, 2), ((0, 0), (0, SHARD_IN_PAD - SHARD_IN), (0, 0)))
    shards = [s.astype(MXU) for s in (w_in_t, jnp.swapaxes(w_uq, 1, 2), jnp.swapaxes(w_ukv, 1, 2),
                                      jnp.concatenate([w_mem_k, w_mem_v], axis=-1), w_out)]

    def gather_items(l):
        return [(s[l], False, None) for s in shards]

    (first_in,) = _exchange("gather_weights", gather_items(0)[:1])

    def in_rows(g_in):
        return _in_rows_from_shards(g_in.reshape(D_IN_PAD, D_MODEL))

    def other_weights(g_uq, g_ukv, g_mem, g_out):
        return dict(w_uq=_uq_rows_pad(g_uq.reshape(-1, MLA_Q_RANK)), w_ukv=_ukv_rows_split(g_ukv.reshape(-1, MLA_KV_RANK)),
                    w_mem_kv=g_mem.reshape(D_MODEL, 2 * MEM_WIDTH), w_out=g_out.reshape(D_MODEL, D_MODEL))

    def weights_of(l, received):
        w = dict(q_norm_g=q_norm_g[l], kv_norm_g=kv_norm_g[l], sg_ln_g=sg_ln_g[l], sg_ln_b=sg_ln_b[l], sg_w=sg_w[l],
                 sg_b=sg_b[l], ln_g=ln_g[l], ln_b=ln_b[l])
        if l == 0:
            return dict(w, w_in=in_rows(first_in))
        return dict(w, w_in=in_rows(received[0]), **other_weights(*received[1:]))

    first_rest = (_Comm(gather_items(0)[1:]), lambda received: other_weights(*received))

    to_rows = dict(w_in=_in_rows_to_shards, w_uq=_uq_rows_unpad, w_ukv=_ukv_rows_unsplit)

    def grad_items(l, g, recv):
        return [(n, (to_rows.get(n, lambda a: a)(a).astype(GRAD_WIRE), True, (recv[n], l))) for n, a in g.items()]

    shard_rows = dict(w_in=(D_IN_PAD, D_MODEL), w_uq=(MLA_HEADS * (MLA_NOPE + MLA_ROPE), MLA_Q_RANK),
                      w_ukv=(MLA_HEADS * 256, MLA_KV_RANK), w_mem_kv=(D_MODEL, 2 * MEM_WIDTH), w_out=(D_MODEL, D_MODEL))
    recv = {n: lax.empty((N_DEV, depth, rows // N_DEV, width), GRAD_WIRE) for n, (rows, width) in shard_rows.items()}
    rep_sizes = [wl[n].size for n in REPLICATED]
    rep_rows = -(-sum(rep_sizes) // (SUBLANES_16BIT * LANES)) * SUBLANES_16BIT

    def packed(d):
        flat = jnp.concatenate([d[n].reshape(-1) for n in REPLICATED])
        return jnp.pad(flat, (0, rep_rows * LANES - flat.shape[0])).reshape(rep_rows, LANES)

    def rep_item(per_layer, dsgb):
        rep_g = dict(sg_b=dsgb, **{n: jnp.stack([g[n] for g in per_layer]) for n in REPLICATED if n != "sg_b"})
        return (packed(rep_g), False, None)

    loss_part, grad_x, grads, dsgb, recv = _local_step(
        x[0], mem[0], positions[0], loss_target[0], depth, weights_of, lambda l: _Comm(gather_items(l)), first_rest,
        grad_items, rep_item, recv)
    loss = lax.psum(loss_part, ("x", "y", "c"))
    r_in, r_uq, r_ukv, r_mem, r_out, r_rep = (recv[n] for n in ("w_in", "w_uq", "w_ukv", "w_mem_kv", "w_out", "replicated"))

    def rows2d(a):
        return a.reshape(-1, a.shape[-1])

    def update(name, contribs):
        outs = _adamw(contribs, rows2d(wl[name]), rows2d(ml[name]), rows2d(vl[name]))
        return [o.reshape(wl[name].shape) for o in outs]

    def from_transposed(recv, keep):
        s = _sum_contribs(recv.reshape(N_DEV, -1, recv.shape[-1])).reshape(recv.shape[1:])
        return rows2d(jnp.swapaxes(s[:, :keep], 1, 2))[None]

    res = dict(
        w_in=update("w_in", from_transposed(r_in, SHARD_IN)),
        w_uq=update("w_uq", from_transposed(r_uq, r_uq.shape[2])),
        w_ukv=update("w_ukv", from_transposed(r_ukv, r_ukv.shape[2])),
        w_mem_k=update("w_mem_k", r_mem[..., :MEM_WIDTH].reshape(N_DEV, -1, MEM_WIDTH)),
        w_mem_v=update("w_mem_v", r_mem[..., MEM_WIDTH:].reshape(N_DEV, -1, MEM_WIDTH)),
        w_out=update("w_out", r_out.reshape(N_DEV, -1, D_MODEL)))
    rep_outs = _adamw(r_rep, packed(wl), packed(ml), packed(vl))
    off = 0
    for n, size in zip(REPLICATED, rep_sizes):
        res[n] = [o.reshape(-1)[off:off + size].reshape(wl[n].shape) for o in rep_outs]
        off += size
    return (loss, grad_x[None], *[res[n][0] for n in WEIGHTS], *[res[n][1] for n in WEIGHTS],
            *[res[n][2] for n in WEIGHTS], *[res[n][3] for n in WEIGHTS])
```

```python
import functools
import math

import jax
import jax.numpy as jnp
from jax import lax
from jax.experimental import pallas as pl
from jax.experimental.pallas import tpu as pltpu

F32 = jnp.float32
MXU = jnp.bfloat16
GRAD_WIRE = jnp.bfloat16

DEPTH_FOR_NORM = 4
D_MODEL = 2048
CHUNK = 64
MLA_HEADS, MLA_NOPE, MLA_ROPE, MLA_V = 6, 128, 64, 128
MLA_Q_RANK, MLA_KV_RANK = 512, 256
MLA_WIDTH = MLA_HEADS * MLA_V
MLA_HEAD_PAD = 256
ROPE_THETA = 10000.0
SG_GROUPS, SG_CH, SG_CHUNK = 4, 128, 128
SG_WIDTH = SG_GROUPS * SG_CH
SB_HEADS, SB_DIM = 4, 128
SB_WIDTH = SB_HEADS * SB_DIM
MEM_TOKENS, MEM_HEADS, MEM_DIM = 256, 4, 64
MEM_WIDTH = MEM_HEADS * MEM_DIM
D_IN = 5696
D_IN_PAD = 5760
ALPHA = (2.0 * DEPTH_FOR_NORM) ** 0.25
LN_EPS = 1e-5
RMS_EPS = 1e-6
MLA_SCALE = 1.0 / math.sqrt(MLA_NOPE + MLA_ROPE)
SB_SCALE = 1.0 / math.sqrt(SB_DIM)
MEM_SCALE = 1.0 / math.sqrt(MEM_DIM)
LOG2E = math.log2(math.e)
LN2 = math.log(2.0)
ADAM_LR, ADAM_B1, ADAM_B2, ADAM_EPS, ADAM_WD, ADAM_STEP = 0.001, 0.9, 0.999, 1e-08, 0.01, 10
N_DEV = 8
LANES = 128
NEG = -1e30
SCAN = 256
SB_CUTOFF = 256.0
VMEM_LIMIT = 56 * 1024 * 1024

C_CQ, C_CKV, C_GA, C_SGU, C_SGV, C_GB = 0, 512, 768, 1536, 2048, 2560
C_SBQ, C_SBK, C_SBV, C_GC, C_MQ, C_GM, C_KPE = 3072, 3584, 4096, 4608, 5120, 5376, 5632

REPLICATED = ("q_norm_g", "kv_norm_g", "sg_ln_g", "sg_ln_b", "sg_w", "sg_b", "ln_g", "ln_b")
WEIGHTS = ("w_in", "q_norm_g", "w_uq", "kv_norm_g", "w_ukv", "sg_ln_g", "sg_ln_b", "sg_w", "sg_b",
           "w_mem_k", "w_mem_v", "w_out", "ln_g", "ln_b")


def _params(sem):
    return pltpu.CompilerParams(dimension_semantics=sem, vmem_limit_bytes=VMEM_LIMIT)


def _dot(a, b, dims):
    return lax.dot_general(a.astype(MXU), b.astype(MXU), (dims, ((), ())), preferred_element_type=F32)


NN, NT, TN = ((1,), (0,)), ((1,), (1,)), ((0,), (0,))


@jax.custom_vjp
def _mm(a, b):
    return _dot(a, b, NN)


def _mm_fwd(a, b):
    return _dot(a, b, NN), (a, b)


def _mm_bwd(res, g):
    a, b = res
    return _dot(g, b, NT), _dot(a, g, TN)


_mm.defvjp(_mm_fwd, _mm_bwd)


@jax.custom_vjp
def _mm_nt(a, b):
    return _dot(a, b, NT)


def _mm_nt_fwd(a, b):
    return _dot(a, b, NT), (a, b)


def _mm_nt_bwd(res, g):
    a, b = res
    return _dot(g, b, NN), _dot(g, a, TN)


_mm_nt.defvjp(_mm_nt_fwd, _mm_nt_bwd)


def _split_dot(x, tri):
    hi = x.astype(MXU)
    lo = (x - hi.astype(F32)).astype(MXU)
    return _dot(hi, tri, NN) + _dot(lo, tri, NN)


def _scan_keys(x, tri, reverse):
    w = tri.shape[0]
    nb = x.shape[1] // w
    outs, carry = [None] * nb, None
    for c in (reversed(range(nb)) if reverse else range(nb)):
        xc = x[:, c * w:(c + 1) * w]
        sc = _split_dot(xc, tri)
        outs[c] = sc if carry is None else sc + carry
        rs = jnp.sum(xc, axis=1, keepdims=True)
        carry = rs if carry is None else carry + rs
    return (outs[0] if nb == 1 else jnp.concatenate(outs, axis=1)), carry


def _col2row(col):
    return jnp.transpose(jnp.broadcast_to(col, (col.shape[0], LANES)))[0:8]


def _row2col(row):
    return jnp.transpose(jnp.broadcast_to(row, (LANES, row.shape[1])))[:, 0:1]


def _matmul(name, a, b, mode, tm, tn, tk, out_dtype=F32, add=None, add_scale=1.0, second_dtype=None,
            second_scale=None, comm=None, with_received=False):
    if mode == "nn":
        (m, k), n = a.shape, b.shape[1]
    elif mode == "nt":
        (m, k), n = a.shape, b.shape[0]
    else:
        (k, m), n = a.shape, b.shape[1]
    tm, tn, tk = min(tm, m), min(tn, n), min(tk, k)
    assert m % tm == 0 and n % tn == 0 and k % tk == 0, (name, m, n, k, tm, tn, tk)
    nk = k // tk
    dims = {"nn": NN, "nt": NT, "tn": TN}[mode]
    a_spec = (pl.BlockSpec((tk, tm), lambda j, i, kk: (kk, i)) if mode == "tn"
              else pl.BlockSpec((tm, tk), lambda j, i, kk: (i, kk)))
    b_spec = (pl.BlockSpec((tn, tk), lambda j, i, kk: (j, kk)) if mode == "nt"
              else pl.BlockSpec((tk, tn), lambda j, i, kk: (kk, j)))
    o_spec = pl.BlockSpec((tm, tn), lambda j, i, kk: (i, j))
    in_specs, operands = [a_spec, b_spec], [a, b]
    if add is not None:
        in_specs.append(o_spec)
        operands.append(add)
    if second_scale is not None:
        in_specs.append(pl.BlockSpec((1, tn), lambda j, i, kk: (0, j)))
        operands.append(second_scale)
    out_shape = [jax.ShapeDtypeStruct((m, n), out_dtype)]
    out_specs = [o_spec]
    if second_dtype is not None:
        out_shape.append(jax.ShapeDtypeStruct((m, n), second_dtype))
        out_specs.append(o_spec)

    def body(*refs):
        a_ref, b_ref = refs[0], refs[1]
        extra = list(refs[2:len(operands)])
        add_ref = extra.pop(0) if add is not None else None
        scale_ref = extra.pop(0) if second_scale is not None else None
        outs = refs[len(operands):-1]
        acc_ref = refs[-1]
        kk = pl.program_id(2)

        def finish(r):
            if add_ref is not None:
                r = r + add_scale * add_ref[...]
            outs[0][...] = r.astype(out_dtype)
            if second_dtype is not None:
                outs[1][...] = (r if scale_ref is None else r * scale_ref[...]).astype(second_dtype)

        if nk == 1:
            finish(_dot(a_ref[...], b_ref[...], dims))
            return

        @pl.when(kk == 0)
        def _():
            acc_ref[...] = jnp.zeros_like(acc_ref)

        acc_ref[...] += _dot(a_ref[...], b_ref[...], dims)

        @pl.when(kk == nk - 1)
        def _():
            finish(acc_ref[...])

    res, received = _comm_call(body, name, (n // tn, m // tm, nk), in_specs, out_specs, out_shape, operands,
                               ("parallel", "parallel", "arbitrary"), comm, scratch=[pltpu.VMEM((tm, tn) if nk > 1 else (8, LANES), F32)])
    res = res if second_dtype is not None else res[0]
    return (res, received) if with_received else res


def _accumulate(ref, val, first):
    @pl.when(first)
    def _():
        ref[...] = val

    @pl.when(jnp.logical_not(first))
    def _():
        ref[...] += val


def _rowcall(name, fn, rows, tm, row_in, const_in, row_out, acc_out=()):
    tm = min(tm, rows)
    assert rows % tm == 0
    n_in = len(row_in) + len(const_in)
    in_specs = [pl.BlockSpec((tm, w), lambda i, cb=cb: (i, cb)) for _, w, cb in row_in]
    in_specs += [pl.BlockSpec(c.shape, lambda i, nd=c.ndim: (0,) * nd) for c in const_in]
    out_specs = [pl.BlockSpec(blk, imap) for _, _, blk, imap in row_out]
    out_specs += [pl.BlockSpec(s, lambda i, nd=len(s): (0,) * nd) for s, _ in acc_out]
    out_shape = [jax.ShapeDtypeStruct(s, dt) for s, dt, _, _ in row_out]
    out_shape += [jax.ShapeDtypeStruct(s, dt) for s, dt in acc_out]

    def body(*refs):
        vals = fn(*[r[...] for r in refs[:n_in]])
        outs = refs[n_in:]
        first = pl.program_id(0) == 0
        for k, (_, dt, _, _) in enumerate(row_out):
            outs[k][...] = vals[k].astype(dt)
        for k in range(len(acc_out)):
            _accumulate(outs[len(row_out) + k], vals[len(row_out) + k], first)

    return pl.pallas_call(
        body, name=name, grid=(rows // tm,), in_specs=in_specs, out_specs=out_specs, out_shape=out_shape,
        compiler_params=_params(("arbitrary",) if acc_out else ("parallel",)),
    )(*[a for a, _, _ in row_in], *const_in)


def _ro(rows, width, dtype, tm):
    tm = min(tm, rows)
    return ((rows, width), dtype, (tm, width), lambda i: (i, 0))


def _rms(x, g):
    ms = jnp.mean(x * x, axis=-1, keepdims=True)
    return x * lax.rsqrt(ms + RMS_EPS) * g


def _ln(x, g, b):
    mu = jnp.mean(x, axis=-1, keepdims=True)
    xc = x - mu
    var = jnp.mean(xc * xc, axis=-1, keepdims=True)
    return xc * lax.rsqrt(var + LN_EPS) * g + b


def _gelu(x):
    return 0.5 * x * (1.0 + jnp.tanh(math.sqrt(2.0 / math.pi) * (x + 0.044715 * (x * x * x))))


def _silu(x):
    return x / (1.0 + jnp.exp(-x))


@jax.custom_vjp
def _rope(r, cc, sa, sb):
    return r * cc + pltpu.roll(r, 96, 1) * sa + pltpu.roll(r, 32, 1) * sb


def _rope_fwd(r, cc, sa, sb):
    return _rope(r, cc, sa, sb), (cc, sa, sb)


def _rope_bwd(res, d):
    cc, sa, sb = res
    return d * cc + pltpu.roll(d * sa, 32, 1) + pltpu.roll(d * sb, 96, 1), None, None, None


_rope.defvjp(_rope_fwd, _rope_bwd)


def _rope_tables_fn(pos, invf):
    ang = pos.astype(F32) * invf
    lane = lax.broadcasted_iota(jnp.int32, ang.shape, 1)
    cos, sin = jnp.cos(ang), jnp.sin(ang)
    zero = jnp.zeros_like(ang)
    cc = jnp.where(lane < 64, cos, zero)
    sa = jnp.where(lane < 32, -sin, zero)
    sb = jnp.where((lane >= 32) & (lane < 64), sin, zero)
    return cc, sa, sb


def _prep_fn(cq, ckv, kpe, cc, sa, sb, gq, wuq, gkv, wukv):
    q = _mm_nt(_rms(cq, gq), wuq) * (MLA_SCALE * LOG2E)
    kv = _mm_nt(_rms(ckv, gkv), wukv)
    krot = _rope(kpe, cc, sa, sb)
    qs, ks = [], []
    for h in range(MLA_HEADS):
        o = h * MLA_HEAD_PAD
        qs += [q[:, o:o + 128], _rope(q[:, o + 128:o + 256], cc, sa, sb)]
        ks += [kv[:, h * 128:(h + 1) * 128], krot]
    return jnp.concatenate(qs, axis=1), jnp.concatenate(ks, axis=1), kv[:, MLA_HEADS * 128:]


def _post_fn(ga, sgu, sgv, gb, gc, mq, gm, oa, oc, mkv, lng, lnb, sgw, biasf):
    tm = ga.shape[0]
    u = _gelu(sgu)
    vn = _ln(_gelu(sgv), lng, lnb)
    t_chunk = lax.broadcasted_iota(jnp.int32, (SG_CHUNK, SG_CHUNK), 0) // CHUNK
    s_chunk = lax.broadcasted_iota(jnp.int32, (SG_CHUNK, SG_CHUNK), 1) // CHUNK
    sg_mask = (s_chunk <= t_chunk).astype(F32)
    cols = []
    for g in range(SG_GROUPS):
        wg = sgw[g] * sg_mask
        vg = vn[:, g * SG_CH:(g + 1) * SG_CH]
        cols.append(jnp.concatenate(
            [_mm(wg, vg[n * SG_CHUNK:(n + 1) * SG_CHUNK, :]) for n in range(tm // SG_CHUNK)], axis=0))
    mixed = jnp.concatenate(cols, axis=1) + jnp.concatenate([biasf] * (tm // SG_CHUNK), axis=0)
    ob = u * mixed
    mk, mv = mkv[:, :MEM_WIDTH], mkv[:, MEM_WIDTH:]
    head = lax.broadcasted_iota(jnp.int32, (1, MEM_WIDTH), 1) // MEM_DIM
    om = jnp.zeros((tm, MEM_WIDTH), F32)
    for hd in range(MEM_HEADS):
        hm = (head == hd).astype(F32)
        s = _mm_nt(mq, mk * hm) * MEM_SCALE
        e = jnp.exp(s - lax.stop_gradient(jnp.max(s, axis=-1, keepdims=True)))
        p = e / jnp.sum(e, axis=-1, keepdims=True)
        om = om + _mm(p, mv * hm)
    return jnp.concatenate([oa * _silu(ga), ob * _silu(gb), oc * _silu(gc), om * _silu(gm)], axis=1)


def _post_bwd_fn(dyg, ga, sgu, sgv, gb, gc, mq, gm, oa, oc, mkv, lng, lnb, sgw, biasf):
    _, vjp = jax.vjp(_post_fn, ga, sgu, sgv, gb, gc, mq, gm, oa, oc, mkv, lng, lnb, sgw, biasf)
    dga, dsgu, dsgv, dgb, dgc, dmq, dgm, doa, doc, dmkv, dlng, dlnb, dsgw, dbias = vjp(dyg)
    drows = jnp.stack([_col2row(jnp.sum((doa * oa)[:, h * 128:(h + 1) * 128], axis=1, keepdims=True))
                       for h in range(MLA_HEADS)], axis=0)
    return dga, dsgu, dsgv, dgb, dgc, dmq, dgm, doa, doc, drows, dmkv, dlng, dlnb, dsgw, dbias


def _prep_bwd_fn(dq, dk, dv, cq, ckv, kpe, cc, sa, sb, gq, wuq, gkv, wukv):
    wuq, wukv = wuq.astype(F32), wukv.astype(F32)
    _, vjp = jax.vjp(lambda a, b, c, d, e, f, g: _prep_fn(a, b, c, cc, sa, sb, d, e, f, g), cq, ckv, kpe, gq, wuq, gkv, wukv)
    return vjp((dq, dk, dv))


def _ln_bwd_fn(dxn, z, g, b):
    _, vjp = jax.vjp(_ln, z, g, b)
    dz, dg, db = vjp(dxn)
    return dz, dz, dg, db


def _outproj_fn(yg, x, w, g, b):
    z = ALPHA * x + _dot(yg, w, NN)
    xn = _ln(z, g, b)
    return z, xn, xn


def _loss_fn(xl, tgt):
    err = xl - tgt
    part = 0.5 * jnp.sum(jnp.mean(err * err, axis=-1, keepdims=True), axis=0, keepdims=True)
    return err * (1.0 / D_MODEL), jnp.broadcast_to(part, (1, LANES))


def _mla_fwd(q, k, v, tq, comm=None):
    t = q.shape[0]
    tq = min(tq, t)
    nq = t // tq

    def body(q_ref, k_ref, v_ref, o_ref, lse_ref):
        i = pl.program_id(1)
        qb = q_ref[...]
        r_chunk = lax.broadcasted_iota(jnp.int32, (tq, tq), 0) // CHUNK
        c_chunk = lax.broadcasted_iota(jnp.int32, (tq, tq), 1) // CHUNK
        diag_ok = c_chunk <= r_chunk

        def block(j, carry, masked):
            m, l, acc = carry
            off = pl.multiple_of(j * tq, tq)
            s = _dot(qb, k_ref[pl.ds(off, tq), :], NT)
            if masked:
                s = jnp.where(diag_ok, s, NEG)
            m2 = jnp.maximum(m, jnp.max(s, axis=1, keepdims=True))
            p = jnp.exp2(s - m2)
            a = jnp.exp2(m - m2)
            return m2, a * l + jnp.sum(p, axis=1, keepdims=True), a * acc + _dot(p, v_ref[pl.ds(off, tq), :], NN)

        init = (jnp.full((tq, 1), NEG, F32), jnp.zeros((tq, 1), F32), jnp.zeros((tq, MLA_V), F32))
        carry = lax.fori_loop(0, i, lambda j, c: block(j, c, False), init)
        m, l, acc = block(i, carry, True)
        o_ref[...] = acc / l
        lse_ref[...] = _col2row(m + jnp.log2(l))

    (o, lse), received = _comm_call(
        body, "mla_fwd", (MLA_HEADS, nq),
        [pl.BlockSpec((tq, MLA_HEAD_PAD), lambda h, i: (i, h)),
         pl.BlockSpec((t, MLA_HEAD_PAD), lambda h, i: (0, h)),
         pl.BlockSpec((t, MLA_V), lambda h, i: (0, h))],
        [pl.BlockSpec((tq, MLA_V), lambda h, i: (i, h)),
         pl.BlockSpec((None, 8, tq), lambda h, i: (h, 0, i))],
        [jax.ShapeDtypeStruct((t, MLA_WIDTH), F32), jax.ShapeDtypeStruct((MLA_HEADS, 8, t), F32)],
        [q, k, v], ("parallel", "parallel"), comm)
    return o, lse, received


def _mla_bwd(q, k, v, do, lse, drow, tq, comm=None):
    t = q.shape[0]
    tq = min(tq, t)
    nq = t // tq

    def body(k_ref, v_ref, q_ref, do_ref, lse_ref, d_ref, dk_ref, dv_ref, dq_ref):
        j = pl.program_id(1)

        @pl.when(j == 0)
        def _():
            dq_ref[...] = jnp.zeros_like(dq_ref)

        kj, vj = k_ref[...], v_ref[...]
        k_chunk = lax.broadcasted_iota(jnp.int32, (tq, tq), 0) // CHUNK
        q_chunk = lax.broadcasted_iota(jnp.int32, (tq, tq), 1) // CHUNK
        diag_ok = k_chunk <= q_chunk

        def block(i, carry, masked):
            dk, dv = carry
            off = pl.multiple_of(i * tq, tq)
            qi, doi = q_ref[pl.ds(off, tq), :], do_ref[pl.ds(off, tq), :]
            pt = jnp.exp2(_dot(kj, qi, NT) - lse_ref[0:1, pl.ds(off, tq)])
            if masked:
                pt = jnp.where(diag_ok, pt, 0.0)
            dv = dv + _dot(pt, doi, NN)
            dst = (pt * (_dot(vj, doi, NT) - d_ref[0:1, pl.ds(off, tq)])).astype(MXU)
            dk = dk + _dot(dst, qi, NN)
            dq_ref[pl.ds(off, tq), :] += _dot(dst, kj, TN) * LN2
            return dk, dv

        carry = block(j, (jnp.zeros((tq, MLA_HEAD_PAD), F32), jnp.zeros((tq, MLA_V), F32)), True)
        dk, dv = lax.fori_loop(j + 1, nq, lambda i, c: block(i, c, False), carry)
        dk_ref[...] = dk * LN2
        dv_ref[...] = dv

    (dk, dv, dq), received = _comm_call(
        body, "mla_bwd", (MLA_HEADS, nq),
        [pl.BlockSpec((tq, MLA_HEAD_PAD), lambda h, j: (j, h)),
         pl.BlockSpec((tq, MLA_V), lambda h, j: (j, h)),
         pl.BlockSpec((t, MLA_HEAD_PAD), lambda h, j: (0, h)),
         pl.BlockSpec((t, MLA_V), lambda h, j: (0, h)),
         pl.BlockSpec((None, 8, t), lambda h, j: (h, 0, 0)),
         pl.BlockSpec((None, 8, t), lambda h, j: (h, 0, 0))],
        [pl.BlockSpec((tq, MLA_HEAD_PAD), lambda h, j: (j, h)),
         pl.BlockSpec((tq, MLA_V), lambda h, j: (j, h)),
         pl.BlockSpec((t, MLA_HEAD_PAD), lambda h, j: (0, h))],
        [jax.ShapeDtypeStruct((t, MLA_HEADS * MLA_HEAD_PAD), F32), jax.ShapeDtypeStruct((t, MLA_WIDTH), F32),
         jax.ShapeDtypeStruct((t, MLA_HEADS * MLA_HEAD_PAD), F32)],
        [k, v, q, do, lse, drow], ("parallel", "arbitrary"), comm)
    return dq, dk, dv, received


def _sb_logs(z2):
    lb = jnp.minimum(z2, 0.0) - jnp.log2(1.0 + jnp.exp2(jnp.minimum(z2, -z2)))
    return lb, lb - z2


def _tri(n, keep):
    r = lax.broadcasted_iota(jnp.int32, (n, n), 0)
    c = lax.broadcasted_iota(jnp.int32, (n, n), 1)
    return keep(r, c)


def _sb_fwd(hb, tq):
    t = hb.shape[0]
    tq = min(tq, t)
    nq, tk, hq = t // tq, min(SCAN, tq), tq // 2
    cq, ck, cv = C_SBQ // SB_DIM, C_SBK // SB_DIM, C_SBV // SB_DIM

    def body(q_ref, k_ref, v_ref, o_ref, tot_ref, visited_ref):
        h, i = pl.program_id(0), pl.program_id(1)
        qb = q_ref[...]
        strict = _tri(tq, lambda r, c: c < r)
        later = _tri(tk, lambda r, c: r > c).astype(MXU)

        def block(qv, off, n, carry, masked):
            run, acc = carry
            lb, l1 = _sb_logs(_dot(qv, k_ref[pl.ds(off, n), :], NT))
            if masked:
                l1 = jnp.where(strict, l1, 0.0)
            after, rowsum = _scan_keys(l1, later, True)
            a = jnp.exp2(lb + after + run)
            if masked:
                a = jnp.where(strict, a, 0.0)
            return run + rowsum, acc + _dot(a, v_ref[pl.ds(off, n), :], NN)

        diag = pl.multiple_of(i * tq, tq)
        carry = block(qb, diag, tq, (jnp.zeros((tq, 1), F32), jnp.zeros((tq, SB_DIM), F32)), True)

        def back(qv, first_row):
            def live(state):
                jj, (run, _) = state
                return jnp.logical_and(jj < i * (tq // tk), jnp.max(run[first_row:]) > -SB_CUTOFF)

            def step(state):
                jj, c = state
                return jj + 1, block(qv, pl.multiple_of(diag - (jj + 1) * tk, tk), tk, c, False)

            return live, step

        n_all, carry = lax.while_loop(*back(qb, hq), (jnp.int32(0), carry))
        n_any, upper = lax.while_loop(*back(qb[:hq], 0), (n_all, tuple(c[:hq] for c in carry)))
        run, acc = (jnp.concatenate([u, c[hq:]], axis=0) for u, c in zip(upper, carry))
        o_ref[...] = acc
        tot_ref[...] = _col2row(run)
        visited_ref[h, i] = n_all
        visited_ref[SB_HEADS + h, i] = n_any

    return _comm_call(
        body, "sb_fwd", (SB_HEADS, nq),
        [pl.BlockSpec((tq, SB_DIM), lambda h, i: (i, cq + h)),
         pl.BlockSpec((t, SB_DIM), lambda h, i: (0, ck + h)),
         pl.BlockSpec((t, SB_DIM), lambda h, i: (0, cv + h))],
        [pl.BlockSpec((tq, SB_DIM), lambda h, i: (i, h)),
         pl.BlockSpec((None, 8, tq), lambda h, i: (h, 0, i)),
         pl.BlockSpec(memory_space=pltpu.SMEM)],
        [jax.ShapeDtypeStruct((t, SB_WIDTH), F32), jax.ShapeDtypeStruct((SB_HEADS, 8, t), F32),
         jax.ShapeDtypeStruct((2 * SB_HEADS, nq), jnp.int32)],
        [hb, hb, hb], ("arbitrary", "arbitrary"), None)[0]


def _sb_bwd(hb, do, tot, visited, tq):
    t = hb.shape[0]
    tq = min(tq, t)
    nq, tk, hq = t // tq, min(SCAN, tq), tq // 2
    cq, ck, cv = C_SBQ // SB_DIM, C_SBK // SB_DIM, C_SBV // SB_DIM

    def body(q_ref, do_ref, tot_ref, k_ref, v_ref, visited_ref, dq_ref, dk_ref, dv_ref):
        h, i = pl.program_id(0), pl.program_id(1)

        @pl.when(i == 0)
        def _():
            dk_ref[...] = jnp.zeros_like(dk_ref)
            dv_ref[...] = jnp.zeros_like(dv_ref)

        qb, dob = q_ref[...], do_ref[...]
        total = _row2col(tot_ref[0:1, :])
        strict = _tri(tq, lambda r, c: c < r)
        upto = _tri(tk, lambda r, c: r <= c).astype(MXU)
        before = _tri(tk, lambda r, c: r < c).astype(MXU)

        def block(rows, off, n, carry, masked):
            pre, gpre, dq = carry
            qv, dov, tot_v = qb[:rows], dob[:rows], total[:rows]
            kj, vj = k_ref[pl.ds(off, n), :], v_ref[pl.ds(off, n), :]
            lb, l1 = _sb_logs(_dot(qv, kj, NT))
            if masked:
                l1 = jnp.where(strict, l1, 0.0)
            upto_sum, rowsum = _scan_keys(l1, upto, False)
            a = jnp.exp2(lb + (tot_v - (upto_sum + pre)))
            if masked:
                a = jnp.where(strict, a, 0.0)
            g = a * _dot(dov, vj, NT)
            before_sum, grow = _scan_keys(g, before, False)
            beta = jnp.exp2(lb)
            dz = g * (1.0 - beta) - (before_sum + gpre) * beta
            if masked:
                dz = jnp.where(strict, dz, 0.0)
            dzb = dz.astype(MXU)
            dk_ref[pl.ds(off, n), :] += _dot(dzb, qv, TN) * LN2
            dv_ref[pl.ds(off, n), :] += _dot(a, dov, TN)
            return pre + rowsum, gpre + grow, dq + _dot(dzb, kj, NN)

        diag = pl.multiple_of(i * tq, tq)
        n_all, n_any = visited_ref[h, i], visited_ref[SB_HEADS + h, i]
        start = diag - n_any * tk

        def earlier(rows, first, count, carry):
            return lax.fori_loop(
                0, count, lambda jj, c: block(rows, pl.multiple_of(start + (first + jj) * tk, tk), tk, c, False), carry)

        def zeros(rows):
            return jnp.zeros((rows, 1), F32), jnp.zeros((rows, 1), F32), jnp.zeros((rows, SB_DIM), F32)

        upper = earlier(hq, 0, n_any - n_all, zeros(hq))
        carry = tuple(jnp.concatenate([u, z], axis=0) for u, z in zip(upper, zeros(tq - hq)))
        carry = earlier(tq, n_any - n_all, n_all, carry)
        dq_ref[...] = (block(tq, diag, tq, carry, True)[2] * SB_SCALE).astype(dq_ref.dtype)

    return _comm_call(
        body, "sb_bwd", (SB_HEADS, nq),
        [pl.BlockSpec((tq, SB_DIM), lambda h, i: (i, cq + h)),
         pl.BlockSpec((tq, SB_DIM), lambda h, i: (i, h)),
         pl.BlockSpec((None, 8, tq), lambda h, i: (h, 0, i)),
         pl.BlockSpec((t, SB_DIM), lambda h, i: (0, ck + h)),
         pl.BlockSpec((t, SB_DIM), lambda h, i: (0, cv + h)),
         pl.BlockSpec(memory_space=pltpu.SMEM)],
        [pl.BlockSpec((tq, SB_DIM), lambda h, i: (i, h)),
         pl.BlockSpec((t, SB_DIM), lambda h, i: (0, h)),
         pl.BlockSpec((t, SB_DIM), lambda h, i: (0, h))],
        [jax.ShapeDtypeStruct((t, SB_WIDTH), MXU), jax.ShapeDtypeStruct((t, SB_WIDTH), F32),
         jax.ShapeDtypeStruct((t, SB_WIDTH), F32)],
        [hb, do, tot, hb, hb, visited], ("parallel", "arbitrary"), None)[0]


TM = 256
TM_WIDE = 512
TQ_MLA_FWD = 2048
TQ_MLA_BWD = 1024
TQ_SB = 512


def _layer_fwd(xf, xb, w, consts, comm=None, rest=None):
    t = xf.shape[0]
    cc, sa, sb, mem = consts
    col_scale = jnp.ones((1, D_IN_PAD), F32).at[:, C_SBQ:C_SBQ + SB_WIDTH].set(SB_SCALE * LOG2E)
    (h, hb), arrived = _matmul("in_proj", xb, w["w_in"], "nt", 512, 1920, D_MODEL, second_dtype=MXU, second_scale=col_scale,
                               comm=rest[0] if rest else None, with_received=True)
    if rest:
        w = dict(w, **rest[1](arrived))
    gq, gkv = w["q_norm_g"][None, :], w["kv_norm_g"][None, :]
    q, k, v = _rowcall(
        "mla_prep", _prep_fn, t, TM,
        [(h, 512, C_CQ // 512), (h, 256, C_CKV // 256), (h, 128, C_KPE // 128), (cc, 128, 0), (sa, 128, 0), (sb, 128, 0)],
        [gq, w["w_uq"], gkv, w["w_ukv"]],
        [_ro(t, MLA_HEADS * MLA_HEAD_PAD, MXU, TM), _ro(t, MLA_HEADS * MLA_HEAD_PAD, MXU, TM), _ro(t, MLA_WIDTH, MXU, TM)])
    oa, lse, received = _mla_fwd(q, k, v, TQ_MLA_FWD, comm)
    oc, tot, visited = _sb_fwd(hb, TQ_SB)
    mkv = _matmul("mem_kv", mem, w["w_mem_kv"], "nn", 256, 512, 2048)
    lng, lnb = w["sg_ln_g"][None, :], w["sg_ln_b"][None, :]
    biasf = jnp.repeat(w["sg_b"].T, SG_CH, axis=1)
    post_rows = [(h, 768, C_GA // 768), (h, 512, C_SGU // 512), (h, 512, C_SGV // 512), (h, 512, C_GB // 512),
                 (h, 512, C_GC // 512), (h, 256, C_MQ // 256), (h, 256, C_GM // 256), (oa, 768, 0), (oc, 512, 0)]
    post_consts = [mkv, lng, lnb, w["sg_w"], biasf]
    (yg,) = _rowcall("mixer_post", lambda *a: (_post_fn(*a),), t, TM, post_rows, post_consts, [_ro(t, D_MODEL, MXU, TM)])
    z, xn, xnb = _rowcall(
        "out_proj_ln", _outproj_fn, t, TM_WIDE, [(yg, D_MODEL, 0), (xf, D_MODEL, 0)],
        [w["w_out"], w["ln_g"][None, :], w["ln_b"][None, :]],
        [_ro(t, D_MODEL, F32, TM_WIDE), _ro(t, D_MODEL, F32, TM_WIDE), _ro(t, D_MODEL, MXU, TM_WIDE)])
    saved = dict(w=w, xb=xb, h=h, hb=hb, q=q, k=k, v=v, oa=oa, lse=lse, oc=oc, tot=tot, visited=visited, mkv=mkv, yg=yg, z=z,
                 post_rows=post_rows, post_consts=post_consts, gq=gq, gkv=gkv)
    return xn, xnb, saved, received


def _layer_bwd(dxn, w, s, consts, recv=None, early_items=None, late_items=None):
    t = dxn.shape[0]
    cc, sa, sb, mem = consts
    h = s["h"]
    dz, dzb, dlng2, dlnb2 = _rowcall(
        "ln_bwd", _ln_bwd_fn, t, TM_WIDE, [(dxn, D_MODEL, 0), (s["z"], D_MODEL, 0)],
        [w["ln_g"][None, :], w["ln_b"][None, :]],
        [_ro(t, D_MODEL, F32, TM_WIDE), _ro(t, D_MODEL, MXU, TM_WIDE)], [((1, D_MODEL), F32), ((1, D_MODEL), F32)])
    dyg = _matmul("out_proj_dx", dzb, w["w_out"], "nt", 512, 1024, 2048)
    dw_out = _matmul("out_proj_dw", s["yg"], dzb, "tn", 1024, 1024, 2048, out_dtype=GRAD_WIRE)
    tm = min(TM, t)
    outs = _rowcall(
        "mixer_post_bwd", _post_bwd_fn, t, TM, [(dyg, D_MODEL, 0)] + s["post_rows"], s["post_consts"],
        [_ro(t, 768, MXU, TM), _ro(t, 512, MXU, TM), _ro(t, 512, MXU, TM), _ro(t, 512, MXU, TM), _ro(t, 512, MXU, TM),
         _ro(t, 256, MXU, TM), _ro(t, 256, MXU, TM), _ro(t, 768, MXU, TM), _ro(t, 512, MXU, TM),
         ((MLA_HEADS, 8, t), F32, (MLA_HEADS, 8, tm), lambda i: (0, 0, i))],
        [((MEM_TOKENS, 2 * MEM_WIDTH), F32), ((1, SG_WIDTH), F32), ((1, SG_WIDTH), F32),
         ((SG_GROUPS, SG_CHUNK, SG_CHUNK), F32), ((SG_CHUNK, SG_WIDTH), F32)])
    dga, dsgu, dsgv, dgb, dgc, dmq, dgm, doa, doc, drow, dmkv, dsg_ln_g, dsg_ln_b, dsgw, dbias = outs
    dw_mem_kv = _matmul("mem_kv_dw", mem, dmkv, "tn", 1024, 512, 256, out_dtype=GRAD_WIRE)
    dsbq, dsbk, dsbv = _sb_bwd(s["hb"], doc, s["tot"], s["visited"], TQ_SB)
    named = early_items(dict(w_out=dw_out, w_mem_kv=dw_mem_kv)) if early_items else []
    dq, dk, dv, received = _mla_bwd(s["q"], s["k"], s["v"], doa, s["lse"], drow, TQ_MLA_BWD,
                                    _Comm([item for _, item in named]) if named else None)
    if named:
        recv.update({n: r for (n, _), r in zip(named, received)})
    wq, wkv = MLA_HEADS * MLA_HEAD_PAD, MLA_WIDTH
    dcq, dckv, dkpe, dgq, dwuq, dgkv, dwukv = _rowcall(
        "mla_prep_bwd", _prep_bwd_fn, t, TM,
        [(dq, wq, 0), (dk, wq, 0), (dv, wkv, 0), (h, 512, C_CQ // 512), (h, 256, C_CKV // 256), (h, 128, C_KPE // 128),
         (cc, 128, 0), (sa, 128, 0), (sb, 128, 0)],
        [s["gq"], w["w_uq"], s["gkv"], w["w_ukv"]],
        [_ro(t, 512, MXU, TM), _ro(t, 256, MXU, TM), _ro(t, 128, MXU, TM)],
        [((1, MLA_Q_RANK), F32), ((wq, MLA_Q_RANK), F32), ((1, MLA_KV_RANK), F32), ((2 * wkv, MLA_KV_RANK), F32)])
    dh = jnp.concatenate([dcq, dckv, dga, dsgu, dsgv, dgb, dsbq, dsbk.astype(MXU), dsbv.astype(MXU), dgc, dmq, dgm, dkpe], axis=1)
    dw_in = _matmul("in_proj_dw", dh, s["xb"], "tn", 1152, 1024, 2048, out_dtype=GRAD_WIRE)
    replicated = dict(q_norm_g=dgq[0], kv_norm_g=dgkv[0], sg_ln_g=dsg_ln_g[0], sg_ln_b=dsg_ln_b[0], sg_w=dsgw, sg_bias=dbias,
                      ln_g=dlng2[0], ln_b=dlnb2[0])
    named = late_items(dict(w_in=dw_in, w_uq=dwuq, w_ukv=dwukv), replicated) if late_items else []
    dx, received = _matmul("in_proj_dx", dh, w["w_in"], "nn", 512, 1024, D_IN_PAD, add=dz, add_scale=ALPHA,
                           comm=_Comm([item for _, item in named]) if named else None, with_received=True)
    if named:
        recv.update({n: r for (n, _), r in zip(named, received)})
    return dx, dict(replicated, w_in=dw_in, w_uq=dwuq, w_ukv=dwukv, w_mem_kv=dw_mem_kv, w_out=dw_out)


def _sg_bias_grad(bias_grads):
    db = jnp.concatenate(bias_grads, axis=0)

    def lane_sums(d):
        return (jnp.concatenate([jnp.broadcast_to(jnp.sum(d[:, g * SG_CH:(g + 1) * SG_CH], axis=1, keepdims=True),
                                                  (d.shape[0], SG_CH)) for g in range(SG_GROUPS)], axis=1),)

    (dbs,) = _rowcall("sg_bias_sum", lane_sums, db.shape[0], SG_CHUNK, [(db, SG_WIDTH, 0)], [], [_ro(db.shape[0], SG_WIDTH, F32, SG_CHUNK)])
    return dbs[:, ::SG_CH].reshape(len(bias_grads), SG_CHUNK, SG_GROUPS).transpose(0, 2, 1)


def _local_step(x, mem, positions, target, depth, weights_of, gather_comm, first_rest, grad_items, rep_item, recv):
    t = x.shape[0]
    inv_freq = ROPE_THETA ** (-jnp.arange(0, MLA_ROPE, 2, dtype=F32) / MLA_ROPE)
    invf = jnp.tile(inv_freq, 4)[None, :]
    cc, sa, sb = _rowcall("rope_tables", _rope_tables_fn, t, TM, [(positions[:, None], 1, 0)], [invf],
                          [_ro(t, 128, F32, TM)] * 3)
    consts = (cc, sa, sb, mem)
    xf, xb, saved, received = x, x.astype(MXU), [], None
    for l in range(depth):
        xf, xb, s, received = _layer_fwd(xf, xb, weights_of(l, received), consts, gather_comm(l + 1) if l + 1 < depth else None,
                                         first_rest if l == 0 else None)
        saved.append(s)
    dx, loss_part = _rowcall("loss", _loss_fn, t, TM_WIDE, [(xf, D_MODEL, 0), (target, D_MODEL, 0)], [],
                             [_ro(t, D_MODEL, F32, TM_WIDE)], [((1, LANES), F32)])
    grads, recv, sg_b_grad = [None] * depth, dict(recv or {}), []
    for l in reversed(range(depth)):
        def early(g, l=l):
            rest = {n: grads[l + 1][n] for n in ("w_in", "w_uq", "w_ukv")} if l + 1 < depth else {}
            return grad_items(l, g, recv) + (grad_items(l + 1, rest, recv) if rest else [])

        def late(g, replicated):
            per_layer = [replicated] + grads[1:]
            sg_b_grad.append(_sg_bias_grad([r["sg_bias"] for r in per_layer]))
            return grad_items(0, g, recv) + [("replicated", rep_item(per_layer, sg_b_grad[0]))]

        dx, grads[l] = _layer_bwd(dx, saved[l]["w"], saved[l], consts, recv, early if grad_items else None,
                                  late if grad_items and l == 0 else None)
    dsgb = sg_b_grad[0] if sg_b_grad else _sg_bias_grad([g["sg_bias"] for g in grads])
    return loss_part[0, 0], dx, grads, dsgb, recv


SHARD_IN = D_IN // N_DEV
SHARD_IN_PAD = D_IN_PAD // N_DEV
IN_RANGES = ((0, 768), (832, D_IN), (768, 832))


def _in_rows_from_shards(w):
    pieces = []
    for a, b in IN_RANGES:
        while a < b:
            e = min(b, (a // SHARD_IN + 1) * SHARD_IN)
            row = (a // SHARD_IN) * SHARD_IN_PAD + a % SHARD_IN
            pieces.append(w[row:row + e - a])
            a = e
    pieces.append(jnp.zeros((D_IN_PAD - D_IN, w.shape[1]), w.dtype))
    return jnp.concatenate(pieces, axis=0)


def _in_rows_to_shards(g):
    pieces = []
    for d in range(N_DEV):
        a, b = d * SHARD_IN, (d + 1) * SHARD_IN
        while a < b:
            e = min(b, 768 if a < 768 else (832 if a < 832 else D_IN))
            row = a if a < 768 else (C_KPE + a - 768 if a < 832 else a - 64)
            pieces.append(g[row:row + e - a])
            a = e
        pieces.append(jnp.zeros((SHARD_IN_PAD - SHARD_IN, g.shape[1]), g.dtype))
    return jnp.concatenate(pieces, axis=0)


def _uq_rows_pad(w):
    w = w.reshape(MLA_HEADS, MLA_NOPE + MLA_ROPE, w.shape[-1])
    return jnp.pad(w, ((0, 0), (0, MLA_HEAD_PAD - MLA_NOPE - MLA_ROPE), (0, 0))).reshape(MLA_HEADS * MLA_HEAD_PAD, -1)


def _uq_rows_unpad(g):
    return g.reshape(MLA_HEADS, MLA_HEAD_PAD, -1)[:, :MLA_NOPE + MLA_ROPE].reshape(MLA_HEADS * (MLA_NOPE + MLA_ROPE), -1)


def _ukv_rows_split(w):
    return jnp.swapaxes(w.reshape(MLA_HEADS, 2, 128, -1), 0, 1).reshape(MLA_HEADS * 256, -1)


def _ukv_rows_unsplit(g):
    return jnp.swapaxes(g.reshape(2, MLA_HEADS, 128, -1), 0, 1).reshape(MLA_HEADS * 256, -1)


class _Comm:
    def __init__(self, items):
        self.items = items
        self.n = len(items)
        self.pieces = [((s.shape[0] // N_DEV,) + s.shape[1:]) if sc else s.shape for s, sc, _ in items]
        self.inputs = [s for s, _, _ in items] + [d[0] for _, _, d in items if d is not None]
        any_space = pl.BlockSpec(memory_space=pl.ANY)
        self.in_specs = [any_space] * len(self.inputs)
        self.out_specs = [any_space] * self.n
        self.out_shape = [jax.ShapeDtypeStruct((N_DEV,) + tuple(p), s.dtype) if d is None
                          else jax.ShapeDtypeStruct(d[0].shape, d[0].dtype) for p, (s, _, d) in zip(self.pieces, items)]
        self.scratch = [pltpu.SemaphoreType.DMA((N_DEV - 1, self.n)), pltpu.SemaphoreType.DMA((N_DEV - 1, self.n)),
                        pltpu.SemaphoreType.DMA((self.n,))]

    def aliases(self, first_in, first_out):
        passed = [a for a, (_, _, d) in enumerate(self.items) if d is not None]
        return {first_in + self.n + b: first_out + a for b, a in enumerate(passed)}

    def copies(self, in_refs, out_refs, sems):
        send_sems, recv_sems, local_sems = sems
        x, y, c = lax.axis_index("x"), lax.axis_index("y"), lax.axis_index("c")
        me = 4 * x + 2 * y + c

        def block_for(a, p):
            if not self.items[a][1]:
                return in_refs[a]
            n = self.pieces[a][0]
            return in_refs[a].at[pl.ds(pl.multiple_of(p * n, SUBLANES_16BIT), n)]

        def slot(a, dev):
            dst = self.items[a][2]
            return out_refs[a].at[dev] if dst is None else out_refs[a].at[dev, dst[1]]

        starts, recv_waits, send_waits = [], [], []
        for a in range(self.n):
            cp = pltpu.make_async_copy(block_for(a, me), slot(a, me), local_sems.at[a])
            starts.append(cp.start)
            send_waits.append(cp.wait)
        for k in range(1, N_DEV):
            px = 1 - x if (k >> 2) & 1 else x
            py = 1 - y if (k >> 1) & 1 else y
            pc = 1 - c if k & 1 else c
            peer = 4 * px + 2 * py + pc
            for a in range(self.n):
                out = pltpu.make_async_remote_copy(
                    src_ref=block_for(a, peer), dst_ref=slot(a, me), send_sem=send_sems.at[k - 1, a],
                    recv_sem=recv_sems.at[k - 1, a], device_id=(px, py, pc), device_id_type=pl.DeviceIdType.MESH)
                back = pltpu.make_async_remote_copy(
                    src_ref=block_for(a, peer), dst_ref=slot(a, peer), send_sem=send_sems.at[k - 1, a],
                    recv_sem=recv_sems.at[k - 1, a], device_id=(px, py, pc), device_id_type=pl.DeviceIdType.MESH)
                starts.append(out.start)
                send_waits.append(out.wait_send)
                recv_waits.append(back.wait_recv)
        return starts, recv_waits + send_waits


def _comm_call(body, name, grid, in_specs, out_specs, out_shape, operands, semantics, comm, scratch=()):
    n_in, n_out, n_scr = len(operands), len(out_shape), len(scratch)
    if comm is None:
        outs = pl.pallas_call(body, name=name, grid=grid, in_specs=in_specs, out_specs=out_specs, out_shape=out_shape,
                              scratch_shapes=list(scratch), compiler_params=_params(semantics))(*operands)
        return outs, []
    n_ci = len(comm.inputs)

    def fused(*refs):
        ins, c_in = refs[:n_in], refs[n_in:n_in + n_ci]
        outs = refs[n_in + n_ci:n_in + n_ci + n_out]
        c_out = refs[n_in + n_ci + n_out:n_in + n_ci + n_out + comm.n]
        first_scr = n_in + n_ci + n_out + comm.n
        outs = outs + refs[first_scr:first_scr + n_scr]
        starts, waits = comm.copies(c_in, c_out, refs[first_scr + n_scr:])
        first = functools.reduce(jnp.logical_and, [pl.program_id(d) == 0 for d in range(len(grid))])
        last = functools.reduce(jnp.logical_and, [pl.program_id(d) == g - 1 for d, g in enumerate(grid)])

        @pl.when(first)
        def _():
            for start in starts:
                start()

        body(*ins, *outs)

        @pl.when(last)
        def _():
            for wait in waits:
                wait()

    outs = pl.pallas_call(
        fused, name=name, grid=grid, in_specs=list(in_specs) + comm.in_specs, out_specs=list(out_specs) + comm.out_specs,
        out_shape=list(out_shape) + comm.out_shape, scratch_shapes=list(scratch) + comm.scratch,
        input_output_aliases=comm.aliases(n_in, n_out), compiler_params=_params(("arbitrary",) * len(grid)),
    )(*operands, *comm.inputs)
    return outs[:n_out], outs[n_out:]


def _exchange(name, items):
    return _comm_call(lambda: None, name, (1,), [], [], [], [], ("arbitrary",), _Comm(items))[1]


BLOCK_BYTES = 6 << 20


SUBLANES_16BIT = 16


def _row_block(rows, row_bytes):
    step = SUBLANES_16BIT
    if rows * row_bytes <= BLOCK_BYTES or rows % step:
        return rows
    return max(tb for tb in range(step, rows + 1, step) if rows % tb == 0 and (tb == step or tb * row_bytes <= BLOCK_BYTES))


def _sum_contribs(recv):
    n, rows, width = recv.shape
    tb = _row_block(rows, width * 4 * (n + 1))

    def body(r_ref, g_ref):
        g = r_ref[0].astype(F32)
        for k in range(1, n):
            g = g + r_ref[k].astype(F32)
        g_ref[...] = g

    return pl.pallas_call(
        body, name="sum_grads", grid=(rows // tb,), in_specs=[pl.BlockSpec((n, tb, width), lambda i: (0, i, 0))],
        out_specs=pl.BlockSpec((tb, width), lambda i: (i, 0)), out_shape=jax.ShapeDtypeStruct((rows, width), F32),
        compiler_params=_params(("parallel",)),
    )(recv)


def _adamw(contribs, w, m, v):
    n, rows, width = contribs.shape
    tb = _row_block(rows, width * 4 * (n + 7))
    c1 = 1.0 - ADAM_B1 ** ADAM_STEP
    c2 = 1.0 - ADAM_B2 ** ADAM_STEP

    def body(r_ref, w_ref, m_ref, v_ref, g_ref, d_ref, nm_ref, nv_ref):
        g = r_ref[0].astype(F32)
        for k in range(1, n):
            g = g + r_ref[k].astype(F32)
        nm = ADAM_B1 * m_ref[...] + (1.0 - ADAM_B1) * g
        nv = ADAM_B2 * v_ref[...] + (1.0 - ADAM_B2) * (g * g)
        g_ref[...] = g
        nm_ref[...] = nm
        nv_ref[...] = nv
        d_ref[...] = -ADAM_LR * ((nm / c1) / (jnp.sqrt(nv / c2) + ADAM_EPS) + ADAM_WD * w_ref[...])

    blk = pl.BlockSpec((tb, width), lambda i: (i, 0))
    return pl.pallas_call(
        body, name="adamw", grid=(rows // tb,),
        in_specs=[pl.BlockSpec((n, tb, width), lambda i: (0, i, 0)), blk, blk, blk],
        out_specs=[blk] * 4, out_shape=[jax.ShapeDtypeStruct((rows, width), F32)] * 4,
        compiler_params=_params(("parallel",)),
    )(contribs, w, m, v)


def kernel(x, mem, positions, w_in, q_norm_g, w_uq, kv_norm_g, w_ukv, sg_ln_g, sg_ln_b, sg_w, sg_b, w_mem_k, w_mem_v, w_out, ln_g, ln_b, loss_target, m_w_in, m_q_norm_g, m_w_uq, m_kv_norm_g, m_w_ukv, m_sg_ln_g, m_sg_ln_b, m_sg_w, m_sg_b, m_w_mem_k, m_w_mem_v, m_w_out, m_ln_g, m_ln_b, v_w_in, v_q_norm_g, v_w_uq, v_kv_norm_g, v_w_ukv, v_sg_ln_g, v_sg_ln_b, v_sg_w, v_sg_b, v_w_mem_k, v_w_mem_v, v_w_out, v_ln_g, v_ln_b):
    wl = dict(w_in=w_in, q_norm_g=q_norm_g, w_uq=w_uq, kv_norm_g=kv_norm_g, w_ukv=w_ukv, sg_ln_g=sg_ln_g, sg_ln_b=sg_ln_b,
              sg_w=sg_w, sg_b=sg_b, w_mem_k=w_mem_k, w_mem_v=w_mem_v, w_out=w_out, ln_g=ln_g, ln_b=ln_b)
    ml = dict(w_in=m_w_in, q_norm_g=m_q_norm_g, w_uq=m_w_uq, kv_norm_g=m_kv_norm_g, w_ukv=m_w_ukv, sg_ln_g=m_sg_ln_g,
              sg_ln_b=m_sg_ln_b, sg_w=m_sg_w, sg_b=m_sg_b, w_mem_k=m_w_mem_k, w_mem_v=m_w_mem_v, w_out=m_w_out, ln_g=m_ln_g, ln_b=m_ln_b)
    vl = dict(w_in=v_w_in, q_norm_g=v_q_norm_g, w_uq=v_w_uq, kv_norm_g=v_kv_norm_g, w_ukv=v_w_ukv, sg_ln_g=v_sg_ln_g,
              sg_ln_b=v_sg_ln_b, sg_w=v_sg_w, sg_b=v_sg_b, w_mem_k=v_w_mem_k, w_mem_v=v_w_mem_v, w_out=v_w_out, ln_g=v_ln_g, ln_b=v_ln_b)
    depth = w_in.shape[0]

    w_in_t = jnp.pad(jnp.swapaxes(w_in, 1, 2), ((0, 0), (0, SHARD_IN_PAD - SHARD_IN), (0, 0)))
    shards = [s.astype(MXU) for s in (w_in_t, jnp.swapaxes(w_uq, 1, 2), jnp.swapaxes(w_ukv, 1, 2),
                                      jnp.concatenate([w_mem_k, w_mem_v], axis=-1), w_out)]

    def gather_items(l):
        return [(s[l], False, None) for s in shards]

    (first_in,) = _exchange("gather_weights", gather_items(0)[:1])

    def in_rows(g_in):
        return _in_rows_from_shards(g_in.reshape(D_IN_PAD, D_MODEL))

    def other_weights(g_uq, g_ukv, g_mem, g_out):
        return dict(w_uq=_uq_rows_pad(g_uq.reshape(-1, MLA_Q_RANK)), w_ukv=_ukv_rows_split(g_ukv.reshape(-1, MLA_KV_RANK)),
                    w_mem_kv=g_mem.reshape(D_MODEL, 2 * MEM_WIDTH), w_out=g_out.reshape(D_MODEL, D_MODEL))

    def weights_of(l, received):
        w = dict(q_norm_g=q_norm_g[l], kv_norm_g=kv_norm_g[l], sg_ln_g=sg_ln_g[l], sg_ln_b=sg_ln_b[l], sg_w=sg_w[l],
                 sg_b=sg_b[l], ln_g=ln_g[l], ln_b=ln_b[l])
        if l == 0:
            return dict(w, w_in=in_rows(first_in))
        return dict(w, w_in=in_rows(received[0]), **other_weights(*received[1:]))

    first_rest = (_Comm(gather_items(0)[1:]), lambda received: other_weights(*received))

    to_rows = dict(w_in=_in_rows_to_shards, w_uq=_uq_rows_unpad, w_ukv=_ukv_rows_unsplit)

    def grad_items(l, g, recv):
        return [(n, (to_rows.get(n, lambda a: a)(a).astype(GRAD_WIRE), True, (recv[n], l))) for n, a in g.items()]

    shard_rows = dict(w_in=(D_IN_PAD, D_MODEL), w_uq=(MLA_HEADS * (MLA_NOPE + MLA_ROPE), MLA_Q_RANK),
                      w_ukv=(MLA_HEADS * 256, MLA_KV_RANK), w_mem_kv=(D_MODEL, 2 * MEM_WIDTH), w_out=(D_MODEL, D_MODEL))
    recv = {n: lax.empty((N_DEV, depth, rows // N_DEV, width), GRAD_WIRE) for n, (rows, width) in shard_rows.items()}
    rep_sizes = [wl[n].size for n in REPLICATED]
    rep_rows = -(-sum(rep_sizes) // (SUBLANES_16BIT * LANES)) * SUBLANES_16BIT

    def packed(d):
        flat = jnp.concatenate([d[n].reshape(-1) for n in REPLICATED])
        return jnp.pad(flat, (0, rep_rows * LANES - flat.shape[0])).reshape(rep_rows, LANES)

    def rep_item(per_layer, dsgb):
        rep_g = dict(sg_b=dsgb, **{n: jnp.stack([g[n] for g in per_layer]) for n in REPLICATED if n != "sg_b"})
        return (packed(rep_g), False, None)

    loss_part, grad_x, grads, dsgb, recv = _local_step(
        x[0], mem[0], positions[0], loss_target[0], depth, weights_of, lambda l: _Comm(gather_items(l)), first_rest,
        grad_items, rep_item, recv)
    loss = lax.psum(loss_part, ("x", "y", "c"))
    r_in, r_uq, r_ukv, r_mem, r_out, r_rep = (recv[n] for n in ("w_in", "w_uq", "w_ukv", "w_mem_kv", "w_out", "replicated"))

    def rows2d(a):
        return a.reshape(-1, a.shape[-1])

    def update(name, contribs):
        outs = _adamw(contribs, rows2d(wl[name]), rows2d(ml[name]), rows2d(vl[name]))
        return [o.reshape(wl[name].shape) for o in outs]

    def from_transposed(recv, keep):
        s = _sum_contribs(recv.reshape(N_DEV, -1, recv.shape[-1])).reshape(recv.shape[1:])
        return rows2d(jnp.swapaxes(s[:, :keep], 1, 2))[None]

    res = dict(
        w_in=update("w_in", from_transposed(r_in, SHARD_IN)),
        w_uq=update("w_uq", from_transposed(r_uq, r_uq.shape[2])),
        w_ukv=update("w_ukv", from_transposed(r_ukv, r_ukv.shape[2])),
        w_mem_k=update("w_mem_k", r_mem[..., :MEM_WIDTH].reshape(N_DEV, -1, MEM_WIDTH)),
        w_mem_v=update("w_mem_v", r_mem[..., MEM_WIDTH:].reshape(N_DEV, -1, MEM_WIDTH)),
        w_out=update("w_out", r_out.reshape(N_DEV, -1, D_MODEL)))
    rep_outs = _adamw(r_rep, packed(wl), packed(ml), packed(vl))
    off = 0
    for n, size in zip(REPLICATED, rep_sizes):
        res[n] = [o.reshape(-1)[off:off + size].reshape(wl[n].shape) for o in rep_outs]
        off += size
    return (loss, grad_x[None], *[res[n][0] for n in WEIGHTS], *[res[n][1] for n in WEIGHTS],
            *[res[n][2] for n in WEIGHTS], *[res[n][3] for n in WEIGHTS])
```

```python
import functools
import math

import jax
import jax.numpy as jnp
from jax import lax
from jax.experimental import pallas as pl
from jax.experimental.pallas import tpu as pltpu

F32 = jnp.float32
MXU = jnp.bfloat16
GRAD_WIRE = jnp.bfloat16

DEPTH_FOR_NORM = 4
D_MODEL = 2048
CHUNK = 64
MLA_HEADS, MLA_NOPE, MLA_ROPE, MLA_V = 6, 128, 64, 128
MLA_Q_RANK, MLA_KV_RANK = 512, 256
MLA_WIDTH = MLA_HEADS * MLA_V
MLA_HEAD_PAD = 256
ROPE_THETA = 10000.0
SG_GROUPS, SG_CH, SG_CHUNK = 4, 128, 128
SG_WIDTH = SG_GROUPS * SG_CH
SB_HEADS, SB_DIM = 4, 128
SB_WIDTH = SB_HEADS * SB_DIM
MEM_TOKENS, MEM_HEADS, MEM_DIM = 256, 4, 64
MEM_WIDTH = MEM_HEADS * MEM_DIM
D_IN = 5696
D_IN_PAD = 5760
ALPHA = (2.0 * DEPTH_FOR_NORM) ** 0.25
LN_EPS = 1e-5
RMS_EPS = 1e-6
MLA_SCALE = 1.0 / math.sqrt(MLA_NOPE + MLA_ROPE)
SB_SCALE = 1.0 / math.sqrt(SB_DIM)
MEM_SCALE = 1.0 / math.sqrt(MEM_DIM)
LOG2E = math.log2(math.e)
LN2 = math.log(2.0)
ADAM_LR, ADAM_B1, ADAM_B2, ADAM_EPS, ADAM_WD, ADAM_STEP = 0.001, 0.9, 0.999, 1e-08, 0.01, 10
N_DEV = 8
LANES = 128
NEG = -1e30
SCAN = 256
SB_CUTOFF = 256.0
VMEM_LIMIT = 56 * 1024 * 1024

C_CQ, C_CKV, C_GA, C_SGU, C_SGV, C_GB = 0, 512, 768, 1536, 2048, 2560
C_SBQ, C_SBK, C_SBV, C_GC, C_MQ, C_GM, C_KPE = 3072, 3584, 4096, 4608, 5120, 5376, 5632

REPLICATED = ("q_norm_g", "kv_norm_g", "sg_ln_g", "sg_ln_b", "sg_w", "sg_b", "ln_g", "ln_b")
WEIGHTS = ("w_in", "q_norm_g", "w_uq", "kv_norm_g", "w_ukv", "sg_ln_g", "sg_ln_b", "sg_w", "sg_b",
           "w_mem_k", "w_mem_v", "w_out", "ln_g", "ln_b")


def _params(sem):
    return pltpu.CompilerParams(dimension_semantics=sem, vmem_limit_bytes=VMEM_LIMIT)


def _dot(a, b, dims):
    return lax.dot_general(a.astype(MXU), b.astype(MXU), (dims, ((), ())), preferred_element_type=F32)


NN, NT, TN = ((1,), (0,)), ((1,), (1,)), ((0,), (0,))


@jax.custom_vjp
def _mm(a, b):
    return _dot(a, b, NN)


def _mm_fwd(a, b):
    return _dot(a, b, NN), (a, b)


def _mm_bwd(res, g):
    a, b = res
    return _dot(g, b, NT), _dot(a, g, TN)


_mm.defvjp(_mm_fwd, _mm_bwd)


@jax.custom_vjp
def _mm_nt(a, b):
    return _dot(a, b, NT)


def _mm_nt_fwd(a, b):
    return _dot(a, b, NT), (a, b)


def _mm_nt_bwd(res, g):
    a, b = res
    return _dot(g, b, NN), _dot(g, a, TN)


_mm_nt.defvjp(_mm_nt_fwd, _mm_nt_bwd)


def _split_dot(x, tri):
    hi = x.astype(MXU)
    lo = (x - hi.astype(F32)).astype(MXU)
    return _dot(hi, tri, NN) + _dot(lo, tri, NN)


def _scan_keys(x, tri, reverse):
    w = tri.shape[0]
    nb = x.shape[1] // w
    outs, carry = [None] * nb, None
    for c in (reversed(range(nb)) if reverse else range(nb)):
        xc = x[:, c * w:(c + 1) * w]
        sc = _split_dot(xc, tri)
        outs[c] = sc if carry is None else sc + carry
        rs = jnp.sum(xc, axis=1, keepdims=True)
        carry = rs if carry is None else carry + rs
    return (outs[0] if nb == 1 else jnp.concatenate(outs, axis=1)), carry


def _col2row(col):
    return jnp.transpose(jnp.broadcast_to(col, (col.shape[0], LANES)))[0:8]


def _row2col(row):
    return jnp.transpose(jnp.broadcast_to(row, (LANES, row.shape[1])))[:, 0:1]


def _matmul(name, a, b, mode, tm, tn, tk, out_dtype=F32, add=None, add_scale=1.0, second_dtype=None,
            second_scale=None, comm=None, with_received=False):
    if mode == "nn":
        (m, k), n = a.shape, b.shape[1]
    elif mode == "nt":
        (m, k), n = a.shape, b.shape[0]
    else:
        (k, m), n = a.shape, b.shape[1]
    tm, tn, tk = min(tm, m), min(tn, n), min(tk, k)
    assert m % tm == 0 and n % tn == 0 and k % tk == 0, (name, m, n, k, tm, tn, tk)
    nk = k // tk
    dims = {"nn": NN, "nt": NT, "tn": TN}[mode]
    a_spec = (pl.BlockSpec((tk, tm), lambda j, i, kk: (kk, i)) if mode == "tn"
              else pl.BlockSpec((tm, tk), lambda j, i, kk: (i, kk)))
    b_spec = (pl.BlockSpec((tn, tk), lambda j, i, kk: (j, kk)) if mode == "nt"
              else pl.BlockSpec((tk, tn), lambda j, i, kk: (kk, j)))
    o_spec = pl.BlockSpec((tm, tn), lambda j, i, kk: (i, j))
    in_specs, operands = [a_spec, b_spec], [a, b]
    if add is not None:
        in_specs.append(o_spec)
        operands.append(add)
    if second_scale is not None:
        in_specs.append(pl.BlockSpec((1, tn), lambda j, i, kk: (0, j)))
        operands.append(second_scale)
    out_shape = [jax.ShapeDtypeStruct((m, n), out_dtype)]
    out_specs = [o_spec]
    if second_dtype is not None:
        out_shape.append(jax.ShapeDtypeStruct((m, n), second_dtype))
        out_specs.append(o_spec)

    def body(*refs):
        a_ref, b_ref = refs[0], refs[1]
        extra = list(refs[2:len(operands)])
        add_ref = extra.pop(0) if add is not None else None
        scale_ref = extra.pop(0) if second_scale is not None else None
        outs = refs[len(operands):-1]
        acc_ref = refs[-1]
        kk = pl.program_id(2)

        def finish(r):
            if add_ref is not None:
                r = r + add_scale * add_ref[...]
            outs[0][...] = r.astype(out_dtype)
            if second_dtype is not None:
                outs[1][...] = (r if scale_ref is None else r * scale_ref[...]).astype(second_dtype)

        if nk == 1:
            finish(_dot(a_ref[...], b_ref[...], dims))
            return

        @pl.when(kk == 0)
        def _():
            acc_ref[...] = jnp.zeros_like(acc_ref)

        acc_ref[...] += _dot(a_ref[...], b_ref[...], dims)

        @pl.when(kk == nk - 1)
        def _():
            finish(acc_ref[...])

    res, received = _comm_call(body, name, (n // tn, m // tm, nk), in_specs, out_specs, out_shape, operands,
                               ("parallel", "parallel", "arbitrary"), comm, scratch=[pltpu.VMEM((tm, tn) if nk > 1 else (8, LANES), F32)])
    res = res if second_dtype is not None else res[0]
    return (res, received) if with_received else res


def _accumulate(ref, val, first):
    @pl.when(first)
    def _():
        ref[...] = val

    @pl.when(jnp.logical_not(first))
    def _():
        ref[...] += val


def _rowcall(name, fn, rows, tm, row_in, const_in, row_out, acc_out=()):
    tm = min(tm, rows)
    assert rows % tm == 0
    n_in = len(row_in) + len(const_in)
    in_specs = [pl.BlockSpec((tm, w), lambda i, cb=cb: (i, cb)) for _, w, cb in row_in]
    in_specs += [pl.BlockSpec(c.shape, lambda i, nd=c.ndim: (0,) * nd) for c in const_in]
    out_specs = [pl.BlockSpec(blk, imap) for _, _, blk, imap in row_out]
    out_specs += [pl.BlockSpec(s, lambda i, nd=len(s): (0,) * nd) for s, _ in acc_out]
    out_shape = [jax.ShapeDtypeStruct(s, dt) for s, dt, _, _ in row_out]
    out_shape += [jax.ShapeDtypeStruct(s, dt) for s, dt in acc_out]

    def body(*refs):
        vals = fn(*[r[...] for r in refs[:n_in]])
        outs = refs[n_in:]
        first = pl.program_id(0) == 0
        for k, (_, dt, _, _) in enumerate(row_out):
            outs[k][...] = vals[k].astype(dt)
        for k in range(len(acc_out)):
            _accumulate(outs[len(row_out) + k], vals[len(row_out) + k], first)

    return pl.pallas_call(
        body, name=name, grid=(rows // tm,), in_specs=in_specs, out_specs=out_specs, out_shape=out_shape,
        compiler_params=_params(("arbitrary",) if acc_out else ("parallel",)),
    )(*[a for a, _, _ in row_in], *const_in)


def _ro(rows, width, dtype, tm):
    tm = min(tm, rows)
    return ((rows, width), dtype, (tm, width), lambda i: (i, 0))


def _rms(x, g):
    ms = jnp.mean(x * x, axis=-1, keepdims=True)
    return x * lax.rsqrt(ms + RMS_EPS) * g


def _ln(x, g, b):
    mu = jnp.mean(x, axis=-1, keepdims=True)
    xc = x - mu
    var = jnp.mean(xc * xc, axis=-1, keepdims=True)
    return xc * lax.rsqrt(var + LN_EPS) * g + b


def _gelu(x):
    return 0.5 * x * (1.0 + jnp.tanh(math.sqrt(2.0 / math.pi) * (x + 0.044715 * (x * x * x))))


def _silu(x):
    return x / (1.0 + jnp.exp(-x))


@jax.custom_vjp
def _rope(r, cc, sa, sb):
    return r * cc + pltpu.roll(r, 96, 1) * sa + pltpu.roll(r, 32, 1) * sb


def _rope_fwd(r, cc, sa, sb):
    return _rope(r, cc, sa, sb), (cc, sa, sb)


def _rope_bwd(res, d):
    cc, sa, sb = res
    return d * cc + pltpu.roll(d * sa, 32, 1) + pltpu.roll(d * sb, 96, 1), None, None, None


_rope.defvjp(_rope_fwd, _rope_bwd)


def _rope_tables_fn(pos, invf):
    ang = pos.astype(F32) * invf
    lane = lax.broadcasted_iota(jnp.int32, ang.shape, 1)
    cos, sin = jnp.cos(ang), jnp.sin(ang)
    zero = jnp.zeros_like(ang)
    cc = jnp.where(lane < 64, cos, zero)
    sa = jnp.where(lane < 32, -sin, zero)
    sb = jnp.where((lane >= 32) & (lane < 64), sin, zero)
    return cc, sa, sb


def _prep_fn(cq, ckv, kpe, cc, sa, sb, gq, wuq, gkv, wukv):
    q = _mm_nt(_rms(cq, gq), wuq) * (MLA_SCALE * LOG2E)
    kv = _mm_nt(_rms(ckv, gkv), wukv)
    krot = _rope(kpe, cc, sa, sb)
    qs, ks = [], []
    for h in range(MLA_HEADS):
        o = h * MLA_HEAD_PAD
        qs += [q[:, o:o + 128], _rope(q[:, o + 128:o + 256], cc, sa, sb)]
        ks += [kv[:, h * 128:(h + 1) * 128], krot]
    return jnp.concatenate(qs, axis=1), jnp.concatenate(ks, axis=1), kv[:, MLA_HEADS * 128:]


def _post_fn(ga, sgu, sgv, gb, gc, mq, gm, oa, oc, mkv, lng, lnb, sgw, biasf):
    tm = ga.shape[0]
    u = _gelu(sgu)
    vn = _ln(_gelu(sgv), lng, lnb)
    t_chunk = lax.broadcasted_iota(jnp.int32, (SG_CHUNK, SG_CHUNK), 0) // CHUNK
    s_chunk = lax.broadcasted_iota(jnp.int32, (SG_CHUNK, SG_CHUNK), 1) // CHUNK
    sg_mask = (s_chunk <= t_chunk).astype(F32)
    cols = []
    for g in range(SG_GROUPS):
        wg = sgw[g] * sg_mask
        vg = vn[:, g * SG_CH:(g + 1) * SG_CH]
        cols.append(jnp.concatenate(
            [_mm(wg, vg[n * SG_CHUNK:(n + 1) * SG_CHUNK, :]) for n in range(tm // SG_CHUNK)], axis=0))
    mixed = jnp.concatenate(cols, axis=1) + jnp.concatenate([biasf] * (tm // SG_CHUNK), axis=0)
    ob = u * mixed
    mk, mv = mkv[:, :MEM_WIDTH], mkv[:, MEM_WIDTH:]
    head = lax.broadcasted_iota(jnp.int32, (1, MEM_WIDTH), 1) // MEM_DIM
    om = jnp.zeros((tm, MEM_WIDTH), F32)
    for hd in range(MEM_HEADS):
        hm = (head == hd).astype(F32)
        s = _mm_nt(mq, mk * hm) * MEM_SCALE
        e = jnp.exp(s - lax.stop_gradient(jnp.max(s, axis=-1, keepdims=True)))
        p = e / jnp.sum(e, axis=-1, keepdims=True)
        om = om + _mm(p, mv * hm)
    return jnp.concatenate([oa * _silu(ga), ob * _silu(gb), oc * _silu(gc), om * _silu(gm)], axis=1)


def _post_bwd_fn(dyg, ga, sgu, sgv, gb, gc, mq, gm, oa, oc, mkv, lng, lnb, sgw, biasf):
    _, vjp = jax.vjp(_post_fn, ga, sgu, sgv, gb, gc, mq, gm, oa, oc, mkv, lng, lnb, sgw, biasf)
    dga, dsgu, dsgv, dgb, dgc, dmq, dgm, doa, doc, dmkv, dlng, dlnb, dsgw, dbias = vjp(dyg)
    drows = jnp.stack([_col2row(jnp.sum((doa * oa)[:, h * 128:(h + 1) * 128], axis=1, keepdims=True))
                       for h in range(MLA_HEADS)], axis=0)
    return dga, dsgu, dsgv, dgb, dgc, dmq, dgm, doa, doc, drows, dmkv, dlng, dlnb, dsgw, dbias


def _prep_bwd_fn(dq, dk, dv, cq, ckv, kpe, cc, sa, sb, gq, wuq, gkv, wukv):
    wuq, wukv = wuq.astype(F32), wukv.astype(F32)
    _, vjp = jax.vjp(lambda a, b, c, d, e, f, g: _prep_fn(a, b, c, cc, sa, sb, d, e, f, g), cq, ckv, kpe, gq, wuq, gkv, wukv)
    return vjp((dq, dk, dv))


def _ln_bwd_fn(dxn, z, g, b):
    _, vjp = jax.vjp(_ln, z, g, b)
    dz, dg, db = vjp(dxn)
    return dz, dz, dg, db


def _outproj_fn(yg, x, w, g, b):
    z = ALPHA * x + _dot(yg, w, NN)
    xn = _ln(z, g, b)
    return z, xn, xn


def _loss_fn(xl, tgt):
    err = xl - tgt
    part = 0.5 * jnp.sum(jnp.mean(err * err, axis=-1, keepdims=True), axis=0, keepdims=True)
    return err * (1.0 / D_MODEL), jnp.broadcast_to(part, (1, LANES))


def _mla_fwd(q, k, v, tq, comm=None):
    t = q.shape[0]
    tq = min(tq, t)
    nq = t // tq

    def body(q_ref, k_ref, v_ref, o_ref, lse_ref):
        i = pl.program_id(1)
        qb = q_ref[...]
        r_chunk = lax.broadcasted_iota(jnp.int32, (tq, tq), 0) // CHUNK
        c_chunk = lax.broadcasted_iota(jnp.int32, (tq, tq), 1) // CHUNK
        diag_ok = c_chunk <= r_chunk

        def block(j, carry, masked):
            m, l, acc = carry
            off = pl.multiple_of(j * tq, tq)
            s = _dot(qb, k_ref[pl.ds(off, tq), :], NT)
            if masked:
                s = jnp.where(diag_ok, s, NEG)
            m2 = jnp.maximum(m, jnp.max(s, axis=1, keepdims=True))
            p = jnp.exp2(s - m2)
            a = jnp.exp2(m - m2)
            return m2, a * l + jnp.sum(p, axis=1, keepdims=True), a * acc + _dot(p, v_ref[pl.ds(off, tq), :], NN)

        init = (jnp.full((tq, 1), NEG, F32), jnp.zeros((tq, 1), F32), jnp.zeros((tq, MLA_V), F32))
        carry = lax.fori_loop(0, i, lambda j, c: block(j, c, False), init)
        m, l, acc = block(i, carry, True)
        o_ref[...] = acc / l
        lse_ref[...] = _col2row(m + jnp.log2(l))

    (o, lse), received = _comm_call(
        body, "mla_fwd", (MLA_HEADS, nq),
        [pl.BlockSpec((tq, MLA_HEAD_PAD), lambda h, i: (i, h)),
         pl.BlockSpec((t, MLA_HEAD_PAD), lambda h, i: (0, h)),
         pl.BlockSpec((t, MLA_V), lambda h, i: (0, h))],
        [pl.BlockSpec((tq, MLA_V), lambda h, i: (i, h)),
         pl.BlockSpec((None, 8, tq), lambda h, i: (h, 0, i))],
        [jax.ShapeDtypeStruct((t, MLA_WIDTH), F32), jax.ShapeDtypeStruct((MLA_HEADS, 8, t), F32)],
        [q, k, v], ("parallel", "parallel"), comm)
    return o, lse, received


def _mla_bwd(q, k, v, do, lse, drow, tq, comm=None):
    t = q.shape[0]
    tq = min(tq, t)
    nq = t // tq

    def body(k_ref, v_ref, q_ref, do_ref, lse_ref, d_ref, dk_ref, dv_ref, dq_ref):
        j = pl.program_id(1)

        @pl.when(j == 0)
        def _():
            dq_ref[...] = jnp.zeros_like(dq_ref)

        kj, vj = k_ref[...], v_ref[...]
        k_chunk = lax.broadcasted_iota(jnp.int32, (tq, tq), 0) // CHUNK
        q_chunk = lax.broadcasted_iota(jnp.int32, (tq, tq), 1) // CHUNK
        diag_ok = k_chunk <= q_chunk

        def block(i, carry, masked):
            dk, dv = carry
            off = pl.multiple_of(i * tq, tq)
            qi, doi = q_ref[pl.ds(off, tq), :], do_ref[pl.ds(off, tq), :]
            pt = jnp.exp2(_dot(kj, qi, NT) - lse_ref[0:1, pl.ds(off, tq)])
            if masked:
                pt = jnp.where(diag_ok, pt, 0.0)
            dv = dv + _dot(pt, doi, NN)
            dst = (pt * (_dot(vj, doi, NT) - d_ref[0:1, pl.ds(off, tq)])).astype(MXU)
            dk = dk + _dot(dst, qi, NN)
            dq_ref[pl.ds(off, tq), :] += _dot(dst, kj, TN) * LN2
            return dk, dv

        carry = block(j, (jnp.zeros((tq, MLA_HEAD_PAD), F32), jnp.zeros((tq, MLA_V), F32)), True)
        dk, dv = lax.fori_loop(j + 1, nq, lambda i, c: block(i, c, False), carry)
        dk_ref[...] = dk * LN2
        dv_ref[...] = dv

    (dk, dv, dq), received = _comm_call(
        body, "mla_bwd", (MLA_HEADS, nq),
        [pl.BlockSpec((tq, MLA_HEAD_PAD), lambda h, j: (j, h)),
         pl.BlockSpec((tq, MLA_V), lambda h, j: (j, h)),
         pl.BlockSpec((t, MLA_HEAD_PAD), lambda h, j: (0, h)),
         pl.BlockSpec((t, MLA_V), lambda h, j: (0, h)),
         pl.BlockSpec((None, 8, t), lambda h, j: (h, 0, 0)),
         pl.BlockSpec((None, 8, t), lambda h, j: (h, 0, 0))],
        [pl.BlockSpec((tq, MLA_HEAD_PAD), lambda h, j: (j, h)),
         pl.BlockSpec((tq, MLA_V), lambda h, j: (j, h)),
         pl.BlockSpec((t, MLA_HEAD_PAD), lambda h, j: (0, h))],
        [jax.ShapeDtypeStruct((t, MLA_HEADS * MLA_HEAD_PAD), F32), jax.ShapeDtypeStruct((t, MLA_WIDTH), F32),
         jax.ShapeDtypeStruct((t, MLA_HEADS * MLA_HEAD_PAD), F32)],
        [k, v, q, do, lse, drow], ("parallel", "arbitrary"), comm)
    return dq, dk, dv, received


def _sb_logs(z2):
    lb = jnp.minimum(z2, 0.0) - jnp.log2(1.0 + jnp.exp2(jnp.minimum(z2, -z2)))
    return lb, lb - z2


def _tri(n, keep):
    r = lax.broadcasted_iota(jnp.int32, (n, n), 0)
    c = lax.broadcasted_iota(jnp.int32, (n, n), 1)
    return keep(r, c)


def _sb_fwd(hb, tq):
    t = hb.shape[0]
    tq = min(tq, t)
    nq, tk, hq = t // tq, min(SCAN, tq), tq // 2
    cq, ck, cv = C_SBQ // SB_DIM, C_SBK // SB_DIM, C_SBV // SB_DIM

    def body(q_ref, k_ref, v_ref, o_ref, tot_ref, visited_ref):
        h, i = pl.program_id(0), pl.program_id(1)
        qb = q_ref[...]
        strict = _tri(tq, lambda r, c: c < r)
        later = _tri(tk, lambda r, c: r > c).astype(MXU)

        def block(qv, off, n, carry, masked):
            run, acc = carry
            lb, l1 = _sb_logs(_dot(qv, k_ref[pl.ds(off, n), :], NT))
            if masked:
                l1 = jnp.where(strict, l1, 0.0)
            after, rowsum = _scan_keys(l1, later, True)
            a = jnp.exp2(lb + after + run)
            if masked:
                a = jnp.where(strict, a, 0.0)
            return run + rowsum, acc + _dot(a, v_ref[pl.ds(off, n), :], NN)

        diag = pl.multiple_of(i * tq, tq)
        carry = block(qb, diag, tq, (jnp.zeros((tq, 1), F32), jnp.zeros((tq, SB_DIM), F32)), True)

        def back(qv, first_row):
            def live(state):
                jj, (run, _) = state
                return jnp.logical_and(jj < i * (tq // tk), jnp.max(run[first_row:]) > -SB_CUTOFF)

            def step(state):
                jj, c = state
                return jj + 1, block(qv, pl.multiple_of(diag - (jj + 1) * tk, tk), tk, c, False)

            return live, step

        n_all, carry = lax.while_loop(*back(qb, hq), (jnp.int32(0), carry))
        n_any, upper = lax.while_loop(*back(qb[:hq], 0), (n_all, tuple(c[:hq] for c in carry)))
        run, acc = (jnp.concatenate([u, c[hq:]], axis=0) for u, c in zip(upper, carry))
        o_ref[...] = acc
        tot_ref[...] = _col2row(run)
        visited_ref[h, i] = n_all
        visited_ref[SB_HEADS + h, i] = n_any

    return _comm_call(
        body, "sb_fwd", (SB_HEADS, nq),
        [pl.BlockSpec((tq, SB_DIM), lambda h, i: (i, cq + h)),
         pl.BlockSpec((t, SB_DIM), lambda h, i: (0, ck + h)),
         pl.BlockSpec((t, SB_DIM), lambda h, i: (0, cv + h))],
        [pl.BlockSpec((tq, SB_DIM), lambda h, i: (i, h)),
         pl.BlockSpec((None, 8, tq), lambda h, i: (h, 0, i)),
         pl.BlockSpec(memory_space=pltpu.SMEM)],
        [jax.ShapeDtypeStruct((t, SB_WIDTH), F32), jax.ShapeDtypeStruct((SB_HEADS, 8, t), F32),
         jax.ShapeDtypeStruct((2 * SB_HEADS, nq), jnp.int32)],
        [hb, hb, hb], ("arbitrary", "arbitrary"), None)[0]


def _sb_bwd(hb, do, tot, visited, tq):
    t = hb.shape[0]
    tq = min(tq, t)
    nq, tk, hq = t // tq, min(SCAN, tq), tq // 2
    cq, ck, cv = C_SBQ // SB_DIM, C_SBK // SB_DIM, C_SBV // SB_DIM

    def body(q_ref, do_ref, tot_ref, k_ref, v_ref, visited_ref, dq_ref, dk_ref, dv_ref):
        h, i = pl.program_id(0), pl.program_id(1)

        @pl.when(i == 0)
        def _():
            dk_ref[...] = jnp.zeros_like(dk_ref)
            dv_ref[...] = jnp.zeros_like(dv_ref)

        qb, dob = q_ref[...], do_ref[...]
        total = _row2col(tot_ref[0:1, :])
        strict = _tri(tq, lambda r, c: c < r)
        upto = _tri(tk, lambda r, c: r <= c).astype(MXU)
        before = _tri(tk, lambda r, c: r < c).astype(MXU)

        def block(rows, off, n, carry, masked):
            pre, gpre, dq = carry
            qv, dov, tot_v = qb[:rows], dob[:rows], total[:rows]
            kj, vj = k_ref[pl.ds(off, n), :], v_ref[pl.ds(off, n), :]
            lb, l1 = _sb_logs(_dot(qv, kj, NT))
            if masked:
                l1 = jnp.where(strict, l1, 0.0)
            upto_sum, rowsum = _scan_keys(l1, upto, False)
            a = jnp.exp2(lb + (tot_v - (upto_sum + pre)))
            if masked:
                a = jnp.where(strict, a, 0.0)
            g = a * _dot(dov, vj, NT)
            before_sum, grow = _scan_keys(g, before, False)
            beta = jnp.exp2(lb)
            dz = g * (1.0 - beta) - (before_sum + gpre) * beta
            if masked:
                dz = jnp.where(strict, dz, 0.0)
            dzb = dz.astype(MXU)
            dk_ref[pl.ds(off, n), :] += _dot(dzb, qv, TN) * LN2
            dv_ref[pl.ds(off, n), :] += _dot(a, dov, TN)
            return pre + rowsum, gpre + grow, dq + _dot(dzb, kj, NN)

        diag = pl.multiple_of(i * tq, tq)
        n_all, n_any = visited_ref[h, i], visited_ref[SB_HEADS + h, i]
        start = diag - n_any * tk

        def earlier(rows, first, count, carry):
            return lax.fori_loop(
                0, count, lambda jj, c: block(rows, pl.multiple_of(start + (first + jj) * tk, tk), tk, c, False), carry)

        def zeros(rows):
            return jnp.zeros((rows, 1), F32), jnp.zeros((rows, 1), F32), jnp.zeros((rows, SB_DIM), F32)

        upper = earlier(hq, 0, n_any - n_all, zeros(hq))
        carry = tuple(jnp.concatenate([u, z], axis=0) for u, z in zip(upper, zeros(tq - hq)))
        carry = earlier(tq, n_any - n_all, n_all, carry)
        dq_ref[...] = (block(tq, diag, tq, carry, True)[2] * SB_SCALE).astype(dq_ref.dtype)

    return _comm_call(
        body, "sb_bwd", (SB_HEADS, nq),
        [pl.BlockSpec((tq, SB_DIM), lambda h, i: (i, cq + h)),
         pl.BlockSpec((tq, SB_DIM), lambda h, i: (i, h)),
         pl.BlockSpec((None, 8, tq), lambda h, i: (h, 0, i)),
         pl.BlockSpec((t, SB_DIM), lambda h, i: (0, ck + h)),
         pl.BlockSpec((t, SB_DIM), lambda h, i: (0, cv + h)),
         pl.BlockSpec(memory_space=pltpu.SMEM)],
        [pl.BlockSpec((tq, SB_DIM), lambda h, i: (i, h)),
         pl.BlockSpec((t, SB_DIM), lambda h, i: (0, h)),
         pl.BlockSpec((t, SB_DIM), lambda h, i: (0, h))],
        [jax.ShapeDtypeStruct((t, SB_WIDTH), MXU), jax.ShapeDtypeStruct((t, SB_WIDTH), F32),
         jax.ShapeDtypeStruct((t, SB_WIDTH), F32)],
        [hb, do, tot, hb, hb, visited], ("parallel", "arbitrary"), None)[0]


TM = 256
TM_WIDE = 512
TQ_MLA_FWD = 2048
TQ_MLA_BWD = 1024
TQ_SB = 512


def _layer_fwd(xf, xb, w, consts, comm=None, rest=None):
    t = xf.shape[0]
    cc, sa, sb, mem = consts
    col_scale = jnp.ones((1, D_IN_PAD), F32).at[:, C_SBQ:C_SBQ + SB_WIDTH].set(SB_SCALE * LOG2E)
    (h, hb), arrived = _matmul("in_proj", xb, w["w_in"], "nt", 256, D_IN_PAD, D_MODEL, second_dtype=MXU, second_scale=col_scale,
                               comm=rest[0] if rest else None, with_received=True)
    if rest:
        w = dict(w, **rest[1](arrived))
    gq, gkv = w["q_norm_g"][None, :], w["kv_norm_g"][None, :]
    q, k, v = _rowcall(
        "mla_prep", _prep_fn, t, TM,
        [(h, 512, C_CQ // 512), (h, 256, C_CKV // 256), (h, 128, C_KPE // 128), (cc, 128, 0), (sa, 128, 0), (sb, 128, 0)],
        [gq, w["w_uq"], gkv, w["w_ukv"]],
        [_ro(t, MLA_HEADS * MLA_HEAD_PAD, MXU, TM), _ro(t, MLA_HEADS * MLA_HEAD_PAD, MXU, TM), _ro(t, MLA_WIDTH, MXU, TM)])
    oa, lse, received = _mla_fwd(q, k, v, TQ_MLA_FWD, comm)
    oc, tot, visited = _sb_fwd(hb, TQ_SB)
    mkv = _matmul("mem_kv", mem, w["w_mem_kv"], "nn", 256, 512, 2048)
    lng, lnb = w["sg_ln_g"][None, :], w["sg_ln_b"][None, :]
    biasf = jnp.repeat(w["sg_b"].T, SG_CH, axis=1)
    post_rows = [(h, 768, C_GA // 768), (h, 512, C_SGU // 512), (h, 512, C_SGV // 512), (h, 512, C_GB // 512),
                 (h, 512, C_GC // 512), (h, 256, C_MQ // 256), (h, 256, C_GM // 256), (oa, 768, 0), (oc, 512, 0)]
    post_consts = [mkv, lng, lnb, w["sg_w"], biasf]
    (yg,) = _rowcall("mixer_post", lambda *a: (_post_fn(*a),), t, TM, post_rows, post_consts, [_ro(t, D_MODEL, MXU, TM)])
    z, xn, xnb = _rowcall(
        "out_proj_ln", _outproj_fn, t, TM_WIDE, [(yg, D_MODEL, 0), (xf, D_MODEL, 0)],
        [w["w_out"], w["ln_g"][None, :], w["ln_b"][None, :]],
        [_ro(t, D_MODEL, F32, TM_WIDE), _ro(t, D_MODEL, F32, TM_WIDE), _ro(t, D_MODEL, MXU, TM_WIDE)])
    saved = dict(w=w, xb=xb, h=h, hb=hb, q=q, k=k, v=v, oa=oa, lse=lse, oc=oc, tot=tot, visited=visited, mkv=mkv, yg=yg, z=z,
                 post_rows=post_rows, post_consts=post_consts, gq=gq, gkv=gkv)
    return xn, xnb, saved, received


def _layer_bwd(dxn, w, s, consts, recv=None, early_items=None, late_items=None):
    t = dxn.shape[0]
    cc, sa, sb, mem = consts
    h = s["h"]
    dz, dzb, dlng2, dlnb2 = _rowcall(
        "ln_bwd", _ln_bwd_fn, t, TM_WIDE, [(dxn, D_MODEL, 0), (s["z"], D_MODEL, 0)],
        [w["ln_g"][None, :], w["ln_b"][None, :]],
        [_ro(t, D_MODEL, F32, TM_WIDE), _ro(t, D_MODEL, MXU, TM_WIDE)], [((1, D_MODEL), F32), ((1, D_MODEL), F32)])
    dyg = _matmul("out_proj_dx", dzb, w["w_out"], "nt", 512, 1024, 2048)
    dw_out = _matmul("out_proj_dw", s["yg"], dzb, "tn", 1024, 1024, 2048, out_dtype=GRAD_WIRE)
    tm = min(TM, t)
    outs = _rowcall(
        "mixer_post_bwd", _post_bwd_fn, t, TM, [(dyg, D_MODEL, 0)] + s["post_rows"], s["post_consts"],
        [_ro(t, 768, MXU, TM), _ro(t, 512, MXU, TM), _ro(t, 512, MXU, TM), _ro(t, 512, MXU, TM), _ro(t, 512, MXU, TM),
         _ro(t, 256, MXU, TM), _ro(t, 256, MXU, TM), _ro(t, 768, MXU, TM), _ro(t, 512, MXU, TM),
         ((MLA_HEADS, 8, t), F32, (MLA_HEADS, 8, tm), lambda i: (0, 0, i))],
        [((MEM_TOKENS, 2 * MEM_WIDTH), F32), ((1, SG_WIDTH), F32), ((1, SG_WIDTH), F32),
         ((SG_GROUPS, SG_CHUNK, SG_CHUNK), F32), ((SG_CHUNK, SG_WIDTH), F32)])
    dga, dsgu, dsgv, dgb, dgc, dmq, dgm, doa, doc, drow, dmkv, dsg_ln_g, dsg_ln_b, dsgw, dbias = outs
    dw_mem_kv = _matmul("mem_kv_dw", mem, dmkv, "tn", 1024, 512, 256, out_dtype=GRAD_WIRE)
    dsbq, dsbk, dsbv = _sb_bwd(s["hb"], doc, s["tot"], s["visited"], TQ_SB)
    named = early_items(dict(w_out=dw_out, w_mem_kv=dw_mem_kv)) if early_items else []
    dq, dk, dv, received = _mla_bwd(s["q"], s["k"], s["v"], doa, s["lse"], drow, TQ_MLA_BWD,
                                    _Comm([item for _, item in named]) if named else None)
    if named:
        recv.update({n: r for (n, _), r in zip(named, received)})
    wq, wkv = MLA_HEADS * MLA_HEAD_PAD, MLA_WIDTH
    dcq, dckv, dkpe, dgq, dwuq, dgkv, dwukv = _rowcall(
        "mla_prep_bwd", _prep_bwd_fn, t, TM,
        [(dq, wq, 0), (dk, wq, 0), (dv, wkv, 0), (h, 512, C_CQ // 512), (h, 256, C_CKV // 256), (h, 128, C_KPE // 128),
         (cc, 128, 0), (sa, 128, 0), (sb, 128, 0)],
        [s["gq"], w["w_uq"], s["gkv"], w["w_ukv"]],
        [_ro(t, 512, MXU, TM), _ro(t, 256, MXU, TM), _ro(t, 128, MXU, TM)],
        [((1, MLA_Q_RANK), F32), ((wq, MLA_Q_RANK), F32), ((1, MLA_KV_RANK), F32), ((2 * wkv, MLA_KV_RANK), F32)])
    dh = jnp.concatenate([dcq, dckv, dga, dsgu, dsgv, dgb, dsbq, dsbk.astype(MXU), dsbv.astype(MXU), dgc, dmq, dgm, dkpe], axis=1)
    dw_in = _matmul("in_proj_dw", dh, s["xb"], "tn", 1152, 1024, 2048, out_dtype=GRAD_WIRE)
    replicated = dict(q_norm_g=dgq[0], kv_norm_g=dgkv[0], sg_ln_g=dsg_ln_g[0], sg_ln_b=dsg_ln_b[0], sg_w=dsgw, sg_bias=dbias,
                      ln_g=dlng2[0], ln_b=dlnb2[0])
    named = late_items(dict(w_in=dw_in, w_uq=dwuq, w_ukv=dwukv), replicated) if late_items else []
    dx, received = _matmul("in_proj_dx", dh, w["w_in"], "nn", 512, 1024, D_IN_PAD, add=dz, add_scale=ALPHA,
                           comm=_Comm([item for _, item in named]) if named else None, with_received=True)
    if named:
        recv.update({n: r for (n, _), r in zip(named, received)})
    return dx, dict(replicated, w_in=dw_in, w_uq=dwuq, w_ukv=dwukv, w_mem_kv=dw_mem_kv, w_out=dw_out)


def _sg_bias_grad(bias_grads):
    db = jnp.concatenate(bias_grads, axis=0)

    def lane_sums(d):
        return (jnp.concatenate([jnp.broadcast_to(jnp.sum(d[:, g * SG_CH:(g + 1) * SG_CH], axis=1, keepdims=True),
                                                  (d.shape[0], SG_CH)) for g in range(SG_GROUPS)], axis=1),)

    (dbs,) = _rowcall("sg_bias_sum", lane_sums, db.shape[0], SG_CHUNK, [(db, SG_WIDTH, 0)], [], [_ro(db.shape[0], SG_WIDTH, F32, SG_CHUNK)])
    return dbs[:, ::SG_CH].reshape(len(bias_grads), SG_CHUNK, SG_GROUPS).transpose(0, 2, 1)


def _local_step(x, mem, positions, target, depth, weights_of, gather_comm, first_rest, grad_items, rep_item, recv):
    t = x.shape[0]
    inv_freq = ROPE_THETA ** (-jnp.arange(0, MLA_ROPE, 2, dtype=F32) / MLA_ROPE)
    invf = jnp.tile(inv_freq, 4)[None, :]
    cc, sa, sb = _rowcall("rope_tables", _rope_tables_fn, t, TM, [(positions[:, None], 1, 0)], [invf],
                          [_ro(t, 128, F32, TM)] * 3)
    consts = (cc, sa, sb, mem)
    xf, xb, saved, received = x, x.astype(MXU), [], None
    for l in range(depth):
        xf, xb, s, received = _layer_fwd(xf, xb, weights_of(l, received), consts, gather_comm(l + 1) if l + 1 < depth else None,
                                         first_rest if l == 0 else None)
        saved.append(s)
    dx, loss_part = _rowcall("loss", _loss_fn, t, TM_WIDE, [(xf, D_MODEL, 0), (target, D_MODEL, 0)], [],
                             [_ro(t, D_MODEL, F32, TM_WIDE)], [((1, LANES), F32)])
    grads, recv, sg_b_grad = [None] * depth, dict(recv or {}), []
    for l in reversed(range(depth)):
        def early(g, l=l):
            rest = {n: grads[l + 1][n] for n in ("w_in", "w_uq", "w_ukv")} if l + 1 < depth else {}
            return grad_items(l, g, recv) + (grad_items(l + 1, rest, recv) if rest else [])

        def late(g, replicated):
            per_layer = [replicated] + grads[1:]
            sg_b_grad.append(_sg_bias_grad([r["sg_bias"] for r in per_layer]))
            return grad_items(0, g, recv) + [("replicated", rep_item(per_layer, sg_b_grad[0]))]

        dx, grads[l] = _layer_bwd(dx, saved[l]["w"], saved[l], consts, recv, early if grad_items else None,
                                  late if grad_items and l == 0 else None)
    dsgb = sg_b_grad[0] if sg_b_grad else _sg_bias_grad([g["sg_bias"] for g in grads])
    return loss_part[0, 0], dx, grads, dsgb, recv


SHARD_IN = D_IN // N_DEV
SHARD_IN_PAD = D_IN_PAD // N_DEV
IN_RANGES = ((0, 768), (832, D_IN), (768, 832))


def _in_rows_from_shards(w):
    pieces = []
    for a, b in IN_RANGES:
        while a < b:
            e = min(b, (a // SHARD_IN + 1) * SHARD_IN)
            row = (a // SHARD_IN) * SHARD_IN_PAD + a % SHARD_IN
            pieces.append(w[row:row + e - a])
            a = e
    pieces.append(jnp.zeros((D_IN_PAD - D_IN, w.shape[1]), w.dtype))
    return jnp.concatenate(pieces, axis=0)


def _in_rows_to_shards(g):
    pieces = []
    for d in range(N_DEV):
        a, b = d * SHARD_IN, (d + 1) * SHARD_IN
        while a < b:
            e = min(b, 768 if a < 768 else (832 if a < 832 else D_IN))
            row = a if a < 768 else (C_KPE + a - 768 if a < 832 else a - 64)
            pieces.append(g[row:row + e - a])
            a = e
        pieces.append(jnp.zeros((SHARD_IN_PAD - SHARD_IN, g.shape[1]), g.dtype))
    return jnp.concatenate(pieces, axis=0)


def _uq_rows_pad(w):
    w = w.reshape(MLA_HEADS, MLA_NOPE + MLA_ROPE, w.shape[-1])
    return jnp.pad(w, ((0, 0), (0, MLA_HEAD_PAD - MLA_NOPE - MLA_ROPE), (0, 0))).reshape(MLA_HEADS * MLA_HEAD_PAD, -1)


def _uq_rows_unpad(g):
    return g.reshape(MLA_HEADS, MLA_HEAD_PAD, -1)[:, :MLA_NOPE + MLA_ROPE].reshape(MLA_HEADS * (MLA_NOPE + MLA_ROPE), -1)


def _ukv_rows_split(w):
    return jnp.swapaxes(w.reshape(MLA_HEADS, 2, 128, -1), 0, 1).reshape(MLA_HEADS * 256, -1)


def _ukv_rows_unsplit(g):
    return jnp.swapaxes(g.reshape(2, MLA_HEADS, 128, -1), 0, 1).reshape(MLA_HEADS * 256, -1)


class _Comm:
    def __init__(self, items):
        self.items = items
        self.n = len(items)
        self.pieces = [((s.shape[0] // N_DEV,) + s.shape[1:]) if sc else s.shape for s, sc, _ in items]
        self.inputs = [s for s, _, _ in items] + [d[0] for _, _, d in items if d is not None]
        any_space = pl.BlockSpec(memory_space=pl.ANY)
        self.in_specs = [any_space] * len(self.inputs)
        self.out_specs = [any_space] * self.n
        self.out_shape = [jax.ShapeDtypeStruct((N_DEV,) + tuple(p), s.dtype) if d is None
                          else jax.ShapeDtypeStruct(d[0].shape, d[0].dtype) for p, (s, _, d) in zip(self.pieces, items)]
        self.scratch = [pltpu.SemaphoreType.DMA((N_DEV - 1, self.n)), pltpu.SemaphoreType.DMA((N_DEV - 1, self.n)),
                        pltpu.SemaphoreType.DMA((self.n,))]

    def aliases(self, first_in, first_out):
        passed = [a for a, (_, _, d) in enumerate(self.items) if d is not None]
        return {first_in + self.n + b: first_out + a for b, a in enumerate(passed)}

    def copies(self, in_refs, out_refs, sems):
        send_sems, recv_sems, local_sems = sems
        x, y, c = lax.axis_index("x"), lax.axis_index("y"), lax.axis_index("c")
        me = 4 * x + 2 * y + c

        def block_for(a, p):
            if not self.items[a][1]:
                return in_refs[a]
            n = self.pieces[a][0]
            return in_refs[a].at[pl.ds(pl.multiple_of(p * n, SUBLANES_16BIT), n)]

        def slot(a, dev):
            dst = self.items[a][2]
            return out_refs[a].at[dev] if dst is None else out_refs[a].at[dev, dst[1]]

        starts, recv_waits, send_waits = [], [], []
        for a in range(self.n):
            cp = pltpu.make_async_copy(block_for(a, me), slot(a, me), local_sems.at[a])
            starts.append(cp.start)
            send_waits.append(cp.wait)
        for k in range(1, N_DEV):
            px = 1 - x if (k >> 2) & 1 else x
            py = 1 - y if (k >> 1) & 1 else y
            pc = 1 - c if k & 1 else c
            peer = 4 * px + 2 * py + pc
            for a in range(self.n):
                out = pltpu.make_async_remote_copy(
                    src_ref=block_for(a, peer), dst_ref=slot(a, me), send_sem=send_sems.at[k - 1, a],
                    recv_sem=recv_sems.at[k - 1, a], device_id=(px, py, pc), device_id_type=pl.DeviceIdType.MESH)
                back = pltpu.make_async_remote_copy(
                    src_ref=block_for(a, peer), dst_ref=slot(a, peer), send_sem=send_sems.at[k - 1, a],
                    recv_sem=recv_sems.at[k - 1, a], device_id=(px, py, pc), device_id_type=pl.DeviceIdType.MESH)
                starts.append(out.start)
                send_waits.append(out.wait_send)
                recv_waits.append(back.wait_recv)
        return starts, recv_waits + send_waits


def _comm_call(body, name, grid, in_specs, out_specs, out_shape, operands, semantics, comm, scratch=()):
    n_in, n_out, n_scr = len(operands), len(out_shape), len(scratch)
    if comm is None:
        outs = pl.pallas_call(body, name=name, grid=grid, in_specs=in_specs, out_specs=out_specs, out_shape=out_shape,
                              scratch_shapes=list(scratch), compiler_params=_params(semantics))(*operands)
        return outs, []
    n_ci = len(comm.inputs)

    def fused(*refs):
        ins, c_in = refs[:n_in], refs[n_in:n_in + n_ci]
        outs = refs[n_in + n_ci:n_in + n_ci + n_out]
        c_out = refs[n_in + n_ci + n_out:n_in + n_ci + n_out + comm.n]
        first_scr = n_in + n_ci + n_out + comm.n
        outs = outs + refs[first_scr:first_scr + n_scr]
        starts, waits = comm.copies(c_in, c_out, refs[first_scr + n_scr:])
        first = functools.reduce(jnp.logical_and, [pl.program_id(d) == 0 for d in range(len(grid))])
        last = functools.reduce(jnp.logical_and, [pl.program_id(d) == g - 1 for d, g in enumerate(grid)])

        @pl.when(first)
        def _():
            for start in starts:
                start()

        body(*ins, *outs)

        @pl.when(last)
        def _():
            for wait in waits:
                wait()

    outs = pl.pallas_call(
        fused, name=name, grid=grid, in_specs=list(in_specs) + comm.in_specs, out_specs=list(out_specs) + comm.out_specs,
        out_shape=list(out_shape) + comm.out_shape, scratch_shapes=list(scratch) + comm.scratch,
        input_output_aliases=comm.aliases(n_in, n_out), compiler_params=_params(("arbitrary",) * len(grid)),
    )(*operands, *comm.inputs)
    return outs[:n_out], outs[n_out:]


def _exchange(name, items):
    return _comm_call(lambda: None, name, (1,), [], [], [], [], ("arbitrary",), _Comm(items))[1]


BLOCK_BYTES = 6 << 20


SUBLANES_16BIT = 16


def _row_block(rows, row_bytes):
    step = SUBLANES_16BIT
    if rows * row_bytes <= BLOCK_BYTES or rows % step:
        return rows
    return max(tb for tb in range(step, rows + 1, step) if rows % tb == 0 and (tb == step or tb * row_bytes <= BLOCK_BYTES))


def _sum_contribs(recv):
    n, rows, width = recv.shape
    tb = _row_block(rows, width * 4 * (n + 1))

    def body(r_ref, g_ref):
        g = r_ref[0].astype(F32)
        for k in range(1, n):
            g = g + r_ref[k].astype(F32)
        g_ref[...] = g

    return pl.pallas_call(
        body, name="sum_grads", grid=(rows // tb,), in_specs=[pl.BlockSpec((n, tb, width), lambda i: (0, i, 0))],
        out_specs=pl.BlockSpec((tb, width), lambda i: (i, 0)), out_shape=jax.ShapeDtypeStruct((rows, width), F32),
        compiler_params=_params(("parallel",)),
    )(recv)


def _adamw(contribs, w, m, v):
    n, rows, width = contribs.shape
    tb = _row_block(rows, width * 4 * (n + 7))
    c1 = 1.0 - ADAM_B1 ** ADAM_STEP
    c2 = 1.0 - ADAM_B2 ** ADAM_STEP

    def body(r_ref, w_ref, m_ref, v_ref, g_ref, d_ref, nm_ref, nv_ref):
        g = r_ref[0].astype(F32)
        for k in range(1, n):
            g = g + r_ref[k].astype(F32)
        nm = ADAM_B1 * m_ref[...] + (1.0 - ADAM_B1) * g
        nv = ADAM_B2 * v_ref[...] + (1.0 - ADAM_B2) * (g * g)
        g_ref[...] = g
        nm_ref[...] = nm
        nv_ref[...] = nv
        d_ref[...] = -ADAM_LR * ((nm / c1) / (jnp.sqrt(nv / c2) + ADAM_EPS) + ADAM_WD * w_ref[...])

    blk = pl.BlockSpec((tb, width), lambda i: (i, 0))
    return pl.pallas_call(
        body, name="adamw", grid=(rows // tb,),
        in_specs=[pl.BlockSpec((n, tb, width), lambda i: (0, i, 0)), blk, blk, blk],
        out_specs=[blk] * 4, out_shape=[jax.ShapeDtypeStruct((rows, width), F32)] * 4,
        compiler_params=_params(("parallel",)),
    )(contribs, w, m, v)


def kernel(x, mem, positions, w_in, q_norm_g, w_uq, kv_norm_g, w_ukv, sg_ln_g, sg_ln_b, sg_w, sg_b, w_mem_k, w_mem_v, w_out, ln_g, ln_b, loss_target, m_w_in, m_q_norm_g, m_w_uq, m_kv_norm_g, m_w_ukv, m_sg_ln_g, m_sg_ln_b, m_sg_w, m_sg_b, m_w_mem_k, m_w_mem_v, m_w_out, m_ln_g, m_ln_b, v_w_in, v_q_norm_g, v_w_uq, v_kv_norm_g, v_w_ukv, v_sg_ln_g, v_sg_ln_b, v_sg_w, v_sg_b, v_w_mem_k, v_w_mem_v, v_w_out, v_ln_g, v_ln_b):
    wl = dict(w_in=w_in, q_norm_g=q_norm_g, w_uq=w_uq, kv_norm_g=kv_norm_g, w_ukv=w_ukv, sg_ln_g=sg_ln_g, sg_ln_b=sg_ln_b,
              sg_w=sg_w, sg_b=sg_b, w_mem_k=w_mem_k, w_mem_v=w_mem_v, w_out=w_out, ln_g=ln_g, ln_b=ln_b)
    ml = dict(w_in=m_w_in, q_norm_g=m_q_norm_g, w_uq=m_w_uq, kv_norm_g=m_kv_norm_g, w_ukv=m_w_ukv, sg_ln_g=m_sg_ln_g,
              sg_ln_b=m_sg_ln_b, sg_w=m_sg_w, sg_b=m_sg_b, w_mem_k=m_w_mem_k, w_mem_v=m_w_mem_v, w_out=m_w_out, ln_g=m_ln_g, ln_b=m_ln_b)
    vl = dict(w_in=v_w_in, q_norm_g=v_q_norm_g, w_uq=v_w_uq, kv_norm_g=v_kv_norm_g, w_ukv=v_w_ukv, sg_ln_g=v_sg_ln_g,
              sg_ln_b=v_sg_ln_b, sg_w=v_sg_w, sg_b=v_sg_b, w_mem_k=v_w_mem_k, w_mem_v=v_w_mem_v, w_out=v_w_out, ln_g=v_ln_g, ln_b=v_ln_b)
    depth = w_in.shape[0]

    w_in_t = jnp.pad(jnp.swapaxes(w_in, 1, 2), ((0, 0), (0, SHARD_IN_PAD - SHARD_IN), (0, 0)))
    shards = [s.astype(MXU) for s in (w_in_t, jnp.swapaxes(w_uq, 1, 2), jnp.swapaxes(w_ukv, 1, 2),
                                      jnp.concatenate([w_mem_k, w_mem_v], axis=-1), w_out)]

    def gather_items(l):
        return [(s[l], False, None) for s in shards]

    (first_in,) = _exchange("gather_weights", gather_items(0)[:1])

    def in_rows(g_in):
        return _in_rows_from_shards(g_in.reshape(D_IN_PAD, D_MODEL))

    def other_weights(g_uq, g_ukv, g_mem, g_out):
        return dict(w_uq=_uq_rows_pad(g_uq.reshape(-1, MLA_Q_RANK)), w_ukv=_ukv_rows_split(g_ukv.reshape(-1, MLA_KV_RANK)),
                    w_mem_kv=g_mem.reshape(D_MODEL, 2 * MEM_WIDTH), w_out=g_out.reshape(D_MODEL, D_MODEL))

    def weights_of(l, received):
        w = dict(q_norm_g=q_norm_g[l], kv_norm_g=kv_norm_g[l], sg_ln_g=sg_ln_g[l], sg_ln_b=sg_ln_b[l], sg_w=sg_w[l],
                 sg_b=sg_b[l], ln_g=ln_g[l], ln_b=ln_b[l])
        if l == 0:
            return dict(w, w_in=in_rows(first_in))
        return dict(w, w_in=in_rows(received[0]), **other_weights(*received[1:]))

    first_rest = (_Comm(gather_items(0)[1:]), lambda received: other_weights(*received))

    to_rows = dict(w_in=_in_rows_to_shards, w_uq=_uq_rows_unpad, w_ukv=_ukv_rows_unsplit)

    def grad_items(l, g, recv):
        return [(n, (to_rows.get(n, lambda a: a)(a).astype(GRAD_WIRE), True, (recv[n], l))) for n, a in g.items()]

    shard_rows = dict(w_in=(D_IN_PAD, D_MODEL), w_uq=(MLA_HEADS * (MLA_NOPE + MLA_ROPE), MLA_Q_RANK),
                      w_ukv=(MLA_HEADS * 256, MLA_KV_RANK), w_mem_kv=(D_MODEL, 2 * MEM_WIDTH), w_out=(D_MODEL, D_MODEL))
    recv = {n: lax.empty((N_DEV, depth, rows // N_DEV, width), GRAD_WIRE) for n, (rows, width) in shard_rows.items()}
    rep_sizes = [wl[n].size for n in REPLICATED]
    rep_rows = -(-sum(rep_sizes) // (SUBLANES_16BIT * LANES)) * SUBLANES_16BIT

    def packed(d):
        flat = jnp.concatenate([d[n].reshape(-1) for n in REPLICATED])
        return jnp.pad(flat, (0, rep_rows * LANES - flat.shape[0])).reshape(rep_rows, LANES)

    def rep_item(per_layer, dsgb):
        rep_g = dict(sg_b=dsgb, **{n: jnp.stack([g[n] for g in per_layer]) for n in REPLICATED if n != "sg_b"})
        return (packed(rep_g), False, None)

    loss_part, grad_x, grads, dsgb, recv = _local_step(
        x[0], mem[0], positions[0], loss_target[0], depth, weights_of, lambda l: _Comm(gather_items(l)), first_rest,
        grad_items, rep_item, recv)
    loss = lax.psum(loss_part, ("x", "y", "c"))
    r_in, r_uq, r_ukv, r_mem, r_out, r_rep = (recv[n] for n in ("w_in", "w_uq", "w_ukv", "w_mem_kv", "w_out", "replicated"))

    def rows2d(a):
        return a.reshape(-1, a.shape[-1])

    def update(name, contribs):
        outs = _adamw(contribs, rows2d(wl[name]), rows2d(ml[name]), rows2d(vl[name]))
        return [o.reshape(wl[name].shape) for o in outs]

    def from_transposed(recv, keep):
        s = _sum_contribs(recv.reshape(N_DEV, -1, recv.shape[-1])).reshape(recv.shape[1:])
        return rows2d(jnp.swapaxes(s[:, :keep], 1, 2))[None]

    res = dict(
        w_in=update("w_in", from_transposed(r_in, SHARD_IN)),
        w_uq=update("w_uq", from_transposed(r_uq, r_uq.shape[2])),
        w_ukv=update("w_ukv", from_transposed(r_ukv, r_ukv.shape[2])),
        w_mem_k=update("w_mem_k", r_mem[..., :MEM_WIDTH].reshape(N_DEV, -1, MEM_WIDTH)),
        w_mem_v=update("w_mem_v", r_mem[..., MEM_WIDTH:].reshape(N_DEV, -1, MEM_WIDTH)),
        w_out=update("w_out", r_out.reshape(N_DEV, -1, D_MODEL)))
    rep_outs = _adamw(r_rep, packed(wl), packed(ml), packed(vl))
    off = 0
    for n, size in zip(REPLICATED, rep_sizes):
        res[n] = [o.reshape(-1)[off:off + size].reshape(wl[n].shape) for o in rep_outs]
        off += size
    return (loss, grad_x[None], *[res[n][0] for n in WEIGHTS], *[res[n][1] for n in WEIGHTS],
            *[res[n][2] for n in WEIGHTS], *[res[n][3] for n in WEIGHTS])
```

```python
import functools
import math

import jax
import jax.numpy as jnp
from jax import lax
from jax.experimental import pallas as pl
from jax.experimental.pallas import tpu as pltpu

F32 = jnp.float32
MXU = jnp.bfloat16
GRAD_WIRE = jnp.bfloat16

DEPTH_FOR_NORM = 4
D_MODEL = 2048
CHUNK = 64
MLA_HEADS, MLA_NOPE, MLA_ROPE, MLA_V = 6, 128, 64, 128
MLA_Q_RANK, MLA_KV_RANK = 512, 256
MLA_WIDTH = MLA_HEADS * MLA_V
MLA_HEAD_PAD = 256
ROPE_THETA = 10000.0
SG_GROUPS, SG_CH, SG_CHUNK = 4, 128, 128
SG_WIDTH = SG_GROUPS * SG_CH
SB_HEADS, SB_DIM = 4, 128
SB_WIDTH = SB_HEADS * SB_DIM
MEM_TOKENS, MEM_HEADS, MEM_DIM = 256, 4, 64
MEM_WIDTH = MEM_HEADS * MEM_DIM
D_IN = 5696
D_IN_PAD = 5760
ALPHA = (2.0 * DEPTH_FOR_NORM) ** 0.25
LN_EPS = 1e-5
RMS_EPS = 1e-6
MLA_SCALE = 1.0 / math.sqrt(MLA_NOPE + MLA_ROPE)
SB_SCALE = 1.0 / math.sqrt(SB_DIM)
MEM_SCALE = 1.0 / math.sqrt(MEM_DIM)
LOG2E = math.log2(math.e)
LN2 = math.log(2.0)
ADAM_LR, ADAM_B1, ADAM_B2, ADAM_EPS, ADAM_WD, ADAM_STEP = 0.001, 0.9, 0.999, 1e-08, 0.01, 10
N_DEV = 8
LANES = 128
NEG = -1e30
SCAN = 256
SB_CUTOFF = 256.0
VMEM_LIMIT = 56 * 1024 * 1024

C_CQ, C_CKV, C_GA, C_SGU, C_SGV, C_GB = 0, 512, 768, 1536, 2048, 2560
C_SBQ, C_SBK, C_SBV, C_GC, C_MQ, C_GM, C_KPE = 3072, 3584, 4096, 4608, 5120, 5376, 5632

REPLICATED = ("q_norm_g", "kv_norm_g", "sg_ln_g", "sg_ln_b", "sg_w", "sg_b", "ln_g", "ln_b")
WEIGHTS = ("w_in", "q_norm_g", "w_uq", "kv_norm_g", "w_ukv", "sg_ln_g", "sg_ln_b", "sg_w", "sg_b",
           "w_mem_k", "w_mem_v", "w_out", "ln_g", "ln_b")


def _params(sem):
    return pltpu.CompilerParams(dimension_semantics=sem, vmem_limit_bytes=VMEM_LIMIT)


def _dot(a, b, dims):
    return lax.dot_general(a.astype(MXU), b.astype(MXU), (dims, ((), ())), preferred_element_type=F32)


NN, NT, TN = ((1,), (0,)), ((1,), (1,)), ((0,), (0,))


@jax.custom_vjp
def _mm(a, b):
    return _dot(a, b, NN)


def _mm_fwd(a, b):
    return _dot(a, b, NN), (a, b)


def _mm_bwd(res, g):
    a, b = res
    return _dot(g, b, NT), _dot(a, g, TN)


_mm.defvjp(_mm_fwd, _mm_bwd)


@jax.custom_vjp
def _mm_nt(a, b):
    return _dot(a, b, NT)


def _mm_nt_fwd(a, b):
    return _dot(a, b, NT), (a, b)


def _mm_nt_bwd(res, g):
    a, b = res
    return _dot(g, b, NN), _dot(g, a, TN)


_mm_nt.defvjp(_mm_nt_fwd, _mm_nt_bwd)


def _split_dot(x, tri):
    hi = x.astype(MXU)
    lo = (x - hi.astype(F32)).astype(MXU)
    return _dot(hi, tri, NN) + _dot(lo, tri, NN)


def _scan_keys(x, tri, reverse):
    w = tri.shape[0]
    nb = x.shape[1] // w
    outs, carry = [None] * nb, None
    for c in (reversed(range(nb)) if reverse else range(nb)):
        xc = x[:, c * w:(c + 1) * w]
        sc = _split_dot(xc, tri)
        outs[c] = sc if carry is None else sc + carry
        rs = jnp.sum(xc, axis=1, keepdims=True)
        carry = rs if carry is None else carry + rs
    return (outs[0] if nb == 1 else jnp.concatenate(outs, axis=1)), carry


def _col2row(col):
    return jnp.transpose(jnp.broadcast_to(col, (col.shape[0], LANES)))[0:8]


def _row2col(row):
    return jnp.transpose(jnp.broadcast_to(row, (LANES, row.shape[1])))[:, 0:1]


def _matmul(name, a, b, mode, tm, tn, tk, out_dtype=F32, add=None, add_scale=1.0, second_dtype=None,
            second_scale=None, comm=None, with_received=False):
    if mode == "nn":
        (m, k), n = a.shape, b.shape[1]
    elif mode == "nt":
        (m, k), n = a.shape, b.shape[0]
    else:
        (k, m), n = a.shape, b.shape[1]
    tm, tn, tk = min(tm, m), min(tn, n), min(tk, k)
    assert m % tm == 0 and n % tn == 0 and k % tk == 0, (name, m, n, k, tm, tn, tk)
    nk = k // tk
    dims = {"nn": NN, "nt": NT, "tn": TN}[mode]
    a_spec = (pl.BlockSpec((tk, tm), lambda j, i, kk: (kk, i)) if mode == "tn"
              else pl.BlockSpec((tm, tk), lambda j, i, kk: (i, kk)))
    b_spec = (pl.BlockSpec((tn, tk), lambda j, i, kk: (j, kk)) if mode == "nt"
              else pl.BlockSpec((tk, tn), lambda j, i, kk: (kk, j)))
    o_spec = pl.BlockSpec((tm, tn), lambda j, i, kk: (i, j))
    in_specs, operands = [a_spec, b_spec], [a, b]
    if add is not None:
        in_specs.append(o_spec)
        operands.append(add)
    if second_scale is not None:
        in_specs.append(pl.BlockSpec((1, tn), lambda j, i, kk: (0, j)))
        operands.append(second_scale)
    out_shape = [jax.ShapeDtypeStruct((m, n), out_dtype)]
    out_specs = [o_spec]
    if second_dtype is not None:
        out_shape.append(jax.ShapeDtypeStruct((m, n), second_dtype))
        out_specs.append(o_spec)

    def body(*refs):
        a_ref, b_ref = refs[0], refs[1]
        extra = list(refs[2:len(operands)])
        add_ref = extra.pop(0) if add is not None else None
        scale_ref = extra.pop(0) if second_scale is not None else None
        outs = refs[len(operands):-1]
        acc_ref = refs[-1]
        kk = pl.program_id(2)

        def finish(r):
            if add_ref is not None:
                r = r + add_scale * add_ref[...]
            outs[0][...] = r.astype(out_dtype)
            if second_dtype is not None:
                outs[1][...] = (r if scale_ref is None else r * scale_ref[...]).astype(second_dtype)

        if nk == 1:
            finish(_dot(a_ref[...], b_ref[...], dims))
            return

        @pl.when(kk == 0)
        def _():
            acc_ref[...] = jnp.zeros_like(acc_ref)

        acc_ref[...] += _dot(a_ref[...], b_ref[...], dims)

        @pl.when(kk == nk - 1)
        def _():
            finish(acc_ref[...])

    res, received = _comm_call(body, name, (n // tn, m // tm, nk), in_specs, out_specs, out_shape, operands,
                               ("parallel", "parallel", "arbitrary"), comm, scratch=[pltpu.VMEM((tm, tn) if nk > 1 else (8, LANES), F32)])
    res = res if second_dtype is not None else res[0]
    return (res, received) if with_received else res


def _accumulate(ref, val, first):
    @pl.when(first)
    def _():
        ref[...] = val

    @pl.when(jnp.logical_not(first))
    def _():
        ref[...] += val


def _rowcall(name, fn, rows, tm, row_in, const_in, row_out, acc_out=()):
    tm = min(tm, rows)
    assert rows % tm == 0
    n_in = len(row_in) + len(const_in)
    in_specs = [pl.BlockSpec((tm, w), lambda i, cb=cb: (i, cb)) for _, w, cb in row_in]
    in_specs += [pl.BlockSpec(c.shape, lambda i, nd=c.ndim: (0,) * nd) for c in const_in]
    out_specs = [pl.BlockSpec(blk, imap) for _, _, blk, imap in row_out]
    out_specs += [pl.BlockSpec(s, lambda i, nd=len(s): (0,) * nd) for s, _ in acc_out]
    out_shape = [jax.ShapeDtypeStruct(s, dt) for s, dt, _, _ in row_out]
    out_shape += [jax.ShapeDtypeStruct(s, dt) for s, dt in acc_out]

    def body(*refs):
        vals = fn(*[r[...] for r in refs[:n_in]])
        outs = refs[n_in:]
        first = pl.program_id(0) == 0
        for k, (_, dt, _, _) in enumerate(row_out):
            outs[k][...] = vals[k].astype(dt)
        for k in range(len(acc_out)):
            _accumulate(outs[len(row_out) + k], vals[len(row_out) + k], first)

    return pl.pallas_call(
        body, name=name, grid=(rows // tm,), in_specs=in_specs, out_specs=out_specs, out_shape=out_shape,
        compiler_params=_params(("arbitrary",) if acc_out else ("parallel",)),
    )(*[a for a, _, _ in row_in], *const_in)


def _ro(rows, width, dtype, tm):
    tm = min(tm, rows)
    return ((rows, width), dtype, (tm, width), lambda i: (i, 0))


def _rms(x, g):
    ms = jnp.mean(x * x, axis=-1, keepdims=True)
    return x * lax.rsqrt(ms + RMS_EPS) * g


def _ln(x, g, b):
    mu = jnp.mean(x, axis=-1, keepdims=True)
    xc = x - mu
    var = jnp.mean(xc * xc, axis=-1, keepdims=True)
    return xc * lax.rsqrt(var + LN_EPS) * g + b


def _gelu(x):
    return 0.5 * x * (1.0 + jnp.tanh(math.sqrt(2.0 / math.pi) * (x + 0.044715 * (x * x * x))))


def _silu(x):
    return x / (1.0 + jnp.exp(-x))


@jax.custom_vjp
def _rope(r, cc, sa, sb):
    return r * cc + pltpu.roll(r, 96, 1) * sa + pltpu.roll(r, 32, 1) * sb


def _rope_fwd(r, cc, sa, sb):
    return _rope(r, cc, sa, sb), (cc, sa, sb)


def _rope_bwd(res, d):
    cc, sa, sb = res
    return d * cc + pltpu.roll(d * sa, 32, 1) + pltpu.roll(d * sb, 96, 1), None, None, None


_rope.defvjp(_rope_fwd, _rope_bwd)


def _rope_tables_fn(pos, invf):
    ang = pos.astype(F32) * invf
    lane = lax.broadcasted_iota(jnp.int32, ang.shape, 1)
    cos, sin = jnp.cos(ang), jnp.sin(ang)
    zero = jnp.zeros_like(ang)
    cc = jnp.where(lane < 64, cos, zero)
    sa = jnp.where(lane < 32, -sin, zero)
    sb = jnp.where((lane >= 32) & (lane < 64), sin, zero)
    return cc, sa, sb


def _prep_fn(cq, ckv, kpe, cc, sa, sb, gq, wuq, gkv, wukv):
    q = _mm_nt(_rms(cq, gq), wuq) * (MLA_SCALE * LOG2E)
    kv = _mm_nt(_rms(ckv, gkv), wukv)
    krot = _rope(kpe, cc, sa, sb)
    qs, ks = [], []
    for h in range(MLA_HEADS):
        o = h * MLA_HEAD_PAD
        qs += [q[:, o:o + 128], _rope(q[:, o + 128:o + 256], cc, sa, sb)]
        ks += [kv[:, h * 128:(h + 1) * 128], krot]
    return jnp.concatenate(qs, axis=1), jnp.concatenate(ks, axis=1), kv[:, MLA_HEADS * 128:]


def _post_fn(ga, sgu, sgv, gb, gc, mq, gm, oa, oc, mkv, lng, lnb, sgw, biasf):
    tm = ga.shape[0]
    u = _gelu(sgu)
    vn = _ln(_gelu(sgv), lng, lnb)
    t_chunk = lax.broadcasted_iota(jnp.int32, (SG_CHUNK, SG_CHUNK), 0) // CHUNK
    s_chunk = lax.broadcasted_iota(jnp.int32, (SG_CHUNK, SG_CHUNK), 1) // CHUNK
    sg_mask = (s_chunk <= t_chunk).astype(F32)
    cols = []
    for g in range(SG_GROUPS):
        wg = sgw[g] * sg_mask
        vg = vn[:, g * SG_CH:(g + 1) * SG_CH]
        cols.append(jnp.concatenate(
            [_mm(wg, vg[n * SG_CHUNK:(n + 1) * SG_CHUNK, :]) for n in range(tm // SG_CHUNK)], axis=0))
    mixed = jnp.concatenate(cols, axis=1) + jnp.concatenate([biasf] * (tm // SG_CHUNK), axis=0)
    ob = u * mixed
    mk, mv = mkv[:, :MEM_WIDTH], mkv[:, MEM_WIDTH:]
    head = lax.broadcasted_iota(jnp.int32, (1, MEM_WIDTH), 1) // MEM_DIM
    om = jnp.zeros((tm, MEM_WIDTH), F32)
    for hd in range(MEM_HEADS):
        hm = (head == hd).astype(F32)
        s = _mm_nt(mq, mk * hm) * MEM_SCALE
        e = jnp.exp(s - lax.stop_gradient(jnp.max(s, axis=-1, keepdims=True)))
        p = e / jnp.sum(e, axis=-1, keepdims=True)
        om = om + _mm(p, mv * hm)
    return jnp.concatenate([oa * _silu(ga), ob * _silu(gb), oc * _silu(gc), om * _silu(gm)], axis=1)


def _post_bwd_fn(dyg, ga, sgu, sgv, gb, gc, mq, gm, oa, oc, mkv, lng, lnb, sgw, biasf):
    _, vjp = jax.vjp(_post_fn, ga, sgu, sgv, gb, gc, mq, gm, oa, oc, mkv, lng, lnb, sgw, biasf)
    dga, dsgu, dsgv, dgb, dgc, dmq, dgm, doa, doc, dmkv, dlng, dlnb, dsgw, dbias = vjp(dyg)
    drows = jnp.stack([_col2row(jnp.sum((doa * oa)[:, h * 128:(h + 1) * 128], axis=1, keepdims=True))
                       for h in range(MLA_HEADS)], axis=0)
    return dga, dsgu, dsgv, dgb, dgc, dmq, dgm, doa, doc, drows, dmkv, dlng, dlnb, dsgw, dbias


def _prep_bwd_fn(dq, dk, dv, cq, ckv, kpe, cc, sa, sb, gq, wuq, gkv, wukv):
    wuq, wukv = wuq.astype(F32), wukv.astype(F32)
    _, vjp = jax.vjp(lambda a, b, c, d, e, f, g: _prep_fn(a, b, c, cc, sa, sb, d, e, f, g), cq, ckv, kpe, gq, wuq, gkv, wukv)
    return vjp((dq, dk, dv))


def _ln_bwd_proj_fn(dxn, z, g, b, w_out):
    _, vjp = jax.vjp(_ln, z, g, b)
    dz, dg, db = vjp(dxn)
    return dz, dz, _dot(dz, w_out, NT), dg, db


def _outproj_fn(yg, x, w, g, b):
    z = ALPHA * x + _dot(yg, w, NN)
    xn = _ln(z, g, b)
    return z, xn, xn


def _loss_fn(xl, tgt):
    err = xl - tgt
    part = 0.5 * jnp.sum(jnp.mean(err * err, axis=-1, keepdims=True), axis=0, keepdims=True)
    return err * (1.0 / D_MODEL), jnp.broadcast_to(part, (1, LANES))


def _mla_fwd(q, k, v, tq, comm=None):
    t = q.shape[0]
    tq = min(tq, t)
    nq = t // tq

    def body(q_ref, k_ref, v_ref, o_ref, lse_ref):
        i = pl.program_id(1)
        qb = q_ref[...]
        r_chunk = lax.broadcasted_iota(jnp.int32, (tq, tq), 0) // CHUNK
        c_chunk = lax.broadcasted_iota(jnp.int32, (tq, tq), 1) // CHUNK
        diag_ok = c_chunk <= r_chunk

        def block(j, carry, masked):
            m, l, acc = carry
            off = pl.multiple_of(j * tq, tq)
            s = _dot(qb, k_ref[pl.ds(off, tq), :], NT)
            if masked:
                s = jnp.where(diag_ok, s, NEG)
            m2 = jnp.maximum(m, jnp.max(s, axis=1, keepdims=True))
            p = jnp.exp2(s - m2)
            a = jnp.exp2(m - m2)
            return m2, a * l + jnp.sum(p, axis=1, keepdims=True), a * acc + _dot(p, v_ref[pl.ds(off, tq), :], NN)

        init = (jnp.full((tq, 1), NEG, F32), jnp.zeros((tq, 1), F32), jnp.zeros((tq, MLA_V), F32))
        carry = lax.fori_loop(0, i, lambda j, c: block(j, c, False), init)
        m, l, acc = block(i, carry, True)
        o_ref[...] = acc / l
        lse_ref[...] = _col2row(m + jnp.log2(l))

    (o, lse), received = _comm_call(
        body, "mla_fwd", (MLA_HEADS, nq),
        [pl.BlockSpec((tq, MLA_HEAD_PAD), lambda h, i: (i, h)),
         pl.BlockSpec((t, MLA_HEAD_PAD), lambda h, i: (0, h)),
         pl.BlockSpec((t, MLA_V), lambda h, i: (0, h))],
        [pl.BlockSpec((tq, MLA_V), lambda h, i: (i, h)),
         pl.BlockSpec((None, 8, tq), lambda h, i: (h, 0, i))],
        [jax.ShapeDtypeStruct((t, MLA_WIDTH), F32), jax.ShapeDtypeStruct((MLA_HEADS, 8, t), F32)],
        [q, k, v], ("parallel", "parallel"), comm)
    return o, lse, received


def _mla_bwd(q, k, v, do, lse, drow, tq, comm=None):
    t = q.shape[0]
    tq = min(tq, t)
    nq = t // tq

    def body(k_ref, v_ref, q_ref, do_ref, lse_ref, d_ref, dk_ref, dv_ref, dq_ref):
        j = pl.program_id(1)

        @pl.when(j == 0)
        def _():
            dq_ref[...] = jnp.zeros_like(dq_ref)

        kj, vj = k_ref[...], v_ref[...]
        k_chunk = lax.broadcasted_iota(jnp.int32, (tq, tq), 0) // CHUNK
        q_chunk = lax.broadcasted_iota(jnp.int32, (tq, tq), 1) // CHUNK
        diag_ok = k_chunk <= q_chunk

        def block(i, carry, masked):
            dk, dv = carry
            off = pl.multiple_of(i * tq, tq)
            qi, doi = q_ref[pl.ds(off, tq), :], do_ref[pl.ds(off, tq), :]
            pt = jnp.exp2(_dot(kj, qi, NT) - lse_ref[0:1, pl.ds(off, tq)])
            if masked:
                pt = jnp.where(diag_ok, pt, 0.0)
            dv = dv + _dot(pt, doi, NN)
            dst = (pt * (_dot(vj, doi, NT) - d_ref[0:1, pl.ds(off, tq)])).astype(MXU)
            dk = dk + _dot(dst, qi, NN)
            dq_ref[pl.ds(off, tq), :] += _dot(dst, kj, TN) * LN2
            return dk, dv

        carry = block(j, (jnp.zeros((tq, MLA_HEAD_PAD), F32), jnp.zeros((tq, MLA_V), F32)), True)
        dk, dv = lax.fori_loop(j + 1, nq, lambda i, c: block(i, c, False), carry)
        dk_ref[...] = dk * LN2
        dv_ref[...] = dv

    (dk, dv, dq), received = _comm_call(
        body, "mla_bwd", (MLA_HEADS, nq),
        [pl.BlockSpec((tq, MLA_HEAD_PAD), lambda h, j: (j, h)),
         pl.BlockSpec((tq, MLA_V), lambda h, j: (j, h)),
         pl.BlockSpec((t, MLA_HEAD_PAD), lambda h, j: (0, h)),
         pl.BlockSpec((t, MLA_V), lambda h, j: (0, h)),
         pl.BlockSpec((None, 8, t), lambda h, j: (h, 0, 0)),
         pl.BlockSpec((None, 8, t), lambda h, j: (h, 0, 0))],
        [pl.BlockSpec((tq, MLA_HEAD_PAD), lambda h, j: (j, h)),
         pl.BlockSpec((tq, MLA_V), lambda h, j: (j, h)),
         pl.BlockSpec((t, MLA_HEAD_PAD), lambda h, j: (0, h))],
        [jax.ShapeDtypeStruct((t, MLA_HEADS * MLA_HEAD_PAD), F32), jax.ShapeDtypeStruct((t, MLA_WIDTH), F32),
         jax.ShapeDtypeStruct((t, MLA_HEADS * MLA_HEAD_PAD), F32)],
        [k, v, q, do, lse, drow], ("parallel", "arbitrary"), comm)
    return dq, dk, dv, received


def _sb_logs(z2):
    lb = jnp.minimum(z2, 0.0) - jnp.log2(1.0 + jnp.exp2(jnp.minimum(z2, -z2)))
    return lb, lb - z2


def _tri(n, keep):
    r = lax.broadcasted_iota(jnp.int32, (n, n), 0)
    c = lax.broadcasted_iota(jnp.int32, (n, n), 1)
    return keep(r, c)


def _sb_fwd(hb, tq):
    t = hb.shape[0]
    tq = min(tq, t)
    nq, tk, hq = t // tq, min(SCAN, tq), tq // 2
    cq, ck, cv = C_SBQ // SB_DIM, C_SBK // SB_DIM, C_SBV // SB_DIM

    def body(q_ref, k_ref, v_ref, o_ref, tot_ref, visited_ref):
        h, i = pl.program_id(0), pl.program_id(1)
        qb = q_ref[...]
        strict = _tri(tq, lambda r, c: c < r)
        later = _tri(tk, lambda r, c: r > c).astype(MXU)

        def block(qv, off, n, carry, masked):
            run, acc = carry
            lb, l1 = _sb_logs(_dot(qv, k_ref[pl.ds(off, n), :], NT))
            if masked:
                l1 = jnp.where(strict, l1, 0.0)
            after, rowsum = _scan_keys(l1, later, True)
            a = jnp.exp2(lb + after + run)
            if masked:
                a = jnp.where(strict, a, 0.0)
            return run + rowsum, acc + _dot(a, v_ref[pl.ds(off, n), :], NN)

        diag = pl.multiple_of(i * tq, tq)
        carry = block(qb, diag, tq, (jnp.zeros((tq, 1), F32), jnp.zeros((tq, SB_DIM), F32)), True)

        def back(qv, first_row):
            def live(state):
                jj, (run, _) = state
                return jnp.logical_and(jj < i * (tq // tk), jnp.max(run[first_row:]) > -SB_CUTOFF)

            def step(state):
                jj, c = state
                return jj + 1, block(qv, pl.multiple_of(diag - (jj + 1) * tk, tk), tk, c, False)

            return live, step

        n_all, carry = lax.while_loop(*back(qb, hq), (jnp.int32(0), carry))
        n_any, upper = lax.while_loop(*back(qb[:hq], 0), (n_all, tuple(c[:hq] for c in carry)))
        run, acc = (jnp.concatenate([u, c[hq:]], axis=0) for u, c in zip(upper, carry))
        o_ref[...] = acc
        tot_ref[...] = _col2row(run)
        visited_ref[h, i] = n_all
        visited_ref[SB_HEADS + h, i] = n_any

    return _comm_call(
        body, "sb_fwd", (SB_HEADS, nq),
        [pl.BlockSpec((tq, SB_DIM), lambda h, i: (i, cq + h)),
         pl.BlockSpec((t, SB_DIM), lambda h, i: (0, ck + h)),
         pl.BlockSpec((t, SB_DIM), lambda h, i: (0, cv + h))],
        [pl.BlockSpec((tq, SB_DIM), lambda h, i: (i, h)),
         pl.BlockSpec((None, 8, tq), lambda h, i: (h, 0, i)),
         pl.BlockSpec(memory_space=pltpu.SMEM)],
        [jax.ShapeDtypeStruct((t, SB_WIDTH), F32), jax.ShapeDtypeStruct((SB_HEADS, 8, t), F32),
         jax.ShapeDtypeStruct((2 * SB_HEADS, nq), jnp.int32)],
        [hb, hb, hb], ("arbitrary", "arbitrary"), None)[0]


def _sb_bwd(hb, do, tot, visited, tq):
    t = hb.shape[0]
    tq = min(tq, t)
    nq, tk, hq = t // tq, min(SCAN, tq), tq // 2
    cq, ck, cv = C_SBQ // SB_DIM, C_SBK // SB_DIM, C_SBV // SB_DIM

    def body(q_ref, do_ref, tot_ref, k_ref, v_ref, visited_ref, dq_ref, dk_ref, dv_ref):
        h, i = pl.program_id(0), pl.program_id(1)

        @pl.when(i == 0)
        def _():
            dk_ref[...] = jnp.zeros_like(dk_ref)
            dv_ref[...] = jnp.zeros_like(dv_ref)

        qb, dob = q_ref[...], do_ref[...]
        total = _row2col(tot_ref[0:1, :])
        strict = _tri(tq, lambda r, c: c < r)
        upto = _tri(tk, lambda r, c: r <= c).astype(MXU)
        before = _tri(tk, lambda r, c: r < c).astype(MXU)

        def block(rows, off, n, carry, masked):
            pre, gpre, dq = carry
            qv, dov, tot_v = qb[:rows], dob[:rows], total[:rows]
            kj, vj = k_ref[pl.ds(off, n), :], v_ref[pl.ds(off, n), :]
            lb, l1 = _sb_logs(_dot(qv, kj, NT))
            if masked:
                l1 = jnp.where(strict, l1, 0.0)
            upto_sum, rowsum = _scan_keys(l1, upto, False)
            a = jnp.exp2(lb + (tot_v - (upto_sum + pre)))
            if masked:
                a = jnp.where(strict, a, 0.0)
            g = a * _dot(dov, vj, NT)
            before_sum, grow = _scan_keys(g, before, False)
            beta = jnp.exp2(lb)
            dz = g * (1.0 - beta) - (before_sum + gpre) * beta
            if masked:
                dz = jnp.where(strict, dz, 0.0)
            dzb = dz.astype(MXU)
            dk_ref[pl.ds(off, n), :] += _dot(dzb, qv, TN) * LN2
            dv_ref[pl.ds(off, n), :] += _dot(a, dov, TN)
            return pre + rowsum, gpre + grow, dq + _dot(dzb, kj, NN)

        diag = pl.multiple_of(i * tq, tq)
        n_all, n_any = visited_ref[h, i], visited_ref[SB_HEADS + h, i]
        start = diag - n_any * tk

        def earlier(rows, first, count, carry):
            return lax.fori_loop(
                0, count, lambda jj, c: block(rows, pl.multiple_of(start + (first + jj) * tk, tk), tk, c, False), carry)

        def zeros(rows):
            return jnp.zeros((rows, 1), F32), jnp.zeros((rows, 1), F32), jnp.zeros((rows, SB_DIM), F32)

        upper = earlier(hq, 0, n_any - n_all, zeros(hq))
        carry = tuple(jnp.concatenate([u, z], axis=0) for u, z in zip(upper, zeros(tq - hq)))
        carry = earlier(tq, n_any - n_all, n_all, carry)
        dq_ref[...] = (block(tq, diag, tq, carry, True)[2] * SB_SCALE).astype(dq_ref.dtype)

    return _comm_call(
        body, "sb_bwd", (SB_HEADS, nq),
        [pl.BlockSpec((tq, SB_DIM), lambda h, i: (i, cq + h)),
         pl.BlockSpec((tq, SB_DIM), lambda h, i: (i, h)),
         pl.BlockSpec((None, 8, tq), lambda h, i: (h, 0, i)),
         pl.BlockSpec((t, SB_DIM), lambda h, i: (0, ck + h)),
         pl.BlockSpec((t, SB_DIM), lambda h, i: (0, cv + h)),
         pl.BlockSpec(memory_space=pltpu.SMEM)],
        [pl.BlockSpec((tq, SB_DIM), lambda h, i: (i, h)),
         pl.BlockSpec((t, SB_DIM), lambda h, i: (0, h)),
         pl.BlockSpec((t, SB_DIM), lambda h, i: (0, h))],
        [jax.ShapeDtypeStruct((t, SB_WIDTH), MXU), jax.ShapeDtypeStruct((t, SB_WIDTH), F32),
         jax.ShapeDtypeStruct((t, SB_WIDTH), F32)],
        [hb, do, tot, hb, hb, visited], ("parallel", "arbitrary"), None)[0]


TM = 256
TM_WIDE = 512
TQ_MLA_FWD = 2048
TQ_MLA_BWD = 1024
TQ_SB = 512


def _layer_fwd(xf, xb, w, consts, comm=None, rest=None):
    t = xf.shape[0]
    cc, sa, sb, mem = consts
    col_scale = jnp.ones((1, D_IN_PAD), F32).at[:, C_SBQ:C_SBQ + SB_WIDTH].set(SB_SCALE * LOG2E)
    (h, hb), arrived = _matmul("in_proj", xb, w["w_in"], "nt", 256, D_IN_PAD, D_MODEL, second_dtype=MXU, second_scale=col_scale,
                               comm=rest[0] if rest else None, with_received=True)
    if rest:
        w = dict(w, **rest[1](arrived))
    gq, gkv = w["q_norm_g"][None, :], w["kv_norm_g"][None, :]
    q, k, v = _rowcall(
        "mla_prep", _prep_fn, t, TM,
        [(h, 512, C_CQ // 512), (h, 256, C_CKV // 256), (h, 128, C_KPE // 128), (cc, 128, 0), (sa, 128, 0), (sb, 128, 0)],
        [gq, w["w_uq"], gkv, w["w_ukv"]],
        [_ro(t, MLA_HEADS * MLA_HEAD_PAD, MXU, TM), _ro(t, MLA_HEADS * MLA_HEAD_PAD, MXU, TM), _ro(t, MLA_WIDTH, MXU, TM)])
    oa, lse, received = _mla_fwd(q, k, v, TQ_MLA_FWD, comm)
    oc, tot, visited = _sb_fwd(hb, TQ_SB)
    mkv = _matmul("mem_kv", mem, w["w_mem_kv"], "nn", 256, 512, 2048)
    lng, lnb = w["sg_ln_g"][None, :], w["sg_ln_b"][None, :]
    biasf = jnp.repeat(w["sg_b"].T, SG_CH, axis=1)
    post_rows = [(h, 768, C_GA // 768), (h, 512, C_SGU // 512), (h, 512, C_SGV // 512), (h, 512, C_GB // 512),
                 (h, 512, C_GC // 512), (h, 256, C_MQ // 256), (h, 256, C_GM // 256), (oa, 768, 0), (oc, 512, 0)]
    post_consts = [mkv, lng, lnb, w["sg_w"], biasf]
    (yg,) = _rowcall("mixer_post", lambda *a: (_post_fn(*a),), t, TM, post_rows, post_consts, [_ro(t, D_MODEL, MXU, TM)])
    z, xn, xnb = _rowcall(
        "out_proj_ln", _outproj_fn, t, TM_WIDE, [(yg, D_MODEL, 0), (xf, D_MODEL, 0)],
        [w["w_out"], w["ln_g"][None, :], w["ln_b"][None, :]],
        [_ro(t, D_MODEL, F32, TM_WIDE), _ro(t, D_MODEL, F32, TM_WIDE), _ro(t, D_MODEL, MXU, TM_WIDE)])
    saved = dict(w=w, xb=xb, h=h, hb=hb, q=q, k=k, v=v, oa=oa, lse=lse, oc=oc, tot=tot, visited=visited, mkv=mkv, yg=yg, z=z,
                 post_rows=post_rows, post_consts=post_consts, gq=gq, gkv=gkv)
    return xn, xnb, saved, received


def _layer_bwd(dxn, w, s, consts, recv=None, early_items=None, late_items=None):
    t = dxn.shape[0]
    cc, sa, sb, mem = consts
    h = s["h"]
    dz, dzb, dyg, dlng2, dlnb2 = _rowcall(
        "ln_bwd_out_proj_dx", _ln_bwd_proj_fn, t, TM, [(dxn, D_MODEL, 0), (s["z"], D_MODEL, 0)],
        [w["ln_g"][None, :], w["ln_b"][None, :], w["w_out"]],
        [_ro(t, D_MODEL, F32, TM), _ro(t, D_MODEL, MXU, TM), _ro(t, D_MODEL, F32, TM)],
        [((1, D_MODEL), F32), ((1, D_MODEL), F32)])
    dw_out = _matmul("out_proj_dw", s["yg"], dzb, "tn", 1024, 1024, 2048, out_dtype=GRAD_WIRE)
    tm = min(TM, t)
    outs = _rowcall(
        "mixer_post_bwd", _post_bwd_fn, t, TM, [(dyg, D_MODEL, 0)] + s["post_rows"], s["post_consts"],
        [_ro(t, 768, MXU, TM), _ro(t, 512, MXU, TM), _ro(t, 512, MXU, TM), _ro(t, 512, MXU, TM), _ro(t, 512, MXU, TM),
         _ro(t, 256, MXU, TM), _ro(t, 256, MXU, TM), _ro(t, 768, MXU, TM), _ro(t, 512, MXU, TM),
         ((MLA_HEADS, 8, t), F32, (MLA_HEADS, 8, tm), lambda i: (0, 0, i))],
        [((MEM_TOKENS, 2 * MEM_WIDTH), F32), ((1, SG_WIDTH), F32), ((1, SG_WIDTH), F32),
         ((SG_GROUPS, SG_CHUNK, SG_CHUNK), F32), ((SG_CHUNK, SG_WIDTH), F32)])
    dga, dsgu, dsgv, dgb, dgc, dmq, dgm, doa, doc, drow, dmkv, dsg_ln_g, dsg_ln_b, dsgw, dbias = outs
    dw_mem_kv = _matmul("mem_kv_dw", mem, dmkv, "tn", 1024, 512, 256, out_dtype=GRAD_WIRE)
    dsbq, dsbk, dsbv = _sb_bwd(s["hb"], doc, s["tot"], s["visited"], TQ_SB)
    named = early_items(dict(w_out=dw_out, w_mem_kv=dw_mem_kv)) if early_items else []
    dq, dk, dv, received = _mla_bwd(s["q"], s["k"], s["v"], doa, s["lse"], drow, TQ_MLA_BWD,
                                    _Comm([item for _, item in named]) if named else None)
    if named:
        recv.update({n: r for (n, _), r in zip(named, received)})
    wq, wkv = MLA_HEADS * MLA_HEAD_PAD, MLA_WIDTH
    dcq, dckv, dkpe, dgq, dwuq, dgkv, dwukv = _rowcall(
        "mla_prep_bwd", _prep_bwd_fn, t, TM,
        [(dq, wq, 0), (dk, wq, 0), (dv, wkv, 0), (h, 512, C_CQ // 512), (h, 256, C_CKV // 256), (h, 128, C_KPE // 128),
         (cc, 128, 0), (sa, 128, 0), (sb, 128, 0)],
        [s["gq"], w["w_uq"], s["gkv"], w["w_ukv"]],
        [_ro(t, 512, MXU, TM), _ro(t, 256, MXU, TM), _ro(t, 128, MXU, TM)],
        [((1, MLA_Q_RANK), F32), ((wq, MLA_Q_RANK), F32), ((1, MLA_KV_RANK), F32), ((2 * wkv, MLA_KV_RANK), F32)])
    dh = jnp.concatenate([dcq, dckv, dga, dsgu, dsgv, dgb, dsbq, dsbk.astype(MXU), dsbv.astype(MXU), dgc, dmq, dgm, dkpe], axis=1)
    dw_in = _matmul("in_proj_dw", dh, s["xb"], "tn", 1152, 1024, 2048, out_dtype=GRAD_WIRE)
    replicated = dict(q_norm_g=dgq[0], kv_norm_g=dgkv[0], sg_ln_g=dsg_ln_g[0], sg_ln_b=dsg_ln_b[0], sg_w=dsgw, sg_bias=dbias,
                      ln_g=dlng2[0], ln_b=dlnb2[0])
    named = late_items(dict(w_in=dw_in, w_uq=dwuq, w_ukv=dwukv), replicated) if late_items else []
    dx, received = _matmul("in_proj_dx", dh, w["w_in"], "nn", 512, 1024, D_IN_PAD, add=dz, add_scale=ALPHA,
                           comm=_Comm([item for _, item in named]) if named else None, with_received=True)
    if named:
        recv.update({n: r for (n, _), r in zip(named, received)})
    return dx, dict(replicated, w_in=dw_in, w_uq=dwuq, w_ukv=dwukv, w_mem_kv=dw_mem_kv, w_out=dw_out)


def _sg_bias_grad(bias_grads):
    db = jnp.concatenate(bias_grads, axis=0)

    def lane_sums(d):
        return (jnp.concatenate([jnp.broadcast_to(jnp.sum(d[:, g * SG_CH:(g + 1) * SG_CH], axis=1, keepdims=True),
                                                  (d.shape[0], SG_CH)) for g in range(SG_GROUPS)], axis=1),)

    (dbs,) = _rowcall("sg_bias_sum", lane_sums, db.shape[0], SG_CHUNK, [(db, SG_WIDTH, 0)], [], [_ro(db.shape[0], SG_WIDTH, F32, SG_CHUNK)])
    return dbs[:, ::SG_CH].reshape(len(bias_grads), SG_CHUNK, SG_GROUPS).transpose(0, 2, 1)


def _local_step(x, mem, positions, target, depth, weights_of, gather_comm, first_rest, grad_items, rep_item, recv):
    t = x.shape[0]
    inv_freq = ROPE_THETA ** (-jnp.arange(0, MLA_ROPE, 2, dtype=F32) / MLA_ROPE)
    invf = jnp.tile(inv_freq, 4)[None, :]
    cc, sa, sb = _rowcall("rope_tables", _rope_tables_fn, t, TM, [(positions[:, None], 1, 0)], [invf],
                          [_ro(t, 128, F32, TM)] * 3)
    consts = (cc, sa, sb, mem)
    xf, xb, saved, received = x, x.astype(MXU), [], None
    for l in range(depth):
        xf, xb, s, received = _layer_fwd(xf, xb, weights_of(l, received), consts, gather_comm(l + 1) if l + 1 < depth else None,
                                         first_rest if l == 0 else None)
        saved.append(s)
    dx, loss_part = _rowcall("loss", _loss_fn, t, TM_WIDE, [(xf, D_MODEL, 0), (target, D_MODEL, 0)], [],
                             [_ro(t, D_MODEL, F32, TM_WIDE)], [((1, LANES), F32)])
    grads, recv, sg_b_grad = [None] * depth, dict(recv or {}), []
    for l in reversed(range(depth)):
        def early(g, l=l):
            rest = {n: grads[l + 1][n] for n in ("w_in", "w_uq", "w_ukv")} if l + 1 < depth else {}
            return grad_items(l, g, recv) + (grad_items(l + 1, rest, recv) if rest else [])

        def late(g, replicated):
            per_layer = [replicated] + grads[1:]
            sg_b_grad.append(_sg_bias_grad([r["sg_bias"] for r in per_layer]))
            return grad_items(0, g, recv) + [("replicated", rep_item(per_layer, sg_b_grad[0]))]

        dx, grads[l] = _layer_bwd(dx, saved[l]["w"], saved[l], consts, recv, early if grad_items else None,
                                  late if grad_items and l == 0 else None)
    dsgb = sg_b_grad[0] if sg_b_grad else _sg_bias_grad([g["sg_bias"] for g in grads])
    return loss_part[0, 0], dx, grads, dsgb, recv


SHARD_IN = D_IN // N_DEV
SHARD_IN_PAD = D_IN_PAD // N_DEV
IN_RANGES = ((0, 768), (832, D_IN), (768, 832))


def _in_rows_from_shards(w):
    pieces = []
    for a, b in IN_RANGES:
        while a < b:
            e = min(b, (a // SHARD_IN + 1) * SHARD_IN)
            row = (a // SHARD_IN) * SHARD_IN_PAD + a % SHARD_IN
            pieces.append(w[row:row + e - a])
            a = e
    pieces.append(jnp.zeros((D_IN_PAD - D_IN, w.shape[1]), w.dtype))
    return jnp.concatenate(pieces, axis=0)


def _in_rows_to_shards(g):
    pieces = []
    for d in range(N_DEV):
        a, b = d * SHARD_IN, (d + 1) * SHARD_IN
        while a < b:
            e = min(b, 768 if a < 768 else (832 if a < 832 else D_IN))
            row = a if a < 768 else (C_KPE + a - 768 if a < 832 else a - 64)
            pieces.append(g[row:row + e - a])
            a = e
        pieces.append(jnp.zeros((SHARD_IN_PAD - SHARD_IN, g.shape[1]), g.dtype))
    return jnp.concatenate(pieces, axis=0)


def _uq_rows_pad(w):
    w = w.reshape(MLA_HEADS, MLA_NOPE + MLA_ROPE, w.shape[-1])
    return jnp.pad(w, ((0, 0), (0, MLA_HEAD_PAD - MLA_NOPE - MLA_ROPE), (0, 0))).reshape(MLA_HEADS * MLA_HEAD_PAD, -1)


def _uq_rows_unpad(g):
    return g.reshape(MLA_HEADS, MLA_HEAD_PAD, -1)[:, :MLA_NOPE + MLA_ROPE].reshape(MLA_HEADS * (MLA_NOPE + MLA_ROPE), -1)


def _ukv_rows_split(w):
    return jnp.swapaxes(w.reshape(MLA_HEADS, 2, 128, -1), 0, 1).reshape(MLA_HEADS * 256, -1)


def _ukv_rows_unsplit(g):
    return jnp.swapaxes(g.reshape(2, MLA_HEADS, 128, -1), 0, 1).reshape(MLA_HEADS * 256, -1)


class _Comm:
    def __init__(self, items):
        self.items = items
        self.n = len(items)
        self.pieces = [((s.shape[0] // N_DEV,) + s.shape[1:]) if sc else s.shape for s, sc, _ in items]
        self.inputs = [s for s, _, _ in items] + [d[0] for _, _, d in items if d is not None]
        any_space = pl.BlockSpec(memory_space=pl.ANY)
        self.in_specs = [any_space] * len(self.inputs)
        self.out_specs = [any_space] * self.n
        self.out_shape = [jax.ShapeDtypeStruct((N_DEV,) + tuple(p), s.dtype) if d is None
                          else jax.ShapeDtypeStruct(d[0].shape, d[0].dtype) for p, (s, _, d) in zip(self.pieces, items)]
        self.scratch = [pltpu.SemaphoreType.DMA((N_DEV - 1, self.n)), pltpu.SemaphoreType.DMA((N_DEV - 1, self.n)),
                        pltpu.SemaphoreType.DMA((self.n,))]

    def aliases(self, first_in, first_out):
        passed = [a for a, (_, _, d) in enumerate(self.items) if d is not None]
        return {first_in + self.n + b: first_out + a for b, a in enumerate(passed)}

    def copies(self, in_refs, out_refs, sems):
        send_sems, recv_sems, local_sems = sems
        x, y, c = lax.axis_index("x"), lax.axis_index("y"), lax.axis_index("c")
        me = 4 * x + 2 * y + c

        def block_for(a, p):
            if not self.items[a][1]:
                return in_refs[a]
            n = self.pieces[a][0]
            return in_refs[a].at[pl.ds(pl.multiple_of(p * n, SUBLANES_16BIT), n)]

        def slot(a, dev):
            dst = self.items[a][2]
            return out_refs[a].at[dev] if dst is None else out_refs[a].at[dev, dst[1]]

        starts, recv_waits, send_waits = [], [], []
        for a in range(self.n):
            cp = pltpu.make_async_copy(block_for(a, me), slot(a, me), local_sems.at[a])
            starts.append(cp.start)
            send_waits.append(cp.wait)
        for k in range(1, N_DEV):
            px = 1 - x if (k >> 2) & 1 else x
            py = 1 - y if (k >> 1) & 1 else y
            pc = 1 - c if k & 1 else c
            peer = 4 * px + 2 * py + pc
            for a in range(self.n):
                out = pltpu.make_async_remote_copy(
                    src_ref=block_for(a, peer), dst_ref=slot(a, me), send_sem=send_sems.at[k - 1, a],
                    recv_sem=recv_sems.at[k - 1, a], device_id=(px, py, pc), device_id_type=pl.DeviceIdType.MESH)
                back = pltpu.make_async_remote_copy(
                    src_ref=block_for(a, peer), dst_ref=slot(a, peer), send_sem=send_sems.at[k - 1, a],
                    recv_sem=recv_sems.at[k - 1, a], device_id=(px, py, pc), device_id_type=pl.DeviceIdType.MESH)
                starts.append(out.start)
                send_waits.append(out.wait_send)
                recv_waits.append(back.wait_recv)
        return starts, recv_waits + send_waits


def _comm_call(body, name, grid, in_specs, out_specs, out_shape, operands, semantics, comm, scratch=()):
    n_in, n_out, n_scr = len(operands), len(out_shape), len(scratch)
    if comm is None:
        outs = pl.pallas_call(body, name=name, grid=grid, in_specs=in_specs, out_specs=out_specs, out_shape=out_shape,
                              scratch_shapes=list(scratch), compiler_params=_params(semantics))(*operands)
        return outs, []
    n_ci = len(comm.inputs)

    def fused(*refs):
        ins, c_in = refs[:n_in], refs[n_in:n_in + n_ci]
        outs = refs[n_in + n_ci:n_in + n_ci + n_out]
        c_out = refs[n_in + n_ci + n_out:n_in + n_ci + n_out + comm.n]
        first_scr = n_in + n_ci + n_out + comm.n
        outs = outs + refs[first_scr:first_scr + n_scr]
        starts, waits = comm.copies(c_in, c_out, refs[first_scr + n_scr:])
        first = functools.reduce(jnp.logical_and, [pl.program_id(d) == 0 for d in range(len(grid))])
        last = functools.reduce(jnp.logical_and, [pl.program_id(d) == g - 1 for d, g in enumerate(grid)])

        @pl.when(first)
        def _():
            for start in starts:
                start()

        body(*ins, *outs)

        @pl.when(last)
        def _():
            for wait in waits:
                wait()

    outs = pl.pallas_call(
        fused, name=name, grid=grid, in_specs=list(in_specs) + comm.in_specs, out_specs=list(out_specs) + comm.out_specs,
        out_shape=list(out_shape) + comm.out_shape, scratch_shapes=list(scratch) + comm.scratch,
        input_output_aliases=comm.aliases(n_in, n_out), compiler_params=_params(("arbitrary",) * len(grid)),
    )(*operands, *comm.inputs)
    return outs[:n_out], outs[n_out:]


def _exchange(name, items):
    return _comm_call(lambda: None, name, (1,), [], [], [], [], ("arbitrary",), _Comm(items))[1]


BLOCK_BYTES = 6 << 20


SUBLANES_16BIT = 16


def _row_block(rows, row_bytes):
    step = SUBLANES_16BIT
    if rows * row_bytes <= BLOCK_BYTES or rows % step:
        return rows
    return max(tb for tb in range(step, rows + 1, step) if rows % tb == 0 and (tb == step or tb * row_bytes <= BLOCK_BYTES))


def _sum_contribs(recv):
    n, rows, width = recv.shape
    tb = _row_block(rows, width * 4 * (n + 1))

    def body(r_ref, g_ref):
        g = r_ref[0].astype(F32)
        for k in range(1, n):
            g = g + r_ref[k].astype(F32)
        g_ref[...] = g

    return pl.pallas_call(
        body, name="sum_grads", grid=(rows // tb,), in_specs=[pl.BlockSpec((n, tb, width), lambda i: (0, i, 0))],
        out_specs=pl.BlockSpec((tb, width), lambda i: (i, 0)), out_shape=jax.ShapeDtypeStruct((rows, width), F32),
        compiler_params=_params(("parallel",)),
    )(recv)


def _adamw(contribs, w, m, v):
    n, rows, width = contribs.shape
    tb = _row_block(rows, width * 4 * (n + 7))
    c1 = 1.0 - ADAM_B1 ** ADAM_STEP
    c2 = 1.0 - ADAM_B2 ** ADAM_STEP

    def body(r_ref, w_ref, m_ref, v_ref, g_ref, d_ref, nm_ref, nv_ref):
        g = r_ref[0].astype(F32)
        for k in range(1, n):
            g = g + r_ref[k].astype(F32)
        nm = ADAM_B1 * m_ref[...] + (1.0 - ADAM_B1) * g
        nv = ADAM_B2 * v_ref[...] + (1.0 - ADAM_B2) * (g * g)
        g_ref[...] = g
        nm_ref[...] = nm
        nv_ref[...] = nv
        d_ref[...] = -ADAM_LR * ((nm / c1) / (jnp.sqrt(nv / c2) + ADAM_EPS) + ADAM_WD * w_ref[...])

    blk = pl.BlockSpec((tb, width), lambda i: (i, 0))
    return pl.pallas_call(
        body, name="adamw", grid=(rows // tb,),
        in_specs=[pl.BlockSpec((n, tb, width), lambda i: (0, i, 0)), blk, blk, blk],
        out_specs=[blk] * 4, out_shape=[jax.ShapeDtypeStruct((rows, width), F32)] * 4,
        compiler_params=_params(("parallel",)),
    )(contribs, w, m, v)


def kernel(x, mem, positions, w_in, q_norm_g, w_uq, kv_norm_g, w_ukv, sg_ln_g, sg_ln_b, sg_w, sg_b, w_mem_k, w_mem_v, w_out, ln_g, ln_b, loss_target, m_w_in, m_q_norm_g, m_w_uq, m_kv_norm_g, m_w_ukv, m_sg_ln_g, m_sg_ln_b, m_sg_w, m_sg_b, m_w_mem_k, m_w_mem_v, m_w_out, m_ln_g, m_ln_b, v_w_in, v_q_norm_g, v_w_uq, v_kv_norm_g, v_w_ukv, v_sg_ln_g, v_sg_ln_b, v_sg_w, v_sg_b, v_w_mem_k, v_w_mem_v, v_w_out, v_ln_g, v_ln_b):
    wl = dict(w_in=w_in, q_norm_g=q_norm_g, w_uq=w_uq, kv_norm_g=kv_norm_g, w_ukv=w_ukv, sg_ln_g=sg_ln_g, sg_ln_b=sg_ln_b,
              sg_w=sg_w, sg_b=sg_b, w_mem_k=w_mem_k, w_mem_v=w_mem_v, w_out=w_out, ln_g=ln_g, ln_b=ln_b)
    ml = dict(w_in=m_w_in, q_norm_g=m_q_norm_g, w_uq=m_w_uq, kv_norm_g=m_kv_norm_g, w_ukv=m_w_ukv, sg_ln_g=m_sg_ln_g,
              sg_ln_b=m_sg_ln_b, sg_w=m_sg_w, sg_b=m_sg_b, w_mem_k=m_w_mem_k, w_mem_v=m_w_mem_v, w_out=m_w_out, ln_g=m_ln_g, ln_b=m_ln_b)
    vl = dict(w_in=v_w_in, q_norm_g=v_q_norm_g, w_uq=v_w_uq, kv_norm_g=v_kv_norm_g, w_ukv=v_w_ukv, sg_ln_g=v_sg_ln_g,
              sg_ln_b=v_sg_ln_b, sg_w=v_sg_w, sg_b=v_sg_b, w_mem_k=v_w_mem_k, w_mem_v=v_w_mem_v, w_out=v_w_out, ln_g=v_ln_g, ln_b=v_ln_b)
    depth = w_in.shape[0]

    w_in_t = jnp.pad(jnp.swapaxes(w_in, 1, 2), ((0, 0), (0, SHARD_IN_PAD - SHARD_IN), (0, 0)))
    shards = [s.astype(MXU) for s in (w_in_t, jnp.swapaxes(w_uq, 1, 2), jnp.swapaxes(w_ukv, 1, 2),
                                      jnp.concatenate([w_mem_k, w_mem_v], axis=-1), w_out)]

    def gather_items(l):
        return [(s[l], False, None) for s in shards]

    (first_in,) = _exchange("gather_weights", gather_items(0)[:1])

    def in_rows(g_in):
        return _in_rows_from_shards(g_in.reshape(D_IN_PAD, D_MODEL))

    def other_weights(g_uq, g_ukv, g_mem, g_out):
        return dict(w_uq=_uq_rows_pad(g_uq.reshape(-1, MLA_Q_RANK)), w_ukv=_ukv_rows_split(g_ukv.reshape(-1, MLA_KV_RANK)),
                    w_mem_kv=g_mem.reshape(D_MODEL, 2 * MEM_WIDTH), w_out=g_out.reshape(D_MODEL, D_MODEL))

    def weights_of(l, received):
        w = dict(q_norm_g=q_norm_g[l], kv_norm_g=kv_norm_g[l], sg_ln_g=sg_ln_g[l], sg_ln_b=sg_ln_b[l], sg_w=sg_w[l],
                 sg_b=sg_b[l], ln_g=ln_g[l], ln_b=ln_b[l])
        if l == 0:
            return dict(w, w_in=in_rows(first_in))
        return dict(w, w_in=in_rows(received[0]), **other_weights(*received[1:]))

    first_rest = (_Comm(gather_items(0)[1:]), lambda received: other_weights(*received))

    to_rows = dict(w_in=_in_rows_to_shards, w_uq=_uq_rows_unpad, w_ukv=_ukv_rows_unsplit)

    def grad_items(l, g, recv):
        return [(n, (to_rows.get(n, lambda a: a)(a).astype(GRAD_WIRE), True, (recv[n], l))) for n, a in g.items()]

    shard_rows = dict(w_in=(D_IN_PAD, D_MODEL), w_uq=(MLA_HEADS * (MLA_NOPE + MLA_ROPE), MLA_Q_RANK),
                      w_ukv=(MLA_HEADS * 256, MLA_KV_RANK), w_mem_kv=(D_MODEL, 2 * MEM_WIDTH), w_out=(D_MODEL, D_MODEL))
    recv = {n: lax.empty((N_DEV, depth, rows // N_DEV, width), GRAD_WIRE) for n, (rows, width) in shard_rows.items()}
    rep_sizes = [wl[n].size for n in REPLICATED]
    rep_rows = -(-sum(rep_sizes) // (SUBLANES_16BIT * LANES)) * SUBLANES_16BIT

    def packed(d):
        flat = jnp.concatenate([d[n].reshape(-1) for n in REPLICATED])
        return jnp.pad(flat, (0, rep_rows * LANES - flat.shape[0])).reshape(rep_rows, LANES)

    def rep_item(per_layer, dsgb):
        rep_g = dict(sg_b=dsgb, **{n: jnp.stack([g[n] for g in per_layer]) for n in REPLICATED if n != "sg_b"})
        return (packed(rep_g), False, None)

    loss_part, grad_x, grads, dsgb, recv = _local_step(
        x[0], mem[0], positions[0], loss_target[0], depth, weights_of, lambda l: _Comm(gather_items(l)), first_rest,
        grad_items, rep_item, recv)
    loss = lax.psum(loss_part, ("x", "y", "c"))
    r_in, r_uq, r_ukv, r_mem, r_out, r_rep = (recv[n] for n in ("w_in", "w_uq", "w_ukv", "w_mem_kv", "w_out", "replicated"))

    def rows2d(a):
        return a.reshape(-1, a.shape[-1])

    def update(name, contribs):
        outs = _adamw(contribs, rows2d(wl[name]), rows2d(ml[name]), rows2d(vl[name]))
        return [o.reshape(wl[name].shape) for o in outs]

    def from_transposed(recv, keep):
        s = _sum_contribs(recv.reshape(N_DEV, -1, recv.shape[-1])).reshape(recv.shape[1:])
        return rows2d(jnp.swapaxes(s[:, :keep], 1, 2))[None]

    res = dict(
        w_in=update("w_in", from_transposed(r_in, SHARD_IN)),
        w_uq=update("w_uq", from_transposed(r_uq, r_uq.shape[2])),
        w_ukv=update("w_ukv", from_transposed(r_ukv, r_ukv.shape[2])),
        w_mem_k=update("w_mem_k", r_mem[..., :MEM_WIDTH].reshape(N_DEV, -1, MEM_WIDTH)),
        w_mem_v=update("w_mem_v", r_mem[..., MEM_WIDTH:].reshape(N_DEV, -1, MEM_WIDTH)),
        w_out=update("w_out", r_out.reshape(N_DEV, -1, D_MODEL)))
    rep_outs = _adamw(r_rep, packed(wl), packed(ml), packed(vl))
    off = 0
    for n, size in zip(REPLICATED, rep_sizes):
        res[n] = [o.reshape(-1)[off:off + size].reshape(wl[n].shape) for o in rep_outs]
        off += size
    return (loss, grad_x[None], *[res[n][0] for n in WEIGHTS], *[res[n][1] for n in WEIGHTS],
            *[res[n][2] for n in WEIGHTS], *[res[n][3] for n in WEIGHTS])
```
